```python
import jax
import jax.numpy as jnp
from jax import lax
import numpy as np

D_MODEL = 1024
BATCH = 32
SEQ = 256
DEPTH = 4
DEC_BATCH = 4
DEC_SEQ = 2048
PAST_LEN = 512

GRID_W = 64
N_MIXERS = 2
N_HGRN_LAYERS = (DEPTH + N_MIXERS - 1) // N_MIXERS
N_FOURIER_LAYERS = DEPTH // N_MIXERS
HGRN_HEADS = 8
HGRN_DK = 128
HGRN_DV = D_MODEL // HGRN_HEADS
HGRN_HK = HGRN_HEADS * HGRN_DK
HGRN_HV = HGRN_HEADS * HGRN_DV
HGRN_IN = 3 * HGRN_HK + 2 * HGRN_HV
CHUNK = 16
FOURIER_GROUPS = 4
FOURIER_CG = D_MODEL // FOURIER_GROUPS
MOE_GROUPS = 4
MOE_EXPERTS_PER_GROUP = 4
MOE_EXPERTS = MOE_GROUPS * MOE_EXPERTS_PER_GROUP
MOE_TOP_K = 2
MOE_D_FF = D_MODEL // 2
NORM_EPS = 1e-6
POS_BASE = 10000.0

kernel_name = 'hybrid_hgrn2_fnet_hmoe_diffusion_step'


def _rmsnorm(x, g):
    xf = x.astype(jnp.float32)
    y = xf * lax.rsqrt(jnp.mean(xf * xf, axis=-1, keepdims=True) + NORM_EPS)
    return (y * g.astype(jnp.float32)).astype(x.dtype)


def _modulate(h, shift, scale):
    return h * (1 + scale[:, None, :]) + shift[:, None, :]


def _grid_pos_embed(n_tokens, dtype):
    rows = n_tokens // GRID_W
    t = jnp.arange(rows * GRID_W)
    r = (t // GRID_W).astype(jnp.float32)
    col = (t % GRID_W).astype(jnp.float32)
    quarter = D_MODEL // 4
    omega = 1.0 / (POS_BASE ** (jnp.arange(quarter, dtype=jnp.float32) / quarter))
    ar = r[:, None] * omega[None, :]
    ac = col[:, None] * omega[None, :]
    return jnp.concatenate([jnp.sin(ar), jnp.cos(ar), jnp.sin(ac), jnp.cos(ac)], axis=-1).astype(dtype)


def _hgrn_lower_bounds(lb_logits):
    p = jax.nn.softmax(lb_logits.astype(jnp.float32), axis=0)
    cs = jnp.cumsum(p, axis=0)
    return cs - cs[0:1]


def _chunk_scan(q, log_f, k, v, s0):
    B, H, L, DK = q.shape
    DV = v.shape[-1]
    N = L // CHUNK
    q, log_f, k = (t.reshape(B, H, N, CHUNK, DK) for t in (q, log_f, k))
    v = v.reshape(B, H, N, CHUNK, DV)
    b = jnp.cumsum(log_f, axis=3)
    b_last = b[:, :, :, -1:, :]
    tri = jnp.tril(jnp.ones((CHUNK, CHUNK), dtype=bool))[:, :, None]
    diff = b[:, :, :, :, None, :] - b[:, :, :, None, :, :]
    decay = jnp.where(tri, jnp.exp(jnp.where(tri, diff, 0.0)), 0.0)
    scores = jnp.einsum('bhntk,bhntsk,bhnsk->bhnts', q, decay, k)
    o_intra = jnp.einsum('bhnts,bhnsv->bhntv', scores, v)
    u = jnp.einsum('bhnsk,bhnsv->bhnkv', k * jnp.exp(b_last - b), v)
    g = jnp.exp(b_last[:, :, :, 0, :])

    def step(s, gu):
        gn, un = gu
        return gn[..., None] * s + un, s

    s_final, s_start = lax.scan(step, s0, (jnp.moveaxis(g, 2, 0), jnp.moveaxis(u, 2, 0)))
    s_start = jnp.moveaxis(s_start, 0, 2)
    o_inter = jnp.einsum('bhntk,bhnkv->bhntv', q * jnp.exp(b), s_start)
    return (o_intra + o_inter).reshape(B, H, L, DV), s_final


def _hgrn_mixer(h, w_in, lb, norm_g, w_out, s0_f, s0_b):
    B, L, _ = h.shape
    proj = jnp.einsum('bld,de->ble', h, w_in).astype(jnp.float32)
    q, zf, zb, v, g = jnp.split(proj, [HGRN_HK, 2 * HGRN_HK, 3 * HGRN_HK, 3 * HGRN_HK + HGRN_HV], axis=-1)

    def heads(t, d):
        return t.reshape(B, L, HGRN_HEADS, d).transpose(0, 2, 1, 3)

    def gates(z, lb_dir):
        log_f = jnp.logaddexp(jnp.log(lb_dir), jnp.log1p(-lb_dir) + jax.nn.log_sigmoid(z))
        log_f = heads(log_f, HGRN_DK)
        return log_f, -jnp.expm1(log_f)

    q = heads(q, HGRN_DK)
    v = heads(v, HGRN_DV)
    lf_f, k_f = gates(zf, lb[0])
    lf_b, k_b = gates(zb, lb[1])
    o_f, s_f = _chunk_scan(q, lf_f, k_f, v, s0_f)
    flip = lambda t: jnp.flip(t, axis=2)
    o_b, s_b = _chunk_scan(flip(q), flip(lf_b), flip(k_b), flip(v), s0_b)
    o = (o_f + flip(o_b)).transpose(0, 2, 1, 3)
    o = o * lax.rsqrt(jnp.mean(o * o, axis=-1, keepdims=True) + NORM_EPS)
    o = o * norm_g.astype(jnp.float32).reshape(HGRN_HEADS, HGRN_DV)
    o = o.reshape(B, L, HGRN_HV) * jax.nn.silu(g)
    out = jnp.einsum('ble,ed->bld', o.astype(h.dtype), w_out)
    return out, s_f, s_b


def _fourier_mixer(h, w_out):
    B, L, _ = h.shape
    hg = h.astype(jnp.float32).reshape(B, L, FOURIER_GROUPS, FOURIER_CG)
    z = jnp.fft.fftn(hg, axes=(1, 3), norm='ortho').real.reshape(B, L, D_MODEL)
    return jnp.einsum('bld,de->ble', z.astype(h.dtype), w_out)


def _hier_moe(h, w_rg, b_rg, w_re, b_re, w_gate, w_up, w_down):
    B, L, _ = h.shape
    t = h.reshape(B * L, D_MODEL)
    T = t.shape[0]
    pg = jax.nn.softmax((t @ w_rg + b_rg).astype(jnp.float32), axis=-1)
    gi = jnp.argmax(pg, axis=-1)
    pgv = jnp.max(pg, axis=-1, keepdims=True)
    el = (t @ w_re + b_re).astype(jnp.float32).reshape(T, MOE_GROUPS, MOE_EXPERTS_PER_GROUP)
    sel = jnp.take_along_axis(el, gi[:, None, None], axis=1)[:, 0]
    pe = jax.nn.softmax(sel, axis=-1)
    tv, ti = lax.top_k(pe, MOE_TOP_K)
    tv = tv / jnp.sum(tv, axis=-1, keepdims=True)
    eidx = gi[:, None] * MOE_EXPERTS_PER_GROUP + ti
    gates = jnp.sum(jax.nn.one_hot(eidx, MOE_EXPERTS, dtype=jnp.float32) * (pgv * tv)[..., None], axis=1)
    hid = jax.nn.silu(jnp.einsum('td,edf->tef', t, w_gate)) * jnp.einsum('td,edf->tef', t, w_up)
    y = jnp.einsum('tef,te,efd->td', hid, gates.astype(hid.dtype), w_down)
    return y.reshape(B, L, D_MODEL)


def _run_trunk(x, cond, init_states, w, lb):
    B = x.shape[0]
    states = []
    for i in range(DEPTH):
        j = i // N_MIXERS
        mod = jnp.einsum('bd,de->be', jax.nn.silu(cond), w['w_mod'][i]) + w['b_mod'][i]
        sh1, sc1, g1, sh2, sc2, g2 = jnp.split(mod, 6, axis=-1)
        h = _modulate(_rmsnorm(x, w['norm1_g'][i]), sh1, sc1)
        if i % N_MIXERS == 0:
            if init_states is None:
                s0_f = jnp.zeros((B, HGRN_HEADS, HGRN_DK, HGRN_DV), jnp.float32)
                s0_b = s0_f
            else:
                s0_f = init_states[:, j, 0].astype(jnp.float32)
                s0_b = init_states[:, j, 1].astype(jnp.float32)
            out, s_f, s_b = _hgrn_mixer(h, w['hgrn_w_in'][j], lb[j], w['hgrn_norm_g'][j], w['hgrn_w_out'][j], s0_f, s0_b)
            states.append(jnp.stack([s_f, s_b], axis=1))
        else:
            out = _fourier_mixer(h, w['fourier_w_out'][j])
        x = x + g1[:, None, :] * out
        h = _modulate(_rmsnorm(x, w['norm2_g'][i]), sh2, sc2)
        x = x + g2[:, None, :] * _hier_moe(h, w['router_group_w'][i], w['router_group_b'][i],
                                            w['router_expert_w'][i], w['router_expert_b'][i],
                                            w['moe_w_gate'][i], w['moe_w_up'][i], w['moe_w_down'][i])
    return _rmsnorm(x, w['final_norm_g']), states


def setup_inputs(seed: int = 0) -> dict:
    key = jax.random.key(seed)
    ks = jax.random.split(key, 22)
    f32 = jnp.float32

    def nrm(k, shape, s):
        return jax.random.normal(k, shape, f32) * s

    return {
        'x_prompt': nrm(ks[0], (BATCH, SEQ, D_MODEL), 1.0),
        'x_sample': nrm(ks[1], (DEC_BATCH, DEC_SEQ, D_MODEL), 1.0),
        'state_hgrn': nrm(ks[2], (DEC_BATCH, N_HGRN_LAYERS, 2, HGRN_HEADS, HGRN_DK, HGRN_DV), 0.5),
        'c': nrm(ks[3], (DEC_BATCH, D_MODEL), 1.0),
        'c_ctx': nrm(ks[4], (D_MODEL,), 1.0),
        'w_mod': nrm(ks[5], (DEPTH, D_MODEL, 6 * D_MODEL), 0.5 * D_MODEL ** -0.5),
        'b_mod': nrm(ks[6], (DEPTH, 6 * D_MODEL), 0.02),
        'norm1_g': 1.0 + nrm(ks[7], (DEPTH, D_MODEL), 0.02),
        'norm2_g': 1.0 + nrm(ks[8], (DEPTH, D_MODEL), 0.02),
        'hgrn_w_in': nrm(ks[9], (N_HGRN_LAYERS, D_MODEL, HGRN_IN), D_MODEL ** -0.5),
        'hgrn_lb_logits': nrm(ks[10], (N_HGRN_LAYERS, 2, HGRN_HK), 0.5),
        'hgrn_norm_g': 1.0 + nrm(ks[11], (N_HGRN_LAYERS, HGRN_HV), 0.02),
        'hgrn_w_out': nrm(ks[12], (N_HGRN_LAYERS, HGRN_HV, D_MODEL), HGRN_HV ** -0.5),
        'fourier_w_out': nrm(ks[13], (N_FOURIER_LAYERS, D_MODEL, D_MODEL), D_MODEL ** -0.5),
        'router_group_w': nrm(ks[14], (DEPTH, D_MODEL, MOE_GROUPS), D_MODEL ** -0.5),
        'router_group_b': nrm(ks[15], (DEPTH, MOE_GROUPS), 0.01),
        'router_expert_w': nrm(ks[16], (DEPTH, D_MODEL, MOE_EXPERTS), D_MODEL ** -0.5),
        'router_expert_b': nrm(ks[17], (DEPTH, MOE_EXPERTS), 0.01),
        'moe_w_gate': nrm(ks[18], (DEPTH, MOE_EXPERTS, D_MODEL, MOE_D_FF), D_MODEL ** -0.5),
        'moe_w_up': nrm(ks[19], (DEPTH, MOE_EXPERTS, D_MODEL, MOE_D_FF), D_MODEL ** -0.5),
        'moe_w_down': nrm(ks[20], (DEPTH, MOE_EXPERTS, MOE_D_FF, D_MODEL), MOE_D_FF ** -0.5),
        'final_norm_g': 1.0 + nrm(ks[21], (D_MODEL,), 0.02),
    }


def reference(x_prompt, x_sample, state_hgrn, c, c_ctx, w_mod, b_mod, norm1_g, norm2_g,
              hgrn_w_in, hgrn_lb_logits, hgrn_norm_g, hgrn_w_out, fourier_w_out,
              router_group_w, router_group_b, router_expert_w, router_expert_b,
              moe_w_gate, moe_w_up, moe_w_down, final_norm_g):
    w = {
        'w_mod': w_mod, 'b_mod': b_mod, 'norm1_g': norm1_g, 'norm2_g': norm2_g,
        'hgrn_w_in': hgrn_w_in, 'hgrn_norm_g': hgrn_norm_g, 'hgrn_w_out': hgrn_w_out,
        'fourier_w_out': fourier_w_out,
        'router_group_w': router_group_w, 'router_group_b': router_group_b,
        'router_expert_w': router_expert_w, 'router_expert_b': router_expert_b,
        'moe_w_gate': moe_w_gate, 'moe_w_up': moe_w_up, 'moe_w_down': moe_w_down,
        'final_norm_g': final_norm_g,
    }
    lb = _hgrn_lower_bounds(hgrn_lb_logits)
    y_prompt, ctx_states = _run_trunk(x_prompt, c_ctx[None, :], None, w, lb)
    new_state_hgrn = jnp.stack(ctx_states, axis=1).astype(x_prompt.dtype)
    x_lat = x_sample + _grid_pos_embed(x_sample.shape[1], x_sample.dtype)[None]
    y_sample, _ = _run_trunk(x_lat, c, state_hgrn, w, lb)
    return (y_prompt, y_sample, new_state_hgrn)
```

```python
import functools
import math

import jax
import jax.numpy as jnp
from jax import lax
from jax.experimental import pallas as pl
from jax.experimental.pallas import tpu as pltpu

F32 = jnp.float32
BF16 = jnp.bfloat16
HIGHEST = lax.Precision.HIGHEST

D_MODEL = 1024
DEPTH = 4
GRID_W = 64
HEADS = 8
DK = 128
DV = 128
FOURIER_GROUPS = 4
FOURIER_CG = D_MODEL // FOURIER_GROUPS
MOE_GROUPS = 4
MOE_EPG = 4
MOE_EXPERTS = 16
MOE_D_FF = 512
NORM_EPS = 1e-6
POS_BASE = 10000.0

COND_ROWS = 8
CTX_ROW = 4
ROUTER_ROWS = 32
LANES = 128
SCAN_CHUNK = 64
SCAN_SUB = 16
SEQ_BLOCK = 2048
ROW_TILE = 256
POST_TILE = 512
MOE_BLOCK = 1024
VMEM_LIMIT = 56 * 1024 * 1024


def _cparams(*sem):
    return pltpu.CompilerParams(dimension_semantics=sem, vmem_limit_bytes=VMEM_LIMIT)


def _silu(x):
    return x * jax.nn.sigmoid(x)


def _norm_mod(x, g, shift, scale):
    ms = jnp.mean(x * x, axis=-1, keepdims=True)
    return (x * lax.rsqrt(ms + NORM_EPS) * g) * (1.0 + scale) + shift


def _mod_body(c_ref, w_ref, b_ref, o_ref):
    s = _silu(c_ref[...])
    o_ref[0] = jnp.dot(s, w_ref[0], precision=HIGHEST, preferred_element_type=F32) + b_ref[0]


def _mod_call(cond, w_mod, b_mod):
    n_col = 6 * D_MODEL // D_MODEL
    return pl.pallas_call(
        _mod_body,
        grid=(DEPTH, n_col),
        in_specs=[
            pl.BlockSpec((COND_ROWS, D_MODEL), lambda i, n: (0, 0)),
            pl.BlockSpec((1, D_MODEL, D_MODEL), lambda i, n: (i, 0, n)),
            pl.BlockSpec((1, 1, D_MODEL), lambda i, n: (i, 0, n)),
        ],
        out_specs=pl.BlockSpec((1, COND_ROWS, D_MODEL), lambda i, n: (i, 0, n)),
        out_shape=jax.ShapeDtypeStruct((DEPTH, COND_ROWS, 6 * D_MODEL), F32),
        compiler_params=_cparams("arbitrary", "arbitrary"),
        name="mod",
    )(cond, w_mod, b_mod.reshape(DEPTH, 1, 6 * D_MODEL))


def _addpos_body(x_ref, p_ref, o_ref):
    o_ref[...] = x_ref[...] + p_ref[...]


def _addpos_call(x, pos):
    t = x.shape[0]
    per_seq = pos.shape[0] // POST_TILE
    return pl.pallas_call(
        _addpos_body,
        grid=(t // POST_TILE,),
        in_specs=[
            pl.BlockSpec((POST_TILE, D_MODEL), lambda i: (i, 0)),
            pl.BlockSpec((POST_TILE, D_MODEL), lambda i: (i % per_seq, 0)),
        ],
        out_specs=pl.BlockSpec((POST_TILE, D_MODEL), lambda i: (i, 0)),
        out_shape=jax.ShapeDtypeStruct(x.shape, F32),
        compiler_params=_cparams("arbitrary"),
        name="addpos",
    )(x, pos)


def _log_f_and_key(z, log_lb, log_1m_lb):
    log_sig = jnp.minimum(z, 0.0) - jnp.log1p(jnp.exp(-jnp.abs(z)))
    b = log_1m_lb + log_sig
    log_f = jnp.maximum(log_lb, b) + jnp.log1p(jnp.exp(-jnp.abs(log_lb - b)))
    return log_f, jnp.exp(b - z)


def _scan_masks(reverse):
    c, sb = SCAN_CHUNK, SCAN_SUB
    t = lax.broadcasted_iota(jnp.int32, (c, c), 0)
    s = lax.broadcasted_iota(jnp.int32, (c, c), 1)
    if reverse:
        t, s = c - 1 - t, c - 1 - s
    pt, ps = t // sb, s // sb
    tri = (t >= s).astype(F32)
    m_diag = (pt == ps) & (t >= s)
    m_16 = (pt == ps + 1) & (pt % 2 == 1)
    m_32 = (pt >= 2) & (ps <= 1)
    return tri, m_diag, m_16, m_32


def _scan_chunk(q_scr, k_scr, lf_scr, v_scr, o_scr, r0, state_t, masks, reverse):
    c, sb = SCAN_CHUNK, SCAN_SUB
    nb = c // sb
    tri, m_diag, m_16, m_32 = masks
    rows = pl.ds(r0, c)
    b = jnp.dot(tri, lf_scr[rows, :], precision=HIGHEST, preferred_element_type=F32)
    q = q_scr[rows, :]
    k = k_scr[rows, :]
    v = v_scr[rows, :].astype(BF16)
    if reverse:
        order = list(range(nb - 1, -1, -1))
        mid = [b[i * sb + sb // 2:i * sb + sb // 2 + 1] for i in range(nb)]
        end = [b[i * sb:i * sb + 1] for i in range(nb)]
    else:
        order = list(range(nb))
        mid = [b[i * sb + sb // 2 - 1:i * sb + sb // 2] for i in range(nb)]
        end = [b[i * sb + sb - 1:i * sb + sb] for i in range(nb)]
    half_end = end[order[1]]
    edge = end[order[nb - 1]]
    parts = {name: [None] * nb for name in ("qd", "q16", "q32", "qs", "kd", "k16", "k32", "kl")}
    for p, i in enumerate(order):
        sl = slice(i * sb, (i + 1) * sb)
        qd = q[sl] * jnp.exp(b[sl] - mid[i])
        kd = k[sl] * jnp.exp(mid[i] - b[sl])
        parts["qd"][i] = qd
        parts["kd"][i] = kd
        parts["q16"][i] = qd * jnp.exp(mid[i] - end[order[p - 1]]) if p % 2 == 1 else qd
        parts["q32"][i] = qd * jnp.exp(mid[i] - half_end) if p >= 2 else qd
        parts["qs"][i] = qd * jnp.exp(mid[i])
        parts["k16"][i] = kd * jnp.exp(end[i] - mid[i]) if p % 2 == 0 else kd
        parts["k32"][i] = kd * jnp.exp(half_end - mid[i]) if p < 2 else kd
        parts["kl"][i] = kd * jnp.exp(edge - mid[i])
    f = {name: jnp.concatenate(blocks, axis=0).astype(BF16) for name, blocks in parts.items()}

    def qk(a, bb):
        return lax.dot_general(a, bb, (((1,), (1,)), ((), ())), preferred_element_type=F32)
    scores = (jnp.where(m_diag, qk(f["qd"], f["kd"]), 0.0) + jnp.where(m_16, qk(f["q16"], f["k16"]), 0.0)
              + jnp.where(m_32, qk(f["q32"], f["k32"]), 0.0))
    o = jnp.dot(scores.astype(BF16), v, preferred_element_type=F32)
    o_scr[rows, :] = o + qk(f["qs"], state_t.astype(BF16))
    upd_t = lax.dot_general(v, f["kl"], (((0,), (0,)), ((), ())), preferred_element_type=F32)
    return state_t * jnp.exp(edge) + upd_t


def _hgrn_body(*refs, n_seq, cps, has_s0):
    if has_s0:
        (x_ref, mod_ref, n1g_ref, wq_ref, wf_ref, wb_ref, wv_ref, wg_ref, llb_ref, l1m_ref, ng_ref,
         s0_ref, og_ref, h_scr, q_scr, v_scr, g_scr, lff_scr, kf_scr, lfb_scr, kb_scr,
         of_scr, ob_scr) = refs
        st_ref = None
    else:
        (x_ref, mod_ref, n1g_ref, wq_ref, wf_ref, wb_ref, wv_ref, wg_ref, llb_ref, l1m_ref, ng_ref,
         og_ref, st_ref, h_scr, q_scr, v_scr, g_scr, lff_scr, kf_scr, lfb_scr, kb_scr,
         of_scr, ob_scr) = refs
        s0_ref = None
    c = SCAN_CHUNK
    n_tiles = SEQ_BLOCK // ROW_TILE

    @pl.when(pl.program_id(1) == 0)
    def _():
        m = mod_ref[0]
        g = n1g_ref[...]

        def tile(t, carry):
            rows = pl.ds(pl.multiple_of(t * ROW_TILE, ROW_TILE), ROW_TILE)
            h_scr[rows, :] = _norm_mod(x_ref[rows, :], g, m[0:1], m[1:2]).astype(BF16)
            return carry
        lax.fori_loop(0, n_tiles, tile, 0)

    llb = llb_ref[...]
    l1m = l1m_ref[...]

    def proj_tile(t, carry):
        rows = pl.ds(pl.multiple_of(t * ROW_TILE, ROW_TILE), ROW_TILE)
        h = h_scr[rows, :]
        q_scr[rows, :] = jnp.dot(h, wq_ref[...], preferred_element_type=F32)
        v_scr[rows, :] = jnp.dot(h, wv_ref[...], preferred_element_type=F32)
        g_scr[rows, :] = jnp.dot(h, wg_ref[...], preferred_element_type=F32)
        zf = jnp.dot(h, wf_ref[...], preferred_element_type=F32)
        lf, k = _log_f_and_key(zf, llb[0:1], l1m[0:1])
        lff_scr[rows, :] = lf
        kf_scr[rows, :] = k
        zb = jnp.dot(h, wb_ref[...], preferred_element_type=F32)
        lf, k = _log_f_and_key(zb, llb[1:2], l1m[1:2])
        lfb_scr[rows, :] = lf
        kb_scr[rows, :] = k
        return carry
    lax.fori_loop(0, n_tiles, proj_tile, 0)

    fwd = functools.partial(_scan_chunk, q_scr, kf_scr, lff_scr, v_scr, of_scr,
                            masks=_scan_masks(False), reverse=False)
    bwd = functools.partial(_scan_chunk, q_scr, kb_scr, lfb_scr, v_scr, ob_scr,
                            masks=_scan_masks(True), reverse=True)

    if has_s0:
        def step(i, carry):
            sf, sb = carry
            sf = fwd(r0=pl.multiple_of(i * c, c), state_t=sf)
            sb = bwd(r0=pl.multiple_of((cps - 1 - i) * c, c), state_t=sb)
            return sf, sb
        lax.fori_loop(0, cps, step, (s0_ref[0, 0, 0].T, s0_ref[0, 1, 0].T))
    else:
        def seq(s, carry):
            base = s * (cps * c)
            sf = jnp.zeros((DV, DK), F32)
            sb = jnp.zeros((DV, DK), F32)
            for i in range(cps):
                sf = fwd(r0=pl.multiple_of(base + i * c, c), state_t=sf)
                sb = bwd(r0=pl.multiple_of(base + (cps - 1 - i) * c, c), state_t=sb)
            st_ref[s, 0, 0] = sf.T
            st_ref[s, 1, 0] = sb.T
            return carry
        lax.fori_loop(0, n_seq, seq, 0)

    ng = ng_ref[...]

    def out_tile(t, carry):
        rows = pl.ds(pl.multiple_of(t * ROW_TILE, ROW_TILE), ROW_TILE)
        o = of_scr[rows, :] + ob_scr[rows, :]
        o = o * lax.rsqrt(jnp.mean(o * o, axis=-1, keepdims=True) + NORM_EPS) * ng
        og_ref[rows, :] = (o * _silu(g_scr[rows, :])).astype(BF16)
        return carry
    lax.fori_loop(0, n_tiles, out_tile, 0)


def _hgrn_call(x, mod_i, n1g, w_in, log_lb, log_1m_lb, norm_g, s0, seq_len, mod_row):
    t = x.shape[0]
    n_blk = t // SEQ_BLOCK
    n_seq = SEQ_BLOCK // seq_len
    cps = seq_len // SCAN_CHUNK
    has_s0 = s0 is not None

    def wspec(section):
        return pl.BlockSpec((D_MODEL, DK), lambda b, h: (0, section * HEADS + h))

    in_specs = [
        pl.BlockSpec((SEQ_BLOCK, D_MODEL), lambda b, h: (b, 0)),
        pl.BlockSpec((1, 6, D_MODEL), lambda b, h: (mod_row(b), 0, 0)),
        pl.BlockSpec((1, D_MODEL), lambda b, h: (0, 0)),
        wspec(0), wspec(1), wspec(2), wspec(3), wspec(4),
        pl.BlockSpec((2, DK), lambda b, h: (0, h)),
        pl.BlockSpec((2, DK), lambda b, h: (0, h)),
        pl.BlockSpec((1, DV), lambda b, h: (0, h)),
    ]
    args = [x, mod_i, n1g, w_in, w_in, w_in, w_in, w_in, log_lb, log_1m_lb, norm_g]
    og_spec = pl.BlockSpec((SEQ_BLOCK, DV), lambda b, h: (b, h))
    og_shape = jax.ShapeDtypeStruct((t, D_MODEL), BF16)
    if has_s0:
        in_specs.append(pl.BlockSpec((1, 2, 1, DK, DV), lambda b, h: (b, 0, h, 0, 0)))
        args.append(s0)
        out_specs, out_shape = og_spec, og_shape
    else:
        out_specs = (og_spec, pl.BlockSpec((n_seq, 2, 1, DK, DV), lambda b, h: (b, 0, h, 0, 0)))
        out_shape = (og_shape, jax.ShapeDtypeStruct((n_blk * n_seq, 2, HEADS, DK, DV), F32))
    col = pltpu.VMEM((SEQ_BLOCK, DK), F32)
    res = pl.pallas_call(
        functools.partial(_hgrn_body, n_seq=n_seq, cps=cps, has_s0=has_s0),
        grid=(n_blk, HEADS),
        in_specs=in_specs,
        out_specs=out_specs,
        out_shape=out_shape,
        scratch_shapes=[pltpu.VMEM((SEQ_BLOCK, D_MODEL), BF16)] + [col] * 9,
        compiler_params=_cparams("arbitrary", "arbitrary"),
        name="hgrn_lat" if has_s0 else "hgrn_ctx",
    )(*args)
    return (res, None) if has_s0 else res


def _fourier_body(x_ref, mod_ref, n1g_ref, cl_ref, sl_ref, cc_ref, z_ref, h_scr, *, seq_len):
    rt = pl.program_id(1)
    n_tiles = SEQ_BLOCK // ROW_TILE

    @pl.when(rt == 0)
    def _():
        m = mod_ref[0]
        g = n1g_ref[...]

        def tile(t, carry):
            rows = pl.ds(pl.multiple_of(t * ROW_TILE, ROW_TILE), ROW_TILE)
            h_scr[rows, :] = _norm_mod(x_ref[rows, :], g, m[0:1], m[1:2]).astype(BF16)
            return carry
        lax.fori_loop(0, n_tiles, tile, 0)

    if seq_len == ROW_TILE:
        h = h_scr[pl.ds(pl.multiple_of(rt * ROW_TILE, ROW_TILE), ROW_TILE), :]
    else:
        h = h_scr[...]
    zc = jnp.dot(cl_ref[...], h, preferred_element_type=F32).astype(BF16)
    zs = jnp.dot(sl_ref[...], h, preferred_element_type=F32).astype(BF16)
    scale = 1.0 / math.sqrt(seq_len * FOURIER_CG)
    cg = FOURIER_CG
    for g in range(FOURIER_GROUPS):
        cat = jnp.concatenate([zc[:, g * cg:(g + 1) * cg], zs[:, g * cg:(g + 1) * cg]], axis=1)
        out = jnp.dot(cat, cc_ref[...], preferred_element_type=F32) * scale
        z_ref[:, g * cg:(g + 1) * cg] = out.astype(BF16)


def _dft_tables(n):
    j = jnp.arange(n, dtype=jnp.int32)
    ang = ((j[:, None] * j[None, :]) % n).astype(F32) * (2.0 * math.pi / n)
    return jnp.cos(ang), jnp.sin(ang)


def _fourier_call(x, mod_i, n1g, seq_len, mod_row):
    t = x.shape[0]
    n_blk = t // SEQ_BLOCK
    n_rt = SEQ_BLOCK // ROW_TILE
    cl, sl = _dft_tables(seq_len)
    cc, sc = _dft_tables(FOURIER_CG)
    cc2 = jnp.concatenate([cc, -sc], axis=0).astype(BF16)
    if seq_len == ROW_TILE:
        pos_spec = pl.BlockSpec((ROW_TILE, seq_len), lambda b, r: (0, 0))
    else:
        pos_spec = pl.BlockSpec((ROW_TILE, seq_len), lambda b, r: (r, 0))
    return pl.pallas_call(
        functools.partial(_fourier_body, seq_len=seq_len),
        grid=(n_blk, n_rt),
        in_specs=[
            pl.BlockSpec((SEQ_BLOCK, D_MODEL), lambda b, r: (b, 0)),
            pl.BlockSpec((1, 6, D_MODEL), lambda b, r: (mod_row(b), 0, 0)),
            pl.BlockSpec((1, D_MODEL), lambda b, r: (0, 0)),
            pos_spec, pos_spec,
            pl.BlockSpec((2 * FOURIER_CG, FOURIER_CG), lambda b, r: (0, 0)),
        ],
        out_specs=pl.BlockSpec((ROW_TILE, D_MODEL), lambda b, r: (b * n_rt + r, 0)),
        out_shape=jax.ShapeDtypeStruct((t, D_MODEL), BF16),
        scratch_shapes=[pltpu.VMEM((SEQ_BLOCK, D_MODEL), BF16)],
        compiler_params=_cparams("arbitrary", "arbitrary"),
        name="fourier",
    )(x, mod_i, n1g, cl.astype(BF16), sl.astype(BF16), cc2)


def _router_gates_t(lg):
    grp = [lg[g:g + 1] for g in range(MOE_GROUPS)]
    gmax = functools.reduce(jnp.maximum, grp)
    gi = jnp.where(grp[0] == gmax, 0, jnp.where(grp[1] == gmax, 1, jnp.where(grp[2] == gmax, 2, 3)))
    pgv = 1.0 / functools.reduce(lambda a, b: a + b, [jnp.exp(g - gmax) for g in grp])
    sel = []
    for j in range(MOE_EPG):
        rows = [lg[MOE_GROUPS + g * MOE_EPG + j:MOE_GROUPS + g * MOE_EPG + j + 1] for g in range(MOE_GROUPS)]
        sel.append(jnp.where(gi == 0, rows[0], jnp.where(gi == 1, rows[1], jnp.where(gi == 2, rows[2], rows[3]))))
    m1 = functools.reduce(jnp.maximum, sel)
    i1 = jnp.where(sel[0] == m1, 0, jnp.where(sel[1] == m1, 1, jnp.where(sel[2] == m1, 2, 3)))
    rest = [jnp.where(i1 == j, -jnp.inf, sel[j]) for j in range(MOE_EPG)]
    m2 = functools.reduce(jnp.maximum, rest)
    i2 = jnp.where(rest[0] == m2, 0, jnp.where(rest[1] == m2, 1, jnp.where(rest[2] == m2, 2, 3)))
    e2 = jnp.exp(m2 - m1)
    w1 = pgv / (1.0 + e2)
    w2 = pgv * e2 / (1.0 + e2)
    r = lg.shape[1]
    eid = lax.broadcasted_iota(jnp.int32, (LANES, r), 0)
    return (jnp.where(eid == gi * MOE_EPG + i1, w1, 0.0) + jnp.where(eid == gi * MOE_EPG + i2, w2, 0.0))


def _post_body(x_ref, mix_ref, mod_ref, w_ref, n2g_ref, wr_ref, br_ref, xo_ref, h2_ref, gt_ref):
    m = mod_ref[0]
    out = jnp.dot(mix_ref[...], w_ref[...], preferred_element_type=F32)
    xn = x_ref[...] + m[2:3] * out
    xo_ref[...] = xn
    h2 = _norm_mod(xn, n2g_ref[...], m[3:4], m[4:5])
    h2_ref[...] = h2.astype(BF16)
    lg = lax.dot_general(wr_ref[...], h2, (((1,), (1,)), ((), ())), precision=HIGHEST,
                         preferred_element_type=F32) + br_ref[...]
    gt_ref[...] = _router_gates_t(lg).T


def _post_call(x, mix, mod_i, w_out, n2g, wr_t, br, mod_row):
    t = x.shape[0]
    tile = lambda i: (i, 0)
    full = lambda i: (0, 0)
    return pl.pallas_call(
        _post_body,
        grid=(t // POST_TILE,),
        in_specs=[
            pl.BlockSpec((POST_TILE, D_MODEL), tile),
            pl.BlockSpec((POST_TILE, D_MODEL), tile),
            pl.BlockSpec((1, 6, D_MODEL), lambda i: (mod_row(i), 0, 0)),
            pl.BlockSpec((D_MODEL, D_MODEL), full),
            pl.BlockSpec((1, D_MODEL), full),
            pl.BlockSpec((ROUTER_ROWS, D_MODEL), full),
            pl.BlockSpec((ROUTER_ROWS, 1), full),
        ],
        out_specs=(
            pl.BlockSpec((POST_TILE, D_MODEL), tile),
            pl.BlockSpec((POST_TILE, D_MODEL), tile),
            pl.BlockSpec((POST_TILE, LANES), tile),
        ),
        out_shape=(
            jax.ShapeDtypeStruct((t, D_MODEL), F32),
            jax.ShapeDtypeStruct((t, D_MODEL), BF16),
            jax.ShapeDtypeStruct((t, LANES), F32),
        ),
        compiler_params=_cparams("arbitrary"),
        name="post",
    )(x, mix, mod_i, w_out, n2g, wr_t, br)


def _moe_body(*refs, final):
    if final:
        x_ref, h2_ref, gt_ref, mod_ref, wg_ref, wu_ref, wd_ref, fg_ref, o_ref, acc_scr = refs
    else:
        x_ref, h2_ref, gt_ref, mod_ref, wg_ref, wu_ref, wd_ref, o_ref, acc_scr = refs
        fg_ref = None
    e = pl.program_id(1)

    @pl.when(e == 0)
    def _():
        acc_scr[...] = jnp.zeros_like(acc_scr)

    wg = wg_ref[0].astype(BF16)
    wu = wu_ref[0].astype(BF16)
    wd = wd_ref[0].astype(BF16)
    n_tiles = MOE_BLOCK // ROW_TILE

    def tile(t, carry):
        rows = pl.ds(pl.multiple_of(t * ROW_TILE, ROW_TILE), ROW_TILE)
        h = h2_ref[rows, :]
        a = jnp.dot(h, wg, preferred_element_type=F32)
        u = jnp.dot(h, wu, preferred_element_type=F32)
        gates = gt_ref[rows, :]
        lane = lax.broadcasted_iota(jnp.int32, gates.shape, 1)
        gate = jnp.sum(jnp.where(lane == e, gates, 0.0), axis=1, keepdims=True)
        hid = (_silu(a) * u * gate).astype(BF16)
        acc_scr[rows, :] += jnp.dot(hid, wd, preferred_element_type=F32)
        return carry
    lax.fori_loop(0, n_tiles, tile, 0)

    @pl.when(e == MOE_EXPERTS - 1)
    def _():
        m = mod_ref[0]
        y = x_ref[...] + m[5:6] * acc_scr[...]
        if final:
            ms = jnp.mean(y * y, axis=-1, keepdims=True)
            y = y * lax.rsqrt(ms + NORM_EPS) * fg_ref[...]
        o_ref[...] = y


def _moe_call(x, h2, gates, mod_i, w_gate, w_up, w_down, final_g, mod_row):
    t = x.shape[0]
    final = final_g is not None
    blk = lambda b, e: (b, 0)
    in_specs = [
        pl.BlockSpec((MOE_BLOCK, D_MODEL), blk),
        pl.BlockSpec((MOE_BLOCK, D_MODEL), blk),
        pl.BlockSpec((MOE_BLOCK, LANES), blk),
        pl.BlockSpec((1, 6, D_MODEL), lambda b, e: (mod_row(b), 0, 0)),
        pl.BlockSpec((1, D_MODEL, MOE_D_FF), lambda b, e: (e, 0, 0)),
        pl.BlockSpec((1, D_MODEL, MOE_D_FF), lambda b, e: (e, 0, 0)),
        pl.BlockSpec((1, MOE_D_FF, D_MODEL), lambda b, e: (e, 0, 0)),
    ]
    args = [x, h2, gates, mod_i, w_gate, w_up, w_down]
    if final:
        in_specs.append(pl.BlockSpec((1, D_MODEL), lambda b, e: (0, 0)))
        args.append(final_g)
    return pl.pallas_call(
        functools.partial(_moe_body, final=final),
        grid=(t // MOE_BLOCK, MOE_EXPERTS),
        in_specs=in_specs,
        out_specs=pl.BlockSpec((MOE_BLOCK, D_MODEL), blk),
        out_shape=jax.ShapeDtypeStruct((t, D_MODEL), F32),
        scratch_shapes=[pltpu.VMEM((MOE_BLOCK, D_MODEL), F32)],
        compiler_params=_cparams("arbitrary", "arbitrary"),
        name="moe",
    )(*args)


def _grid_pos_embed(n_tokens):
    t = jnp.arange(n_tokens)
    r = (t // GRID_W).astype(F32)
    col = (t % GRID_W).astype(F32)
    quarter = D_MODEL // 4
    omega = 1.0 / (POS_BASE ** (jnp.arange(quarter, dtype=F32) / quarter))
    ar = r[:, None] * omega[None, :]
    ac = col[:, None] * omega[None, :]
    return jnp.concatenate([jnp.sin(ar), jnp.cos(ar), jnp.sin(ac), jnp.cos(ac)], axis=-1)


def _run_trunk(x, seq_len, latent, init_states, mod_all, p):
    def rows_per(block):
        if latent:
            per = seq_len // block
            return lambda i: i // per
        return lambda i: CTX_ROW

    states = []
    for i in range(DEPTH):
        j = i // 2
        mod_i = mod_all[i]
        n1g = p["norm1_g"][i:i + 1]
        if i % 2 == 0:
            s0 = init_states[:, j] if latent else None
            mix, st = _hgrn_call(x, mod_i, n1g, p["hgrn_w_in"][j], p["log_lb"][j], p["log_1m_lb"][j],
                                 p["hgrn_norm_g"][j:j + 1], s0, seq_len, rows_per(SEQ_BLOCK))
            states.append(st)
            w_out = p["hgrn_w_out"][j]
        else:
            mix = _fourier_call(x, mod_i, n1g, seq_len, rows_per(SEQ_BLOCK))
            w_out = p["fourier_w_out"][j]
        x, h2, gates = _post_call(x, mix, mod_i, w_out, p["norm2_g"][i:i + 1], p["router_w_t"][i],
                                  p["router_b"][i], rows_per(POST_TILE))
        final_g = p["final_norm_g"] if i == DEPTH - 1 else None
        x = _moe_call(x, h2, gates, mod_i, p["moe_w_gate"][i], p["moe_w_up"][i], p["moe_w_down"][i],
                      final_g, rows_per(MOE_BLOCK))
    return x, states


def kernel(x_prompt, x_sample, state_hgrn, c, c_ctx, w_mod, b_mod, norm1_g, norm2_g, hgrn_w_in,
           hgrn_lb_logits, hgrn_norm_g, hgrn_w_out, fourier_w_out, router_group_w, router_group_b,
           router_expert_w, router_expert_b, moe_w_gate, moe_w_up, moe_w_down, final_norm_g):
    batch, seq, _ = x_prompt.shape
    dec_batch, dec_seq, _ = x_sample.shape
    assert dec_batch <= CTX_ROW and seq == ROW_TILE and dec_seq == SEQ_BLOCK

    cond = jnp.zeros((COND_ROWS, D_MODEL), F32).at[:dec_batch].set(c).at[CTX_ROW].set(c_ctx)
    mod_all = _mod_call(cond, w_mod, b_mod).reshape(DEPTH, COND_ROWS, 6, D_MODEL)

    probs = jax.nn.softmax(hgrn_lb_logits.astype(F32), axis=0)
    cs = jnp.cumsum(probs, axis=0)
    lb = cs - cs[0:1]
    router_w = jnp.concatenate([router_group_w, router_expert_w], axis=-1)
    router_b = jnp.concatenate([router_group_b, router_expert_b], axis=-1)
    pad = ROUTER_ROWS - router_w.shape[-1]
    p = {
        "norm1_g": norm1_g, "norm2_g": norm2_g, "hgrn_norm_g": hgrn_norm_g,
        "hgrn_w_in": hgrn_w_in.astype(BF16), "hgrn_w_out": hgrn_w_out.astype(BF16),
        "fourier_w_out": fourier_w_out.astype(BF16),
        "log_lb": jnp.log(lb), "log_1m_lb": jnp.log1p(-lb),
        "router_w_t": jnp.pad(jnp.swapaxes(router_w, 1, 2), ((0, 0), (0, pad), (0, 0))),
        "router_b": jnp.pad(router_b, ((0, 0), (0, pad)))[..., None],
        "moe_w_gate": moe_w_gate, "moe_w_up": moe_w_up, "moe_w_down": moe_w_down,
        "final_norm_g": final_norm_g.reshape(1, D_MODEL),
    }

    y_prompt, ctx_states = _run_trunk(x_prompt.reshape(batch * seq, D_MODEL), seq, False, None, mod_all, p)
    new_state = jnp.stack(ctx_states, axis=1)

    x_lat = _addpos_call(x_sample.reshape(dec_batch * dec_seq, D_MODEL), _grid_pos_embed(dec_seq))
    y_sample, _ = _run_trunk(x_lat, dec_seq, True, state_hgrn, mod_all, p)
    return (y_prompt.reshape(batch, seq, D_MODEL), y_sample.reshape(dec_batch, dec_seq, D_MODEL), new_state)
```

```python
import functools
import math

import jax
import jax.numpy as jnp
from jax import lax
from jax.experimental import pallas as pl
from jax.experimental.pallas import tpu as pltpu

F32 = jnp.float32
BF16 = jnp.bfloat16
HIGHEST = lax.Precision.HIGHEST

D_MODEL = 1024
DEPTH = 4
GRID_W = 64
HEADS = 8
DK = 128
DV = 128
FOURIER_GROUPS = 4
FOURIER_CG = D_MODEL // FOURIER_GROUPS
MOE_GROUPS = 4
MOE_EPG = 4
MOE_EXPERTS = 16
MOE_D_FF = 512
NORM_EPS = 1e-6
POS_BASE = 10000.0
LOG2_E = 1.0 / math.log(2.0)

COND_ROWS = 8
CTX_ROW = 4
ROUTER_ROWS = 32
LANES = 128
SCAN_CHUNK = 64
SCAN_SUB = 16
GROUP = 4
GROUP_ROWS = GROUP * SCAN_CHUNK
LOCAL_GROUPS = 2
SEQ_BLOCK = 2048
ROW_TILE = 256
POST_TILE = 512
MOE_BLOCK = 1024
VMEM_LIMIT = 56 * 1024 * 1024


def _cparams(*sem):
    return pltpu.CompilerParams(dimension_semantics=sem, vmem_limit_bytes=VMEM_LIMIT)


def _silu(x):
    return x * jax.nn.sigmoid(x)


def _norm_mod(x, g, shift, scale):
    ms = jnp.mean(x * x, axis=-1, keepdims=True)
    return (x * lax.rsqrt(ms + NORM_EPS) * g) * (1.0 + scale) + shift


def _mod_body(c_ref, w_ref, b_ref, o_ref):
    s = _silu(c_ref[...])
    o_ref[0] = jnp.dot(s, w_ref[0], precision=HIGHEST, preferred_element_type=F32) + b_ref[0]


def _mod_call(cond, w_mod, b_mod):
    n_col = 6 * D_MODEL // D_MODEL
    return pl.pallas_call(
        _mod_body,
        grid=(DEPTH, n_col),
        in_specs=[
            pl.BlockSpec((COND_ROWS, D_MODEL), lambda i, n: (0, 0)),
            pl.BlockSpec((1, D_MODEL, D_MODEL), lambda i, n: (i, 0, n)),
            pl.BlockSpec((1, 1, D_MODEL), lambda i, n: (i, 0, n)),
        ],
        out_specs=pl.BlockSpec((1, COND_ROWS, D_MODEL), lambda i, n: (i, 0, n)),
        out_shape=jax.ShapeDtypeStruct((DEPTH, COND_ROWS, 6 * D_MODEL), F32),
        compiler_params=_cparams("arbitrary", "arbitrary"),
        name="mod",
    )(cond, w_mod, b_mod.reshape(DEPTH, 1, 6 * D_MODEL))


def _addpos_body(x_ref, p_ref, o_ref):
    o_ref[...] = x_ref[...] + p_ref[...]


def _addpos_call(x, pos):
    t = x.shape[0]
    per_seq = pos.shape[0] // POST_TILE
    return pl.pallas_call(
        _addpos_body,
        grid=(t // POST_TILE,),
        in_specs=[
            pl.BlockSpec((POST_TILE, D_MODEL), lambda i: (i, 0)),
            pl.BlockSpec((POST_TILE, D_MODEL), lambda i: (i % per_seq, 0)),
        ],
        out_specs=pl.BlockSpec((POST_TILE, D_MODEL), lambda i: (i, 0)),
        out_shape=jax.ShapeDtypeStruct(x.shape, F32),
        compiler_params=_cparams("arbitrary"),
        name="addpos",
    )(x, pos)


def _log_f_and_key(z, log_lb, log_1m_lb):
    log_sig = jnp.minimum(z, 0.0) - jnp.log1p(jnp.exp(-jnp.abs(z)))
    b = log_1m_lb + log_sig
    log_f = jnp.maximum(log_lb, b) + jnp.log1p(jnp.exp(-jnp.abs(log_lb - b)))
    return log_f, jnp.exp(b - z)


def _group_masks():
    n = GROUP_ROWS
    t = lax.broadcasted_iota(jnp.int32, (n, n), 0)
    s = lax.broadcasted_iota(jnp.int32, (n, n), 1)
    same_chunk = (t // SCAN_CHUNK) == (s // SCAN_CHUNK)
    same_sub = (t // SCAN_SUB) == (s // SCAN_SUB)
    prefix = (same_chunk & (t >= s)).astype(BF16)
    return prefix, same_chunk, same_sub & (t >= s), same_sub & (t <= s)


def _qk(a, b):
    return lax.dot_general(a, b, (((1,), (1,)), ((), ())), preferred_element_type=F32)


def _split2(x):
    hi = x.astype(BF16)
    return [hi, (x - hi.astype(F32)).astype(BF16)]


def _sum2(s, i):
    o = 2 * i * DK
    return s[:, o:o + DK] + s[:, o + DK:o + 2 * DK]


def _group_dir(b, q, k, v, vt, m_chunk, m_diag, reverse):
    c, sb = SCAN_CHUNK, SCAN_SUB
    nb = c // sb
    zero = jnp.zeros((sb, DK), BF16)
    qd, kd, qs, kl, decs = [], [], [], [], []
    q_lev = [[] for _ in range(nb - 1)]
    k_lev = [[] for _ in range(nb - 1)]
    for g in range(GROUP):
        starts = [g * c + i * sb for i in range(nb)]
        pos = [nb - 1 - i for i in range(nb)] if reverse else list(range(nb))
        end_at = {}
        for i, r in enumerate(starts):
            e = r if reverse else r + sb - 1
            end_at[pos[i]] = b[e:e + 1]
        edge = end_at[nb - 1]
        decs.append(jnp.exp2(edge))
        for i, r in enumerate(starts):
            m = r + sb // 2 if reverse else r + sb // 2 - 1
            bi, qi, ki, mid = b[r:r + sb], q[r:r + sb], k[r:r + sb], b[m:m + 1]
            qd.append((qi * jnp.exp2(bi - mid)).astype(BF16))
            kd.append((ki * jnp.exp2(mid - bi)).astype(BF16))
            for j in range(nb - 1):
                q_lev[j].append((qi * jnp.exp2(bi - end_at[j])).astype(BF16) if pos[i] > j else zero)
                k_lev[j].append((ki * jnp.exp2(end_at[j] - bi)).astype(BF16) if pos[i] == j else zero)
            qs.append((qi * jnp.exp2(bi)).astype(BF16))
            kl.append((ki * jnp.exp2(edge - bi)).astype(BF16))

    def rows(blocks):
        return jnp.concatenate(blocks, axis=0)
    s_diag = _qk(rows(qd), rows(kd))
    s_lev = _qk(jnp.concatenate([rows(x) for x in q_lev], axis=1),
                jnp.concatenate([rows(x) for x in k_lev], axis=1))
    scores = jnp.where(m_diag, s_diag, 0.0) + jnp.where(m_chunk, s_lev, 0.0)
    o = jnp.dot(scores.astype(BF16), v, preferred_element_type=F32)
    kl_wide = jnp.concatenate(
        [rows([blk if n // nb == g else zero for n, blk in enumerate(kl)]) for g in range(GROUP)], axis=1)
    upd_t = jnp.dot(vt, kl_wide, preferred_element_type=F32)
    return o, rows(qs), upd_t, decs


def _hgrn_body(*refs, n_in, n_seq, cps, has_s0):
    x_ref, mod_ref, n1g_ref, w_ref, llb_ref, l1m_ref, ng_ref = refs[:7]
    s0_ref = refs[7] if has_s0 else None
    og_ref = refs[n_in]
    st_ref = None if has_s0 else refs[n_in + 1]
    (h_scr, q_scr, v_scr, g_scr, lff_scr, kf_scr, lfb_scr, kb_scr, of_scr, ob_scr,
     qsf_scr, qsb_scr, uf_scr, ub_scr, decf_scr, decb_scr, sf_scr, sb_scr) = refs[n_in + (1 if has_s0 else 2):]
    c = SCAN_CHUNK
    n_chunks = SEQ_BLOCK // c
    n_tiles = SEQ_BLOCK // ROW_TILE

    @pl.when(pl.program_id(1) == 0)
    def _():
        m = mod_ref[0]
        g = n1g_ref[...]

        def tile(t, carry):
            rows = pl.ds(pl.multiple_of(t * ROW_TILE, ROW_TILE), ROW_TILE)
            h_scr[rows, :] = _norm_mod(x_ref[rows, :], g, m[0:1], m[1:2]).astype(BF16)
            return carry
        lax.fori_loop(0, n_tiles, tile, 0)

    llb = llb_ref[...]
    l1m = l1m_ref[...]

    def proj_tiles(g, carry):
        tiles = []
        for u in range(2):
            rows = pl.ds(pl.multiple_of((2 * g + u) * ROW_TILE, ROW_TILE), ROW_TILE)
            tiles.append((rows, h_scr[rows, :]))
        outs = []
        for rows, h in tiles:
            p = jnp.dot(h, w_ref[...], preferred_element_type=F32)
            outs.append((rows, p, _log_f_and_key(p[:, DK:2 * DK], llb[0:1], l1m[0:1]),
                         _log_f_and_key(p[:, 2 * DK:3 * DK], llb[1:2], l1m[1:2])))
        for rows, p, (lf_f, k_f), (lf_b, k_b) in outs:
            q_scr[rows, :] = p[:, 0:DK]
            v_scr[rows, :] = p[:, 3 * DK:3 * DK + DV]
            g_scr[rows, :] = p[:, 3 * DK + DV:3 * DK + 2 * DV]
            lff_scr[rows, :] = lf_f
            kf_scr[rows, :] = k_f
            lfb_scr[rows, :] = lf_b
            kb_scr[rows, :] = k_b
        return carry
    lax.fori_loop(0, n_tiles // 2, proj_tiles, 0)

    prefix, m_chunk, m_diag_f, m_diag_b = _group_masks()

    def group_rows(grp):
        return pl.ds(pl.multiple_of(grp * GROUP_ROWS, GROUP_ROWS), GROUP_ROWS)

    def local_load(grp):
        rows = group_rows(grp)
        return (q_scr[rows, :], v_scr[rows, :], lff_scr[rows, :], kf_scr[rows, :], lfb_scr[rows, :],
                kb_scr[rows, :])

    def local_compute(q, v32, lf_f, k_f, lf_b, k_b):
        v = v32.astype(BF16)
        vt = v32.T.astype(BF16)
        sums = jnp.dot(prefix, jnp.concatenate(_split2(lf_f) + _split2(lf_b), axis=1),
                       preferred_element_type=F32)
        b_f = _sum2(sums, 0) * LOG2_E
        pre_b = _sum2(sums, 1)
        total_b = jnp.concatenate(
            [jnp.broadcast_to(pre_b[g * c + c - 1:g * c + c], (c, DK)) for g in range(GROUP)], axis=0)
        b_b = ((total_b - pre_b) + lf_b) * LOG2_E
        return (_group_dir(b_f, q, k_f, v, vt, m_chunk, m_diag_f, reverse=False),
                _group_dir(b_b, q, k_b, v, vt, m_chunk, m_diag_b, reverse=True))

    def local_store(grp, fwd, bwd):
        rows = group_rows(grp)
        for (o, qs, upd_t, dec), o_scr, qs_scr, u_scr, dec_scr in (
                (fwd, of_scr, qsf_scr, uf_scr, decf_scr), (bwd, ob_scr, qsb_scr, ub_scr, decb_scr)):
            o_scr[rows, :] = o
            qs_scr[rows, :] = qs
            for g in range(GROUP):
                u_scr[grp * GROUP + g] = upd_t[:, g * DK:(g + 1) * DK]
                dec_scr[grp * GROUP + g] = dec[g]

    def local(it, carry):
        groups = [it * LOCAL_GROUPS + u for u in range(LOCAL_GROUPS)]
        loaded = [local_load(grp) for grp in groups]
        results = [local_compute(*vals) for vals in loaded]
        for grp, (fwd, bwd) in zip(groups, results):
            local_store(grp, fwd, bwd)
        return carry
    lax.fori_loop(0, n_chunks // GROUP // LOCAL_GROUPS, local, 0)

    def advance(ci, cj, sf, sb):
        sf_scr[ci] = sf.astype(BF16)
        sb_scr[cj] = sb.astype(BF16)
        return sf * decf_scr[ci] + uf_scr[ci], sb * decb_scr[cj] + ub_scr[cj]

    if has_s0:
        def step(i, carry):
            return advance(i, n_chunks - 1 - i, *carry)
        lax.fori_loop(0, n_chunks, step, (s0_ref[0].T, s0_ref[1].T), unroll=4)
    else:
        def seq(s, carry):
            sf = jnp.zeros((DV, DK), F32)
            sb = jnp.zeros((DV, DK), F32)
            for i in range(cps):
                sf, sb = advance(s * cps + i, s * cps + cps - 1 - i, sf, sb)
            st_ref[s, 0] = sf.T
            st_ref[s, 1] = sb.T
            return carry
        lax.fori_loop(0, n_seq, seq, 0)

    ng = ng_ref[...]
    zero_chunk = jnp.zeros((c, DK), BF16)

    def widen(qs):
        return jnp.concatenate(
            [jnp.concatenate([qs[h * c:(h + 1) * c] if h == g else zero_chunk for h in range(GROUP)], axis=0)
             for g in range(GROUP)], axis=1)

    def out_group(grp, carry):
        rows = group_rows(grp)
        qs = jnp.concatenate([widen(qsf_scr[rows, :]), widen(qsb_scr[rows, :])], axis=1)
        st = jnp.concatenate([sf_scr[grp * GROUP + g] for g in range(GROUP)]
                             + [sb_scr[grp * GROUP + g] for g in range(GROUP)], axis=1)
        o = of_scr[rows, :] + ob_scr[rows, :] + _qk(qs, st)
        o = o * lax.rsqrt(jnp.mean(o * o, axis=-1, keepdims=True) + NORM_EPS) * ng
        og_ref[rows, :] = (o * _silu(g_scr[rows, :])).astype(BF16)
        return carry
    lax.fori_loop(0, n_chunks // GROUP, out_group, 0)


def _hgrn_call(x, mod_i, n1g, w_in, log_lb, log_1m_lb, norm_g, j, s0, states, seq_len, mod_row):
    t = x.shape[0]
    n_blk = t // SEQ_BLOCK
    n_seq = SEQ_BLOCK // seq_len
    cps = seq_len // SCAN_CHUNK
    n_chunks = SEQ_BLOCK // SCAN_CHUNK
    has_s0 = s0 is not None
    n_layers = w_in.shape[0]
    wide = 3 * DK + 2 * DV

    in_specs = [
        pl.BlockSpec((SEQ_BLOCK, D_MODEL), lambda b, h: (b, 0)),
        pl.BlockSpec((1, 6, D_MODEL), lambda b, h: (mod_row(b), 0, 0)),
        pl.BlockSpec((1, D_MODEL), lambda b, h: (0, 0)),
        pl.BlockSpec((None, D_MODEL, wide), lambda b, h: (j, 0, h)),
        pl.BlockSpec((2, DK), lambda b, h: (0, h)),
        pl.BlockSpec((2, DK), lambda b, h: (0, h)),
        pl.BlockSpec((1, DV), lambda b, h: (0, h)),
    ]
    args = [x, mod_i, n1g, w_in, log_lb, log_1m_lb, norm_g]
    og_spec = pl.BlockSpec((SEQ_BLOCK, DV), lambda b, h: (b, h))
    og_shape = jax.ShapeDtypeStruct((t, D_MODEL), BF16)
    aliases = {}
    if has_s0:
        in_specs.append(pl.BlockSpec((None, None, 2, None, DK, DV), lambda b, h: (b, j, 0, h, 0, 0)))
        args.append(s0)
        out_specs, out_shape = og_spec, og_shape
    else:
        st_shape = (n_blk * n_seq, n_layers, 2, HEADS, DK, DV)
        out_specs = (og_spec,
                     pl.BlockSpec((n_seq, None, 2, None, DK, DV), lambda b, h: (b, j, 0, h, 0, 0)))
        out_shape = (og_shape, jax.ShapeDtypeStruct(st_shape, F32))
        if states is not None:
            in_specs.append(pl.BlockSpec(memory_space=pl.ANY))
            args.append(states)
            aliases = {len(args) - 1: 1}
    col = pltpu.VMEM((SEQ_BLOCK, DK), F32)
    colb = pltpu.VMEM((SEQ_BLOCK, DK), BF16)
    upd = pltpu.VMEM((n_chunks, DV, DK), F32)
    dec = pltpu.VMEM((n_chunks, 1, DK), F32)
    start = pltpu.VMEM((n_chunks, DV, DK), BF16)
    res = pl.pallas_call(
        functools.partial(_hgrn_body, n_in=len(args), n_seq=n_seq, cps=cps, has_s0=has_s0),
        grid=(n_blk, HEADS),
        in_specs=in_specs,
        out_specs=out_specs,
        out_shape=out_shape,
        scratch_shapes=[pltpu.VMEM((SEQ_BLOCK, D_MODEL), BF16)] + [col] * 9 + [colb, colb, upd, upd, dec, dec,
                                                                                  start, start],
        input_output_aliases=aliases,
        compiler_params=_cparams("arbitrary", "arbitrary"),
        name="hgrn_lat" if has_s0 else "hgrn_ctx",
    )(*args)
    return (res, None) if has_s0 else res


def _fourier_body(x_ref, mod_ref, n1g_ref, cl_ref, sl_ref, cc_ref, z_ref, h_scr, *, seq_len):
    rt = pl.program_id(1)
    n_tiles = SEQ_BLOCK // ROW_TILE

    @pl.when(rt == 0)
    def _():
        m = mod_ref[0]
        g = n1g_ref[...]

        def tile(t, carry):
            rows = pl.ds(pl.multiple_of(t * ROW_TILE, ROW_TILE), ROW_TILE)
            h_scr[rows, :] = _norm_mod(x_ref[rows, :], g, m[0:1], m[1:2]).astype(BF16)
            return carry
        lax.fori_loop(0, n_tiles, tile, 0)

    if seq_len == ROW_TILE:
        h = h_scr[pl.ds(pl.multiple_of(rt * ROW_TILE, ROW_TILE), ROW_TILE), :]
    else:
        h = h_scr[...]
    zc = jnp.dot(cl_ref[...], h, preferred_element_type=F32).astype(BF16)
    zs = jnp.dot(sl_ref[...], h, preferred_element_type=F32).astype(BF16)
    scale = 1.0 / math.sqrt(seq_len * FOURIER_CG)
    cg = FOURIER_CG
    for g in range(FOURIER_GROUPS):
        cat = jnp.concatenate([zc[:, g * cg:(g + 1) * cg], zs[:, g * cg:(g + 1) * cg]], axis=1)
        out = jnp.dot(cat, cc_ref[...], preferred_element_type=F32) * scale
        z_ref[:, g * cg:(g + 1) * cg] = out.astype(BF16)


def _dft_tables(n):
    j = jnp.arange(n, dtype=jnp.int32)
    ang = ((j[:, None] * j[None, :]) % n).astype(F32) * (2.0 * math.pi / n)
    return jnp.cos(ang), jnp.sin(ang)


def _fourier_call(x, mod_i, n1g, seq_len, mod_row):
    t = x.shape[0]
    n_blk = t // SEQ_BLOCK
    n_rt = SEQ_BLOCK // ROW_TILE
    cl, sl = _dft_tables(seq_len)
    cc, sc = _dft_tables(FOURIER_CG)
    cc2 = jnp.concatenate([cc, -sc], axis=0).astype(BF16)
    if seq_len == ROW_TILE:
        pos_spec = pl.BlockSpec((ROW_TILE, seq_len), lambda b, r: (0, 0))
    else:
        pos_spec = pl.BlockSpec((ROW_TILE, seq_len), lambda b, r: (r, 0))
    return pl.pallas_call(
        functools.partial(_fourier_body, seq_len=seq_len),
        grid=(n_blk, n_rt),
        in_specs=[
            pl.BlockSpec((SEQ_BLOCK, D_MODEL), lambda b, r: (b, 0)),
            pl.BlockSpec((1, 6, D_MODEL), lambda b, r: (mod_row(b), 0, 0)),
            pl.BlockSpec((1, D_MODEL), lambda b, r: (0, 0)),
            pos_spec, pos_spec,
            pl.BlockSpec((2 * FOURIER_CG, FOURIER_CG), lambda b, r: (0, 0)),
        ],
        out_specs=pl.BlockSpec((ROW_TILE, D_MODEL), lambda b, r: (b * n_rt + r, 0)),
        out_shape=jax.ShapeDtypeStruct((t, D_MODEL), BF16),
        scratch_shapes=[pltpu.VMEM((SEQ_BLOCK, D_MODEL), BF16)],
        compiler_params=_cparams("arbitrary", "arbitrary"),
        name="fourier",
    )(x, mod_i, n1g, cl.astype(BF16), sl.astype(BF16), cc2)


def _router_gates_t(lg):
    grp = [lg[g:g + 1] for g in range(MOE_GROUPS)]
    gmax = functools.reduce(jnp.maximum, grp)
    gi = jnp.where(grp[0] == gmax, 0, jnp.where(grp[1] == gmax, 1, jnp.where(grp[2] == gmax, 2, 3)))
    pgv = 1.0 / functools.reduce(lambda a, b: a + b, [jnp.exp(g - gmax) for g in grp])
    sel = []
    for j in range(MOE_EPG):
        rows = [lg[MOE_GROUPS + g * MOE_EPG + j:MOE_GROUPS + g * MOE_EPG + j + 1] for g in range(MOE_GROUPS)]
        sel.append(jnp.where(gi == 0, rows[0], jnp.where(gi == 1, rows[1], jnp.where(gi == 2, rows[2], rows[3]))))
    m1 = functools.reduce(jnp.maximum, sel)
    i1 = jnp.where(sel[0] == m1, 0, jnp.where(sel[1] == m1, 1, jnp.where(sel[2] == m1, 2, 3)))
    rest = [jnp.where(i1 == j, -jnp.inf, sel[j]) for j in range(MOE_EPG)]
    m2 = functools.reduce(jnp.maximum, rest)
    i2 = jnp.where(rest[0] == m2, 0, jnp.where(rest[1] == m2, 1, jnp.where(rest[2] == m2, 2, 3)))
    e2 = jnp.exp(m2 - m1)
    w1 = pgv / (1.0 + e2)
    w2 = pgv * e2 / (1.0 + e2)
    r = lg.shape[1]
    eid = lax.broadcasted_iota(jnp.int32, (LANES, r), 0)
    return (jnp.where(eid == gi * MOE_EPG + i1, w1, 0.0) + jnp.where(eid == gi * MOE_EPG + i2, w2, 0.0))


def _post_body(x_ref, mix_ref, mod_ref, w_ref, n2g_ref, wr_ref, br_ref, xo_ref, h2_ref, gt_ref):
    m = mod_ref[0]
    out = jnp.dot(mix_ref[...], w_ref[...], preferred_element_type=F32)
    xn = x_ref[...] + m[2:3] * out
    xo_ref[...] = xn
    h2 = _norm_mod(xn, n2g_ref[...], m[3:4], m[4:5])
    h2_ref[...] = h2.astype(BF16)
    lg = lax.dot_general(wr_ref[...], h2, (((1,), (1,)), ((), ())), precision=HIGHEST,
                         preferred_element_type=F32) + br_ref[...]
    gt_ref[...] = _router_gates_t(lg).T


def _post_call(x, mix, mod_i, w_out, j, n2g, wr_t, br, mod_row):
    t = x.shape[0]
    tile = lambda i: (i, 0)
    full = lambda i: (0, 0)
    return pl.pallas_call(
        _post_body,
        grid=(t // POST_TILE,),
        in_specs=[
            pl.BlockSpec((POST_TILE, D_MODEL), tile),
            pl.BlockSpec((POST_TILE, D_MODEL), tile),
            pl.BlockSpec((1, 6, D_MODEL), lambda i: (mod_row(i), 0, 0)),
            pl.BlockSpec((None, D_MODEL, D_MODEL), lambda i: (j, 0, 0)),
            pl.BlockSpec((1, D_MODEL), full),
            pl.BlockSpec((ROUTER_ROWS, D_MODEL), full),
            pl.BlockSpec((ROUTER_ROWS, 1), full),
        ],
        out_specs=(
            pl.BlockSpec((POST_TILE, D_MODEL), tile),
            pl.BlockSpec((POST_TILE, D_MODEL), tile),
            pl.BlockSpec((POST_TILE, LANES), tile),
        ),
        out_shape=(
            jax.ShapeDtypeStruct((t, D_MODEL), F32),
            jax.ShapeDtypeStruct((t, D_MODEL), BF16),
            jax.ShapeDtypeStruct((t, LANES), F32),
        ),
        compiler_params=_cparams("arbitrary"),
        name="post",
    )(x, mix, mod_i, w_out, n2g, wr_t, br)


def _moe_body(*refs, final):
    if final:
        x_ref, h2_ref, gt_ref, mod_ref, wg_ref, wu_ref, wd_ref, fg_ref, o_ref, acc_scr = refs
    else:
        x_ref, h2_ref, gt_ref, mod_ref, wg_ref, wu_ref, wd_ref, o_ref, acc_scr = refs
        fg_ref = None
    e = pl.program_id(1)

    @pl.when(e == 0)
    def _():
        acc_scr[...] = jnp.zeros_like(acc_scr)

    wg = wg_ref[0].astype(BF16)
    wu = wu_ref[0].astype(BF16)
    wd = wd_ref[0].astype(BF16)
    n_tiles = MOE_BLOCK // ROW_TILE

    def tile(t, carry):
        rows = pl.ds(pl.multiple_of(t * ROW_TILE, ROW_TILE), ROW_TILE)
        h = h2_ref[rows, :]
        a = jnp.dot(h, wg, preferred_element_type=F32)
        u = jnp.dot(h, wu, preferred_element_type=F32)
        gates = gt_ref[rows, :]
        lane = lax.broadcasted_iota(jnp.int32, gates.shape, 1)
        gate = jnp.sum(jnp.where(lane == e, gates, 0.0), axis=1, keepdims=True)
        hid = (_silu(a) * u * gate).astype(BF16)
        acc_scr[rows, :] += jnp.dot(hid, wd, preferred_element_type=F32)
        return carry
    lax.fori_loop(0, n_tiles, tile, 0)

    @pl.when(e == MOE_EXPERTS - 1)
    def _():
        m = mod_ref[0]
        y = x_ref[...] + m[5:6] * acc_scr[...]
        if final:
            ms = jnp.mean(y * y, axis=-1, keepdims=True)
            y = y * lax.rsqrt(ms + NORM_EPS) * fg_ref[...]
        o_ref[...] = y


def _moe_call(x, h2, gates, mod_i, w_gate, w_up, w_down, layer, final_g, mod_row):
    t = x.shape[0]
    final = final_g is not None
    blk = lambda b, e: (b, 0)
    in_specs = [
        pl.BlockSpec((MOE_BLOCK, D_MODEL), blk),
        pl.BlockSpec((MOE_BLOCK, D_MODEL), blk),
        pl.BlockSpec((MOE_BLOCK, LANES), blk),
        pl.BlockSpec((1, 6, D_MODEL), lambda b, e: (mod_row(b), 0, 0)),
        pl.BlockSpec((None, 1, D_MODEL, MOE_D_FF), lambda b, e: (layer, e, 0, 0)),
        pl.BlockSpec((None, 1, D_MODEL, MOE_D_FF), lambda b, e: (layer, e, 0, 0)),
        pl.BlockSpec((None, 1, MOE_D_FF, D_MODEL), lambda b, e: (layer, e, 0, 0)),
    ]
    args = [x, h2, gates, mod_i, w_gate, w_up, w_down]
    if final:
        in_specs.append(pl.BlockSpec((1, D_MODEL), lambda b, e: (0, 0)))
        args.append(final_g)
    return pl.pallas_call(
        functools.partial(_moe_body, final=final),
        grid=(t // MOE_BLOCK, MOE_EXPERTS),
        in_specs=in_specs,
        out_specs=pl.BlockSpec((MOE_BLOCK, D_MODEL), blk),
        out_shape=jax.ShapeDtypeStruct((t, D_MODEL), F32),
        scratch_shapes=[pltpu.VMEM((MOE_BLOCK, D_MODEL), F32)],
        compiler_params=_cparams("arbitrary", "arbitrary"),
        name="moe",
    )(*args)


def _grid_pos_embed(n_tokens):
    t = jnp.arange(n_tokens)
    r = (t // GRID_W).astype(F32)
    col = (t % GRID_W).astype(F32)
    quarter = D_MODEL // 4
    omega = 1.0 / (POS_BASE ** (jnp.arange(quarter, dtype=F32) / quarter))
    ar = r[:, None] * omega[None, :]
    ac = col[:, None] * omega[None, :]
    return jnp.concatenate([jnp.sin(ar), jnp.cos(ar), jnp.sin(ac), jnp.cos(ac)], axis=-1)


def _run_trunk(x, seq_len, latent, init_states, mod_all, p):
    def rows_per(block):
        if latent:
            per = seq_len // block
            return lambda i: i // per
        return lambda i: CTX_ROW

    states = None
    for i in range(DEPTH):
        j = i // 2
        mod_i = mod_all[i]
        n1g = p["norm1_g"][i:i + 1]
        if i % 2 == 0:
            mix, st = _hgrn_call(x, mod_i, n1g, p["hgrn_w_in"], p["log_lb"][j], p["log_1m_lb"][j],
                                 p["hgrn_norm_g"][j:j + 1], j, init_states, states, seq_len,
                                 rows_per(SEQ_BLOCK))
            states = st
            w_out = p["hgrn_w_out"]
        else:
            mix = _fourier_call(x, mod_i, n1g, seq_len, rows_per(SEQ_BLOCK))
            w_out = p["fourier_w_out"]
        x, h2, gates = _post_call(x, mix, mod_i, w_out, j, p["norm2_g"][i:i + 1], p["router_w_t"][i],
                                  p["router_b"][i], rows_per(POST_TILE))
        final_g = p["final_norm_g"] if i == DEPTH - 1 else None
        x = _moe_call(x, h2, gates, mod_i, p["moe_w_gate"], p["moe_w_up"], p["moe_w_down"], i,
                      final_g, rows_per(MOE_BLOCK))
    return x, states


def kernel(x_prompt, x_sample, state_hgrn, c, c_ctx, w_mod, b_mod, norm1_g, norm2_g, hgrn_w_in,
           hgrn_lb_logits, hgrn_norm_g, hgrn_w_out, fourier_w_out, router_group_w, router_group_b,
           router_expert_w, router_expert_b, moe_w_gate, moe_w_up, moe_w_down, final_norm_g):
    batch, seq, _ = x_prompt.shape
    dec_batch, dec_seq, _ = x_sample.shape
    assert dec_batch <= CTX_ROW and seq == ROW_TILE and dec_seq == SEQ_BLOCK

    cond = jnp.zeros((COND_ROWS, D_MODEL), F32).at[:dec_batch].set(c).at[CTX_ROW].set(c_ctx)
    mod_all = _mod_call(cond, w_mod, b_mod).reshape(DEPTH, COND_ROWS, 6, D_MODEL)

    probs = jax.nn.softmax(hgrn_lb_logits.astype(F32), axis=0)
    cs = jnp.cumsum(probs, axis=0)
    lb = cs - cs[0:1]
    router_w = jnp.concatenate([router_group_w, router_expert_w], axis=-1)
    router_b = jnp.concatenate([router_group_b, router_expert_b], axis=-1)
    pad = ROUTER_ROWS - router_w.shape[-1]
    p = {
        "norm1_g": norm1_g, "norm2_g": norm2_g, "hgrn_norm_g": hgrn_norm_g,
        "hgrn_w_in": hgrn_w_in.astype(BF16).reshape(-1, D_MODEL, 5, HEADS, DK).swapaxes(2, 3).reshape(
            -1, D_MODEL, 5 * HEADS * DK),
        "hgrn_w_out": hgrn_w_out.astype(BF16),
        "fourier_w_out": fourier_w_out.astype(BF16),
        "log_lb": jnp.log(lb), "log_1m_lb": jnp.log1p(-lb),
        "router_w_t": jnp.pad(jnp.swapaxes(router_w, 1, 2), ((0, 0), (0, pad), (0, 0))),
        "router_b": jnp.pad(router_b, ((0, 0), (0, pad)))[..., None],
        "moe_w_gate": moe_w_gate, "moe_w_up": moe_w_up, "moe_w_down": moe_w_down,
        "final_norm_g": final_norm_g.reshape(1, D_MODEL),
    }

    y_prompt, new_state = _run_trunk(x_prompt.reshape(batch * seq, D_MODEL), seq, False, None, mod_all, p)

    x_lat = _addpos_call(x_sample.reshape(dec_batch * dec_seq, D_MODEL), _grid_pos_embed(dec_seq))
    y_sample, _ = _run_trunk(x_lat, dec_seq, True, state_hgrn, mod_all, p)
    return (y_prompt.reshape(batch, seq, D_MODEL), y_sample.reshape(dec_batch, dec_seq, D_MODEL), new_state)
```

```python
import functools
import math

import jax
import jax.numpy as jnp
from jax import lax
from jax.experimental import pallas as pl
from jax.experimental.pallas import tpu as pltpu

F32 = jnp.float32
BF16 = jnp.bfloat16
HIGHEST = lax.Precision.HIGHEST

D_MODEL = 1024
DEPTH = 4
GRID_W = 64
HEADS = 8
DK = 128
DV = 128
FOURIER_GROUPS = 4
FOURIER_CG = D_MODEL // FOURIER_GROUPS
MOE_GROUPS = 4
MOE_EPG = 4
MOE_EXPERTS = 16
MOE_D_FF = 512
NORM_EPS = 1e-6
POS_BASE = 10000.0
LOG2_E = 1.0 / math.log(2.0)

COND_ROWS = 8
CTX_ROW = 4
ROUTER_ROWS = 32
LANES = 128
SCAN_CHUNK = 64
SCAN_SUB = 16
GROUP = 4
GROUP_ROWS = GROUP * SCAN_CHUNK
LOCAL_GROUPS = 2
SEQ_BLOCK = 2048
ROW_TILE = 256
TOKEN_BLOCK = 256
MOE_TOP_K = 2
ROUTE_ROWS = 8
ROW_UNIT = 16
BLOCK_ROWS = 768
FFN_TILE = 256
VMEM_LIMIT = 56 * 1024 * 1024


def _cparams(*sem):
    return pltpu.CompilerParams(dimension_semantics=sem, vmem_limit_bytes=VMEM_LIMIT)


def _silu(x):
    return x * jax.nn.sigmoid(x)


def _norm_mod(x, g, shift, scale):
    ms = jnp.mean(x * x, axis=-1, keepdims=True)
    return (x * lax.rsqrt(ms + NORM_EPS) * g) * (1.0 + scale) + shift


def _mod_body(c_ref, w_ref, b_ref, o_ref):
    s = _silu(c_ref[...])
    o_ref[0] = jnp.dot(s, w_ref[0], precision=HIGHEST, preferred_element_type=F32) + b_ref[0]


def _mod_call(cond, w_mod, b_mod):
    n_col = 6 * D_MODEL // D_MODEL
    return pl.pallas_call(
        _mod_body,
        grid=(DEPTH, n_col),
        in_specs=[
            pl.BlockSpec((COND_ROWS, D_MODEL), lambda i, n: (0, 0)),
            pl.BlockSpec((1, D_MODEL, D_MODEL), lambda i, n: (i, 0, n)),
            pl.BlockSpec((1, 1, D_MODEL), lambda i, n: (i, 0, n)),
        ],
        out_specs=pl.BlockSpec((1, COND_ROWS, D_MODEL), lambda i, n: (i, 0, n)),
        out_shape=jax.ShapeDtypeStruct((DEPTH, COND_ROWS, 6 * D_MODEL), F32),
        compiler_params=_cparams("arbitrary", "arbitrary"),
        name="mod",
    )(cond, w_mod, b_mod.reshape(DEPTH, 1, 6 * D_MODEL))


def _embed_body(xc_ref, xl_ref, p_ref, o_ref, *, n_ctx):
    i = pl.program_id(0)

    @pl.when(i < n_ctx)
    def _():
        o_ref[...] = xc_ref[...]

    @pl.when(i >= n_ctx)
    def _():
        o_ref[...] = xl_ref[...] + p_ref[...]


def _embed_call(x_ctx, x_lat, pos):
    n_ctx = x_ctx.shape[0] // TOKEN_BLOCK
    n_lat = x_lat.shape[0] // TOKEN_BLOCK
    per_seq = pos.shape[0] // TOKEN_BLOCK
    return pl.pallas_call(
        functools.partial(_embed_body, n_ctx=n_ctx),
        grid=(n_ctx + n_lat,),
        in_specs=[
            pl.BlockSpec((TOKEN_BLOCK, D_MODEL), lambda i: (jnp.minimum(i, n_ctx - 1), 0)),
            pl.BlockSpec((TOKEN_BLOCK, D_MODEL), lambda i: (jnp.maximum(i - n_ctx, 0), 0)),
            pl.BlockSpec((TOKEN_BLOCK, D_MODEL), lambda i: (jnp.maximum(i - n_ctx, 0) % per_seq, 0)),
        ],
        out_specs=pl.BlockSpec((TOKEN_BLOCK, D_MODEL), lambda i: (i, 0)),
        out_shape=jax.ShapeDtypeStruct((x_ctx.shape[0] + x_lat.shape[0], D_MODEL), F32),
        compiler_params=_cparams("arbitrary"),
        name="embed",
    )(x_ctx, x_lat, pos)


def _log_f_and_key(z, log_lb, log_1m_lb):
    log_sig = jnp.minimum(z, 0.0) - jnp.log1p(jnp.exp(-jnp.abs(z)))
    b = log_1m_lb + log_sig
    log_f = jnp.maximum(log_lb, b) + jnp.log1p(jnp.exp(-jnp.abs(log_lb - b)))
    return log_f, jnp.exp(b - z)


def _group_masks():
    n = GROUP_ROWS
    t = lax.broadcasted_iota(jnp.int32, (n, n), 0)
    s = lax.broadcasted_iota(jnp.int32, (n, n), 1)
    same_chunk = (t // SCAN_CHUNK) == (s // SCAN_CHUNK)
    same_sub = (t // SCAN_SUB) == (s // SCAN_SUB)
    prefix = (same_chunk & (t >= s)).astype(BF16)
    return prefix, same_chunk, same_sub & (t >= s), same_sub & (t <= s)


def _qk(a, b):
    return lax.dot_general(a, b, (((1,), (1,)), ((), ())), preferred_element_type=F32)


def _split2(x):
    hi = x.astype(BF16)
    return [hi, (x - hi.astype(F32)).astype(BF16)]


def _sum2(s, i):
    o = 2 * i * DK
    return s[:, o:o + DK] + s[:, o + DK:o + 2 * DK]


def _group_dir(b, q, k, v, vt, m_chunk, m_diag, reverse):
    c, sb = SCAN_CHUNK, SCAN_SUB
    nb = c // sb
    zero = jnp.zeros((sb, DK), BF16)
    qd, kd, qs, kl, decs = [], [], [], [], []
    q_lev = [[] for _ in range(nb - 1)]
    k_lev = [[] for _ in range(nb - 1)]
    for g in range(GROUP):
        starts = [g * c + i * sb for i in range(nb)]
        pos = [nb - 1 - i for i in range(nb)] if reverse else list(range(nb))
        end_at = {}
        for i, r in enumerate(starts):
            e = r if reverse else r + sb - 1
            end_at[pos[i]] = b[e:e + 1]
        edge = end_at[nb - 1]
        decs.append(jnp.exp2(edge))
        for i, r in enumerate(starts):
            m = r + sb // 2 if reverse else r + sb // 2 - 1
            bi, qi, ki, mid = b[r:r + sb], q[r:r + sb], k[r:r + sb], b[m:m + 1]
            qd.append((qi * jnp.exp2(bi - mid)).astype(BF16))
            kd.append((ki * jnp.exp2(mid - bi)).astype(BF16))
            for j in range(nb - 1):
                q_lev[j].append((qi * jnp.exp2(bi - end_at[j])).astype(BF16) if pos[i] > j else zero)
                k_lev[j].append((ki * jnp.exp2(end_at[j] - bi)).astype(BF16) if pos[i] == j else zero)
            qs.append((qi * jnp.exp2(bi)).astype(BF16))
            kl.append((ki * jnp.exp2(edge - bi)).astype(BF16))

    def rows(blocks):
        return jnp.concatenate(blocks, axis=0)
    s_diag = _qk(rows(qd), rows(kd))
    s_lev = _qk(jnp.concatenate([rows(x) for x in q_lev], axis=1),
                jnp.concatenate([rows(x) for x in k_lev], axis=1))
    scores = jnp.where(m_diag, s_diag, 0.0) + jnp.where(m_chunk, s_lev, 0.0)
    o = jnp.dot(scores.astype(BF16), v, preferred_element_type=F32)
    kl_wide = jnp.concatenate(
        [rows([blk if n // nb == g else zero for n, blk in enumerate(kl)]) for g in range(GROUP)], axis=1)
    upd_t = jnp.dot(vt, kl_wide, preferred_element_type=F32)
    return o, rows(qs), upd_t, decs


def _hgrn_body(*refs, n_in, n_seq, cps, has_s0):
    x_ref, mod_ref, n1g_ref, w_ref, llb_ref, l1m_ref, ng_ref = refs[:7]
    s0_ref = refs[7] if has_s0 else None
    og_ref = refs[n_in]
    st_ref = None if has_s0 else refs[n_in + 1]
    (h_scr, q_scr, v_scr, g_scr, lff_scr, kf_scr, lfb_scr, kb_scr, of_scr, ob_scr,
     qsf_scr, qsb_scr, uf_scr, ub_scr, decf_scr, decb_scr, sf_scr, sb_scr) = refs[n_in + (1 if has_s0 else 2):]
    c = SCAN_CHUNK
    n_chunks = SEQ_BLOCK // c
    n_tiles = SEQ_BLOCK // ROW_TILE

    @pl.when(pl.program_id(1) == 0)
    def _():
        m = mod_ref[0]
        g = n1g_ref[...]

        def tile(t, carry):
            rows = pl.ds(pl.multiple_of(t * ROW_TILE, ROW_TILE), ROW_TILE)
            h_scr[rows, :] = _norm_mod(x_ref[rows, :], g, m[0:1], m[1:2]).astype(BF16)
            return carry
        lax.fori_loop(0, n_tiles, tile, 0)

    llb = llb_ref[...]
    l1m = l1m_ref[...]

    def proj_tiles(g, carry):
        tiles = []
        for u in range(2):
            rows = pl.ds(pl.multiple_of((2 * g + u) * ROW_TILE, ROW_TILE), ROW_TILE)
            tiles.append((rows, h_scr[rows, :]))
        outs = []
        for rows, h in tiles:
            p = jnp.dot(h, w_ref[...], preferred_element_type=F32)
            outs.append((rows, p, _log_f_and_key(p[:, DK:2 * DK], llb[0:1], l1m[0:1]),
                         _log_f_and_key(p[:, 2 * DK:3 * DK], llb[1:2], l1m[1:2])))
        for rows, p, (lf_f, k_f), (lf_b, k_b) in outs:
            q_scr[rows, :] = p[:, 0:DK]
            v_scr[rows, :] = p[:, 3 * DK:3 * DK + DV]
            g_scr[rows, :] = p[:, 3 * DK + DV:3 * DK + 2 * DV]
            lff_scr[rows, :] = lf_f
            kf_scr[rows, :] = k_f
            lfb_scr[rows, :] = lf_b
            kb_scr[rows, :] = k_b
        return carry
    lax.fori_loop(0, n_tiles // 2, proj_tiles, 0)

    prefix, m_chunk, m_diag_f, m_diag_b = _group_masks()

    def group_rows(grp):
        return pl.ds(pl.multiple_of(grp * GROUP_ROWS, GROUP_ROWS), GROUP_ROWS)

    def local_load(grp):
        rows = group_rows(grp)
        return (q_scr[rows, :], v_scr[rows, :], lff_scr[rows, :], kf_scr[rows, :], lfb_scr[rows, :],
                kb_scr[rows, :])

    def local_compute(q, v32, lf_f, k_f, lf_b, k_b):
        v = v32.astype(BF16)
        vt = v32.T.astype(BF16)
        sums = jnp.dot(prefix, jnp.concatenate(_split2(lf_f) + _split2(lf_b), axis=1),
                       preferred_element_type=F32)
        b_f = _sum2(sums, 0) * LOG2_E
        pre_b = _sum2(sums, 1)
        total_b = jnp.concatenate(
            [jnp.broadcast_to(pre_b[g * c + c - 1:g * c + c], (c, DK)) for g in range(GROUP)], axis=0)
        b_b = ((total_b - pre_b) + lf_b) * LOG2_E
        return (_group_dir(b_f, q, k_f, v, vt, m_chunk, m_diag_f, reverse=False),
                _group_dir(b_b, q, k_b, v, vt, m_chunk, m_diag_b, reverse=True))

    def local_store(grp, fwd, bwd):
        rows = group_rows(grp)
        for (o, qs, upd_t, dec), o_scr, qs_scr, u_scr, dec_scr in (
                (fwd, of_scr, qsf_scr, uf_scr, decf_scr), (bwd, ob_scr, qsb_scr, ub_scr, decb_scr)):
            o_scr[rows, :] = o
            qs_scr[rows, :] = qs
            for g in range(GROUP):
                u_scr[grp * GROUP + g] = upd_t[:, g * DK:(g + 1) * DK]
                dec_scr[grp * GROUP + g] = dec[g]

    def local(it, carry):
        groups = [it * LOCAL_GROUPS + u for u in range(LOCAL_GROUPS)]
        loaded = [local_load(grp) for grp in groups]
        results = [local_compute(*vals) for vals in loaded]
        for grp, (fwd, bwd) in zip(groups, results):
            local_store(grp, fwd, bwd)
        return carry
    lax.fori_loop(0, n_chunks // GROUP // LOCAL_GROUPS, local, 0)

    def advance(ci, cj, sf, sb):
        sf_scr[ci] = sf.astype(BF16)
        sb_scr[cj] = sb.astype(BF16)
        return sf * decf_scr[ci] + uf_scr[ci], sb * decb_scr[cj] + ub_scr[cj]

    if has_s0:
        def step(i, carry):
            return advance(i, n_chunks - 1 - i, *carry)
        lax.fori_loop(0, n_chunks, step, (s0_ref[0].T, s0_ref[1].T), unroll=4)
    else:
        def seq(s, carry):
            sf = jnp.zeros((DV, DK), F32)
            sb = jnp.zeros((DV, DK), F32)
            for i in range(cps):
                sf, sb = advance(s * cps + i, s * cps + cps - 1 - i, sf, sb)
            st_ref[s, 0] = sf.T
            st_ref[s, 1] = sb.T
            return carry
        lax.fori_loop(0, n_seq, seq, 0)

    ng = ng_ref[...]
    zero_chunk = jnp.zeros((c, DK), BF16)

    def widen(qs):
        return jnp.concatenate(
            [jnp.concatenate([qs[h * c:(h + 1) * c] if h == g else zero_chunk for h in range(GROUP)], axis=0)
             for g in range(GROUP)], axis=1)

    def out_group(grp, carry):
        rows = group_rows(grp)
        qs = jnp.concatenate([widen(qsf_scr[rows, :]), widen(qsb_scr[rows, :])], axis=1)
        st = jnp.concatenate([sf_scr[grp * GROUP + g] for g in range(GROUP)]
                             + [sb_scr[grp * GROUP + g] for g in range(GROUP)], axis=1)
        o = of_scr[rows, :] + ob_scr[rows, :] + _qk(qs, st)
        o = o * lax.rsqrt(jnp.mean(o * o, axis=-1, keepdims=True) + NORM_EPS) * ng
        og_ref[rows, :] = (o * _silu(g_scr[rows, :])).astype(BF16)
        return carry
    lax.fori_loop(0, n_chunks // GROUP, out_group, 0)


class _Group:
    def __init__(self, first, n, seq_len, latent):
        self.first, self.n, self.seq_len, self.latent = first, n, seq_len, latent

    def cond_row(self, block):
        return block if self.latent else CTX_ROW


def _hgrn_call(x, mix, mod_all, layer, n1g, w_in, log_lb, log_1m_lb, norm_g, j, grp, s0, states):
    t = x.shape[0]
    n_seq = SEQ_BLOCK // grp.seq_len
    cps = grp.seq_len // SCAN_CHUNK
    n_chunks = SEQ_BLOCK // SCAN_CHUNK
    has_s0 = s0 is not None
    n_layers = w_in.shape[0]
    wide = 3 * DK + 2 * DV

    in_specs = [
        pl.BlockSpec((SEQ_BLOCK, D_MODEL), lambda b, h: (b + grp.first, 0)),
        pl.BlockSpec((None, 1, 6, D_MODEL), lambda b, h: (layer, grp.cond_row(b), 0, 0)),
        pl.BlockSpec((1, D_MODEL), lambda b, h: (0, 0)),
        pl.BlockSpec((None, D_MODEL, wide), lambda b, h: (j, 0, h)),
        pl.BlockSpec((2, DK), lambda b, h: (0, h)),
        pl.BlockSpec((2, DK), lambda b, h: (0, h)),
        pl.BlockSpec((1, DV), lambda b, h: (0, h)),
    ]
    args = [x, mod_all, n1g, w_in, log_lb, log_1m_lb, norm_g]
    og_spec = pl.BlockSpec((SEQ_BLOCK, DV), lambda b, h: (b + grp.first, h))
    og_shape = jax.ShapeDtypeStruct((t, D_MODEL), BF16)
    aliases = {}
    if has_s0:
        in_specs.append(pl.BlockSpec((None, None, 2, None, DK, DV), lambda b, h: (b, j, 0, h, 0, 0)))
        args.append(s0)
        out_specs, out_shape = og_spec, og_shape
    else:
        st_shape = (grp.n * n_seq, n_layers, 2, HEADS, DK, DV)
        out_specs = (og_spec,
                     pl.BlockSpec((n_seq, None, 2, None, DK, DV), lambda b, h: (b, j, 0, h, 0, 0)))
        out_shape = (og_shape, jax.ShapeDtypeStruct(st_shape, F32))
        if states is not None:
            in_specs.append(pl.BlockSpec(memory_space=pl.ANY))
            args.append(states)
            aliases[len(args) - 1] = 1
    if mix is not None:
        in_specs.append(pl.BlockSpec(memory_space=pl.ANY))
        args.append(mix)
        aliases[len(args) - 1] = 0
    col = pltpu.VMEM((SEQ_BLOCK, DK), F32)
    colb = pltpu.VMEM((SEQ_BLOCK, DK), BF16)
    upd = pltpu.VMEM((n_chunks, DV, DK), F32)
    dec = pltpu.VMEM((n_chunks, 1, DK), F32)
    start = pltpu.VMEM((n_chunks, DV, DK), BF16)
    res = pl.pallas_call(
        functools.partial(_hgrn_body, n_in=len(args), n_seq=n_seq, cps=cps, has_s0=has_s0),
        grid=(grp.n, HEADS),
        in_specs=in_specs,
        out_specs=out_specs,
        out_shape=out_shape,
        scratch_shapes=[pltpu.VMEM((SEQ_BLOCK, D_MODEL), BF16)] + [col] * 9 + [colb, colb, upd, upd, dec, dec,
                                                                                  start, start],
        input_output_aliases=aliases,
        compiler_params=_cparams("arbitrary", "arbitrary"),
        name="hgrn_lat" if has_s0 else "hgrn_ctx",
    )(*args)
    return (res, None) if has_s0 else res


def _fourier_body(*refs, n_in, seq_len):
    x_ref, mod_ref, n1g_ref, cl_ref, sl_ref, cc_ref = refs[:6]
    z_ref, h_scr = refs[n_in:]
    rt = pl.program_id(1)
    n_tiles = SEQ_BLOCK // ROW_TILE

    @pl.when(rt == 0)
    def _():
        m = mod_ref[0]
        g = n1g_ref[...]

        def tile(t, carry):
            rows = pl.ds(pl.multiple_of(t * ROW_TILE, ROW_TILE), ROW_TILE)
            h_scr[rows, :] = _norm_mod(x_ref[rows, :], g, m[0:1], m[1:2]).astype(BF16)
            return carry
        lax.fori_loop(0, n_tiles, tile, 0)

    if seq_len == ROW_TILE:
        h = h_scr[pl.ds(pl.multiple_of(rt * ROW_TILE, ROW_TILE), ROW_TILE), :]
    else:
        h = h_scr[...]
    zc = jnp.dot(cl_ref[...], h, preferred_element_type=F32).astype(BF16)
    zs = jnp.dot(sl_ref[...], h, preferred_element_type=F32).astype(BF16)
    scale = 1.0 / math.sqrt(seq_len * FOURIER_CG)
    cg = FOURIER_CG
    for g in range(FOURIER_GROUPS):
        cat = jnp.concatenate([zc[:, g * cg:(g + 1) * cg], zs[:, g * cg:(g + 1) * cg]], axis=1)
        out = jnp.dot(cat, cc_ref[...], preferred_element_type=F32) * scale
        z_ref[:, g * cg:(g + 1) * cg] = out.astype(BF16)


def _dft_tables(n):
    j = jnp.arange(n, dtype=jnp.int32)
    ang = ((j[:, None] * j[None, :]) % n).astype(F32) * (2.0 * math.pi / n)
    return jnp.cos(ang), jnp.sin(ang)


def _fourier_call(x, mix, mod_all, layer, n1g, grp):
    t = x.shape[0]
    seq_len = grp.seq_len
    n_rt = SEQ_BLOCK // ROW_TILE
    cl, sl = _dft_tables(seq_len)
    cc, sc = _dft_tables(FOURIER_CG)
    cc2 = jnp.concatenate([cc, -sc], axis=0).astype(BF16)
    if seq_len == ROW_TILE:
        pos_spec = pl.BlockSpec((ROW_TILE, seq_len), lambda b, r: (0, 0))
    else:
        pos_spec = pl.BlockSpec((ROW_TILE, seq_len), lambda b, r: (r, 0))
    in_specs = [
        pl.BlockSpec((SEQ_BLOCK, D_MODEL), lambda b, r: (b + grp.first, 0)),
        pl.BlockSpec((None, 1, 6, D_MODEL), lambda b, r: (layer, grp.cond_row(b), 0, 0)),
        pl.BlockSpec((1, D_MODEL), lambda b, r: (0, 0)),
        pos_spec, pos_spec,
        pl.BlockSpec((2 * FOURIER_CG, FOURIER_CG), lambda b, r: (0, 0)),
    ]
    args = [x, mod_all, n1g, cl.astype(BF16), sl.astype(BF16), cc2]
    aliases = {}
    if mix is not None:
        in_specs.append(pl.BlockSpec(memory_space=pl.ANY))
        args.append(mix)
        aliases[len(args) - 1] = 0
    return pl.pallas_call(
        functools.partial(_fourier_body, n_in=len(args), seq_len=seq_len),
        grid=(grp.n, n_rt),
        in_specs=in_specs,
        out_specs=pl.BlockSpec((ROW_TILE, D_MODEL), lambda b, r: ((b + grp.first) * n_rt + r, 0)),
        out_shape=jax.ShapeDtypeStruct((t, D_MODEL), BF16),
        scratch_shapes=[pltpu.VMEM((SEQ_BLOCK, D_MODEL), BF16)],
        input_output_aliases=aliases,
        compiler_params=_cparams("arbitrary", "arbitrary"),
        name="fourier_lat" if grp.latent else "fourier_ctx",
    )(*args)


def _route(lg):
    grp = [lg[g:g + 1] for g in range(MOE_GROUPS)]
    gmax = functools.reduce(jnp.maximum, grp)
    gi = jnp.where(grp[0] == gmax, 0, jnp.where(grp[1] == gmax, 1, jnp.where(grp[2] == gmax, 2, 3)))
    pgv = 1.0 / functools.reduce(lambda a, b: a + b, [jnp.exp(g - gmax) for g in grp])
    sel = []
    for j in range(MOE_EPG):
        rows = [lg[MOE_GROUPS + g * MOE_EPG + j:MOE_GROUPS + g * MOE_EPG + j + 1] for g in range(MOE_GROUPS)]
        sel.append(jnp.where(gi == 0, rows[0], jnp.where(gi == 1, rows[1], jnp.where(gi == 2, rows[2], rows[3]))))
    m1 = functools.reduce(jnp.maximum, sel)
    i1 = jnp.where(sel[0] == m1, 0, jnp.where(sel[1] == m1, 1, jnp.where(sel[2] == m1, 2, 3)))
    rest = [jnp.where(i1 == j, -jnp.inf, sel[j]) for j in range(MOE_EPG)]
    m2 = functools.reduce(jnp.maximum, rest)
    i2 = jnp.where(rest[0] == m2, 0, jnp.where(rest[1] == m2, 1, jnp.where(rest[2] == m2, 2, 3)))
    e2 = jnp.exp(m2 - m1)
    w1 = pgv / (1.0 + e2)
    w2 = pgv * e2 / (1.0 + e2)
    r = lg.shape[1]
    ex1 = gi * MOE_EPG + i1
    ex2 = gi * MOE_EPG + i2
    eid = lax.broadcasted_iota(jnp.int32, (MOE_EXPERTS, r), 0)
    member = (eid == ex1) | (eid == ex2)
    t0 = lax.broadcasted_iota(jnp.int32, (r, r), 0)
    t1 = lax.broadcasted_iota(jnp.int32, (r, r), 1)
    rank = jnp.dot(member.astype(BF16), (t0 < t1).astype(BF16), preferred_element_type=F32)
    count = jnp.sum(member.astype(F32), axis=1, keepdims=True)
    padded = jnp.floor((count + (ROW_UNIT - 1)) * (1.0 / ROW_UNIT)) * ROW_UNIT
    e0 = lax.broadcasted_iota(jnp.int32, (MOE_EXPERTS, MOE_EXPERTS), 0)
    e1 = lax.broadcasted_iota(jnp.int32, (MOE_EXPERTS, MOE_EXPERTS), 1)
    start = jnp.dot((e1 < e0).astype(F32), jnp.broadcast_to(padded, (MOE_EXPERTS, LANES)),
                    precision=HIGHEST, preferred_element_type=F32)[:, 0:1]
    row = start + rank
    row1 = jnp.sum(jnp.where(eid == ex1, row, 0.0), axis=0, keepdims=True)
    row2 = jnp.sum(jnp.where(eid == ex2, row, 0.0), axis=0, keepdims=True)
    rid = lax.broadcasted_iota(jnp.int32, (LANES, r), 0)
    fields = (row1, row2, w1, w2, ex1.astype(F32), ex2.astype(F32))
    table = jnp.zeros((LANES, r), F32)
    for i, f in enumerate(fields):
        table = jnp.where(rid == i, f, table)
    return table, count


def _post_body(x_ref, mix_ref, mod_ref, w_ref, n2g_ref, wr_ref, br_ref, xo_ref, h2_ref, rt_ref, ct_ref,
               cnt_ref):
    m = mod_ref[0]
    out = jnp.dot(mix_ref[...], w_ref[...], preferred_element_type=F32)
    xn = x_ref[...] + m[2:3] * out
    xo_ref[...] = xn
    h2 = _norm_mod(xn, n2g_ref[...], m[3:4], m[4:5])
    h2_ref[...] = h2.astype(BF16)
    lg = lax.dot_general(wr_ref[...], h2, (((1,), (1,)), ((), ())), precision=HIGHEST,
                         preferred_element_type=F32) + br_ref[...]
    table, count = _route(lg)
    rt_ref[...] = table[0:ROUTE_ROWS]
    ct_ref[...] = table.T
    cnt_ref[0] = jnp.broadcast_to(count, (MOE_EXPERTS, LANES))


def _post_call(x, mix, mod_all, layer, w_out, j, n2g, wr_t, br, mod_row):
    t = x.shape[0]
    nb = t // TOKEN_BLOCK
    tile = lambda i: (i, 0)
    full = lambda i: (0, 0)
    return pl.pallas_call(
        _post_body,
        grid=(nb,),
        in_specs=[
            pl.BlockSpec((TOKEN_BLOCK, D_MODEL), tile),
            pl.BlockSpec((TOKEN_BLOCK, D_MODEL), tile),
            pl.BlockSpec((None, 1, 6, D_MODEL), lambda i: (layer, mod_row(i), 0, 0)),
            pl.BlockSpec((None, D_MODEL, D_MODEL), lambda i: (j, 0, 0)),
            pl.BlockSpec((1, D_MODEL), full),
            pl.BlockSpec((ROUTER_ROWS, D_MODEL), full),
            pl.BlockSpec((ROUTER_ROWS, 1), full),
        ],
        out_specs=(
            pl.BlockSpec((TOKEN_BLOCK, D_MODEL), tile),
            pl.BlockSpec((TOKEN_BLOCK, D_MODEL), tile),
            pl.BlockSpec((ROUTE_ROWS, TOKEN_BLOCK), lambda i: (0, i)),
            pl.BlockSpec((TOKEN_BLOCK, LANES), tile),
            pl.BlockSpec((1, MOE_EXPERTS, LANES), lambda i: (i, 0, 0)),
        ),
        out_shape=(
            jax.ShapeDtypeStruct((t, D_MODEL), F32),
            jax.ShapeDtypeStruct((t, D_MODEL), BF16),
            jax.ShapeDtypeStruct((ROUTE_ROWS, t), F32),
            jax.ShapeDtypeStruct((t, LANES), F32),
            jax.ShapeDtypeStruct((nb, MOE_EXPERTS, LANES), F32),
        ),
        compiler_params=_cparams("arbitrary"),
        name="post",
    )(x, mix, mod_all, w_out, n2g, wr_t, br)


def _moe_schedule(counts, n_tiles_max):
    units = (counts + ROW_UNIT - 1) // ROW_UNIT
    local = jnp.cumsum(units, axis=1) - units
    total = jnp.sum(units, axis=0)
    per_tile = FFN_TILE // ROW_UNIT
    tiles = (total + per_tile - 1) // per_tile
    region = tiles * per_tile
    region_start = jnp.cumsum(region) - region
    glob = region_start[None, :] + jnp.cumsum(units, axis=0) - units
    tile_end = jnp.cumsum(tiles)
    n_tiles = tile_end[-1]
    s = jnp.minimum(jnp.arange(n_tiles_max, dtype=jnp.int32), n_tiles - 1)
    tile_expert = jnp.sum((tile_end[None, :] <= s[:, None]).astype(jnp.int32), axis=1)
    first = (s == (tile_end - tiles)[tile_expert]) & (jnp.arange(n_tiles_max) < n_tiles)
    i32 = lambda a: a.astype(jnp.int32).reshape(-1)
    return {
        "local": i32(local), "units": i32(units), "glob": i32(glob),
        "fill_start": i32(region_start + total), "fill_units": i32(region - total),
        "tile_expert": tile_expert, "tile_first": i32(first), "n_tiles": i32(n_tiles),
    }


def _segment_copies(local_ref, units_ref, glob_ref, blk, make_copy):
    def per_expert(e, n):
        idx = blk * MOE_EXPERTS + e
        loc, cnt, glo = local_ref[idx], units_ref[idx], glob_ref[idx]

        def unit(u, carry):
            make_copy(pl.multiple_of((loc + u) * ROW_UNIT, ROW_UNIT),
                      pl.multiple_of((glo + u) * ROW_UNIT, ROW_UNIT)).start()
            return carry
        lax.fori_loop(0, cnt, unit, 0)
        return n + cnt
    return lax.fori_loop(0, MOE_EXPERTS, per_expert, 0)


def _wait_copies(n, make_copy):
    def one(u, carry):
        make_copy(0, 0).wait()
        return carry
    lax.fori_loop(0, n, one, 0)


def _dispatch_body(local_ref, units_ref, glob_ref, fill_start_ref, fill_units_ref,
                   h2_ref, rt_ref, xs_ref, rows_scr, zero_scr, sem):
    blk = pl.program_id(0)
    table = rt_ref[...]
    row1 = table[0:1].astype(jnp.int32)
    row2 = table[1:2].astype(jnp.int32)
    rid = lax.broadcasted_iota(jnp.int32, (BLOCK_ROWS, TOKEN_BLOCK), 0)
    onehot = ((rid == row1) | (rid == row2)).astype(BF16)
    rows_scr[...] = jnp.dot(onehot, h2_ref[...], preferred_element_type=F32).astype(BF16)

    def copy(src_row, dst_row):
        return pltpu.make_async_copy(rows_scr.at[pl.ds(src_row, ROW_UNIT)], xs_ref.at[pl.ds(dst_row, ROW_UNIT)],
                                     sem)
    _wait_copies(_segment_copies(local_ref, units_ref, glob_ref, blk, copy), copy)

    @pl.when(blk == pl.num_programs(0) - 1)
    def _():
        zero_scr[...] = jnp.zeros_like(zero_scr)

        def fill(dst_row):
            return pltpu.make_async_copy(zero_scr, xs_ref.at[pl.ds(dst_row, ROW_UNIT)], sem)

        def per_expert(e, n):
            def unit(u, carry):
                fill(pl.multiple_of((fill_start_ref[e] + u) * ROW_UNIT, ROW_UNIT)).start()
                return carry
            lax.fori_loop(0, fill_units_ref[e], unit, 0)
            return n + fill_units_ref[e]
        n_fill = lax.fori_loop(0, MOE_EXPERTS, per_expert, 0)

        def one(u, carry):
            fill(0).wait()
            return carry
        lax.fori_loop(0, n_fill, one, 0)


def _dispatch_call(h2, route_t, sched, n_rows):
    nb = h2.shape[0] // TOKEN_BLOCK
    return pl.pallas_call(
        _dispatch_body,
        grid_spec=pltpu.PrefetchScalarGridSpec(
            num_scalar_prefetch=5,
            grid=(nb,),
            in_specs=[
                pl.BlockSpec((TOKEN_BLOCK, D_MODEL), lambda i, *_: (i, 0)),
                pl.BlockSpec((ROUTE_ROWS, TOKEN_BLOCK), lambda i, *_: (0, i)),
            ],
            out_specs=pl.BlockSpec(memory_space=pl.ANY),
            scratch_shapes=[pltpu.VMEM((BLOCK_ROWS, D_MODEL), BF16), pltpu.VMEM((ROW_UNIT, D_MODEL), BF16),
                            pltpu.SemaphoreType.DMA],
        ),
        out_shape=jax.ShapeDtypeStruct((n_rows, D_MODEL), BF16),
        compiler_params=_cparams("arbitrary"),
        name="moe_dispatch",
    )(sched["local"], sched["units"], sched["glob"], sched["fill_start"], sched["fill_units"], h2, route_t)


def _ffn_body(expert_ref, first_ref, n_ref, x_ref, wg_ref, wu_ref, wd_ref, y_ref, wg_scr, wu_scr, wd_scr):
    s = pl.program_id(0)

    @pl.when(s < n_ref[0])
    def _():
        @pl.when(first_ref[s] == 1)
        def _():
            wg_scr[...] = wg_ref[...].astype(BF16)
            wu_scr[...] = wu_ref[...].astype(BF16)
            wd_scr[...] = wd_ref[...].astype(BF16)
        x = x_ref[...]
        a = jnp.dot(x, wg_scr[...], preferred_element_type=F32)
        u = jnp.dot(x, wu_scr[...], preferred_element_type=F32)
        hid = (_silu(a) * u).astype(BF16)
        y_ref[...] = jnp.dot(hid, wd_scr[...], preferred_element_type=F32).astype(BF16)

    @pl.when(s >= n_ref[0])
    def _():
        y_ref[...] = jnp.zeros_like(y_ref)


def _ffn_call(xs, sched, w_gate, w_up, w_down, layer, n_tiles_max):
    def tile(s, expert, first, n):
        return (jnp.minimum(s, n[0] - 1), 0)

    def weight(s, expert, first, n):
        return (layer, expert[s], 0, 0)
    return pl.pallas_call(
        _ffn_body,
        grid_spec=pltpu.PrefetchScalarGridSpec(
            num_scalar_prefetch=3,
            grid=(n_tiles_max,),
            in_specs=[
                pl.BlockSpec((FFN_TILE, D_MODEL), tile),
                pl.BlockSpec((None, None, D_MODEL, MOE_D_FF), weight),
                pl.BlockSpec((None, None, D_MODEL, MOE_D_FF), weight),
                pl.BlockSpec((None, None, MOE_D_FF, D_MODEL), weight),
            ],
            out_specs=pl.BlockSpec((FFN_TILE, D_MODEL), lambda s, *_: (s, 0)),
            scratch_shapes=[pltpu.VMEM((D_MODEL, MOE_D_FF), BF16), pltpu.VMEM((D_MODEL, MOE_D_FF), BF16),
                            pltpu.VMEM((MOE_D_FF, D_MODEL), BF16)],
        ),
        out_shape=jax.ShapeDtypeStruct(xs.shape, BF16),
        compiler_params=_cparams("arbitrary"),
        name="moe_ffn",
    )(sched["tile_expert"], sched["tile_first"], sched["n_tiles"], xs, w_gate, w_up, w_down)


def _combine_body(*refs, split):
    final = split is not None
    local_ref, units_ref, glob_ref, x_ref, ct_ref, mod_ref = refs[:6]
    fg_ref = refs[6] if final else None
    ys_ref = refs[7 if final else 6]
    out_refs = refs[(8 if final else 7):-2]
    rows_scr, sem = refs[-2:]
    blk = pl.program_id(0)

    @pl.when(blk == 0)
    def _():
        rows_scr[...] = jnp.zeros_like(rows_scr)

    def copy(dst_row, src_row):
        return pltpu.make_async_copy(ys_ref.at[pl.ds(src_row, ROW_UNIT)], rows_scr.at[pl.ds(dst_row, ROW_UNIT)],
                                     sem)
    _wait_copies(_segment_copies(local_ref, units_ref, glob_ref, blk, copy), copy)

    table = ct_ref[...]
    row1 = table[:, 0:1].astype(jnp.int32)
    row2 = table[:, 1:2].astype(jnp.int32)
    rid = lax.broadcasted_iota(jnp.int32, (TOKEN_BLOCK, BLOCK_ROWS), 1)
    weights = (jnp.where(rid == row1, table[:, 2:3], 0.0) + jnp.where(rid == row2, table[:, 3:4], 0.0))
    y = jnp.dot(weights.astype(BF16), rows_scr[...], preferred_element_type=F32)
    out = x_ref[...] + mod_ref[0][5:6] * y
    if not final:
        out_refs[0][...] = out
    else:
        ms = jnp.mean(out * out, axis=-1, keepdims=True)
        out = out * lax.rsqrt(ms + NORM_EPS) * fg_ref[...]

        @pl.when(blk < split)
        def _():
            out_refs[0][...] = out

        @pl.when(blk >= split)
        def _():
            out_refs[1][...] = out


def _combine_call(x, route_c, ys, sched, mod_all, layer, final_g, split, mod_row):
    t = x.shape[0]
    final = final_g is not None
    tok = (TOKEN_BLOCK, D_MODEL)
    if final:
        out_specs = (pl.BlockSpec(tok, lambda i, *_: (jnp.minimum(i, split - 1), 0)),
                     pl.BlockSpec(tok, lambda i, *_: (jnp.maximum(i - split, 0), 0)))
        out_shape = (jax.ShapeDtypeStruct((split * TOKEN_BLOCK, D_MODEL), F32),
                     jax.ShapeDtypeStruct((t - split * TOKEN_BLOCK, D_MODEL), F32))
    else:
        out_specs = pl.BlockSpec(tok, lambda i, *_: (i, 0))
        out_shape = jax.ShapeDtypeStruct((t, D_MODEL), F32)
    in_specs = [
        pl.BlockSpec((TOKEN_BLOCK, D_MODEL), lambda i, *_: (i, 0)),
        pl.BlockSpec((TOKEN_BLOCK, LANES), lambda i, *_: (i, 0)),
        pl.BlockSpec((None, 1, 6, D_MODEL), lambda i, *_: (layer, mod_row(i), 0, 0)),
    ]
    args = [x, route_c, mod_all]
    if final:
        in_specs.append(pl.BlockSpec((1, D_MODEL), lambda i, *_: (0, 0)))
        args.append(final_g)
    in_specs.append(pl.BlockSpec(memory_space=pl.ANY))
    args.append(ys)
    return pl.pallas_call(
        functools.partial(_combine_body, split=split if final else None),
        grid_spec=pltpu.PrefetchScalarGridSpec(
            num_scalar_prefetch=3,
            grid=(t // TOKEN_BLOCK,),
            in_specs=in_specs,
            out_specs=out_specs,
            scratch_shapes=[pltpu.VMEM((BLOCK_ROWS, D_MODEL), BF16), pltpu.SemaphoreType.DMA],
        ),
        out_shape=out_shape,
        compiler_params=_cparams("arbitrary"),
        name="moe_combine",
    )(sched["local"], sched["units"], sched["glob"], *args)


def _moe(x, h2, route_t, route_c, counts, mod_all, layer, w_gate, w_up, w_down, final_g, split, mod_row):
    t = x.shape[0]
    nb = t // TOKEN_BLOCK
    max_rows = MOE_TOP_K * t + nb * MOE_EXPERTS * (ROW_UNIT - 1) + MOE_EXPERTS * (FFN_TILE - 1)
    n_tiles_max = -(-max_rows // FFN_TILE)
    sched = _moe_schedule(counts[:, :, 0].astype(jnp.int32), n_tiles_max)
    xs = _dispatch_call(h2, route_t, sched, n_tiles_max * FFN_TILE)
    ys = _ffn_call(xs, sched, w_gate, w_up, w_down, layer, n_tiles_max)
    return _combine_call(x, route_c, ys, sched, mod_all, layer, final_g, split, mod_row)


def _grid_pos_embed(n_tokens):
    t = jnp.arange(n_tokens)
    r = (t // GRID_W).astype(F32)
    col = (t % GRID_W).astype(F32)
    quarter = D_MODEL // 4
    omega = 1.0 / (POS_BASE ** (jnp.arange(quarter, dtype=F32) / quarter))
    ar = r[:, None] * omega[None, :]
    ac = col[:, None] * omega[None, :]
    return jnp.concatenate([jnp.sin(ar), jnp.cos(ar), jnp.sin(ac), jnp.cos(ac)], axis=-1)


def _run_trunk(x, ctx, lat, init_states, mod_all, p):
    n_ctx_blocks = ctx.n * (SEQ_BLOCK // TOKEN_BLOCK)
    lat_per_seq = lat.seq_len // TOKEN_BLOCK

    def cond_row(i):
        return jnp.where(i < n_ctx_blocks, CTX_ROW, (i - n_ctx_blocks) // lat_per_seq)

    states = None
    for i in range(DEPTH):
        j = i // 2
        n1g = p["norm1_g"][i:i + 1]
        if i % 2 == 0:
            hgrn = functools.partial(_hgrn_call, x, mod_all=mod_all, layer=i, n1g=n1g, w_in=p["hgrn_w_in"],
                                     log_lb=p["log_lb"][j], log_1m_lb=p["log_1m_lb"][j],
                                     norm_g=p["hgrn_norm_g"][j:j + 1], j=j)
            mix, states = hgrn(mix=None, grp=ctx, s0=None, states=states)
            mix, _ = hgrn(mix=mix, grp=lat, s0=init_states, states=None)
            w_out = p["hgrn_w_out"]
        else:
            mix = _fourier_call(x, None, mod_all, i, n1g, ctx)
            mix = _fourier_call(x, mix, mod_all, i, n1g, lat)
            w_out = p["fourier_w_out"]
        x, h2, route_t, route_c, counts = _post_call(x, mix, mod_all, i, w_out, j, p["norm2_g"][i:i + 1],
                                                     p["router_w_t"][i], p["router_b"][i], cond_row)
        final_g = p["final_norm_g"] if i == DEPTH - 1 else None
        x = _moe(x, h2, route_t, route_c, counts, mod_all, i, p["moe_w_gate"], p["moe_w_up"],
                 p["moe_w_down"], final_g, n_ctx_blocks, cond_row)
    return x, states


def kernel(x_prompt, x_sample, state_hgrn, c, c_ctx, w_mod, b_mod, norm1_g, norm2_g, hgrn_w_in,
           hgrn_lb_logits, hgrn_norm_g, hgrn_w_out, fourier_w_out, router_group_w, router_group_b,
           router_expert_w, router_expert_b, moe_w_gate, moe_w_up, moe_w_down, final_norm_g):
    batch, seq, _ = x_prompt.shape
    dec_batch, dec_seq, _ = x_sample.shape
    assert dec_batch <= CTX_ROW and seq == ROW_TILE and dec_seq == SEQ_BLOCK

    cond = jnp.zeros((COND_ROWS, D_MODEL), F32).at[:dec_batch].set(c).at[CTX_ROW].set(c_ctx)
    mod_all = _mod_call(cond, w_mod, b_mod).reshape(DEPTH, COND_ROWS, 6, D_MODEL)

    probs = jax.nn.softmax(hgrn_lb_logits.astype(F32), axis=0)
    cs = jnp.cumsum(probs, axis=0)
    lb = cs - cs[0:1]
    router_w = jnp.concatenate([router_group_w, router_expert_w], axis=-1)
    router_b = jnp.concatenate([router_group_b, router_expert_b], axis=-1)
    pad = ROUTER_ROWS - router_w.shape[-1]
    p = {
        "norm1_g": norm1_g, "norm2_g": norm2_g, "hgrn_norm_g": hgrn_norm_g,
        "hgrn_w_in": hgrn_w_in.astype(BF16).reshape(-1, D_MODEL, 5, HEADS, DK).swapaxes(2, 3).reshape(
            -1, D_MODEL, 5 * HEADS * DK),
        "hgrn_w_out": hgrn_w_out.astype(BF16),
        "fourier_w_out": fourier_w_out.astype(BF16),
        "log_lb": jnp.log(lb), "log_1m_lb": jnp.log1p(-lb),
        "router_w_t": jnp.pad(jnp.swapaxes(router_w, 1, 2), ((0, 0), (0, pad), (0, 0))),
        "router_b": jnp.pad(router_b, ((0, 0), (0, pad)))[..., None],
        "moe_w_gate": moe_w_gate, "moe_w_up": moe_w_up, "moe_w_down": moe_w_down,
        "final_norm_g": final_norm_g.reshape(1, D_MODEL),
    }

    t_ctx = batch * seq
    t_lat = dec_batch * dec_seq
    ctx = _Group(0, t_ctx // SEQ_BLOCK, seq, latent=False)
    lat = _Group(t_ctx // SEQ_BLOCK, t_lat // SEQ_BLOCK, dec_seq, latent=True)
    x = _embed_call(x_prompt.reshape(t_ctx, D_MODEL), x_sample.reshape(t_lat, D_MODEL), _grid_pos_embed(dec_seq))
    (y_ctx, y_lat), new_state = _run_trunk(x, ctx, lat, state_hgrn, mod_all, p)
    return (y_ctx.reshape(batch, seq, D_MODEL), y_lat.reshape(dec_batch, dec_seq, D_MODEL), new_state)
```

```python
import functools
import math

import jax
import jax.numpy as jnp
from jax import lax
from jax.experimental import pallas as pl
from jax.experimental.pallas import tpu as pltpu

F32 = jnp.float32
BF16 = jnp.bfloat16
HIGHEST = lax.Precision.HIGHEST

D_MODEL = 1024
DEPTH = 4
GRID_W = 64
HEADS = 8
DK = 128
DV = 128
FOURIER_GROUPS = 4
FOURIER_CG = D_MODEL // FOURIER_GROUPS
MOE_GROUPS = 4
MOE_EPG = 4
MOE_EXPERTS = 16
MOE_D_FF = 512
NORM_EPS = 1e-6
POS_BASE = 10000.0
LOG2_E = 1.0 / math.log(2.0)

COND_ROWS = 8
CTX_ROW = 4
ROUTER_ROWS = 32
LANES = 128
SCAN_CHUNK = 64
SCAN_SUB = 16
GROUP = 4
GROUP_ROWS = GROUP * SCAN_CHUNK
LOCAL_GROUPS = 2
PROJ_TILES = 4
SEQ_BLOCK = 2048
ROW_TILE = 256
TOKEN_BLOCK = 256
POST_BLOCKS = 2
MOE_TOP_K = 2
ROUTE_ROWS = 8
ROW_UNIT = 16
BLOCK_ROWS = 768
FFN_TILE = 512
VMEM_LIMIT = 56 * 1024 * 1024


def _cparams(*sem):
    return pltpu.CompilerParams(dimension_semantics=sem, vmem_limit_bytes=VMEM_LIMIT)


def _silu(x):
    return x * jax.nn.sigmoid(x)


def _norm_mod(x, g, shift, scale):
    ms = jnp.mean(x * x, axis=-1, keepdims=True)
    return (x * lax.rsqrt(ms + NORM_EPS) * g) * (1.0 + scale) + shift


def _mod_body(c_ref, w_ref, b_ref, o_ref):
    s = _silu(c_ref[...])
    o_ref[0] = jnp.dot(s, w_ref[0], precision=HIGHEST, preferred_element_type=F32) + b_ref[0]


def _mod_call(cond, w_mod, b_mod):
    n_col = 6 * D_MODEL // D_MODEL
    return pl.pallas_call(
        _mod_body,
        grid=(DEPTH, n_col),
        in_specs=[
            pl.BlockSpec((COND_ROWS, D_MODEL), lambda i, n: (0, 0)),
            pl.BlockSpec((1, D_MODEL, D_MODEL), lambda i, n: (i, 0, n)),
            pl.BlockSpec((1, 1, D_MODEL), lambda i, n: (i, 0, n)),
        ],
        out_specs=pl.BlockSpec((1, COND_ROWS, D_MODEL), lambda i, n: (i, 0, n)),
        out_shape=jax.ShapeDtypeStruct((DEPTH, COND_ROWS, 6 * D_MODEL), F32),
        compiler_params=_cparams("arbitrary", "arbitrary"),
        name="mod",
    )(cond, w_mod, b_mod.reshape(DEPTH, 1, 6 * D_MODEL))


def _embed_body(xc_ref, xl_ref, p_ref, o_ref, *, n_ctx):
    i = pl.program_id(0)

    @pl.when(i < n_ctx)
    def _():
        o_ref[...] = xc_ref[...]

    @pl.when(i >= n_ctx)
    def _():
        o_ref[...] = xl_ref[...] + p_ref[...]


def _embed_call(x_ctx, x_lat, pos):
    n_ctx = x_ctx.shape[0] // TOKEN_BLOCK
    n_lat = x_lat.shape[0] // TOKEN_BLOCK
    per_seq = pos.shape[0] // TOKEN_BLOCK
    return pl.pallas_call(
        functools.partial(_embed_body, n_ctx=n_ctx),
        grid=(n_ctx + n_lat,),
        in_specs=[
            pl.BlockSpec((TOKEN_BLOCK, D_MODEL), lambda i: (jnp.minimum(i, n_ctx - 1), 0)),
            pl.BlockSpec((TOKEN_BLOCK, D_MODEL), lambda i: (jnp.maximum(i - n_ctx, 0), 0)),
            pl.BlockSpec((TOKEN_BLOCK, D_MODEL), lambda i: (jnp.maximum(i - n_ctx, 0) % per_seq, 0)),
        ],
        out_specs=pl.BlockSpec((TOKEN_BLOCK, D_MODEL), lambda i: (i, 0)),
        out_shape=jax.ShapeDtypeStruct((x_ctx.shape[0] + x_lat.shape[0], D_MODEL), F32),
        compiler_params=_cparams("arbitrary"),
        name="embed",
    )(x_ctx, x_lat, pos)


def _log_f_and_key(z, lb, one_m_lb, log_1m_lb):
    t = jnp.exp(-jnp.abs(z))
    big = 1.0 / (1.0 + t)
    small = t * big
    pos = z >= 0.0
    f = lb + one_m_lb * jnp.where(pos, big, small)
    log_f = jnp.where(f > 0.0, jnp.log(f), log_1m_lb + z)
    return log_f, one_m_lb * jnp.where(pos, small, big)


def _group_masks():
    n = GROUP_ROWS
    t = lax.broadcasted_iota(jnp.int32, (n, n), 0)
    s = lax.broadcasted_iota(jnp.int32, (n, n), 1)
    same_chunk = (t // SCAN_CHUNK) == (s // SCAN_CHUNK)
    same_sub = (t // SCAN_SUB) == (s // SCAN_SUB)
    prefix = (same_chunk & (t >= s)).astype(BF16)
    return prefix, same_chunk, same_sub & (t >= s), same_sub & (t <= s)


def _qk(a, b):
    return lax.dot_general(a, b, (((1,), (1,)), ((), ())), preferred_element_type=F32)


def _split2(x):
    hi = x.astype(BF16)
    return [hi, (x - hi.astype(F32)).astype(BF16)]


def _sum2(s, i):
    o = 2 * i * DK
    return s[:, o:o + DK] + s[:, o + DK:o + 2 * DK]


def _group_dir(b, q, k, v, vt, m_chunk, m_diag, reverse):
    c, sb = SCAN_CHUNK, SCAN_SUB
    nb = c // sb
    zero = jnp.zeros((sb, DK), BF16)
    qd, kd, qs, kl, decs = [], [], [], [], []
    q_lev = [[] for _ in range(nb - 1)]
    k_lev = [[] for _ in range(nb - 1)]
    for g in range(GROUP):
        starts = [g * c + i * sb for i in range(nb)]
        pos = [nb - 1 - i for i in range(nb)] if reverse else list(range(nb))
        end_at = {}
        for i, r in enumerate(starts):
            e = r if reverse else r + sb - 1
            end_at[pos[i]] = b[e:e + 1]
        edge = end_at[nb - 1]
        decs.append(jnp.exp2(edge))
        for i, r in enumerate(starts):
            m = r + sb // 2 if reverse else r + sb // 2 - 1
            bi, qi, ki, mid = b[r:r + sb], q[r:r + sb], k[r:r + sb], b[m:m + 1]
            qd.append((qi * jnp.exp2(bi - mid)).astype(BF16))
            kd.append((ki * jnp.exp2(mid - bi)).astype(BF16))
            for j in range(nb - 1):
                q_lev[j].append((qi * jnp.exp2(bi - end_at[j])).astype(BF16) if pos[i] > j else zero)
                k_lev[j].append((ki * jnp.exp2(end_at[j] - bi)).astype(BF16) if pos[i] == j else zero)
            qs.append((qi * jnp.exp2(bi)).astype(BF16))
            kl.append((ki * jnp.exp2(edge - bi)).astype(BF16))

    def rows(blocks):
        return jnp.concatenate(blocks, axis=0)
    s_diag = _qk(rows(qd), rows(kd))
    s_lev = _qk(jnp.concatenate([rows(x) for x in q_lev], axis=1),
                jnp.concatenate([rows(x) for x in k_lev], axis=1))
    scores = jnp.where(m_diag, s_diag, 0.0) + jnp.where(m_chunk, s_lev, 0.0)
    o = jnp.dot(scores.astype(BF16), v, preferred_element_type=F32)
    kl_wide = jnp.concatenate(
        [rows([blk if n // nb == g else zero for n, blk in enumerate(kl)]) for g in range(GROUP)], axis=1)
    upd_t = jnp.dot(vt, kl_wide, preferred_element_type=F32)
    return o, rows(qs), upd_t, decs


def _hgrn_body(*refs, n_in, n_seq, cps, has_s0):
    x_ref, mod_ref, n1g_ref, w_ref, lb_ref, l1m_ref, ng_ref = refs[:7]
    s0_ref = refs[7] if has_s0 else None
    og_ref = refs[n_in]
    st_ref = None if has_s0 else refs[n_in + 1]
    (h_scr, q_scr, v_scr, g_scr, lff_scr, kf_scr, lfb_scr, kb_scr, of_scr, ob_scr,
     qsf_scr, qsb_scr, uf_scr, ub_scr, decf_scr, decb_scr, sf_scr, sb_scr) = refs[n_in + (1 if has_s0 else 2):]
    c = SCAN_CHUNK
    n_chunks = SEQ_BLOCK // c
    n_tiles = SEQ_BLOCK // ROW_TILE

    @pl.when(pl.program_id(1) == 0)
    def _():
        m = mod_ref[0]
        g = n1g_ref[...]

        def tile(t, carry):
            rows = pl.ds(pl.multiple_of(t * ROW_TILE, ROW_TILE), ROW_TILE)
            h_scr[rows, :] = _norm_mod(x_ref[rows, :], g, m[0:1], m[1:2]).astype(BF16)
            return carry
        lax.fori_loop(0, n_tiles, tile, 0)

    lb = lb_ref[...]
    l1m = l1m_ref[...]

    def proj_tiles(g, carry):
        tiles = []
        for u in range(PROJ_TILES):
            rows = pl.ds(pl.multiple_of((PROJ_TILES * g + u) * ROW_TILE, ROW_TILE), ROW_TILE)
            tiles.append((rows, h_scr[rows, :]))
        outs = []
        for rows, h in tiles:
            p = jnp.dot(h, w_ref[...], preferred_element_type=F32)
            outs.append((rows, p, _log_f_and_key(p[:, DK:2 * DK], lb[0:1], lb[2:3], l1m[0:1]),
                         _log_f_and_key(p[:, 2 * DK:3 * DK], lb[1:2], lb[3:4], l1m[1:2])))
        for rows, p, (lf_f, k_f), (lf_b, k_b) in outs:
            q_scr[rows, :] = p[:, 0:DK]
            v_scr[rows, :] = p[:, 3 * DK:3 * DK + DV]
            g_scr[rows, :] = p[:, 3 * DK + DV:3 * DK + 2 * DV]
            lff_scr[rows, :] = lf_f
            kf_scr[rows, :] = k_f
            lfb_scr[rows, :] = lf_b
            kb_scr[rows, :] = k_b
        return carry
    lax.fori_loop(0, n_tiles // PROJ_TILES, proj_tiles, 0)

    prefix, m_chunk, m_diag_f, m_diag_b = _group_masks()

    def group_rows(grp):
        return pl.ds(pl.multiple_of(grp * GROUP_ROWS, GROUP_ROWS), GROUP_ROWS)

    def local_load(grp):
        rows = group_rows(grp)
        return (q_scr[rows, :], v_scr[rows, :], lff_scr[rows, :], kf_scr[rows, :], lfb_scr[rows, :],
                kb_scr[rows, :])

    def local_compute(q, v32, lf_f, k_f, lf_b, k_b):
        v = v32.astype(BF16)
        vt = v32.T.astype(BF16)
        sums = jnp.dot(prefix, jnp.concatenate(_split2(lf_f) + _split2(lf_b), axis=1),
                       preferred_element_type=F32)
        b_f = _sum2(sums, 0) * LOG2_E
        pre_b = _sum2(sums, 1)
        total_b = jnp.concatenate(
            [jnp.broadcast_to(pre_b[g * c + c - 1:g * c + c], (c, DK)) for g in range(GROUP)], axis=0)
        b_b = ((total_b - pre_b) + lf_b) * LOG2_E
        return (_group_dir(b_f, q, k_f, v, vt, m_chunk, m_diag_f, reverse=False),
                _group_dir(b_b, q, k_b, v, vt, m_chunk, m_diag_b, reverse=True))

    def local_store(grp, fwd, bwd):
        rows = group_rows(grp)
        for (o, qs, upd_t, dec), o_scr, qs_scr, u_scr, dec_scr in (
                (fwd, of_scr, qsf_scr, uf_scr, decf_scr), (bwd, ob_scr, qsb_scr, ub_scr, decb_scr)):
            o_scr[rows, :] = o
            qs_scr[rows, :] = qs
            for g in range(GROUP):
                u_scr[grp * GROUP + g] = upd_t[:, g * DK:(g + 1) * DK]
                dec_scr[grp * GROUP + g] = dec[g]

    def local(it, carry):
        groups = [it * LOCAL_GROUPS + u for u in range(LOCAL_GROUPS)]
        loaded = [local_load(grp) for grp in groups]
        results = [local_compute(*vals) for vals in loaded]
        for grp, (fwd, bwd) in zip(groups, results):
            local_store(grp, fwd, bwd)
        return carry
    lax.fori_loop(0, n_chunks // GROUP // LOCAL_GROUPS, local, 0)

    def advance(ci, cj, sf, sb):
        sf_scr[ci] = sf.astype(BF16)
        sb_scr[cj] = sb.astype(BF16)
        return sf * decf_scr[ci] + uf_scr[ci], sb * decb_scr[cj] + ub_scr[cj]

    if has_s0:
        def step(i, carry):
            return advance(i, n_chunks - 1 - i, *carry)
        lax.fori_loop(0, n_chunks, step, (s0_ref[0].T, s0_ref[1].T), unroll=4)
    else:
        def seq(s, carry):
            sf = jnp.zeros((DV, DK), F32)
            sb = jnp.zeros((DV, DK), F32)
            for i in range(cps):
                sf, sb = advance(s * cps + i, s * cps + cps - 1 - i, sf, sb)
            st_ref[s, 0] = sf.T
            st_ref[s, 1] = sb.T
            return carry
        lax.fori_loop(0, n_seq, seq, 0)

    ng = ng_ref[...]
    zero_chunk = jnp.zeros((c, DK), BF16)

    def widen(qs):
        return jnp.concatenate(
            [jnp.concatenate([qs[h * c:(h + 1) * c] if h == g else zero_chunk for h in range(GROUP)], axis=0)
             for g in range(GROUP)], axis=1)

    def out_groups(it, carry):
        loaded = []
        for u in range(LOCAL_GROUPS):
            grp = it * LOCAL_GROUPS + u
            rows = group_rows(grp)
            qs = jnp.concatenate([widen(qsf_scr[rows, :]), widen(qsb_scr[rows, :])], axis=1)
            st = jnp.concatenate([sf_scr[grp * GROUP + g] for g in range(GROUP)]
                                 + [sb_scr[grp * GROUP + g] for g in range(GROUP)], axis=1)
            loaded.append((rows, qs, st, of_scr[rows, :] + ob_scr[rows, :], g_scr[rows, :]))
        for rows, qs, st, o_local, gate in loaded:
            o = o_local + _qk(qs, st)
            o = o * lax.rsqrt(jnp.mean(o * o, axis=-1, keepdims=True) + NORM_EPS) * ng
            og_ref[rows, :] = (o * _silu(gate)).astype(BF16)
        return carry
    lax.fori_loop(0, n_chunks // GROUP // LOCAL_GROUPS, out_groups, 0)


class _Group:
    def __init__(self, first, n, seq_len, latent):
        self.first, self.n, self.seq_len, self.latent = first, n, seq_len, latent

    def cond_row(self, block):
        return block if self.latent else CTX_ROW


def _hgrn_call(x, mix, mod_all, layer, n1g, w_in, lb, log_1m_lb, norm_g, j, grp, s0, states):
    t = x.shape[0]
    n_seq = SEQ_BLOCK // grp.seq_len
    cps = grp.seq_len // SCAN_CHUNK
    n_chunks = SEQ_BLOCK // SCAN_CHUNK
    has_s0 = s0 is not None
    n_layers = w_in.shape[0]
    wide = 3 * DK + 2 * DV

    in_specs = [
        pl.BlockSpec((SEQ_BLOCK, D_MODEL), lambda b, h: (b + grp.first, 0)),
        pl.BlockSpec((None, 1, 6, D_MODEL), lambda b, h: (layer, grp.cond_row(b), 0, 0)),
        pl.BlockSpec((1, D_MODEL), lambda b, h: (0, 0)),
        pl.BlockSpec((None, D_MODEL, wide), lambda b, h: (j, 0, h)),
        pl.BlockSpec((4, DK), lambda b, h: (0, h)),
        pl.BlockSpec((2, DK), lambda b, h: (0, h)),
        pl.BlockSpec((1, DV), lambda b, h: (0, h)),
    ]
    args = [x, mod_all, n1g, w_in, lb, log_1m_lb, norm_g]
    og_spec = pl.BlockSpec((SEQ_BLOCK, DV), lambda b, h: (b + grp.first, h))
    og_shape = jax.ShapeDtypeStruct((t, D_MODEL), BF16)
    aliases = {}
    if has_s0:
        in_specs.append(pl.BlockSpec((None, None, 2, None, DK, DV), lambda b, h: (b, j, 0, h, 0, 0)))
        args.append(s0)
        out_specs, out_shape = og_spec, og_shape
    else:
        st_shape = (grp.n * n_seq, n_layers, 2, HEADS, DK, DV)
        out_specs = (og_spec,
                     pl.BlockSpec((n_seq, None, 2, None, DK, DV), lambda b, h: (b, j, 0, h, 0, 0)))
        out_shape = (og_shape, jax.ShapeDtypeStruct(st_shape, F32))
        if states is not None:
            in_specs.append(pl.BlockSpec(memory_space=pl.ANY))
            args.append(states)
            aliases[len(args) - 1] = 1
    if mix is not None:
        in_specs.append(pl.BlockSpec(memory_space=pl.ANY))
        args.append(mix)
        aliases[len(args) - 1] = 0
    col = pltpu.VMEM((SEQ_BLOCK, DK), F32)
    colb = pltpu.VMEM((SEQ_BLOCK, DK), BF16)
    upd = pltpu.VMEM((n_chunks, DV, DK), F32)
    dec = pltpu.VMEM((n_chunks, 1, DK), F32)
    start = pltpu.VMEM((n_chunks, DV, DK), BF16)
    res = pl.pallas_call(
        functools.partial(_hgrn_body, n_in=len(args), n_seq=n_seq, cps=cps, has_s0=has_s0),
        grid=(grp.n, HEADS),
        in_specs=in_specs,
        out_specs=out_specs,
        out_shape=out_shape,
        scratch_shapes=[pltpu.VMEM((SEQ_BLOCK, D_MODEL), BF16)] + [col] * 9 + [colb, colb, upd, upd, dec, dec,
                                                                                  start, start],
        input_output_aliases=aliases,
        compiler_params=_cparams("arbitrary", "arbitrary"),
        name="hgrn_lat" if has_s0 else "hgrn_ctx",
    )(*args)
    return (res, None) if has_s0 else res


def _fourier_body(*refs, n_in, seq_len):
    x_ref, mod_ref, n1g_ref, cl_ref, sl_ref, cc_ref = refs[:6]
    z_ref, h_scr = refs[n_in:]
    rt = pl.program_id(1)
    n_tiles = SEQ_BLOCK // ROW_TILE

    @pl.when(rt == 0)
    def _():
        m = mod_ref[0]
        g = n1g_ref[...]

        def tile(t, carry):
            rows = pl.ds(pl.multiple_of(t * ROW_TILE, ROW_TILE), ROW_TILE)
            h_scr[rows, :] = _norm_mod(x_ref[rows, :], g, m[0:1], m[1:2]).astype(BF16)
            return carry
        lax.fori_loop(0, n_tiles, tile, 0)

    if seq_len == ROW_TILE:
        h = h_scr[pl.ds(pl.multiple_of(rt * ROW_TILE, ROW_TILE), ROW_TILE), :]
    else:
        h = h_scr[...]
    zc = jnp.dot(cl_ref[...], h, preferred_element_type=F32).astype(BF16)
    zs = jnp.dot(sl_ref[...], h, preferred_element_type=F32).astype(BF16)
    scale = 1.0 / math.sqrt(seq_len * FOURIER_CG)
    cg = FOURIER_CG
    for g in range(FOURIER_GROUPS):
        cat = jnp.concatenate([zc[:, g * cg:(g + 1) * cg], zs[:, g * cg:(g + 1) * cg]], axis=1)
        out = jnp.dot(cat, cc_ref[...], preferred_element_type=F32) * scale
        z_ref[:, g * cg:(g + 1) * cg] = out.astype(BF16)


def _dft_tables(n):
    j = jnp.arange(n, dtype=jnp.int32)
    ang = ((j[:, None] * j[None, :]) % n).astype(F32) * (2.0 * math.pi / n)
    return jnp.cos(ang), jnp.sin(ang)


def _fourier_call(x, mix, mod_all, layer, n1g, grp):
    t = x.shape[0]
    seq_len = grp.seq_len
    n_rt = SEQ_BLOCK // ROW_TILE
    cl, sl = _dft_tables(seq_len)
    cc, sc = _dft_tables(FOURIER_CG)
    cc2 = jnp.concatenate([cc, -sc], axis=0).astype(BF16)
    if seq_len == ROW_TILE:
        pos_spec = pl.BlockSpec((ROW_TILE, seq_len), lambda b, r: (0, 0))
    else:
        pos_spec = pl.BlockSpec((ROW_TILE, seq_len), lambda b, r: (r, 0))
    in_specs = [
        pl.BlockSpec((SEQ_BLOCK, D_MODEL), lambda b, r: (b + grp.first, 0)),
        pl.BlockSpec((None, 1, 6, D_MODEL), lambda b, r: (layer, grp.cond_row(b), 0, 0)),
        pl.BlockSpec((1, D_MODEL), lambda b, r: (0, 0)),
        pos_spec, pos_spec,
        pl.BlockSpec((2 * FOURIER_CG, FOURIER_CG), lambda b, r: (0, 0)),
    ]
    args = [x, mod_all, n1g, cl.astype(BF16), sl.astype(BF16), cc2]
    aliases = {}
    if mix is not None:
        in_specs.append(pl.BlockSpec(memory_space=pl.ANY))
        args.append(mix)
        aliases[len(args) - 1] = 0
    return pl.pallas_call(
        functools.partial(_fourier_body, n_in=len(args), seq_len=seq_len),
        grid=(grp.n, n_rt),
        in_specs=in_specs,
        out_specs=pl.BlockSpec((ROW_TILE, D_MODEL), lambda b, r: ((b + grp.first) * n_rt + r, 0)),
        out_shape=jax.ShapeDtypeStruct((t, D_MODEL), BF16),
        scratch_shapes=[pltpu.VMEM((SEQ_BLOCK, D_MODEL), BF16)],
        input_output_aliases=aliases,
        compiler_params=_cparams("arbitrary", "arbitrary"),
        name="fourier_lat" if grp.latent else "fourier_ctx",
    )(*args)


def _route(lg):
    grp = [lg[g:g + 1] for g in range(MOE_GROUPS)]
    gmax = functools.reduce(jnp.maximum, grp)
    gi = jnp.where(grp[0] == gmax, 0, jnp.where(grp[1] == gmax, 1, jnp.where(grp[2] == gmax, 2, 3)))
    pgv = 1.0 / functools.reduce(lambda a, b: a + b, [jnp.exp(g - gmax) for g in grp])
    sel = []
    for j in range(MOE_EPG):
        rows = [lg[MOE_GROUPS + g * MOE_EPG + j:MOE_GROUPS + g * MOE_EPG + j + 1] for g in range(MOE_GROUPS)]
        sel.append(jnp.where(gi == 0, rows[0], jnp.where(gi == 1, rows[1], jnp.where(gi == 2, rows[2], rows[3]))))
    m1 = functools.reduce(jnp.maximum, sel)
    i1 = jnp.where(sel[0] == m1, 0, jnp.where(sel[1] == m1, 1, jnp.where(sel[2] == m1, 2, 3)))
    rest = [jnp.where(i1 == j, -jnp.inf, sel[j]) for j in range(MOE_EPG)]
    m2 = functools.reduce(jnp.maximum, rest)
    i2 = jnp.where(rest[0] == m2, 0, jnp.where(rest[1] == m2, 1, jnp.where(rest[2] == m2, 2, 3)))
    e2 = jnp.exp(m2 - m1)
    w1 = pgv / (1.0 + e2)
    w2 = pgv * e2 / (1.0 + e2)
    r = lg.shape[1]
    ex1 = gi * MOE_EPG + i1
    ex2 = gi * MOE_EPG + i2
    eid = lax.broadcasted_iota(jnp.int32, (MOE_EXPERTS, r), 0)
    member = (eid == ex1) | (eid == ex2)
    t0 = lax.broadcasted_iota(jnp.int32, (r, r), 0)
    t1 = lax.broadcasted_iota(jnp.int32, (r, r), 1)
    rank = jnp.dot(member.astype(BF16), (t0 < t1).astype(BF16), preferred_element_type=F32)
    count = jnp.sum(member.astype(F32), axis=1, keepdims=True)
    padded = jnp.floor((count + (ROW_UNIT - 1)) * (1.0 / ROW_UNIT)) * ROW_UNIT
    e0 = lax.broadcasted_iota(jnp.int32, (MOE_EXPERTS, MOE_EXPERTS), 0)
    e1 = lax.broadcasted_iota(jnp.int32, (MOE_EXPERTS, MOE_EXPERTS), 1)
    start = jnp.dot((e1 < e0).astype(F32), jnp.broadcast_to(padded, (MOE_EXPERTS, LANES)),
                    precision=HIGHEST, preferred_element_type=F32)[:, 0:1]
    row = start + rank
    row1 = jnp.sum(jnp.where(eid == ex1, row, 0.0), axis=0, keepdims=True)
    row2 = jnp.sum(jnp.where(eid == ex2, row, 0.0), axis=0, keepdims=True)
    rid = lax.broadcasted_iota(jnp.int32, (LANES, r), 0)
    fields = (row1, row2, w1, w2, ex1.astype(F32), ex2.astype(F32))
    table = jnp.zeros((LANES, r), F32)
    for i, f in enumerate(fields):
        table = jnp.where(rid == i, f, table)
    return table, count


def _post_body(x_ref, mix_ref, mod_ref, w_ref, n2g_ref, wr_ref, br_ref, xo_ref, h2_ref, rt_ref, ct_ref,
               cnt_ref):
    m = mod_ref[0]
    out = jnp.dot(mix_ref[...], w_ref[...], preferred_element_type=F32)
    xn = x_ref[...] + m[2:3] * out
    xo_ref[...] = xn
    h2 = _norm_mod(xn, n2g_ref[...], m[3:4], m[4:5])
    h2_ref[...] = h2.astype(BF16)
    lg = lax.dot_general(wr_ref[...], h2, (((1,), (1,)), ((), ())), precision=HIGHEST,
                         preferred_element_type=F32) + br_ref[...]
    for u in range(POST_BLOCKS):
        cols = slice(u * TOKEN_BLOCK, (u + 1) * TOKEN_BLOCK)
        table, count = _route(lg[:, cols])
        rt_ref[:, cols] = table[0:ROUTE_ROWS]
        ct_ref[cols, :] = table.T
        cnt_ref[u] = jnp.broadcast_to(count, (MOE_EXPERTS, LANES))


def _post_call(x, mix, mod_all, layer, w_out, j, n2g, wr_t, br, mod_row):
    t = x.shape[0]
    nb = t // TOKEN_BLOCK
    rows = POST_BLOCKS * TOKEN_BLOCK
    tile = lambda i: (i, 0)
    full = lambda i: (0, 0)
    return pl.pallas_call(
        _post_body,
        grid=(nb // POST_BLOCKS,),
        in_specs=[
            pl.BlockSpec((rows, D_MODEL), tile),
            pl.BlockSpec((rows, D_MODEL), tile),
            pl.BlockSpec((None, 1, 6, D_MODEL), lambda i: (layer, mod_row(i * POST_BLOCKS), 0, 0)),
            pl.BlockSpec((None, D_MODEL, D_MODEL), lambda i: (j, 0, 0)),
            pl.BlockSpec((1, D_MODEL), full),
            pl.BlockSpec((ROUTER_ROWS, D_MODEL), full),
            pl.BlockSpec((ROUTER_ROWS, 1), full),
        ],
        out_specs=(
            pl.BlockSpec((rows, D_MODEL), tile),
            pl.BlockSpec((rows, D_MODEL), tile),
            pl.BlockSpec((ROUTE_ROWS, rows), lambda i: (0, i)),
            pl.BlockSpec((rows, LANES), tile),
            pl.BlockSpec((POST_BLOCKS, MOE_EXPERTS, LANES), lambda i: (i, 0, 0)),
        ),
        out_shape=(
            jax.ShapeDtypeStruct((t, D_MODEL), F32),
            jax.ShapeDtypeStruct((t, D_MODEL), BF16),
            jax.ShapeDtypeStruct((ROUTE_ROWS, t), F32),
            jax.ShapeDtypeStruct((t, LANES), F32),
            jax.ShapeDtypeStruct((nb, MOE_EXPERTS, LANES), F32),
        ),
        compiler_params=_cparams("arbitrary"),
        name="post",
    )(x, mix, mod_all, w_out, n2g, wr_t, br)


def _moe_schedule(counts, n_tiles_max):
    units = (counts + ROW_UNIT - 1) // ROW_UNIT
    local = jnp.cumsum(units, axis=1) - units
    total = jnp.sum(units, axis=0)
    per_tile = FFN_TILE // ROW_UNIT
    tiles = (total + per_tile - 1) // per_tile
    region = tiles * per_tile
    region_start = jnp.cumsum(region) - region
    glob = region_start[None, :] + jnp.cumsum(units, axis=0) - units
    tile_end = jnp.cumsum(tiles)
    n_tiles = tile_end[-1]
    s = jnp.minimum(jnp.arange(n_tiles_max, dtype=jnp.int32), n_tiles - 1)
    tile_expert = jnp.sum((tile_end[None, :] <= s[:, None]).astype(jnp.int32), axis=1)
    first = (s == (tile_end - tiles)[tile_expert]) & (jnp.arange(n_tiles_max) < n_tiles)
    i32 = lambda a: a.astype(jnp.int32).reshape(-1)
    return {
        "local": i32(local), "units": i32(units), "glob": i32(glob),
        "fill_start": i32(region_start + total), "fill_units": i32(region - total),
        "tile_expert": tile_expert, "tile_first": i32(first), "n_tiles": i32(n_tiles),
    }


def _segment_copies(local_ref, units_ref, glob_ref, blk, make_copy):
    def per_expert(e, n):
        idx = blk * MOE_EXPERTS + e
        loc, cnt, glo = local_ref[idx], units_ref[idx], glob_ref[idx]

        def unit(u, carry):
            make_copy(pl.multiple_of((loc + u) * ROW_UNIT, ROW_UNIT),
                      pl.multiple_of((glo + u) * ROW_UNIT, ROW_UNIT)).start()
            return carry
        lax.fori_loop(0, cnt, unit, 0)
        return n + cnt
    return lax.fori_loop(0, MOE_EXPERTS, per_expert, 0)


def _wait_copies(n, make_copy):
    def one(u, carry):
        make_copy(0, 0).wait()
        return carry
    lax.fori_loop(0, n, one, 0)


def _dispatch_body(local_ref, units_ref, glob_ref, fill_start_ref, fill_units_ref,
                   h2_ref, rt_ref, xs_ref, rows_scr, zero_scr, sem):
    blk = pl.program_id(0)
    last = pl.num_programs(0) - 1
    slot = blk % 2
    table = rt_ref[...]
    row1 = table[0:1].astype(jnp.int32)
    row2 = table[1:2].astype(jnp.int32)
    rid = lax.broadcasted_iota(jnp.int32, (BLOCK_ROWS, TOKEN_BLOCK), 0)
    onehot = ((rid == row1) | (rid == row2)).astype(BF16)
    rows_scr[slot] = jnp.dot(onehot, h2_ref[...], preferred_element_type=F32).astype(BF16)

    def copier(s):
        def copy(src_row, dst_row):
            return pltpu.make_async_copy(rows_scr.at[s, pl.ds(src_row, ROW_UNIT)],
                                         xs_ref.at[pl.ds(dst_row, ROW_UNIT)], sem.at[s])
        return copy

    def n_copies(b):
        return lax.fori_loop(0, MOE_EXPERTS, lambda e, n: n + units_ref[b * MOE_EXPERTS + e], 0)

    _segment_copies(local_ref, units_ref, glob_ref, blk, copier(slot))

    @pl.when(blk > 0)
    def _():
        _wait_copies(n_copies(blk - 1), copier(1 - slot))

    @pl.when(blk == last)
    def _():
        _wait_copies(n_copies(blk), copier(slot))
        zero_scr[...] = jnp.zeros_like(zero_scr)

        def fill(dst_row):
            return pltpu.make_async_copy(zero_scr, xs_ref.at[pl.ds(dst_row, ROW_UNIT)], sem.at[slot])

        def per_expert(e, n):
            def unit(u, carry):
                fill(pl.multiple_of((fill_start_ref[e] + u) * ROW_UNIT, ROW_UNIT)).start()
                return carry
            lax.fori_loop(0, fill_units_ref[e], unit, 0)
            return n + fill_units_ref[e]
        n_fill = lax.fori_loop(0, MOE_EXPERTS, per_expert, 0)

        def one(u, carry):
            fill(0).wait()
            return carry
        lax.fori_loop(0, n_fill, one, 0)


def _dispatch_call(h2, route_t, sched, n_rows):
    nb = h2.shape[0] // TOKEN_BLOCK
    return pl.pallas_call(
        _dispatch_body,
        grid_spec=pltpu.PrefetchScalarGridSpec(
            num_scalar_prefetch=5,
            grid=(nb,),
            in_specs=[
                pl.BlockSpec((TOKEN_BLOCK, D_MODEL), lambda i, *_: (i, 0)),
                pl.BlockSpec((ROUTE_ROWS, TOKEN_BLOCK), lambda i, *_: (0, i)),
            ],
            out_specs=pl.BlockSpec(memory_space=pl.ANY),
            scratch_shapes=[pltpu.VMEM((2, BLOCK_ROWS, D_MODEL), BF16), pltpu.VMEM((ROW_UNIT, D_MODEL), BF16),
                            pltpu.SemaphoreType.DMA((2,))],
        ),
        out_shape=jax.ShapeDtypeStruct((n_rows, D_MODEL), BF16),
        compiler_params=_cparams("arbitrary"),
        name="moe_dispatch",
    )(sched["local"], sched["units"], sched["glob"], sched["fill_start"], sched["fill_units"], h2, route_t)


def _ffn_body(expert_ref, first_ref, n_ref, x_ref, wg_ref, wu_ref, wd_ref, y_ref, wg_scr, wu_scr, wd_scr):
    s = pl.program_id(0)

    @pl.when(s < n_ref[0])
    def _():
        @pl.when(first_ref[s] == 1)
        def _():
            wg_scr[...] = wg_ref[...].astype(BF16)
            wu_scr[...] = wu_ref[...].astype(BF16)
            wd_scr[...] = wd_ref[...].astype(BF16)
        x = x_ref[...]
        a = jnp.dot(x, wg_scr[...], preferred_element_type=F32)
        u = jnp.dot(x, wu_scr[...], preferred_element_type=F32)
        hid = (_silu(a) * u).astype(BF16)
        y_ref[...] = jnp.dot(hid, wd_scr[...], preferred_element_type=F32).astype(BF16)

    @pl.when(s >= n_ref[0])
    def _():
        y_ref[...] = jnp.zeros_like(y_ref)


def _ffn_call(xs, sched, w_gate, w_up, w_down, layer, n_tiles_max):
    def tile(s, expert, first, n):
        return (jnp.minimum(s, n[0] - 1), 0)

    def weight(s, expert, first, n):
        return (layer, expert[s], 0, 0)
    return pl.pallas_call(
        _ffn_body,
        grid_spec=pltpu.PrefetchScalarGridSpec(
            num_scalar_prefetch=3,
            grid=(n_tiles_max,),
            in_specs=[
                pl.BlockSpec((FFN_TILE, D_MODEL), tile),
                pl.BlockSpec((None, None, D_MODEL, MOE_D_FF), weight),
                pl.BlockSpec((None, None, D_MODEL, MOE_D_FF), weight),
                pl.BlockSpec((None, None, MOE_D_FF, D_MODEL), weight),
            ],
            out_specs=pl.BlockSpec((FFN_TILE, D_MODEL), lambda s, *_: (s, 0)),
            scratch_shapes=[pltpu.VMEM((D_MODEL, MOE_D_FF), BF16), pltpu.VMEM((D_MODEL, MOE_D_FF), BF16),
                            pltpu.VMEM((MOE_D_FF, D_MODEL), BF16)],
        ),
        out_shape=jax.ShapeDtypeStruct(xs.shape, BF16),
        compiler_params=_cparams("arbitrary"),
        name="moe_ffn",
    )(sched["tile_expert"], sched["tile_first"], sched["n_tiles"], xs, w_gate, w_up, w_down)


def _combine_body(*refs, split):
    final = split is not None
    local_ref, units_ref, glob_ref, x_ref, ct_ref, mod_ref = refs[:6]
    fg_ref = refs[6] if final else None
    ys_ref = refs[7 if final else 6]
    out_refs = refs[(8 if final else 7):-2]
    rows_scr, sem = refs[-2:]
    blk = pl.program_id(0)

    slot = blk % 2

    def copier(s):
        def copy(dst_row, src_row):
            return pltpu.make_async_copy(ys_ref.at[pl.ds(src_row, ROW_UNIT)],
                                         rows_scr.at[s, pl.ds(dst_row, ROW_UNIT)], sem.at[s])
        return copy

    @pl.when(blk == 0)
    def _():
        rows_scr[...] = jnp.zeros_like(rows_scr)
        _segment_copies(local_ref, units_ref, glob_ref, blk, copier(0))

    @pl.when(blk + 1 < pl.num_programs(0))
    def _():
        _segment_copies(local_ref, units_ref, glob_ref, blk + 1, copier(1 - slot))

    n_copies = lax.fori_loop(0, MOE_EXPERTS, lambda e, n: n + units_ref[blk * MOE_EXPERTS + e], 0)
    _wait_copies(n_copies, copier(slot))

    table = ct_ref[...]
    row1 = table[:, 0:1].astype(jnp.int32)
    row2 = table[:, 1:2].astype(jnp.int32)
    rid = lax.broadcasted_iota(jnp.int32, (TOKEN_BLOCK, BLOCK_ROWS), 1)
    weights = (jnp.where(rid == row1, table[:, 2:3], 0.0) + jnp.where(rid == row2, table[:, 3:4], 0.0))
    y = jnp.dot(weights.astype(BF16), rows_scr[slot], preferred_element_type=F32)
    out = x_ref[...] + mod_ref[0][5:6] * y
    if not final:
        out_refs[0][...] = out
    else:
        ms = jnp.mean(out * out, axis=-1, keepdims=True)
        out = out * lax.rsqrt(ms + NORM_EPS) * fg_ref[...]

        @pl.when(blk < split)
        def _():
            out_refs[0][...] = out

        @pl.when(blk >= split)
        def _():
            out_refs[1][...] = out


def _combine_call(x, route_c, ys, sched, mod_all, layer, final_g, split, mod_row):
    t = x.shape[0]
    final = final_g is not None
    tok = (TOKEN_BLOCK, D_MODEL)
    if final:
        out_specs = (pl.BlockSpec(tok, lambda i, *_: (jnp.minimum(i, split - 1), 0)),
                     pl.BlockSpec(tok, lambda i, *_: (jnp.maximum(i - split, 0), 0)))
        out_shape = (jax.ShapeDtypeStruct((split * TOKEN_BLOCK, D_MODEL), F32),
                     jax.ShapeDtypeStruct((t - split * TOKEN_BLOCK, D_MODEL), F32))
    else:
        out_specs = pl.BlockSpec(tok, lambda i, *_: (i, 0))
        out_shape = jax.ShapeDtypeStruct((t, D_MODEL), F32)
    in_specs = [
        pl.BlockSpec((TOKEN_BLOCK, D_MODEL), lambda i, *_: (i, 0)),
        pl.BlockSpec((TOKEN_BLOCK, LANES), lambda i, *_: (i, 0)),
        pl.BlockSpec((None, 1, 6, D_MODEL), lambda i, *_: (layer, mod_row(i), 0, 0)),
    ]
    args = [x, route_c, mod_all]
    if final:
        in_specs.append(pl.BlockSpec((1, D_MODEL), lambda i, *_: (0, 0)))
        args.append(final_g)
    in_specs.append(pl.BlockSpec(memory_space=pl.ANY))
    args.append(ys)
    return pl.pallas_call(
        functools.partial(_combine_body, split=split if final else None),
        grid_spec=pltpu.PrefetchScalarGridSpec(
            num_scalar_prefetch=3,
            grid=(t // TOKEN_BLOCK,),
            in_specs=in_specs,
            out_specs=out_specs,
            scratch_shapes=[pltpu.VMEM((2, BLOCK_ROWS, D_MODEL), BF16), pltpu.SemaphoreType.DMA((2,))],
        ),
        out_shape=out_shape,
        compiler_params=_cparams("arbitrary"),
        name="moe_combine",
    )(sched["local"], sched["units"], sched["glob"], *args)


def _moe(x, h2, route_t, route_c, counts, mod_all, layer, w_gate, w_up, w_down, final_g, split, mod_row):
    t = x.shape[0]
    nb = t // TOKEN_BLOCK
    max_rows = MOE_TOP_K * t + nb * MOE_EXPERTS * (ROW_UNIT - 1) + MOE_EXPERTS * (FFN_TILE - 1)
    n_tiles_max = -(-max_rows // FFN_TILE)
    sched = _moe_schedule(counts[:, :, 0].astype(jnp.int32), n_tiles_max)
    xs = _dispatch_call(h2, route_t, sched, n_tiles_max * FFN_TILE)
    ys = _ffn_call(xs, sched, w_gate, w_up, w_down, layer, n_tiles_max)
    return _combine_call(x, route_c, ys, sched, mod_all, layer, final_g, split, mod_row)


def _grid_pos_embed(n_tokens):
    t = jnp.arange(n_tokens)
    r = (t // GRID_W).astype(F32)
    col = (t % GRID_W).astype(F32)
    quarter = D_MODEL // 4
    omega = 1.0 / (POS_BASE ** (jnp.arange(quarter, dtype=F32) / quarter))
    ar = r[:, None] * omega[None, :]
    ac = col[:, None] * omega[None, :]
    return jnp.concatenate([jnp.sin(ar), jnp.cos(ar), jnp.sin(ac), jnp.cos(ac)], axis=-1)


def _run_trunk(x, ctx, lat, init_states, mod_all, p):
    n_ctx_blocks = ctx.n * (SEQ_BLOCK // TOKEN_BLOCK)
    lat_per_seq = lat.seq_len // TOKEN_BLOCK

    def cond_row(i):
        return jnp.where(i < n_ctx_blocks, CTX_ROW, (i - n_ctx_blocks) // lat_per_seq)

    states = None
    for i in range(DEPTH):
        j = i // 2
        n1g = p["norm1_g"][i:i + 1]
        if i % 2 == 0:
            hgrn = functools.partial(_hgrn_call, x, mod_all=mod_all, layer=i, n1g=n1g, w_in=p["hgrn_w_in"],
                                     lb=p["lb"][j], log_1m_lb=p["log_1m_lb"][j],
                                     norm_g=p["hgrn_norm_g"][j:j + 1], j=j)
            mix, states = hgrn(mix=None, grp=ctx, s0=None, states=states)
            mix, _ = hgrn(mix=mix, grp=lat, s0=init_states, states=None)
            w_out = p["hgrn_w_out"]
        else:
            mix = _fourier_call(x, None, mod_all, i, n1g, ctx)
            mix = _fourier_call(x, mix, mod_all, i, n1g, lat)
            w_out = p["fourier_w_out"]
        x, h2, route_t, route_c, counts = _post_call(x, mix, mod_all, i, w_out, j, p["norm2_g"][i:i + 1],
                                                     p["router_w_t"][i], p["router_b"][i], cond_row)
        final_g = p["final_norm_g"] if i == DEPTH - 1 else None
        x = _moe(x, h2, route_t, route_c, counts, mod_all, i, p["moe_w_gate"], p["moe_w_up"],
                 p["moe_w_down"], final_g, n_ctx_blocks, cond_row)
    return x, states


def kernel(x_prompt, x_sample, state_hgrn, c, c_ctx, w_mod, b_mod, norm1_g, norm2_g, hgrn_w_in,
           hgrn_lb_logits, hgrn_norm_g, hgrn_w_out, fourier_w_out, router_group_w, router_group_b,
           router_expert_w, router_expert_b, moe_w_gate, moe_w_up, moe_w_down, final_norm_g):
    batch, seq, _ = x_prompt.shape
    dec_batch, dec_seq, _ = x_sample.shape
    assert dec_batch <= CTX_ROW and seq == ROW_TILE and dec_seq == SEQ_BLOCK

    cond = jnp.zeros((COND_ROWS, D_MODEL), F32).at[:dec_batch].set(c).at[CTX_ROW].set(c_ctx)
    mod_all = _mod_call(cond, w_mod, b_mod).reshape(DEPTH, COND_ROWS, 6, D_MODEL)

    probs = jax.nn.softmax(hgrn_lb_logits.astype(F32), axis=0)
    cs = jnp.cumsum(probs, axis=0)
    lb = cs - cs[0:1]
    router_w = jnp.concatenate([router_group_w, router_expert_w], axis=-1)
    router_b = jnp.concatenate([router_group_b, router_expert_b], axis=-1)
    pad = ROUTER_ROWS - router_w.shape[-1]
    p = {
        "norm1_g": norm1_g, "norm2_g": norm2_g, "hgrn_norm_g": hgrn_norm_g,
        "hgrn_w_in": hgrn_w_in.astype(BF16).reshape(-1, D_MODEL, 5, HEADS, DK).swapaxes(2, 3).reshape(
            -1, D_MODEL, 5 * HEADS * DK),
        "hgrn_w_out": hgrn_w_out.astype(BF16),
        "fourier_w_out": fourier_w_out.astype(BF16),
        "lb": jnp.concatenate([lb, 1.0 - lb], axis=1), "log_1m_lb": jnp.log1p(-lb),
        "router_w_t": jnp.pad(jnp.swapaxes(router_w, 1, 2), ((0, 0), (0, pad), (0, 0))),
        "router_b": jnp.pad(router_b, ((0, 0), (0, pad)))[..., None],
        "moe_w_gate": moe_w_gate, "moe_w_up": moe_w_up, "moe_w_down": moe_w_down,
        "final_norm_g": final_norm_g.reshape(1, D_MODEL),
    }

    t_ctx = batch * seq
    t_lat = dec_batch * dec_seq
    ctx = _Group(0, t_ctx // SEQ_BLOCK, seq, latent=False)
    lat = _Group(t_ctx // SEQ_BLOCK, t_lat // SEQ_BLOCK, dec_seq, latent=True)
    x = _embed_call(x_prompt.reshape(t_ctx, D_MODEL), x_sample.reshape(t_lat, D_MODEL), _grid_pos_embed(dec_seq))
    (y_ctx, y_lat), new_state = _run_trunk(x, ctx, lat, state_hgrn, mod_all, p)
    return (y_ctx.reshape(batch, seq, D_MODEL), y_lat.reshape(dec_batch, dec_seq, D_MODEL), new_state)
```

```python
import functools
import math

import jax
import jax.numpy as jnp
from jax import lax
from jax.experimental import pallas as pl
from jax.experimental.pallas import tpu as pltpu

F32 = jnp.float32
BF16 = jnp.bfloat16
HIGHEST = lax.Precision.HIGHEST

D_MODEL = 1024
DEPTH = 4
GRID_W = 64
HEADS = 8
DK = 128
DV = 128
FOURIER_GROUPS = 4
FOURIER_CG = D_MODEL // FOURIER_GROUPS
MOE_GROUPS = 4
MOE_EPG = 4
MOE_EXPERTS = 16
MOE_D_FF = 512
NORM_EPS = 1e-6
POS_BASE = 10000.0
DFT_SPLIT = 64
LOG2_E = 1.0 / math.log(2.0)

COND_ROWS = 8
CTX_ROW = 4
ROUTER_ROWS = 32
LANES = 128
SCAN_CHUNK = 64
SCAN_SUB = 16
GROUP = 4
GROUP_ROWS = GROUP * SCAN_CHUNK
LOCAL_GROUPS = 2
PROJ_TILES = 4
SEQ_BLOCK = 2048
ROW_TILE = 256
TOKEN_BLOCK = 256
POST_BLOCKS = 4
MOE_TOP_K = 2
ROW_UNIT = 16
BLOCK_ROWS = 768
FFN_TILE = 512
VMEM_LIMIT = 56 * 1024 * 1024


def _cparams(*sem):
    return pltpu.CompilerParams(dimension_semantics=sem, vmem_limit_bytes=VMEM_LIMIT)


def _silu(x):
    return x * jax.nn.sigmoid(x)


def _norm_mod(x, g, shift, scale):
    ms = jnp.mean(x * x, axis=-1, keepdims=True)
    return (x * lax.rsqrt(ms + NORM_EPS) * g) * (1.0 + scale) + shift


def _mod_body(c_ref, w_ref, b_ref, o_ref):
    s = _silu(c_ref[...])
    o_ref[0] = jnp.dot(s, w_ref[0], precision=HIGHEST, preferred_element_type=F32) + b_ref[0]


def _mod_call(cond, w_mod, b_mod):
    n_col = 6 * D_MODEL // D_MODEL
    return pl.pallas_call(
        _mod_body,
        grid=(DEPTH, n_col),
        in_specs=[
            pl.BlockSpec((COND_ROWS, D_MODEL), lambda i, n: (0, 0)),
            pl.BlockSpec((1, D_MODEL, D_MODEL), lambda i, n: (i, 0, n)),
            pl.BlockSpec((1, 1, D_MODEL), lambda i, n: (i, 0, n)),
        ],
        out_specs=pl.BlockSpec((1, COND_ROWS, D_MODEL), lambda i, n: (i, 0, n)),
        out_shape=jax.ShapeDtypeStruct((DEPTH, COND_ROWS, 6 * D_MODEL), F32),
        compiler_params=_cparams("arbitrary", "arbitrary"),
        name="mod",
    )(cond, w_mod, b_mod.reshape(DEPTH, 1, 6 * D_MODEL))


def _embed_body(xc_ref, xl_ref, p_ref, o_ref, *, n_ctx):
    i = pl.program_id(0)

    @pl.when(i < n_ctx)
    def _():
        o_ref[...] = xc_ref[...]

    @pl.when(i >= n_ctx)
    def _():
        o_ref[...] = xl_ref[...] + p_ref[...]


def _embed_call(x_ctx, x_lat, pos):
    n_ctx = x_ctx.shape[0] // TOKEN_BLOCK
    n_lat = x_lat.shape[0] // TOKEN_BLOCK
    per_seq = pos.shape[0] // TOKEN_BLOCK
    return pl.pallas_call(
        functools.partial(_embed_body, n_ctx=n_ctx),
        grid=(n_ctx + n_lat,),
        in_specs=[
            pl.BlockSpec((TOKEN_BLOCK, D_MODEL), lambda i: (jnp.minimum(i, n_ctx - 1), 0)),
            pl.BlockSpec((TOKEN_BLOCK, D_MODEL), lambda i: (jnp.maximum(i - n_ctx, 0), 0)),
            pl.BlockSpec((TOKEN_BLOCK, D_MODEL), lambda i: (jnp.maximum(i - n_ctx, 0) % per_seq, 0)),
        ],
        out_specs=pl.BlockSpec((TOKEN_BLOCK, D_MODEL), lambda i: (i, 0)),
        out_shape=jax.ShapeDtypeStruct((x_ctx.shape[0] + x_lat.shape[0], D_MODEL), F32),
        compiler_params=_cparams("arbitrary"),
        name="embed",
    )(x_ctx, x_lat, pos)


def _log_f_and_key(z, lb, one_m_lb, log_1m_lb):
    t = jnp.exp(-jnp.abs(z))
    big = 1.0 / (1.0 + t)
    small = t * big
    pos = z >= 0.0
    f = lb + one_m_lb * jnp.where(pos, big, small)
    log_f = jnp.where(f > 0.0, jnp.log(f), log_1m_lb + z)
    return log_f, one_m_lb * jnp.where(pos, small, big)


def _group_masks():
    n = GROUP_ROWS
    t = lax.broadcasted_iota(jnp.int32, (n, n), 0)
    s = lax.broadcasted_iota(jnp.int32, (n, n), 1)
    same_chunk = (t // SCAN_CHUNK) == (s // SCAN_CHUNK)
    same_sub = (t // SCAN_SUB) == (s // SCAN_SUB)
    prefix = (same_chunk & (t >= s)).astype(BF16)
    return prefix, same_chunk, same_sub & (t >= s), same_sub & (t <= s)


def _qk(a, b):
    return lax.dot_general(a, b, (((1,), (1,)), ((), ())), preferred_element_type=F32)


def _split2(x):
    hi = x.astype(BF16)
    return [hi, (x - hi.astype(F32)).astype(BF16)]


def _sum2(s, i):
    o = 2 * i * DK
    return s[:, o:o + DK] + s[:, o + DK:o + 2 * DK]


def _group_dir(b, q, k, v, vt, m_chunk, m_diag, reverse):
    c, sb = SCAN_CHUNK, SCAN_SUB
    nb = c // sb
    zero = jnp.zeros((sb, DK), BF16)
    qd, kd, qs, kl, decs = [], [], [], [], []
    q_lev = [[] for _ in range(nb - 1)]
    k_lev = [[] for _ in range(nb - 1)]
    for g in range(GROUP):
        starts = [g * c + i * sb for i in range(nb)]
        pos = [nb - 1 - i for i in range(nb)] if reverse else list(range(nb))
        end_at = {}
        for i, r in enumerate(starts):
            e = r if reverse else r + sb - 1
            end_at[pos[i]] = b[e:e + 1]
        edge = end_at[nb - 1]
        decs.append(jnp.exp2(edge))
        for i, r in enumerate(starts):
            m = r + sb // 2 if reverse else r + sb // 2 - 1
            bi, qi, ki, mid = b[r:r + sb], q[r:r + sb], k[r:r + sb], b[m:m + 1]
            qd.append((qi * jnp.exp2(bi - mid)).astype(BF16))
            kd.append((ki * jnp.exp2(mid - bi)).astype(BF16))
            for j in range(nb - 1):
                q_lev[j].append((qi * jnp.exp2(bi - end_at[j])).astype(BF16) if pos[i] > j else zero)
                k_lev[j].append((ki * jnp.exp2(end_at[j] - bi)).astype(BF16) if pos[i] == j else zero)
            qs.append((qi * jnp.exp2(bi)).astype(BF16))
            kl.append((ki * jnp.exp2(edge - bi)).astype(BF16))

    def rows(blocks):
        return jnp.concatenate(blocks, axis=0)
    s_diag = _qk(rows(qd), rows(kd))
    s_lev = _qk(jnp.concatenate([rows(x) for x in q_lev], axis=1),
                jnp.concatenate([rows(x) for x in k_lev], axis=1))
    scores = jnp.where(m_diag, s_diag, 0.0) + jnp.where(m_chunk, s_lev, 0.0)
    o = jnp.dot(scores.astype(BF16), v, preferred_element_type=F32)
    kl_wide = jnp.concatenate(
        [rows([blk if n // nb == g else zero for n, blk in enumerate(kl)]) for g in range(GROUP)], axis=1)
    upd_t = jnp.dot(vt, kl_wide, preferred_element_type=F32)
    return o, rows(qs), upd_t, decs


def _hgrn_body(*refs, n_in, n_seq, cps, has_s0):
    x_ref, mod_ref, n1g_ref = refs[:3]
    w_refs = refs[3:8]
    lb_ref, l1m_ref, ng_ref = refs[8:11]
    s0_ref = refs[11] if has_s0 else None
    og_ref = refs[n_in]
    st_ref = None if has_s0 else refs[n_in + 1]
    (h_scr, q_scr, v_scr, g_scr, lff_scr, kf_scr, lfb_scr, kb_scr, of_scr, ob_scr, qsf_scr, qsb_scr,
     uf_scr, ub_scr, decf_scr, decb_scr, sf_scr, sb_scr, w_ref) = refs[n_in + (1 if has_s0 else 2):]
    c = SCAN_CHUNK
    n_chunks = SEQ_BLOCK // c
    n_tiles = SEQ_BLOCK // ROW_TILE
    for s, ref in enumerate(w_refs):
        w_ref[:, s * DK:(s + 1) * DK] = ref[...].astype(BF16)

    @pl.when(pl.program_id(1) == 0)
    def _():
        m = mod_ref[0]
        g = n1g_ref[...]

        def tile(t, carry):
            rows = pl.ds(pl.multiple_of(t * ROW_TILE, ROW_TILE), ROW_TILE)
            h_scr[rows, :] = _norm_mod(x_ref[rows, :], g, m[0:1], m[1:2]).astype(BF16)
            return carry
        lax.fori_loop(0, n_tiles, tile, 0)

    lb = lb_ref[...]
    l1m = l1m_ref[...]

    def proj_tiles(g, carry):
        tiles = []
        for u in range(PROJ_TILES):
            rows = pl.ds(pl.multiple_of((PROJ_TILES * g + u) * ROW_TILE, ROW_TILE), ROW_TILE)
            tiles.append((rows, h_scr[rows, :]))
        outs = []
        for rows, h in tiles:
            p = jnp.dot(h, w_ref[...], preferred_element_type=F32)
            outs.append((rows, p, _log_f_and_key(p[:, DK:2 * DK], lb[0:1], lb[2:3], l1m[0:1]),
                         _log_f_and_key(p[:, 2 * DK:3 * DK], lb[1:2], lb[3:4], l1m[1:2])))
        for rows, p, (lf_f, k_f), (lf_b, k_b) in outs:
            q_scr[rows, :] = p[:, 0:DK]
            v_scr[rows, :] = p[:, 3 * DK:3 * DK + DV]
            g_scr[rows, :] = p[:, 3 * DK + DV:3 * DK + 2 * DV]
            lff_scr[rows, :] = lf_f
            kf_scr[rows, :] = k_f
            lfb_scr[rows, :] = lf_b
            kb_scr[rows, :] = k_b
        return carry
    lax.fori_loop(0, n_tiles // PROJ_TILES, proj_tiles, 0)

    prefix, m_chunk, m_diag_f, m_diag_b = _group_masks()

    def group_rows(grp):
        return pl.ds(pl.multiple_of(grp * GROUP_ROWS, GROUP_ROWS), GROUP_ROWS)

    def local_load(grp):
        rows = group_rows(grp)
        return (q_scr[rows, :], v_scr[rows, :], lff_scr[rows, :], kf_scr[rows, :], lfb_scr[rows, :],
                kb_scr[rows, :])

    def local_compute(q, v32, lf_f, k_f, lf_b, k_b):
        v = v32.astype(BF16)
        vt = v32.T.astype(BF16)
        sums = jnp.dot(prefix, jnp.concatenate(_split2(lf_f) + _split2(lf_b), axis=1),
                       preferred_element_type=F32)
        b_f = _sum2(sums, 0) * LOG2_E
        pre_b = _sum2(sums, 1)
        total_b = jnp.concatenate(
            [jnp.broadcast_to(pre_b[g * c + c - 1:g * c + c], (c, DK)) for g in range(GROUP)], axis=0)
        b_b = ((total_b - pre_b) + lf_b) * LOG2_E
        return (_group_dir(b_f, q, k_f, v, vt, m_chunk, m_diag_f, reverse=False),
                _group_dir(b_b, q, k_b, v, vt, m_chunk, m_diag_b, reverse=True))

    def local_store(grp, fwd, bwd):
        rows = group_rows(grp)
        for (o, qs, upd_t, dec), o_scr, qs_scr, u_scr, dec_scr in (
                (fwd, of_scr, qsf_scr, uf_scr, decf_scr), (bwd, ob_scr, qsb_scr, ub_scr, decb_scr)):
            o_scr[rows, :] = o
            qs_scr[rows, :] = qs
            for g in range(GROUP):
                u_scr[grp * GROUP + g] = upd_t[:, g * DK:(g + 1) * DK]
                dec_scr[grp * GROUP + g] = dec[g]

    def local(it, carry):
        groups = [it * LOCAL_GROUPS + u for u in range(LOCAL_GROUPS)]
        loaded = [local_load(grp) for grp in groups]
        results = [local_compute(*vals) for vals in loaded]
        for grp, (fwd, bwd) in zip(groups, results):
            local_store(grp, fwd, bwd)
        return carry
    lax.fori_loop(0, n_chunks // GROUP // LOCAL_GROUPS, local, 0)

    def advance(ci, cj, sf, sb):
        sf_scr[ci] = sf.astype(BF16)
        sb_scr[cj] = sb.astype(BF16)
        return sf * decf_scr[ci] + uf_scr[ci], sb * decb_scr[cj] + ub_scr[cj]

    if has_s0:
        def step(i, carry):
            return advance(i, n_chunks - 1 - i, *carry)
        lax.fori_loop(0, n_chunks, step, (s0_ref[0].T, s0_ref[1].T), unroll=4)
    else:
        def seq(s, carry):
            sf = jnp.zeros((DV, DK), F32)
            sb = jnp.zeros((DV, DK), F32)
            for i in range(cps):
                sf, sb = advance(s * cps + i, s * cps + cps - 1 - i, sf, sb)
            st_ref[s, 0] = sf.T
            st_ref[s, 1] = sb.T
            return carry
        lax.fori_loop(0, n_seq, seq, 0)

    ng = ng_ref[...]
    zero_chunk = jnp.zeros((c, DK), BF16)

    def widen(qs):
        return jnp.concatenate(
            [jnp.concatenate([qs[h * c:(h + 1) * c] if h == g else zero_chunk for h in range(GROUP)], axis=0)
             for g in range(GROUP)], axis=1)

    def out_groups(it, carry):
        loaded = []
        for u in range(LOCAL_GROUPS):
            grp = it * LOCAL_GROUPS + u
            rows = group_rows(grp)
            qs = jnp.concatenate([widen(qsf_scr[rows, :]), widen(qsb_scr[rows, :])], axis=1)
            st = jnp.concatenate([sf_scr[grp * GROUP + g] for g in range(GROUP)]
                                 + [sb_scr[grp * GROUP + g] for g in range(GROUP)], axis=1)
            loaded.append((rows, qs, st, of_scr[rows, :] + ob_scr[rows, :], g_scr[rows, :]))
        for rows, qs, st, o_local, gate in loaded:
            o = o_local + _qk(qs, st)
            o = o * lax.rsqrt(jnp.mean(o * o, axis=-1, keepdims=True) + NORM_EPS) * ng
            og_ref[rows, :] = (o * _silu(gate)).astype(BF16)
        return carry
    lax.fori_loop(0, n_chunks // GROUP // LOCAL_GROUPS, out_groups, 0)


class _Group:
    def __init__(self, first, n, seq_len, latent):
        self.first, self.n, self.seq_len, self.latent = first, n, seq_len, latent

    def cond_row(self, block):
        return block if self.latent else CTX_ROW


def _hgrn_call(x, mix, mod_all, layer, n1g, w_in, lb, log_1m_lb, norm_g, j, grp, s0, states):
    t = x.shape[0]
    n_seq = SEQ_BLOCK // grp.seq_len
    cps = grp.seq_len // SCAN_CHUNK
    n_chunks = SEQ_BLOCK // SCAN_CHUNK
    has_s0 = s0 is not None
    n_layers = w_in.shape[0]
    w_cat = pltpu.VMEM((D_MODEL, 3 * DK + 2 * DV), BF16)

    def section(s):
        return pl.BlockSpec((None, D_MODEL, DK), lambda b, h: (j, 0, s * HEADS + h))

    in_specs = [
        pl.BlockSpec((SEQ_BLOCK, D_MODEL), lambda b, h: (b + grp.first, 0)),
        pl.BlockSpec((None, 1, 6, D_MODEL), lambda b, h: (layer, grp.cond_row(b), 0, 0)),
        pl.BlockSpec((1, D_MODEL), lambda b, h: (0, 0)),
        section(0), section(1), section(2), section(3), section(4),
        pl.BlockSpec((4, DK), lambda b, h: (0, h)),
        pl.BlockSpec((2, DK), lambda b, h: (0, h)),
        pl.BlockSpec((1, DV), lambda b, h: (0, h)),
    ]
    args = [x, mod_all, n1g, w_in, w_in, w_in, w_in, w_in, lb, log_1m_lb, norm_g]
    og_spec = pl.BlockSpec((SEQ_BLOCK, DV), lambda b, h: (b + grp.first, h))
    og_shape = jax.ShapeDtypeStruct((t, D_MODEL), BF16)
    aliases = {}
    if has_s0:
        in_specs.append(pl.BlockSpec((None, None, 2, None, DK, DV), lambda b, h: (b, j, 0, h, 0, 0)))
        args.append(s0)
        out_specs, out_shape = og_spec, og_shape
    else:
        st_shape = (grp.n * n_seq, n_layers, 2, HEADS, DK, DV)
        out_specs = (og_spec,
                     pl.BlockSpec((n_seq, None, 2, None, DK, DV), lambda b, h: (b, j, 0, h, 0, 0)))
        out_shape = (og_shape, jax.ShapeDtypeStruct(st_shape, F32))
        if states is not None:
            in_specs.append(pl.BlockSpec(memory_space=pl.ANY))
            args.append(states)
            aliases[len(args) - 1] = 1
    if mix is not None:
        in_specs.append(pl.BlockSpec(memory_space=pl.ANY))
        args.append(mix)
        aliases[len(args) - 1] = 0
    col = pltpu.VMEM((SEQ_BLOCK, DK), F32)
    colb = pltpu.VMEM((SEQ_BLOCK, DK), BF16)
    upd = pltpu.VMEM((n_chunks, DV, DK), F32)
    dec = pltpu.VMEM((n_chunks, 1, DK), F32)
    start = pltpu.VMEM((n_chunks, DV, DK), BF16)
    res = pl.pallas_call(
        functools.partial(_hgrn_body, n_in=len(args), n_seq=n_seq, cps=cps, has_s0=has_s0),
        grid=(grp.n, HEADS),
        in_specs=in_specs,
        out_specs=out_specs,
        out_shape=out_shape,
        scratch_shapes=[pltpu.VMEM((SEQ_BLOCK, D_MODEL), BF16)] + [col] * 9 + [colb, colb, upd, upd, dec, dec,
                                                                                  start, start, w_cat],
        input_output_aliases=aliases,
        compiler_params=_cparams("arbitrary", "arbitrary"),
        name="hgrn_lat" if has_s0 else "hgrn_ctx",
    )(*args)
    return (res, None) if has_s0 else res


def _fourier_body(*refs, n_in, seq_len):
    x_ref, mod_ref, n1g_ref, cl_ref, sl_ref, cc_ref = refs[:6]
    z_ref, h_scr = refs[n_in:]
    rt = pl.program_id(1)
    n_tiles = SEQ_BLOCK // ROW_TILE

    @pl.when(rt == 0)
    def _():
        m = mod_ref[0]
        g = n1g_ref[...]

        def tile(t, carry):
            rows = pl.ds(pl.multiple_of(t * ROW_TILE, ROW_TILE), ROW_TILE)
            h_scr[rows, :] = _norm_mod(x_ref[rows, :], g, m[0:1], m[1:2]).astype(BF16)
            return carry
        lax.fori_loop(0, n_tiles, tile, 0)

    if seq_len == ROW_TILE:
        h = h_scr[pl.ds(pl.multiple_of(rt * ROW_TILE, ROW_TILE), ROW_TILE), :]
    else:
        h = h_scr[...]
    zc = jnp.dot(cl_ref[...], h, preferred_element_type=F32).astype(BF16)
    zs = jnp.dot(sl_ref[...], h, preferred_element_type=F32).astype(BF16)
    scale = 1.0 / math.sqrt(seq_len * FOURIER_CG)
    cg = FOURIER_CG
    for g in range(FOURIER_GROUPS):
        cat = jnp.concatenate([zc[:, g * cg:(g + 1) * cg], zs[:, g * cg:(g + 1) * cg]], axis=1)
        out = jnp.dot(cat, cc_ref[...], preferred_element_type=F32) * scale
        z_ref[:, g * cg:(g + 1) * cg] = out.astype(BF16)


def _dft_tables(n):
    j = jnp.arange(n, dtype=jnp.int32)

    def direct(k):
        ang = ((j[:, None] * k[None, :]) % n).astype(F32) * (2.0 * math.pi / n)
        return jnp.cos(ang), jnp.sin(ang)
    if n <= DFT_SPLIT:
        return direct(j)
    ca, sa = direct(jnp.arange(n // DFT_SPLIT, dtype=jnp.int32) * DFT_SPLIT)
    cb, sb = direct(jnp.arange(DFT_SPLIT, dtype=jnp.int32))
    cos = ca[:, :, None] * cb[:, None, :] - sa[:, :, None] * sb[:, None, :]
    sin = sa[:, :, None] * cb[:, None, :] + ca[:, :, None] * sb[:, None, :]
    return cos.reshape(n, n), sin.reshape(n, n)


def _fourier_call(x, mix, mod_all, layer, n1g, grp):
    t = x.shape[0]
    seq_len = grp.seq_len
    n_rt = SEQ_BLOCK // ROW_TILE
    cl, sl = _dft_tables(seq_len)
    cc, sc = _dft_tables(FOURIER_CG)
    cc2 = jnp.concatenate([cc, -sc], axis=0).astype(BF16)
    if seq_len == ROW_TILE:
        pos_spec = pl.BlockSpec((ROW_TILE, seq_len), lambda b, r: (0, 0))
    else:
        pos_spec = pl.BlockSpec((ROW_TILE, seq_len), lambda b, r: (r, 0))
    in_specs = [
        pl.BlockSpec((SEQ_BLOCK, D_MODEL), lambda b, r: (b + grp.first, 0)),
        pl.BlockSpec((None, 1, 6, D_MODEL), lambda b, r: (layer, grp.cond_row(b), 0, 0)),
        pl.BlockSpec((1, D_MODEL), lambda b, r: (0, 0)),
        pos_spec, pos_spec,
        pl.BlockSpec((2 * FOURIER_CG, FOURIER_CG), lambda b, r: (0, 0)),
    ]
    args = [x, mod_all, n1g, cl.astype(BF16), sl.astype(BF16), cc2]
    aliases = {}
    if mix is not None:
        in_specs.append(pl.BlockSpec(memory_space=pl.ANY))
        args.append(mix)
        aliases[len(args) - 1] = 0
    return pl.pallas_call(
        functools.partial(_fourier_body, n_in=len(args), seq_len=seq_len),
        grid=(grp.n, n_rt),
        in_specs=in_specs,
        out_specs=pl.BlockSpec((ROW_TILE, D_MODEL), lambda b, r: ((b + grp.first) * n_rt + r, 0)),
        out_shape=jax.ShapeDtypeStruct((t, D_MODEL), BF16),
        scratch_shapes=[pltpu.VMEM((SEQ_BLOCK, D_MODEL), BF16)],
        input_output_aliases=aliases,
        compiler_params=_cparams("arbitrary", "arbitrary"),
        name="fourier_lat" if grp.latent else "fourier_ctx",
    )(*args)


def _route(lg):
    grp = [lg[g:g + 1] for g in range(MOE_GROUPS)]
    gmax = functools.reduce(jnp.maximum, grp)
    gi = jnp.where(grp[0] == gmax, 0, jnp.where(grp[1] == gmax, 1, jnp.where(grp[2] == gmax, 2, 3)))
    pgv = 1.0 / functools.reduce(lambda a, b: a + b, [jnp.exp(g - gmax) for g in grp])
    sel = []
    for j in range(MOE_EPG):
        rows = [lg[MOE_GROUPS + g * MOE_EPG + j:MOE_GROUPS + g * MOE_EPG + j + 1] for g in range(MOE_GROUPS)]
        sel.append(jnp.where(gi == 0, rows[0], jnp.where(gi == 1, rows[1], jnp.where(gi == 2, rows[2], rows[3]))))
    m1 = functools.reduce(jnp.maximum, sel)
    i1 = jnp.where(sel[0] == m1, 0, jnp.where(sel[1] == m1, 1, jnp.where(sel[2] == m1, 2, 3)))
    rest = [jnp.where(i1 == j, -jnp.inf, sel[j]) for j in range(MOE_EPG)]
    m2 = functools.reduce(jnp.maximum, rest)
    i2 = jnp.where(rest[0] == m2, 0, jnp.where(rest[1] == m2, 1, jnp.where(rest[2] == m2, 2, 3)))
    e2 = jnp.exp(m2 - m1)
    w1 = pgv / (1.0 + e2)
    w2 = pgv * e2 / (1.0 + e2)
    r = lg.shape[1]
    ex1 = gi * MOE_EPG + i1
    ex2 = gi * MOE_EPG + i2
    eid = lax.broadcasted_iota(jnp.int32, (MOE_EXPERTS, r), 0)
    member = (eid == ex1) | (eid == ex2)
    t0 = lax.broadcasted_iota(jnp.int32, (r, r), 0)
    t1 = lax.broadcasted_iota(jnp.int32, (r, r), 1)
    rank = jnp.dot(member.astype(BF16), (t0 < t1).astype(BF16), preferred_element_type=F32)
    count = jnp.sum(member.astype(F32), axis=1, keepdims=True)
    padded = jnp.floor((count + (ROW_UNIT - 1)) * (1.0 / ROW_UNIT)) * ROW_UNIT
    e0 = lax.broadcasted_iota(jnp.int32, (MOE_EXPERTS, MOE_EXPERTS), 0)
    e1 = lax.broadcasted_iota(jnp.int32, (MOE_EXPERTS, MOE_EXPERTS), 1)
    start = jnp.dot((e1 < e0).astype(F32), jnp.broadcast_to(padded, (MOE_EXPERTS, LANES)),
                    precision=HIGHEST, preferred_element_type=F32)[:, 0:1]
    row = start + rank
    row1 = jnp.sum(jnp.where(eid == ex1, row, 0.0), axis=0, keepdims=True)
    row2 = jnp.sum(jnp.where(eid == ex2, row, 0.0), axis=0, keepdims=True)
    rid = lax.broadcasted_iota(jnp.int32, (LANES, r), 0)
    fields = (row1, row2, w1, w2, ex1.astype(F32), ex2.astype(F32))
    table = jnp.zeros((LANES, r), F32)
    for i, f in enumerate(fields):
        table = jnp.where(rid == i, f, table)
    return table, count


def _post_body(x_ref, mix_ref, mod_ref, w_ref, n2g_ref, wr_ref, br_ref, xo_ref, xb_ref, ct_ref, cnt_ref):
    m = mod_ref[0]
    out = jnp.dot(mix_ref[...], w_ref[...], preferred_element_type=F32)
    xn = x_ref[...] + m[2:3] * out
    xo_ref[...] = xn
    h2 = _norm_mod(xn, n2g_ref[...], m[3:4], m[4:5])
    lg = lax.dot_general(wr_ref[...], h2, (((1,), (1,)), ((), ())), precision=HIGHEST,
                         preferred_element_type=F32) + br_ref[...]
    h2 = h2.astype(BF16)
    rid = lax.broadcasted_iota(jnp.int32, (BLOCK_ROWS, TOKEN_BLOCK), 0)
    for u in range(POST_BLOCKS):
        cols = slice(u * TOKEN_BLOCK, (u + 1) * TOKEN_BLOCK)
        table, count = _route(lg[:, cols])
        onehot = ((rid == table[0:1].astype(jnp.int32)) | (rid == table[1:2].astype(jnp.int32))).astype(BF16)
        xb_ref[u * BLOCK_ROWS:(u + 1) * BLOCK_ROWS, :] = jnp.dot(
            onehot, h2[cols, :], preferred_element_type=F32).astype(BF16)
        ct_ref[cols, :] = table.T
        cnt_ref[u] = jnp.broadcast_to(count, (MOE_EXPERTS, LANES))


def _post_call(x, mix, mod_all, layer, w_out, j, n2g, wr_t, br, mod_row):
    t = x.shape[0]
    nb = t // TOKEN_BLOCK
    rows = POST_BLOCKS * TOKEN_BLOCK
    tile = lambda i: (i, 0)
    full = lambda i: (0, 0)
    return pl.pallas_call(
        _post_body,
        grid=(nb // POST_BLOCKS,),
        in_specs=[
            pl.BlockSpec((rows, D_MODEL), tile),
            pl.BlockSpec((rows, D_MODEL), tile),
            pl.BlockSpec((None, 1, 6, D_MODEL), lambda i: (layer, mod_row(i * POST_BLOCKS), 0, 0)),
            pl.BlockSpec((None, D_MODEL, D_MODEL), lambda i: (j, 0, 0)),
            pl.BlockSpec((1, D_MODEL), full),
            pl.BlockSpec((ROUTER_ROWS, D_MODEL), full),
            pl.BlockSpec((ROUTER_ROWS, 1), full),
        ],
        out_specs=(
            pl.BlockSpec((rows, D_MODEL), tile),
            pl.BlockSpec((POST_BLOCKS * BLOCK_ROWS, D_MODEL), tile),
            pl.BlockSpec((rows, LANES), tile),
            pl.BlockSpec((POST_BLOCKS, MOE_EXPERTS, LANES), lambda i: (i, 0, 0)),
        ),
        out_shape=(
            jax.ShapeDtypeStruct((t, D_MODEL), F32),
            jax.ShapeDtypeStruct((nb * BLOCK_ROWS, D_MODEL), BF16),
            jax.ShapeDtypeStruct((t, LANES), F32),
            jax.ShapeDtypeStruct((nb, MOE_EXPERTS, LANES), F32),
        ),
        compiler_params=_cparams("arbitrary"),
        name="post",
    )(x, mix, mod_all, w_out, n2g, wr_t, br)


def _moe_schedule(counts, n_tiles_max):
    units = (counts + ROW_UNIT - 1) // ROW_UNIT
    local = jnp.cumsum(units, axis=1) - units
    total = jnp.sum(units, axis=0)
    per_tile = FFN_TILE // ROW_UNIT
    tiles = (total + per_tile - 1) // per_tile
    region = tiles * per_tile
    region_start = jnp.cumsum(region) - region
    glob = region_start[None, :] + jnp.cumsum(units, axis=0) - units
    tile_end = jnp.cumsum(tiles)
    n_tiles = tile_end[-1]
    s = jnp.minimum(jnp.arange(n_tiles_max, dtype=jnp.int32), n_tiles - 1)
    tile_expert = jnp.sum((tile_end[None, :] <= s[:, None]).astype(jnp.int32), axis=1)
    first = (s == (tile_end - tiles)[tile_expert]) & (jnp.arange(n_tiles_max) < n_tiles)
    g = jnp.arange(n_tiles_max * per_tile, dtype=jnp.int32)
    expert = tile_expert[g // per_tile]
    offset = g - region_start[expert]
    ends = jnp.cumsum(units, axis=0)
    blk = jnp.sum((ends[:, expert] <= offset[None, :]).astype(jnp.int32), axis=0)
    blk = jnp.minimum(blk, counts.shape[0] - 1)
    inside = offset - (ends - units)[blk, expert]
    block_units = BLOCK_ROWS // ROW_UNIT
    source = jnp.where(offset < total[expert], blk * block_units + local[blk, expert] + inside, block_units - 1)
    i32 = lambda a: a.astype(jnp.int32).reshape(-1)
    return {
        "local": i32(local), "units": i32(units), "glob": i32(glob), "source": i32(source),
        "tile_expert": tile_expert, "tile_first": i32(first), "n_tiles": i32(n_tiles),
    }


def _segment_copies(local_ref, units_ref, glob_ref, blk, make_copy):
    def per_expert(e, n):
        idx = blk * MOE_EXPERTS + e
        loc, cnt, glo = local_ref[idx], units_ref[idx], glob_ref[idx]

        def unit(u, carry):
            make_copy(pl.multiple_of((loc + u) * ROW_UNIT, ROW_UNIT),
                      pl.multiple_of((glo + u) * ROW_UNIT, ROW_UNIT)).start()
            return carry
        lax.fori_loop(0, cnt, unit, 0)
        return n + cnt
    return lax.fori_loop(0, MOE_EXPERTS, per_expert, 0)


def _wait_copies(n, make_copy):
    def one(u, carry):
        make_copy(0, 0).wait()
        return carry
    lax.fori_loop(0, n, one, 0)


def _ffn_body(expert_ref, first_ref, n_ref, source_ref, xb_ref, wg_ref, wu_ref, wd_ref, y_ref,
              x_scr, sem, wg_scr, wu_scr, wd_scr):
    s = pl.program_id(0)
    slot = s % 2
    per_tile = FFN_TILE // ROW_UNIT

    def gather(tile, buf, start):
        for u in range(per_tile):
            src = source_ref[tile * per_tile + u] if start else 0
            copy = pltpu.make_async_copy(xb_ref.at[pl.ds(pl.multiple_of(src * ROW_UNIT, ROW_UNIT), ROW_UNIT)],
                                         x_scr.at[buf, pl.ds(u * ROW_UNIT, ROW_UNIT)], sem.at[buf])
            if start:
                copy.start()
            else:
                copy.wait()

    @pl.when(s == 0)
    def _():
        gather(0, 0, start=True)

    @pl.when(s + 1 < n_ref[0])
    def _():
        gather(s + 1, 1 - slot, start=True)

    @pl.when(s < n_ref[0])
    def _():
        @pl.when(first_ref[s] == 1)
        def _():
            wg_scr[...] = wg_ref[...].astype(BF16)
            wu_scr[...] = wu_ref[...].astype(BF16)
            wd_scr[...] = wd_ref[...].astype(BF16)
        gather(s, slot, start=False)
        x = x_scr[slot]
        a = jnp.dot(x, wg_scr[...], preferred_element_type=F32)
        u = jnp.dot(x, wu_scr[...], preferred_element_type=F32)
        hid = (_silu(a) * u).astype(BF16)
        y_ref[...] = jnp.dot(hid, wd_scr[...], preferred_element_type=F32).astype(BF16)

    @pl.when(s >= n_ref[0])
    def _():
        y_ref[...] = jnp.zeros_like(y_ref)


def _ffn_call(xb, sched, w_gate, w_up, w_down, layer, n_tiles_max):
    def weight(s, expert, first, n, source):
        return (layer, expert[s], 0, 0)
    return pl.pallas_call(
        _ffn_body,
        grid_spec=pltpu.PrefetchScalarGridSpec(
            num_scalar_prefetch=4,
            grid=(n_tiles_max,),
            in_specs=[
                pl.BlockSpec(memory_space=pl.ANY),
                pl.BlockSpec((None, None, D_MODEL, MOE_D_FF), weight),
                pl.BlockSpec((None, None, D_MODEL, MOE_D_FF), weight),
                pl.BlockSpec((None, None, MOE_D_FF, D_MODEL), weight),
            ],
            out_specs=pl.BlockSpec((FFN_TILE, D_MODEL), lambda s, *_: (s, 0)),
            scratch_shapes=[pltpu.VMEM((2, FFN_TILE, D_MODEL), BF16), pltpu.SemaphoreType.DMA((2,)),
                            pltpu.VMEM((D_MODEL, MOE_D_FF), BF16), pltpu.VMEM((D_MODEL, MOE_D_FF), BF16),
                            pltpu.VMEM((MOE_D_FF, D_MODEL), BF16)],
        ),
        out_shape=jax.ShapeDtypeStruct((n_tiles_max * FFN_TILE, D_MODEL), BF16),
        compiler_params=_cparams("arbitrary"),
        name="moe_ffn",
    )(sched["tile_expert"], sched["tile_first"], sched["n_tiles"], sched["source"], xb, w_gate, w_up, w_down)


def _combine_body(*refs, split):
    final = split is not None
    local_ref, units_ref, glob_ref, x_ref, ct_ref, mod_ref = refs[:6]
    fg_ref = refs[6] if final else None
    ys_ref = refs[7 if final else 6]
    out_refs = refs[(8 if final else 7):-2]
    rows_scr, sem = refs[-2:]
    blk = pl.program_id(0)

    slot = blk % 2

    def copier(s):
        def copy(dst_row, src_row):
            return pltpu.make_async_copy(ys_ref.at[pl.ds(src_row, ROW_UNIT)],
                                         rows_scr.at[s, pl.ds(dst_row, ROW_UNIT)], sem.at[s])
        return copy

    @pl.when(blk == 0)
    def _():
        rows_scr[...] = jnp.zeros_like(rows_scr)
        _segment_copies(local_ref, units_ref, glob_ref, blk, copier(0))

    @pl.when(blk + 1 < pl.num_programs(0))
    def _():
        _segment_copies(local_ref, units_ref, glob_ref, blk + 1, copier(1 - slot))

    n_copies = lax.fori_loop(0, MOE_EXPERTS, lambda e, n: n + units_ref[blk * MOE_EXPERTS + e], 0)
    _wait_copies(n_copies, copier(slot))

    table = ct_ref[...]
    row1 = table[:, 0:1].astype(jnp.int32)
    row2 = table[:, 1:2].astype(jnp.int32)
    rid = lax.broadcasted_iota(jnp.int32, (TOKEN_BLOCK, BLOCK_ROWS), 1)
    weights = (jnp.where(rid == row1, table[:, 2:3], 0.0) + jnp.where(rid == row2, table[:, 3:4], 0.0))
    y = jnp.dot(weights.astype(BF16), rows_scr[slot], preferred_element_type=F32)
    out = x_ref[...] + mod_ref[0][5:6] * y
    if not final:
        out_refs[0][...] = out
    else:
        ms = jnp.mean(out * out, axis=-1, keepdims=True)
        out = out * lax.rsqrt(ms + NORM_EPS) * fg_ref[...]

        @pl.when(blk < split)
        def _():
            out_refs[0][...] = out

        @pl.when(blk >= split)
        def _():
            out_refs[1][...] = out


def _combine_call(x, route_c, ys, sched, mod_all, layer, final_g, split, mod_row):
    t = x.shape[0]
    final = final_g is not None
    tok = (TOKEN_BLOCK, D_MODEL)
    if final:
        out_specs = (pl.BlockSpec(tok, lambda i, *_: (jnp.minimum(i, split - 1), 0)),
                     pl.BlockSpec(tok, lambda i, *_: (jnp.maximum(i - split, 0), 0)))
        out_shape = (jax.ShapeDtypeStruct((split * TOKEN_BLOCK, D_MODEL), F32),
                     jax.ShapeDtypeStruct((t - split * TOKEN_BLOCK, D_MODEL), F32))
    else:
        out_specs = pl.BlockSpec(tok, lambda i, *_: (i, 0))
        out_shape = jax.ShapeDtypeStruct((t, D_MODEL), F32)
    in_specs = [
        pl.BlockSpec((TOKEN_BLOCK, D_MODEL), lambda i, *_: (i, 0)),
        pl.BlockSpec((TOKEN_BLOCK, LANES), lambda i, *_: (i, 0)),
        pl.BlockSpec((None, 1, 6, D_MODEL), lambda i, *_: (layer, mod_row(i), 0, 0)),
    ]
    args = [x, route_c, mod_all]
    if final:
        in_specs.append(pl.BlockSpec((1, D_MODEL), lambda i, *_: (0, 0)))
        args.append(final_g)
    in_specs.append(pl.BlockSpec(memory_space=pl.ANY))
    args.append(ys)
    return pl.pallas_call(
        functools.partial(_combine_body, split=split if final else None),
        grid_spec=pltpu.PrefetchScalarGridSpec(
            num_scalar_prefetch=3,
            grid=(t // TOKEN_BLOCK,),
            in_specs=in_specs,
            out_specs=out_specs,
            scratch_shapes=[pltpu.VMEM((2, BLOCK_ROWS, D_MODEL), BF16), pltpu.SemaphoreType.DMA((2,))],
        ),
        out_shape=out_shape,
        compiler_params=_cparams("arbitrary"),
        name="moe_combine",
    )(sched["local"], sched["units"], sched["glob"], *args)


def _moe(x, xb, route, counts, mod_all, layer, w_gate, w_up, w_down, final_g, split, mod_row):
    t = x.shape[0]
    nb = t // TOKEN_BLOCK
    max_rows = MOE_TOP_K * t + nb * MOE_EXPERTS * (ROW_UNIT - 1) + MOE_EXPERTS * (FFN_TILE - 1)
    n_tiles_max = -(-max_rows // FFN_TILE)
    sched = _moe_schedule(counts[:, :, 0].astype(jnp.int32), n_tiles_max)
    ys = _ffn_call(xb, sched, w_gate, w_up, w_down, layer, n_tiles_max)
    return _combine_call(x, route, ys, sched, mod_all, layer, final_g, split, mod_row)


def _grid_pos_embed(n_tokens):
    t = jnp.arange(n_tokens)
    r = (t // GRID_W).astype(F32)
    col = (t % GRID_W).astype(F32)
    quarter = D_MODEL // 4
    omega = 1.0 / (POS_BASE ** (jnp.arange(quarter, dtype=F32) / quarter))
    ar = r[:, None] * omega[None, :]
    ac = col[:, None] * omega[None, :]
    return jnp.concatenate([jnp.sin(ar), jnp.cos(ar), jnp.sin(ac), jnp.cos(ac)], axis=-1)


def _run_trunk(x, ctx, lat, init_states, mod_all, p):
    n_ctx_blocks = ctx.n * (SEQ_BLOCK // TOKEN_BLOCK)
    lat_per_seq = lat.seq_len // TOKEN_BLOCK

    def cond_row(i):
        return jnp.where(i < n_ctx_blocks, CTX_ROW, (i - n_ctx_blocks) // lat_per_seq)

    states = None
    for i in range(DEPTH):
        j = i // 2
        n1g = p["norm1_g"][i:i + 1]
        if i % 2 == 0:
            hgrn = functools.partial(_hgrn_call, x, mod_all=mod_all, layer=i, n1g=n1g, w_in=p["hgrn_w_in"],
                                     lb=p["lb"][j], log_1m_lb=p["log_1m_lb"][j],
                                     norm_g=p["hgrn_norm_g"][j:j + 1], j=j)
            mix, states = hgrn(mix=None, grp=ctx, s0=None, states=states)
            mix, _ = hgrn(mix=mix, grp=lat, s0=init_states, states=None)
            w_out = p["hgrn_w_out"]
        else:
            mix = _fourier_call(x, None, mod_all, i, n1g, ctx)
            mix = _fourier_call(x, mix, mod_all, i, n1g, lat)
            w_out = p["fourier_w_out"]
        x, xb, route, counts = _post_call(x, mix, mod_all, i, w_out, j, p["norm2_g"][i:i + 1],
                                          p["router_w_t"][i], p["router_b"][i], cond_row)
        final_g = p["final_norm_g"] if i == DEPTH - 1 else None
        x = _moe(x, xb, route, counts, mod_all, i, p["moe_w_gate"], p["moe_w_up"], p["moe_w_down"],
                 final_g, n_ctx_blocks, cond_row)
    return x, states


def kernel(x_prompt, x_sample, state_hgrn, c, c_ctx, w_mod, b_mod, norm1_g, norm2_g, hgrn_w_in,
           hgrn_lb_logits, hgrn_norm_g, hgrn_w_out, fourier_w_out, router_group_w, router_group_b,
           router_expert_w, router_expert_b, moe_w_gate, moe_w_up, moe_w_down, final_norm_g):
    batch, seq, _ = x_prompt.shape
    dec_batch, dec_seq, _ = x_sample.shape
    assert dec_batch <= CTX_ROW and seq == ROW_TILE and dec_seq == SEQ_BLOCK

    cond = jnp.zeros((COND_ROWS, D_MODEL), F32).at[:dec_batch].set(c).at[CTX_ROW].set(c_ctx)
    mod_all = _mod_call(cond, w_mod, b_mod).reshape(DEPTH, COND_ROWS, 6, D_MODEL)

    probs = jax.nn.softmax(hgrn_lb_logits.astype(F32), axis=0)
    cs = jnp.cumsum(probs, axis=0)
    lb = cs - cs[0:1]
    router_w = jnp.concatenate([router_group_w, router_expert_w], axis=-1)
    router_b = jnp.concatenate([router_group_b, router_expert_b], axis=-1)
    pad = ROUTER_ROWS - router_w.shape[-1]
    p = {
        "norm1_g": norm1_g, "norm2_g": norm2_g, "hgrn_norm_g": hgrn_norm_g,
        "hgrn_w_in": hgrn_w_in, "hgrn_w_out": hgrn_w_out.astype(BF16),
        "fourier_w_out": fourier_w_out.astype(BF16),
        "lb": jnp.concatenate([lb, 1.0 - lb], axis=1), "log_1m_lb": jnp.log1p(-lb),
        "router_w_t": jnp.pad(jnp.swapaxes(router_w, 1, 2), ((0, 0), (0, pad), (0, 0))),
        "router_b": jnp.pad(router_b, ((0, 0), (0, pad)))[..., None],
        "moe_w_gate": moe_w_gate, "moe_w_up": moe_w_up, "moe_w_down": moe_w_down,
        "final_norm_g": final_norm_g.reshape(1, D_MODEL),
    }

    t_ctx = batch * seq
    t_lat = dec_batch * dec_seq
    ctx = _Group(0, t_ctx // SEQ_BLOCK, seq, latent=False)
    lat = _Group(t_ctx // SEQ_BLOCK, t_lat // SEQ_BLOCK, dec_seq, latent=True)
    x = _embed_call(x_prompt.reshape(t_ctx, D_MODEL), x_sample.reshape(t_lat, D_MODEL), _grid_pos_embed(dec_seq))
    (y_ctx, y_lat), new_state = _run_trunk(x, ctx, lat, state_hgrn, mod_all, p)
    return (y_ctx.reshape(batch, seq, D_MODEL), y_lat.reshape(dec_batch, dec_seq, D_MODEL), new_state)
```

```python
import functools
import math

import jax
import jax.numpy as jnp
from jax import lax
from jax.experimental import pallas as pl
from jax.experimental.pallas import tpu as pltpu

F32 = jnp.float32
BF16 = jnp.bfloat16
HIGHEST = lax.Precision.HIGHEST

D_MODEL = 1024
DEPTH = 4
GRID_W = 64
HEADS = 8
DK = 128
DV = 128
FOURIER_GROUPS = 4
FOURIER_CG = D_MODEL // FOURIER_GROUPS
MOE_GROUPS = 4
MOE_EPG = 4
MOE_EXPERTS = 16
MOE_D_FF = 512
NORM_EPS = 1e-6
POS_BASE = 10000.0
DFT_SPLIT = 64
LOG2_E = 1.0 / math.log(2.0)

COND_ROWS = 8
CTX_ROW = 4
ROUTER_ROWS = 32
LANES = 128
SCAN_CHUNK = 64
SCAN_SUB = 16
GROUP = 4
GROUP_ROWS = GROUP * SCAN_CHUNK
LOCAL_GROUPS = 2
PROJ_TILES = 4
SEQ_BLOCK = 2048
ROW_TILE = 256
TOKEN_BLOCK = 256
POST_BLOCKS = 4
MOE_TOP_K = 2
ROW_UNIT = 16
BLOCK_ROWS = 768
FFN_TILE = 512
VMEM_LIMIT = 56 * 1024 * 1024


def _cparams(*sem):
    return pltpu.CompilerParams(dimension_semantics=sem, vmem_limit_bytes=VMEM_LIMIT)


def _silu(x):
    return x * jax.nn.sigmoid(x)


def _norm_mod(x, g, shift, scale):
    ms = jnp.mean(x * x, axis=-1, keepdims=True)
    return (x * lax.rsqrt(ms + NORM_EPS) * g) * (1.0 + scale) + shift


def _mod_body(c_ref, w_ref, b_ref, o_ref):
    s = _silu(c_ref[...])
    o_ref[0] = jnp.dot(s, w_ref[0], precision=HIGHEST, preferred_element_type=F32) + b_ref[0]


def _mod_call(cond, w_mod, b_mod):
    n_col = 6 * D_MODEL // D_MODEL
    return pl.pallas_call(
        _mod_body,
        grid=(DEPTH, n_col),
        in_specs=[
            pl.BlockSpec((COND_ROWS, D_MODEL), lambda i, n: (0, 0)),
            pl.BlockSpec((1, D_MODEL, D_MODEL), lambda i, n: (i, 0, n)),
            pl.BlockSpec((1, 1, D_MODEL), lambda i, n: (i, 0, n)),
        ],
        out_specs=pl.BlockSpec((1, COND_ROWS, D_MODEL), lambda i, n: (i, 0, n)),
        out_shape=jax.ShapeDtypeStruct((DEPTH, COND_ROWS, 6 * D_MODEL), F32),
        compiler_params=_cparams("arbitrary", "arbitrary"),
        name="mod",
    )(cond, w_mod, b_mod.reshape(DEPTH, 1, 6 * D_MODEL))


def _embed_body(xc_ref, xl_ref, p_ref, o_ref, *, n_ctx):
    i = pl.program_id(0)

    @pl.when(i < n_ctx)
    def _():
        o_ref[...] = xc_ref[...]

    @pl.when(i >= n_ctx)
    def _():
        o_ref[...] = xl_ref[...] + p_ref[...]


def _embed_call(x_ctx, x_lat, pos):
    n_ctx = x_ctx.shape[0] // TOKEN_BLOCK
    n_lat = x_lat.shape[0] // TOKEN_BLOCK
    per_seq = pos.shape[0] // TOKEN_BLOCK
    return pl.pallas_call(
        functools.partial(_embed_body, n_ctx=n_ctx),
        grid=(n_ctx + n_lat,),
        in_specs=[
            pl.BlockSpec((TOKEN_BLOCK, D_MODEL), lambda i: (jnp.minimum(i, n_ctx - 1), 0)),
            pl.BlockSpec((TOKEN_BLOCK, D_MODEL), lambda i: (jnp.maximum(i - n_ctx, 0), 0)),
            pl.BlockSpec((TOKEN_BLOCK, D_MODEL), lambda i: (jnp.maximum(i - n_ctx, 0) % per_seq, 0)),
        ],
        out_specs=pl.BlockSpec((TOKEN_BLOCK, D_MODEL), lambda i: (i, 0)),
        out_shape=jax.ShapeDtypeStruct((x_ctx.shape[0] + x_lat.shape[0], D_MODEL), F32),
        compiler_params=_cparams("arbitrary"),
        name="embed",
    )(x_ctx, x_lat, pos)


def _log_f_and_key(z, lb, one_m_lb, log_1m_lb):
    t = jnp.exp(-jnp.abs(z))
    big = 1.0 / (1.0 + t)
    small = t * big
    pos = z >= 0.0
    f = lb + one_m_lb * jnp.where(pos, big, small)
    log_f = jnp.where(f > 0.0, jnp.log(f), log_1m_lb + z)
    return log_f, one_m_lb * jnp.where(pos, small, big)


def _group_masks():
    n = GROUP_ROWS
    t = lax.broadcasted_iota(jnp.int32, (n, n), 0)
    s = lax.broadcasted_iota(jnp.int32, (n, n), 1)
    same_chunk = (t // SCAN_CHUNK) == (s // SCAN_CHUNK)
    same_sub = (t // SCAN_SUB) == (s // SCAN_SUB)
    prefix = (same_chunk & (t >= s)).astype(BF16)
    return prefix, same_chunk, same_sub & (t >= s), same_sub & (t <= s)


def _qk(a, b):
    return lax.dot_general(a, b, (((1,), (1,)), ((), ())), preferred_element_type=F32)


def _split2(x):
    hi = x.astype(BF16)
    return [hi, (x - hi.astype(F32)).astype(BF16)]


def _sum2(s, i):
    o = 2 * i * DK
    return s[:, o:o + DK] + s[:, o + DK:o + 2 * DK]


def _group_dir(b, q, k, v, vt, m_chunk, m_diag, reverse):
    c, sb = SCAN_CHUNK, SCAN_SUB
    nb = c // sb
    zero = jnp.zeros((sb, DK), BF16)
    qd, kd, qs, kl, decs = [], [], [], [], []
    q_lev = [[] for _ in range(nb - 1)]
    k_lev = [[] for _ in range(nb - 1)]
    for g in range(GROUP):
        starts = [g * c + i * sb for i in range(nb)]
        pos = [nb - 1 - i for i in range(nb)] if reverse else list(range(nb))
        end_at = {}
        for i, r in enumerate(starts):
            e = r if reverse else r + sb - 1
            end_at[pos[i]] = b[e:e + 1]
        edge = end_at[nb - 1]
        decs.append(jnp.exp2(edge))
        for i, r in enumerate(starts):
            m = r + sb // 2 if reverse else r + sb // 2 - 1
            bi, qi, ki, mid = b[r:r + sb], q[r:r + sb], k[r:r + sb], b[m:m + 1]
            qd.append((qi * jnp.exp2(bi - mid)).astype(BF16))
            kd.append((ki * jnp.exp2(mid - bi)).astype(BF16))
            for j in range(nb - 1):
                q_lev[j].append((qi * jnp.exp2(bi - end_at[j])).astype(BF16) if pos[i] > j else zero)
                k_lev[j].append((ki * jnp.exp2(end_at[j] - bi)).astype(BF16) if pos[i] == j else zero)
            qs.append((qi * jnp.exp2(bi)).astype(BF16))
            kl.append((ki * jnp.exp2(edge - bi)).astype(BF16))

    def rows(blocks):
        return jnp.concatenate(blocks, axis=0)
    s_diag = _qk(rows(qd), rows(kd))
    s_lev = _qk(jnp.concatenate([rows(x) for x in q_lev], axis=1),
                jnp.concatenate([rows(x) for x in k_lev], axis=1))
    scores = jnp.where(m_diag, s_diag, 0.0) + jnp.where(m_chunk, s_lev, 0.0)
    o = jnp.dot(scores.astype(BF16), v, preferred_element_type=F32)
    kl_wide = jnp.concatenate(
        [rows([blk if n // nb == g else zero for n, blk in enumerate(kl)]) for g in range(GROUP)], axis=1)
    upd_t = jnp.dot(vt, kl_wide, preferred_element_type=F32)
    return o, rows(qs), upd_t, decs


def _hgrn_body(*refs, n_in, n_seq, cps, has_s0):
    x_ref, mod_ref, n1g_ref = refs[:3]
    w_refs = refs[3:8]
    lb_ref, l1m_ref, ng_ref = refs[8:11]
    s0_ref = refs[11] if has_s0 else None
    og_ref = refs[n_in]
    st_ref = None if has_s0 else refs[n_in + 1]
    (h_scr, q_scr, v_scr, g_scr, lff_scr, kf_scr, lfb_scr, kb_scr, of_scr, ob_scr, qsf_scr, qsb_scr,
     uf_scr, ub_scr, decf_scr, decb_scr, sf_scr, sb_scr, w_ref) = refs[n_in + (1 if has_s0 else 2):]
    c = SCAN_CHUNK
    n_chunks = SEQ_BLOCK // c
    n_tiles = SEQ_BLOCK // ROW_TILE
    for s, ref in enumerate(w_refs):
        w_ref[:, s * DK:(s + 1) * DK] = ref[...].astype(BF16)

    @pl.when(pl.program_id(1) == 0)
    def _():
        m = mod_ref[0]
        g = n1g_ref[...]

        def tile(t, carry):
            rows = pl.ds(pl.multiple_of(t * ROW_TILE, ROW_TILE), ROW_TILE)
            h_scr[rows, :] = _norm_mod(x_ref[rows, :], g, m[0:1], m[1:2]).astype(BF16)
            return carry
        lax.fori_loop(0, n_tiles, tile, 0)

    lb = lb_ref[...]
    l1m = l1m_ref[...]

    def proj_tiles(g, carry):
        tiles = []
        for u in range(PROJ_TILES):
            rows = pl.ds(pl.multiple_of((PROJ_TILES * g + u) * ROW_TILE, ROW_TILE), ROW_TILE)
            tiles.append((rows, h_scr[rows, :]))
        outs = []
        for rows, h in tiles:
            p = jnp.dot(h, w_ref[...], preferred_element_type=F32)
            outs.append((rows, p, _log_f_and_key(p[:, DK:2 * DK], lb[0:1], lb[2:3], l1m[0:1]),
                         _log_f_and_key(p[:, 2 * DK:3 * DK], lb[1:2], lb[3:4], l1m[1:2])))
        for rows, p, (lf_f, k_f), (lf_b, k_b) in outs:
            q_scr[rows, :] = p[:, 0:DK]
            v_scr[rows, :] = p[:, 3 * DK:3 * DK + DV]
            g_scr[rows, :] = p[:, 3 * DK + DV:3 * DK + 2 * DV]
            lff_scr[rows, :] = lf_f
            kf_scr[rows, :] = k_f
            lfb_scr[rows, :] = lf_b
            kb_scr[rows, :] = k_b
        return carry
    lax.fori_loop(0, n_tiles // PROJ_TILES, proj_tiles, 0)

    prefix, m_chunk, m_diag_f, m_diag_b = _group_masks()

    def group_rows(grp):
        return pl.ds(pl.multiple_of(grp * GROUP_ROWS, GROUP_ROWS), GROUP_ROWS)

    def local_load(grp):
        rows = group_rows(grp)
        return (q_scr[rows, :], v_scr[rows, :], lff_scr[rows, :], kf_scr[rows, :], lfb_scr[rows, :],
                kb_scr[rows, :])

    def local_compute(q, v32, lf_f, k_f, lf_b, k_b):
        v = v32.astype(BF16)
        vt = v32.T.astype(BF16)
        sums = jnp.dot(prefix, jnp.concatenate(_split2(lf_f) + _split2(lf_b), axis=1),
                       preferred_element_type=F32)
        b_f = _sum2(sums, 0) * LOG2_E
        pre_b = _sum2(sums, 1)
        total_b = jnp.concatenate(
            [jnp.broadcast_to(pre_b[g * c + c - 1:g * c + c], (c, DK)) for g in range(GROUP)], axis=0)
        b_b = ((total_b - pre_b) + lf_b) * LOG2_E
        return (_group_dir(b_f, q, k_f, v, vt, m_chunk, m_diag_f, reverse=False),
                _group_dir(b_b, q, k_b, v, vt, m_chunk, m_diag_b, reverse=True))

    def local_store(grp, fwd, bwd):
        rows = group_rows(grp)
        for (o, qs, upd_t, dec), o_scr, qs_scr, u_scr, dec_scr in (
                (fwd, of_scr, qsf_scr, uf_scr, decf_scr), (bwd, ob_scr, qsb_scr, ub_scr, decb_scr)):
            o_scr[rows, :] = o
            qs_scr[rows, :] = qs
            for g in range(GROUP):
                u_scr[grp * GROUP + g] = upd_t[:, g * DK:(g + 1) * DK]
                dec_scr[grp * GROUP + g] = dec[g]

    def local(it, carry):
        groups = [it * LOCAL_GROUPS + u for u in range(LOCAL_GROUPS)]
        loaded = [local_load(grp) for grp in groups]
        results = [local_compute(*vals) for vals in loaded]
        for grp, (fwd, bwd) in zip(groups, results):
            local_store(grp, fwd, bwd)
        return carry
    lax.fori_loop(0, n_chunks // GROUP // LOCAL_GROUPS, local, 0)

    def advance(ci, cj, sf, sb):
        sf_scr[ci] = sf.astype(BF16)
        sb_scr[cj] = sb.astype(BF16)
        return sf * decf_scr[ci] + uf_scr[ci], sb * decb_scr[cj] + ub_scr[cj]

    if has_s0:
        def step(i, carry):
            return advance(i, n_chunks - 1 - i, *carry)
        lax.fori_loop(0, n_chunks, step, (s0_ref[0].T, s0_ref[1].T), unroll=4)
    else:
        def seq(s, carry):
            sf = jnp.zeros((DV, DK), F32)
            sb = jnp.zeros((DV, DK), F32)
            for i in range(cps):
                sf, sb = advance(s * cps + i, s * cps + cps - 1 - i, sf, sb)
            st_ref[s, 0] = sf.T
            st_ref[s, 1] = sb.T
            return carry
        lax.fori_loop(0, n_seq, seq, 0)

    ng = ng_ref[...]
    zero_chunk = jnp.zeros((c, DK), BF16)

    def widen(qs):
        return jnp.concatenate(
            [jnp.concatenate([qs[h * c:(h + 1) * c] if h == g else zero_chunk for h in range(GROUP)], axis=0)
             for g in range(GROUP)], axis=1)

    def out_groups(it, carry):
        loaded = []
        for u in range(LOCAL_GROUPS):
            grp = it * LOCAL_GROUPS + u
            rows = group_rows(grp)
            qs = jnp.concatenate([widen(qsf_scr[rows, :]), widen(qsb_scr[rows, :])], axis=1)
            st = jnp.concatenate([sf_scr[grp * GROUP + g] for g in range(GROUP)]
                                 + [sb_scr[grp * GROUP + g] for g in range(GROUP)], axis=1)
            loaded.append((rows, qs, st, of_scr[rows, :] + ob_scr[rows, :], g_scr[rows, :]))
        for rows, qs, st, o_local, gate in loaded:
            o = o_local + _qk(qs, st)
            o = o * lax.rsqrt(jnp.mean(o * o, axis=-1, keepdims=True) + NORM_EPS) * ng
            og_ref[rows, :] = (o * _silu(gate)).astype(BF16)
        return carry
    lax.fori_loop(0, n_chunks // GROUP // LOCAL_GROUPS, out_groups, 0)


class _Group:
    def __init__(self, first, n, seq_len, latent):
        self.first, self.n, self.seq_len, self.latent = first, n, seq_len, latent

    def cond_row(self, block):
        return block if self.latent else CTX_ROW


def _hgrn_call(x, mix, mod_all, layer, n1g, w_in, lb, log_1m_lb, norm_g, j, grp, s0, states):
    t = x.shape[0]
    n_seq = SEQ_BLOCK // grp.seq_len
    cps = grp.seq_len // SCAN_CHUNK
    n_chunks = SEQ_BLOCK // SCAN_CHUNK
    has_s0 = s0 is not None
    n_layers = w_in.shape[0]
    w_cat = pltpu.VMEM((D_MODEL, 3 * DK + 2 * DV), BF16)

    def section(s):
        return pl.BlockSpec((None, D_MODEL, DK), lambda b, h: (j, 0, s * HEADS + h))

    in_specs = [
        pl.BlockSpec((SEQ_BLOCK, D_MODEL), lambda b, h: (b + grp.first, 0)),
        pl.BlockSpec((None, 1, 6, D_MODEL), lambda b, h: (layer, grp.cond_row(b), 0, 0)),
        pl.BlockSpec((1, D_MODEL), lambda b, h: (0, 0)),
        section(0), section(1), section(2), section(3), section(4),
        pl.BlockSpec((4, DK), lambda b, h: (0, h)),
        pl.BlockSpec((2, DK), lambda b, h: (0, h)),
        pl.BlockSpec((1, DV), lambda b, h: (0, h)),
    ]
    args = [x, mod_all, n1g, w_in, w_in, w_in, w_in, w_in, lb, log_1m_lb, norm_g]
    og_spec = pl.BlockSpec((SEQ_BLOCK, DV), lambda b, h: (b + grp.first, h))
    og_shape = jax.ShapeDtypeStruct((t, D_MODEL), BF16)
    aliases = {}
    if has_s0:
        in_specs.append(pl.BlockSpec((None, None, 2, None, DK, DV), lambda b, h: (b, j, 0, h, 0, 0)))
        args.append(s0)
        out_specs, out_shape = og_spec, og_shape
    else:
        st_shape = (grp.n * n_seq, n_layers, 2, HEADS, DK, DV)
        out_specs = (og_spec,
                     pl.BlockSpec((n_seq, None, 2, None, DK, DV), lambda b, h: (b, j, 0, h, 0, 0)))
        out_shape = (og_shape, jax.ShapeDtypeStruct(st_shape, F32))
        if states is not None:
            in_specs.append(pl.BlockSpec(memory_space=pl.ANY))
            args.append(states)
            aliases[len(args) - 1] = 1
    if mix is not None:
        in_specs.append(pl.BlockSpec(memory_space=pl.ANY))
        args.append(mix)
        aliases[len(args) - 1] = 0
    col = pltpu.VMEM((SEQ_BLOCK, DK), F32)
    colb = pltpu.VMEM((SEQ_BLOCK, DK), BF16)
    upd = pltpu.VMEM((n_chunks, DV, DK), F32)
    dec = pltpu.VMEM((n_chunks, 1, DK), F32)
    start = pltpu.VMEM((n_chunks, DV, DK), BF16)
    res = pl.pallas_call(
        functools.partial(_hgrn_body, n_in=len(args), n_seq=n_seq, cps=cps, has_s0=has_s0),
        grid=(grp.n, HEADS),
        in_specs=in_specs,
        out_specs=out_specs,
        out_shape=out_shape,
        scratch_shapes=[pltpu.VMEM((SEQ_BLOCK, D_MODEL), BF16)] + [col] * 9 + [colb, colb, upd, upd, dec, dec,
                                                                                  start, start, w_cat],
        input_output_aliases=aliases,
        compiler_params=_cparams("arbitrary", "arbitrary"),
        name="hgrn_lat" if has_s0 else "hgrn_ctx",
    )(*args)
    return (res, None) if has_s0 else res


def _fourier_body(*refs, n_in, seq_len):
    x_ref, mod_ref, n1g_ref, cl_ref, sl_ref, cc_ref = refs[:6]
    z_ref, h_scr = refs[n_in:]
    rt = pl.program_id(1)
    n_tiles = SEQ_BLOCK // ROW_TILE

    @pl.when(rt == 0)
    def _():
        m = mod_ref[0]
        g = n1g_ref[...]

        def tile(t, carry):
            rows = pl.ds(pl.multiple_of(t * ROW_TILE, ROW_TILE), ROW_TILE)
            h_scr[rows, :] = _norm_mod(x_ref[rows, :], g, m[0:1], m[1:2]).astype(BF16)
            return carry
        lax.fori_loop(0, n_tiles, tile, 0)

    if seq_len == ROW_TILE:
        h = h_scr[pl.ds(pl.multiple_of(rt * ROW_TILE, ROW_TILE), ROW_TILE), :]
    else:
        h = h_scr[...]
    zc = jnp.dot(cl_ref[...], h, preferred_element_type=F32).astype(BF16)
    zs = jnp.dot(sl_ref[...], h, preferred_element_type=F32).astype(BF16)
    scale = 1.0 / math.sqrt(seq_len * FOURIER_CG)
    cg = FOURIER_CG
    for g in range(FOURIER_GROUPS):
        cat = jnp.concatenate([zc[:, g * cg:(g + 1) * cg], zs[:, g * cg:(g + 1) * cg]], axis=1)
        out = jnp.dot(cat, cc_ref[...], preferred_element_type=F32) * scale
        z_ref[:, g * cg:(g + 1) * cg] = out.astype(BF16)


def _dft_tables(n):
    j = jnp.arange(n, dtype=jnp.int32)

    def direct(k):
        ang = ((j[:, None] * k[None, :]) % n).astype(F32) * (2.0 * math.pi / n)
        return jnp.cos(ang), jnp.sin(ang)
    if n <= DFT_SPLIT:
        return direct(j)
    ca, sa = direct(jnp.arange(n // DFT_SPLIT, dtype=jnp.int32) * DFT_SPLIT)
    cb, sb = direct(jnp.arange(DFT_SPLIT, dtype=jnp.int32))
    cos = ca[:, :, None] * cb[:, None, :] - sa[:, :, None] * sb[:, None, :]
    sin = sa[:, :, None] * cb[:, None, :] + ca[:, :, None] * sb[:, None, :]
    return cos.reshape(n, n), sin.reshape(n, n)


def _fourier_call(x, mix, mod_all, layer, n1g, grp):
    t = x.shape[0]
    seq_len = grp.seq_len
    n_rt = SEQ_BLOCK // ROW_TILE
    cl, sl = _dft_tables(seq_len)
    cc, sc = _dft_tables(FOURIER_CG)
    cc2 = jnp.concatenate([cc, -sc], axis=0).astype(BF16)
    if seq_len == ROW_TILE:
        pos_spec = pl.BlockSpec((ROW_TILE, seq_len), lambda b, r: (0, 0))
    else:
        pos_spec = pl.BlockSpec((ROW_TILE, seq_len), lambda b, r: (r, 0))
    in_specs = [
        pl.BlockSpec((SEQ_BLOCK, D_MODEL), lambda b, r: (b + grp.first, 0)),
        pl.BlockSpec((None, 1, 6, D_MODEL), lambda b, r: (layer, grp.cond_row(b), 0, 0)),
        pl.BlockSpec((1, D_MODEL), lambda b, r: (0, 0)),
        pos_spec, pos_spec,
        pl.BlockSpec((2 * FOURIER_CG, FOURIER_CG), lambda b, r: (0, 0)),
    ]
    args = [x, mod_all, n1g, cl.astype(BF16), sl.astype(BF16), cc2]
    aliases = {}
    if mix is not None:
        in_specs.append(pl.BlockSpec(memory_space=pl.ANY))
        args.append(mix)
        aliases[len(args) - 1] = 0
    return pl.pallas_call(
        functools.partial(_fourier_body, n_in=len(args), seq_len=seq_len),
        grid=(grp.n, n_rt),
        in_specs=in_specs,
        out_specs=pl.BlockSpec((ROW_TILE, D_MODEL), lambda b, r: ((b + grp.first) * n_rt + r, 0)),
        out_shape=jax.ShapeDtypeStruct((t, D_MODEL), BF16),
        scratch_shapes=[pltpu.VMEM((SEQ_BLOCK, D_MODEL), BF16)],
        input_output_aliases=aliases,
        compiler_params=_cparams("arbitrary", "arbitrary"),
        name="fourier_lat" if grp.latent else "fourier_ctx",
    )(*args)


def _route(lg):
    grp = [lg[g:g + 1] for g in range(MOE_GROUPS)]
    gmax = functools.reduce(jnp.maximum, grp)
    gi = jnp.where(grp[0] == gmax, 0, jnp.where(grp[1] == gmax, 1, jnp.where(grp[2] == gmax, 2, 3)))
    pgv = 1.0 / functools.reduce(lambda a, b: a + b, [jnp.exp(g - gmax) for g in grp])
    sel = []
    for j in range(MOE_EPG):
        rows = [lg[MOE_GROUPS + g * MOE_EPG + j:MOE_GROUPS + g * MOE_EPG + j + 1] for g in range(MOE_GROUPS)]
        sel.append(jnp.where(gi == 0, rows[0], jnp.where(gi == 1, rows[1], jnp.where(gi == 2, rows[2], rows[3]))))
    m1 = functools.reduce(jnp.maximum, sel)
    i1 = jnp.where(sel[0] == m1, 0, jnp.where(sel[1] == m1, 1, jnp.where(sel[2] == m1, 2, 3)))
    rest = [jnp.where(i1 == j, -jnp.inf, sel[j]) for j in range(MOE_EPG)]
    m2 = functools.reduce(jnp.maximum, rest)
    i2 = jnp.where(rest[0] == m2, 0, jnp.where(rest[1] == m2, 1, jnp.where(rest[2] == m2, 2, 3)))
    e2 = jnp.exp(m2 - m1)
    w1 = pgv / (1.0 + e2)
    w2 = pgv * e2 / (1.0 + e2)
    r = lg.shape[1]
    ex1 = gi * MOE_EPG + i1
    ex2 = gi * MOE_EPG + i2
    eid = lax.broadcasted_iota(jnp.int32, (MOE_EXPERTS, r), 0)
    member = (eid == ex1) | (eid == ex2)
    t0 = lax.broadcasted_iota(jnp.int32, (r, r), 0)
    t1 = lax.broadcasted_iota(jnp.int32, (r, r), 1)
    rank = jnp.dot(member.astype(BF16), (t0 < t1).astype(BF16), preferred_element_type=F32)
    count = jnp.sum(member.astype(F32), axis=1, keepdims=True)
    padded = jnp.floor((count + (ROW_UNIT - 1)) * (1.0 / ROW_UNIT)) * ROW_UNIT
    e0 = lax.broadcasted_iota(jnp.int32, (MOE_EXPERTS, MOE_EXPERTS), 0)
    e1 = lax.broadcasted_iota(jnp.int32, (MOE_EXPERTS, MOE_EXPERTS), 1)
    start = jnp.dot((e1 < e0).astype(F32), jnp.broadcast_to(padded, (MOE_EXPERTS, LANES)),
                    precision=HIGHEST, preferred_element_type=F32)[:, 0:1]
    row = start + rank
    row1 = jnp.sum(jnp.where(eid == ex1, row, 0.0), axis=0, keepdims=True)
    row2 = jnp.sum(jnp.where(eid == ex2, row, 0.0), axis=0, keepdims=True)
    rid = lax.broadcasted_iota(jnp.int32, (LANES, r), 0)
    fields = (row1, row2, w1, w2, ex1.astype(F32), ex2.astype(F32))
    table = jnp.zeros((LANES, r), F32)
    for i, f in enumerate(fields):
        table = jnp.where(rid == i, f, table)
    return table, count


def _post_body(x_ref, mix_ref, mod_ref, w_ref, n2g_ref, wr_ref, br_ref, xo_ref, xb_ref, ct_ref, cnt_ref):
    m = mod_ref[0]
    out = jnp.dot(mix_ref[...], w_ref[...], preferred_element_type=F32)
    xn = x_ref[...] + m[2:3] * out
    xo_ref[...] = xn
    h2 = _norm_mod(xn, n2g_ref[...], m[3:4], m[4:5])
    lg = lax.dot_general(wr_ref[...], h2, (((1,), (1,)), ((), ())), precision=HIGHEST,
                         preferred_element_type=F32) + br_ref[...]
    h2 = h2.astype(BF16)
    rid = lax.broadcasted_iota(jnp.int32, (BLOCK_ROWS, TOKEN_BLOCK), 0)
    for u in range(POST_BLOCKS):
        cols = slice(u * TOKEN_BLOCK, (u + 1) * TOKEN_BLOCK)
        table, count = _route(lg[:, cols])
        onehot = ((rid == table[0:1].astype(jnp.int32)) | (rid == table[1:2].astype(jnp.int32))).astype(BF16)
        xb_ref[u * BLOCK_ROWS:(u + 1) * BLOCK_ROWS, :] = jnp.dot(
            onehot, h2[cols, :], preferred_element_type=F32).astype(BF16)
        ct_ref[cols, :] = table.T
        cnt_ref[u] = jnp.broadcast_to(count, (MOE_EXPERTS, LANES))


def _post_call(x, mix, mod_all, layer, w_out, j, n2g, wr_t, br, mod_row):
    t = x.shape[0]
    nb = t // TOKEN_BLOCK
    rows = POST_BLOCKS * TOKEN_BLOCK
    tile = lambda i: (i, 0)
    full = lambda i: (0, 0)
    return pl.pallas_call(
        _post_body,
        grid=(nb // POST_BLOCKS,),
        in_specs=[
            pl.BlockSpec((rows, D_MODEL), tile),
            pl.BlockSpec((rows, D_MODEL), tile),
            pl.BlockSpec((None, 1, 6, D_MODEL), lambda i: (layer, mod_row(i * POST_BLOCKS), 0, 0)),
            pl.BlockSpec((None, D_MODEL, D_MODEL), lambda i: (j, 0, 0)),
            pl.BlockSpec((1, D_MODEL), full),
            pl.BlockSpec((ROUTER_ROWS, D_MODEL), full),
            pl.BlockSpec((ROUTER_ROWS, 1), full),
        ],
        out_specs=(
            pl.BlockSpec((rows, D_MODEL), tile),
            pl.BlockSpec((POST_BLOCKS * BLOCK_ROWS, D_MODEL), tile),
            pl.BlockSpec((rows, LANES), tile),
            pl.BlockSpec((POST_BLOCKS, MOE_EXPERTS, LANES), lambda i: (i, 0, 0)),
        ),
        out_shape=(
            jax.ShapeDtypeStruct((t, D_MODEL), F32),
            jax.ShapeDtypeStruct((nb * BLOCK_ROWS, D_MODEL), BF16),
            jax.ShapeDtypeStruct((t, LANES), F32),
            jax.ShapeDtypeStruct((nb, MOE_EXPERTS, LANES), F32),
        ),
        compiler_params=_cparams("arbitrary"),
        name="post",
    )(x, mix, mod_all, w_out, n2g, wr_t, br)


def _moe_schedule(counts, n_tiles_max):
    units = (counts + ROW_UNIT - 1) // ROW_UNIT
    local = jnp.cumsum(units, axis=1) - units
    total = jnp.sum(units, axis=0)
    per_tile = FFN_TILE // ROW_UNIT
    tiles = (total + per_tile - 1) // per_tile
    region = tiles * per_tile
    region_start = jnp.cumsum(region) - region
    glob = region_start[None, :] + jnp.cumsum(units, axis=0) - units
    tile_end = jnp.cumsum(tiles)
    tile_start = tile_end - tiles
    n_tiles = tile_end[-1]
    s = jnp.arange(n_tiles_max, dtype=jnp.int32)
    live = s < n_tiles
    tile_expert = jnp.sum((tile_end[None, :] <= jnp.minimum(s, n_tiles - 1)[:, None]).astype(jnp.int32), axis=1)
    first = jnp.any((s[:, None] == tile_start[None, :]) & (tiles[None, :] > 0), axis=1) & live
    g = jnp.arange(n_tiles_max * per_tile, dtype=jnp.int32)
    block_units = BLOCK_ROWS // ROW_UNIT
    blocks = jnp.arange(counts.shape[0], dtype=jnp.int32)[:, None]
    seg_start = glob.T.reshape(-1)
    seg_offset = (blocks * block_units + local - glob).T.reshape(-1)
    step = seg_offset - jnp.concatenate([jnp.zeros((1,), jnp.int32), seg_offset[:-1]])
    source = g + jnp.sum(jnp.where(seg_start[None, :] <= g[:, None], step[None, :], 0), axis=1)
    real = jnp.any((g[:, None] >= region_start[None, :]) & (g[:, None] < (region_start + total)[None, :]), axis=1)
    source = jnp.where(real, source, block_units - 1)
    i32 = lambda a: a.astype(jnp.int32).reshape(-1)
    return {
        "local": i32(local), "units": i32(units), "glob": i32(glob), "source": i32(source),
        "tile_expert": tile_expert, "tile_first": i32(first), "n_tiles": i32(n_tiles),
    }


def _segment_copies(local_ref, units_ref, glob_ref, blk, make_copy):
    def per_expert(e, n):
        idx = blk * MOE_EXPERTS + e
        loc, cnt, glo = local_ref[idx], units_ref[idx], glob_ref[idx]

        def unit(u, carry):
            make_copy(pl.multiple_of((loc + u) * ROW_UNIT, ROW_UNIT),
                      pl.multiple_of((glo + u) * ROW_UNIT, ROW_UNIT)).start()
            return carry
        lax.fori_loop(0, cnt, unit, 0)
        return n + cnt
    return lax.fori_loop(0, MOE_EXPERTS, per_expert, 0)


def _wait_copies(n, make_copy):
    def one(u, carry):
        make_copy(0, 0).wait()
        return carry
    lax.fori_loop(0, n, one, 0)


def _ffn_body(expert_ref, first_ref, n_ref, source_ref, xb_ref, wg_ref, wu_ref, wd_ref, y_ref,
              x_scr, sem, wg_scr, wu_scr, wd_scr):
    s = pl.program_id(0)
    slot = s % 2
    per_tile = FFN_TILE // ROW_UNIT

    def gather(tile, buf, start):
        for u in range(per_tile):
            src = source_ref[tile * per_tile + u] if start else 0
            copy = pltpu.make_async_copy(xb_ref.at[pl.ds(pl.multiple_of(src * ROW_UNIT, ROW_UNIT), ROW_UNIT)],
                                         x_scr.at[buf, pl.ds(u * ROW_UNIT, ROW_UNIT)], sem.at[buf])
            if start:
                copy.start()
            else:
                copy.wait()

    @pl.when(s == 0)
    def _():
        gather(0, 0, start=True)

    @pl.when(s + 1 < n_ref[0])
    def _():
        gather(s + 1, 1 - slot, start=True)

    @pl.when(s < n_ref[0])
    def _():
        @pl.when(first_ref[s] == 1)
        def _():
            wg_scr[...] = wg_ref[...].astype(BF16)
            wu_scr[...] = wu_ref[...].astype(BF16)
            wd_scr[...] = wd_ref[...].astype(BF16)
        gather(s, slot, start=False)
        x = x_scr[slot]
        a = jnp.dot(x, wg_scr[...], preferred_element_type=F32)
        u = jnp.dot(x, wu_scr[...], preferred_element_type=F32)
        hid = (_silu(a) * u).astype(BF16)
        y_ref[...] = jnp.dot(hid, wd_scr[...], preferred_element_type=F32).astype(BF16)

    @pl.when(s >= n_ref[0])
    def _():
        y_ref[...] = jnp.zeros_like(y_ref)


def _ffn_call(xb, sched, w_gate, w_up, w_down, layer, n_tiles_max):
    def weight(s, expert, first, n, source):
        return (layer, expert[s], 0, 0)
    return pl.pallas_call(
        _ffn_body,
        grid_spec=pltpu.PrefetchScalarGridSpec(
            num_scalar_prefetch=4,
            grid=(n_tiles_max,),
            in_specs=[
                pl.BlockSpec(memory_space=pl.ANY),
                pl.BlockSpec((None, None, D_MODEL, MOE_D_FF), weight),
                pl.BlockSpec((None, None, D_MODEL, MOE_D_FF), weight),
                pl.BlockSpec((None, None, MOE_D_FF, D_MODEL), weight),
            ],
            out_specs=pl.BlockSpec((FFN_TILE, D_MODEL), lambda s, *_: (s, 0)),
            scratch_shapes=[pltpu.VMEM((2, FFN_TILE, D_MODEL), BF16), pltpu.SemaphoreType.DMA((2,)),
                            pltpu.VMEM((D_MODEL, MOE_D_FF), BF16), pltpu.VMEM((D_MODEL, MOE_D_FF), BF16),
                            pltpu.VMEM((MOE_D_FF, D_MODEL), BF16)],
        ),
        out_shape=jax.ShapeDtypeStruct((n_tiles_max * FFN_TILE, D_MODEL), BF16),
        compiler_params=_cparams("arbitrary"),
        name="moe_ffn",
    )(sched["tile_expert"], sched["tile_first"], sched["n_tiles"], sched["source"], xb, w_gate, w_up, w_down)


def _combine_body(*refs, split):
    final = split is not None
    local_ref, units_ref, glob_ref, x_ref, ct_ref, mod_ref = refs[:6]
    fg_ref = refs[6] if final else None
    ys_ref = refs[7 if final else 6]
    out_refs = refs[(8 if final else 7):-2]
    rows_scr, sem = refs[-2:]
    blk = pl.program_id(0)

    slot = blk % 2

    def copier(s):
        def copy(dst_row, src_row):
            return pltpu.make_async_copy(ys_ref.at[pl.ds(src_row, ROW_UNIT)],
                                         rows_scr.at[s, pl.ds(dst_row, ROW_UNIT)], sem.at[s])
        return copy

    @pl.when(blk == 0)
    def _():
        rows_scr[...] = jnp.zeros_like(rows_scr)
        _segment_copies(local_ref, units_ref, glob_ref, blk, copier(0))

    @pl.when(blk + 1 < pl.num_programs(0))
    def _():
        _segment_copies(local_ref, units_ref, glob_ref, blk + 1, copier(1 - slot))

    n_copies = lax.fori_loop(0, MOE_EXPERTS, lambda e, n: n + units_ref[blk * MOE_EXPERTS + e], 0)
    _wait_copies(n_copies, copier(slot))

    table = ct_ref[...]
    row1 = table[:, 0:1].astype(jnp.int32)
    row2 = table[:, 1:2].astype(jnp.int32)
    rid = lax.broadcasted_iota(jnp.int32, (TOKEN_BLOCK, BLOCK_ROWS), 1)
    weights = (jnp.where(rid == row1, table[:, 2:3], 0.0) + jnp.where(rid == row2, table[:, 3:4], 0.0))
    y = jnp.dot(weights.astype(BF16), rows_scr[slot], preferred_element_type=F32)
    out = x_ref[...] + mod_ref[0][5:6] * y
    if not final:
        out_refs[0][...] = out
    else:
        ms = jnp.mean(out * out, axis=-1, keepdims=True)
        out = out * lax.rsqrt(ms + NORM_EPS) * fg_ref[...]

        @pl.when(blk < split)
        def _():
            out_refs[0][...] = out

        @pl.when(blk >= split)
        def _():
            out_refs[1][...] = out


def _combine_call(x, route_c, ys, sched, mod_all, layer, final_g, split, mod_row):
    t = x.shape[0]
    final = final_g is not None
    tok = (TOKEN_BLOCK, D_MODEL)
    if final:
        out_specs = (pl.BlockSpec(tok, lambda i, *_: (jnp.minimum(i, split - 1), 0)),
                     pl.BlockSpec(tok, lambda i, *_: (jnp.maximum(i - split, 0), 0)))
        out_shape = (jax.ShapeDtypeStruct((split * TOKEN_BLOCK, D_MODEL), F32),
                     jax.ShapeDtypeStruct((t - split * TOKEN_BLOCK, D_MODEL), F32))
    else:
        out_specs = pl.BlockSpec(tok, lambda i, *_: (i, 0))
        out_shape = jax.ShapeDtypeStruct((t, D_MODEL), F32)
    in_specs = [
        pl.BlockSpec((TOKEN_BLOCK, D_MODEL), lambda i, *_: (i, 0)),
        pl.BlockSpec((TOKEN_BLOCK, LANES), lambda i, *_: (i, 0)),
        pl.BlockSpec((None, 1, 6, D_MODEL), lambda i, *_: (layer, mod_row(i), 0, 0)),
    ]
    args = [x, route_c, mod_all]
    if final:
        in_specs.append(pl.BlockSpec((1, D_MODEL), lambda i, *_: (0, 0)))
        args.append(final_g)
    in_specs.append(pl.BlockSpec(memory_space=pl.ANY))
    args.append(ys)
    return pl.pallas_call(
        functools.partial(_combine_body, split=split if final else None),
        grid_spec=pltpu.PrefetchScalarGridSpec(
            num_scalar_prefetch=3,
            grid=(t // TOKEN_BLOCK,),
            in_specs=in_specs,
            out_specs=out_specs,
            scratch_shapes=[pltpu.VMEM((2, BLOCK_ROWS, D_MODEL), BF16), pltpu.SemaphoreType.DMA((2,))],
        ),
        out_shape=out_shape,
        compiler_params=_cparams("arbitrary"),
        name="moe_combine",
    )(sched["local"], sched["units"], sched["glob"], *args)


def _moe(x, xb, route, counts, mod_all, layer, w_gate, w_up, w_down, final_g, split, mod_row):
    t = x.shape[0]
    nb = t // TOKEN_BLOCK
    max_rows = MOE_TOP_K * t + nb * MOE_EXPERTS * (ROW_UNIT - 1) + MOE_EXPERTS * (FFN_TILE - 1)
    n_tiles_max = -(-max_rows // FFN_TILE)
    sched = _moe_schedule(counts[:, :, 0].astype(jnp.int32), n_tiles_max)
    ys = _ffn_call(xb, sched, w_gate, w_up, w_down, layer, n_tiles_max)
    return _combine_call(x, route, ys, sched, mod_all, layer, final_g, split, mod_row)


def _grid_pos_embed(n_tokens):
    t = jnp.arange(n_tokens)
    r = (t // GRID_W).astype(F32)
    col = (t % GRID_W).astype(F32)
    quarter = D_MODEL // 4
    omega = 1.0 / (POS_BASE ** (jnp.arange(quarter, dtype=F32) / quarter))
    ar = r[:, None] * omega[None, :]
    ac = col[:, None] * omega[None, :]
    return jnp.concatenate([jnp.sin(ar), jnp.cos(ar), jnp.sin(ac), jnp.cos(ac)], axis=-1)


def _run_trunk(x, ctx, lat, init_states, mod_all, p):
    n_ctx_blocks = ctx.n * (SEQ_BLOCK // TOKEN_BLOCK)
    lat_per_seq = lat.seq_len // TOKEN_BLOCK

    def cond_row(i):
        return jnp.where(i < n_ctx_blocks, CTX_ROW, (i - n_ctx_blocks) // lat_per_seq)

    states = None
    for i in range(DEPTH):
        j = i // 2
        n1g = p["norm1_g"][i:i + 1]
        if i % 2 == 0:
            hgrn = functools.partial(_hgrn_call, x, mod_all=mod_all, layer=i, n1g=n1g, w_in=p["hgrn_w_in"],
                                     lb=p["lb"][j], log_1m_lb=p["log_1m_lb"][j],
                                     norm_g=p["hgrn_norm_g"][j:j + 1], j=j)
            mix, states = hgrn(mix=None, grp=ctx, s0=None, states=states)
            mix, _ = hgrn(mix=mix, grp=lat, s0=init_states, states=None)
            w_out = p["hgrn_w_out"]
        else:
            mix = _fourier_call(x, None, mod_all, i, n1g, ctx)
            mix = _fourier_call(x, mix, mod_all, i, n1g, lat)
            w_out = p["fourier_w_out"]
        x, xb, route, counts = _post_call(x, mix, mod_all, i, w_out, j, p["norm2_g"][i:i + 1],
                                          p["router_w_t"][i], p["router_b"][i], cond_row)
        final_g = p["final_norm_g"] if i == DEPTH - 1 else None
        x = _moe(x, xb, route, counts, mod_all, i, p["moe_w_gate"], p["moe_w_up"], p["moe_w_down"],
                 final_g, n_ctx_blocks, cond_row)
    return x, states


def kernel(x_prompt, x_sample, state_hgrn, c, c_ctx, w_mod, b_mod, norm1_g, norm2_g, hgrn_w_in,
           hgrn_lb_logits, hgrn_norm_g, hgrn_w_out, fourier_w_out, router_group_w, router_group_b,
           router_expert_w, router_expert_b, moe_w_gate, moe_w_up, moe_w_down, final_norm_g):
    batch, seq, _ = x_prompt.shape
    dec_batch, dec_seq, _ = x_sample.shape
    assert dec_batch <= CTX_ROW and seq == ROW_TILE and dec_seq == SEQ_BLOCK

    cond = jnp.zeros((COND_ROWS, D_MODEL), F32).at[:dec_batch].set(c).at[CTX_ROW].set(c_ctx)
    mod_all = _mod_call(cond, w_mod, b_mod).reshape(DEPTH, COND_ROWS, 6, D_MODEL)

    probs = jax.nn.softmax(hgrn_lb_logits.astype(F32), axis=0)
    cs = jnp.cumsum(probs, axis=0)
    lb = cs - cs[0:1]
    router_w = jnp.concatenate([router_group_w, router_expert_w], axis=-1)
    router_b = jnp.concatenate([router_group_b, router_expert_b], axis=-1)
    pad = ROUTER_ROWS - router_w.shape[-1]
    p = {
        "norm1_g": norm1_g, "norm2_g": norm2_g, "hgrn_norm_g": hgrn_norm_g,
        "hgrn_w_in": hgrn_w_in, "hgrn_w_out": hgrn_w_out.astype(BF16),
        "fourier_w_out": fourier_w_out.astype(BF16),
        "lb": jnp.concatenate([lb, 1.0 - lb], axis=1), "log_1m_lb": jnp.log1p(-lb),
        "router_w_t": jnp.pad(jnp.swapaxes(router_w, 1, 2), ((0, 0), (0, pad), (0, 0))),
        "router_b": jnp.pad(router_b, ((0, 0), (0, pad)))[..., None],
        "moe_w_gate": moe_w_gate, "moe_w_up": moe_w_up, "moe_w_down": moe_w_down,
        "final_norm_g": final_norm_g.reshape(1, D_MODEL),
    }

    t_ctx = batch * seq
    t_lat = dec_batch * dec_seq
    ctx = _Group(0, t_ctx // SEQ_BLOCK, seq, latent=False)
    lat = _Group(t_ctx // SEQ_BLOCK, t_lat // SEQ_BLOCK, dec_seq, latent=True)
    x = _embed_call(x_prompt.reshape(t_ctx, D_MODEL), x_sample.reshape(t_lat, D_MODEL), _grid_pos_embed(dec_seq))
    (y_ctx, y_lat), new_state = _run_trunk(x, ctx, lat, state_hgrn, mod_all, p)
    return (y_ctx.reshape(batch, seq, D_MODEL), y_lat.reshape(dec_batch, dec_seq, D_MODEL), new_state)
```

```python
import functools
import math

import jax
import jax.numpy as jnp
from jax import lax
from jax.experimental import pallas as pl
from jax.experimental.pallas import tpu as pltpu

F32 = jnp.float32
BF16 = jnp.bfloat16
HIGHEST = lax.Precision.HIGHEST

D_MODEL = 1024
DEPTH = 4
GRID_W = 64
HEADS = 8
DK = 128
DV = 128
FOURIER_GROUPS = 4
FOURIER_CG = D_MODEL // FOURIER_GROUPS
MOE_GROUPS = 4
MOE_EPG = 4
MOE_EXPERTS = 16
MOE_D_FF = 512
NORM_EPS = 1e-6
POS_BASE = 10000.0
DFT_SPLIT = 64
MAX_SUB_DECAY = 100.0
LOG2_E = 1.0 / math.log(2.0)

COND_ROWS = 8
CTX_ROW = 4
ROUTER_ROWS = 32
LANES = 128
SCAN_CHUNK = 64
SCAN_SUB = 16
GROUP = 4
GROUP_ROWS = GROUP * SCAN_CHUNK
LOCAL_GROUPS = 2
PROJ_TILES = 4
SEQ_BLOCK = 2048
ROW_TILE = 256
TOKEN_BLOCK = 256
POST_BLOCKS = 4
MOE_TOP_K = 2
ROW_UNIT = 16
BLOCK_ROWS = 768
FFN_TILE = 512
VMEM_LIMIT = 56 * 1024 * 1024


def _cparams(*sem):
    return pltpu.CompilerParams(dimension_semantics=sem, vmem_limit_bytes=VMEM_LIMIT)


def _silu(x):
    return x * jax.nn.sigmoid(x)


def _norm_mod(x, g, shift, scale):
    ms = jnp.mean(x * x, axis=-1, keepdims=True)
    return (x * lax.rsqrt(ms + NORM_EPS) * g) * (1.0 + scale) + shift


def _mod_body(c_ref, w_ref, b_ref, o_ref):
    s = _silu(c_ref[...])
    o_ref[0] = jnp.dot(s, w_ref[0], precision=HIGHEST, preferred_element_type=F32) + b_ref[0]


def _mod_call(cond, w_mod, b_mod):
    n_col = 6 * D_MODEL // D_MODEL
    return pl.pallas_call(
        _mod_body,
        grid=(DEPTH, n_col),
        in_specs=[
            pl.BlockSpec((COND_ROWS, D_MODEL), lambda i, n: (0, 0)),
            pl.BlockSpec((1, D_MODEL, D_MODEL), lambda i, n: (i, 0, n)),
            pl.BlockSpec((1, 1, D_MODEL), lambda i, n: (i, 0, n)),
        ],
        out_specs=pl.BlockSpec((1, COND_ROWS, D_MODEL), lambda i, n: (i, 0, n)),
        out_shape=jax.ShapeDtypeStruct((DEPTH, COND_ROWS, 6 * D_MODEL), F32),
        compiler_params=_cparams("arbitrary", "arbitrary"),
        name="mod",
    )(cond, w_mod, b_mod.reshape(DEPTH, 1, 6 * D_MODEL))


def _embed_body(xc_ref, xl_ref, p_ref, o_ref, *, n_ctx):
    i = pl.program_id(0)

    @pl.when(i < n_ctx)
    def _():
        o_ref[...] = xc_ref[...]

    @pl.when(i >= n_ctx)
    def _():
        o_ref[...] = xl_ref[...] + p_ref[...]


def _embed_call(x_ctx, x_lat, pos):
    n_ctx = x_ctx.shape[0] // TOKEN_BLOCK
    n_lat = x_lat.shape[0] // TOKEN_BLOCK
    per_seq = pos.shape[0] // TOKEN_BLOCK
    return pl.pallas_call(
        functools.partial(_embed_body, n_ctx=n_ctx),
        grid=(n_ctx + n_lat,),
        in_specs=[
            pl.BlockSpec((TOKEN_BLOCK, D_MODEL), lambda i: (jnp.minimum(i, n_ctx - 1), 0)),
            pl.BlockSpec((TOKEN_BLOCK, D_MODEL), lambda i: (jnp.maximum(i - n_ctx, 0), 0)),
            pl.BlockSpec((TOKEN_BLOCK, D_MODEL), lambda i: (jnp.maximum(i - n_ctx, 0) % per_seq, 0)),
        ],
        out_specs=pl.BlockSpec((TOKEN_BLOCK, D_MODEL), lambda i: (i, 0)),
        out_shape=jax.ShapeDtypeStruct((x_ctx.shape[0] + x_lat.shape[0], D_MODEL), F32),
        compiler_params=_cparams("arbitrary"),
        name="embed",
    )(x_ctx, x_lat, pos)


def _log_f_and_key(z, lb, one_m_lb, log_1m_lb):
    t = jnp.exp(-jnp.abs(z))
    big = 1.0 / (1.0 + t)
    small = t * big
    pos = z >= 0.0
    f = lb + one_m_lb * jnp.where(pos, big, small)
    log_f = jnp.where(f > 0.0, jnp.log(f), log_1m_lb + z)
    return log_f, one_m_lb * jnp.where(pos, small, big)


def _group_masks():
    n = GROUP_ROWS
    t = lax.broadcasted_iota(jnp.int32, (n, n), 0)
    s = lax.broadcasted_iota(jnp.int32, (n, n), 1)
    same_chunk = (t // SCAN_CHUNK) == (s // SCAN_CHUNK)
    same_sub = (t // SCAN_SUB) == (s // SCAN_SUB)
    prefix = (same_chunk & (t >= s)).astype(BF16)
    return prefix, same_chunk, same_sub & (t >= s), same_sub & (t <= s)


def _qk(a, b):
    return lax.dot_general(a, b, (((1,), (1,)), ((), ())), preferred_element_type=F32)


def _split2(x):
    hi = x.astype(BF16)
    return [hi, (x - hi.astype(F32)).astype(BF16)]


def _sum2(s, i):
    o = 2 * i * DK
    return s[:, o:o + DK] + s[:, o + DK:o + 2 * DK]


def _same_sub_block_pairs(b, q, k, v, reverse):
    n = GROUP_ROWS
    pos = lax.broadcasted_iota(jnp.int32, (n, 1), 0) % SCAN_SUB
    vf = v.astype(F32)
    out = jnp.zeros((n, DV), F32)
    for d in range(SCAN_SUB):
        shift = (n - d) % n if reverse else d
        ks, bs, vs = (pltpu.roll(a, shift, 0) if shift else a for a in (k, b, vf))
        inside = (pos + d < SCAN_SUB) if reverse else (pos >= d)
        w = jnp.sum(q * ks * jnp.exp2(jnp.minimum(b - bs, 0.0)), axis=1, keepdims=True)
        out = out + jnp.where(inside, w, 0.0) * vs
    return out


def _group_dir(b, q, k, v, vt, m_chunk, m_diag, unsafe, reverse):
    c, sb = SCAN_CHUNK, SCAN_SUB
    nb = c // sb
    zero = jnp.zeros((sb, DK), BF16)
    qd, kd, qs, kl, decs = [], [], [], [], []
    q_lev = [[] for _ in range(nb - 1)]
    k_lev = [[] for _ in range(nb - 1)]
    for g in range(GROUP):
        starts = [g * c + i * sb for i in range(nb)]
        pos = [nb - 1 - i for i in range(nb)] if reverse else list(range(nb))
        end_at = {}
        for i, r in enumerate(starts):
            e = r if reverse else r + sb - 1
            end_at[pos[i]] = b[e:e + 1]
        edge = end_at[nb - 1]
        decs.append(jnp.exp2(edge))
        for i, r in enumerate(starts):
            m = r + sb // 2 if reverse else r + sb // 2 - 1
            bi, qi, ki, mid = b[r:r + sb], q[r:r + sb], k[r:r + sb], b[m:m + 1]
            qd.append((qi * jnp.exp2(bi - mid)).astype(BF16))
            kd.append((ki * jnp.exp2(mid - bi)).astype(BF16))
            for j in range(nb - 1):
                q_lev[j].append((qi * jnp.exp2(bi - end_at[j])).astype(BF16) if pos[i] > j else zero)
                k_lev[j].append((ki * jnp.exp2(end_at[j] - bi)).astype(BF16) if pos[i] == j else zero)
            qs.append((qi * jnp.exp2(bi)).astype(BF16))
            kl.append((ki * jnp.exp2(edge - bi)).astype(BF16))

    def rows(blocks):
        return jnp.concatenate(blocks, axis=0)
    s_lev = _qk(jnp.concatenate([rows(x) for x in q_lev], axis=1),
                jnp.concatenate([rows(x) for x in k_lev], axis=1))

    s_diag = _qk(rows(qd), rows(kd))
    scores = jnp.where(m_diag & jnp.logical_not(unsafe), s_diag, 0.0) + jnp.where(m_chunk, s_lev, 0.0)
    o = jnp.dot(scores.astype(BF16), v, preferred_element_type=F32)
    kl_wide = jnp.concatenate(
        [rows([blk if n // nb == g else zero for n, blk in enumerate(kl)]) for g in range(GROUP)], axis=1)
    upd_t = jnp.dot(vt, kl_wide, preferred_element_type=F32)
    return o, rows(qs), upd_t, decs


def _hgrn_body(*refs, n_in, n_seq, cps, has_s0):
    x_ref, mod_ref, n1g_ref = refs[:3]
    w_refs = refs[3:8]
    lb_ref, l1m_ref, ng_ref = refs[8:11]
    s0_ref = refs[11] if has_s0 else None
    og_ref = refs[n_in]
    st_ref = None if has_s0 else refs[n_in + 1]
    (h_scr, q_scr, v_scr, g_scr, lff_scr, kf_scr, lfb_scr, kb_scr, of_scr, ob_scr, qsf_scr, qsb_scr,
     uf_scr, ub_scr, decf_scr, decb_scr, sf_scr, sb_scr, w_ref, b_scr) = refs[n_in + (1 if has_s0 else 2):]
    c = SCAN_CHUNK
    n_chunks = SEQ_BLOCK // c
    n_tiles = SEQ_BLOCK // ROW_TILE
    for s, ref in enumerate(w_refs):
        w_ref[:, s * DK:(s + 1) * DK] = ref[...].astype(BF16)

    @pl.when(pl.program_id(1) == 0)
    def _():
        m = mod_ref[0]
        g = n1g_ref[...]

        def tile(t, carry):
            rows = pl.ds(pl.multiple_of(t * ROW_TILE, ROW_TILE), ROW_TILE)
            h_scr[rows, :] = _norm_mod(x_ref[rows, :], g, m[0:1], m[1:2]).astype(BF16)
            return carry
        lax.fori_loop(0, n_tiles, tile, 0)

    lb = lb_ref[...]
    l1m = l1m_ref[...]

    def proj_tiles(g, carry):
        tiles = []
        for u in range(PROJ_TILES):
            rows = pl.ds(pl.multiple_of((PROJ_TILES * g + u) * ROW_TILE, ROW_TILE), ROW_TILE)
            tiles.append((rows, h_scr[rows, :]))
        outs = []
        for rows, h in tiles:
            p = jnp.dot(h, w_ref[...], preferred_element_type=F32)
            outs.append((rows, p, _log_f_and_key(p[:, DK:2 * DK], lb[0:1], lb[2:3], l1m[0:1]),
                         _log_f_and_key(p[:, 2 * DK:3 * DK], lb[1:2], lb[3:4], l1m[1:2])))
        for rows, p, (lf_f, k_f), (lf_b, k_b) in outs:
            q_scr[rows, :] = p[:, 0:DK]
            v_scr[rows, :] = p[:, 3 * DK:3 * DK + DV]
            g_scr[rows, :] = p[:, 3 * DK + DV:3 * DK + 2 * DV]
            lff_scr[rows, :] = lf_f
            kf_scr[rows, :] = k_f
            lfb_scr[rows, :] = lf_b
            kb_scr[rows, :] = k_b
        return carry
    lax.fori_loop(0, n_tiles // PROJ_TILES, proj_tiles, 0)

    prefix, m_chunk, m_diag_f, m_diag_b = _group_masks()

    def group_rows(grp):
        return pl.ds(pl.multiple_of(grp * GROUP_ROWS, GROUP_ROWS), GROUP_ROWS)

    def local_load(grp):
        rows = group_rows(grp)
        return (q_scr[rows, :], v_scr[rows, :], lff_scr[rows, :], kf_scr[rows, :], lfb_scr[rows, :],
                kb_scr[rows, :])

    def log2_decays(lf_f, lf_b):
        sums = jnp.dot(prefix, jnp.concatenate(_split2(lf_f) + _split2(lf_b), axis=1),
                       preferred_element_type=F32)
        pre_b = _sum2(sums, 1)
        total_b = jnp.concatenate(
            [jnp.broadcast_to(pre_b[g * c + c - 1:g * c + c], (c, DK)) for g in range(GROUP)], axis=0)
        return _sum2(sums, 0) * LOG2_E, ((total_b - pre_b) + lf_b) * LOG2_E

    def sub_block_span(b):
        return functools.reduce(jnp.maximum, [jnp.abs(b[r:r + 1] - b[r + SCAN_SUB - 1:r + SCAN_SUB])
                                              for r in range(0, GROUP_ROWS, SCAN_SUB)])

    def local_compute(b_f, b_b, unsafe, q, v32, lf_f, k_f, lf_b, k_b):
        v = v32.astype(BF16)
        vt = v32.T.astype(BF16)
        return (_group_dir(b_f, q, k_f, v, vt, m_chunk, m_diag_f, unsafe, reverse=False),
                _group_dir(b_b, q, k_b, v, vt, m_chunk, m_diag_b, unsafe, reverse=True))

    def local_store(grp, fwd, bwd):
        rows = group_rows(grp)
        for (o, qs, upd_t, dec), o_scr, qs_scr, u_scr, dec_scr in (
                (fwd, of_scr, qsf_scr, uf_scr, decf_scr), (bwd, ob_scr, qsb_scr, ub_scr, decb_scr)):
            o_scr[rows, :] = o
            qs_scr[rows, :] = qs
            for g in range(GROUP):
                u_scr[grp * GROUP + g] = upd_t[:, g * DK:(g + 1) * DK]
                dec_scr[grp * GROUP + g] = dec[g]

    def local(it, carry):
        groups = [it * LOCAL_GROUPS + u for u in range(LOCAL_GROUPS)]
        loaded = [local_load(grp) for grp in groups]
        decays = [log2_decays(vals[2], vals[4]) for vals in loaded]
        span = functools.reduce(jnp.maximum, [sub_block_span(b) for pair in decays for b in pair])
        unsafe = jnp.max(span) > MAX_SUB_DECAY
        for u, (b_f, b_b) in enumerate(decays):
            b_scr[u, 0] = b_f
            b_scr[u, 1] = b_b
        results = [local_compute(b_f, b_b, unsafe, *vals) for (b_f, b_b), vals in zip(decays, loaded)]
        for grp, (fwd, bwd) in zip(groups, results):
            local_store(grp, fwd, bwd)

        @pl.when(unsafe)
        def _():
            for u, grp in enumerate(groups):
                rows = group_rows(grp)
                q, v = q_scr[rows, :], v_scr[rows, :].astype(BF16)
                of_scr[rows, :] += _same_sub_block_pairs(b_scr[u, 0], q, kf_scr[rows, :], v, reverse=False)
                ob_scr[rows, :] += _same_sub_block_pairs(b_scr[u, 1], q, kb_scr[rows, :], v, reverse=True)
        return carry
    lax.fori_loop(0, n_chunks // GROUP // LOCAL_GROUPS, local, 0)

    def advance(ci, cj, sf, sb):
        sf_scr[ci] = sf.astype(BF16)
        sb_scr[cj] = sb.astype(BF16)
        return sf * decf_scr[ci] + uf_scr[ci], sb * decb_scr[cj] + ub_scr[cj]

    if has_s0:
        def step(i, carry):
            return advance(i, n_chunks - 1 - i, *carry)
        lax.fori_loop(0, n_chunks, step, (s0_ref[0].T, s0_ref[1].T), unroll=4)
    else:
        def seq(s, carry):
            sf = jnp.zeros((DV, DK), F32)
            sb = jnp.zeros((DV, DK), F32)
            for i in range(cps):
                sf, sb = advance(s * cps + i, s * cps + cps - 1 - i, sf, sb)
            st_ref[s, 0] = sf.T
            st_ref[s, 1] = sb.T
            return carry
        lax.fori_loop(0, n_seq, seq, 0)

    ng = ng_ref[...]
    zero_chunk = jnp.zeros((c, DK), BF16)

    def widen(qs):
        return jnp.concatenate(
            [jnp.concatenate([qs[h * c:(h + 1) * c] if h == g else zero_chunk for h in range(GROUP)], axis=0)
             for g in range(GROUP)], axis=1)

    def out_groups(it, carry):
        loaded = []
        for u in range(LOCAL_GROUPS):
            grp = it * LOCAL_GROUPS + u
            rows = group_rows(grp)
            qs = jnp.concatenate([widen(qsf_scr[rows, :]), widen(qsb_scr[rows, :])], axis=1)
            st = jnp.concatenate([sf_scr[grp * GROUP + g] for g in range(GROUP)]
                                 + [sb_scr[grp * GROUP + g] for g in range(GROUP)], axis=1)
            loaded.append((rows, qs, st, of_scr[rows, :] + ob_scr[rows, :], g_scr[rows, :]))
        for rows, qs, st, o_local, gate in loaded:
            o = o_local + _qk(qs, st)
            o = o * lax.rsqrt(jnp.mean(o * o, axis=-1, keepdims=True) + NORM_EPS) * ng
            og_ref[rows, :] = (o * _silu(gate)).astype(BF16)
        return carry
    lax.fori_loop(0, n_chunks // GROUP // LOCAL_GROUPS, out_groups, 0)


class _Group:
    def __init__(self, first, n, seq_len, latent):
        self.first, self.n, self.seq_len, self.latent = first, n, seq_len, latent

    def cond_row(self, block):
        return block if self.latent else CTX_ROW


def _hgrn_call(x, mix, mod_all, layer, n1g, w_in, lb, log_1m_lb, norm_g, j, grp, s0, states):
    t = x.shape[0]
    n_seq = SEQ_BLOCK // grp.seq_len
    cps = grp.seq_len // SCAN_CHUNK
    n_chunks = SEQ_BLOCK // SCAN_CHUNK
    has_s0 = s0 is not None
    n_layers = w_in.shape[0]
    w_cat = pltpu.VMEM((D_MODEL, 3 * DK + 2 * DV), BF16)
    decays = pltpu.VMEM((LOCAL_GROUPS, 2, GROUP_ROWS, DK), F32)

    def section(s):
        return pl.BlockSpec((None, D_MODEL, DK), lambda b, h: (j, 0, s * HEADS + h))

    in_specs = [
        pl.BlockSpec((SEQ_BLOCK, D_MODEL), lambda b, h: (b + grp.first, 0)),
        pl.BlockSpec((None, 1, 6, D_MODEL), lambda b, h: (layer, grp.cond_row(b), 0, 0)),
        pl.BlockSpec((1, D_MODEL), lambda b, h: (0, 0)),
        section(0), section(1), section(2), section(3), section(4),
        pl.BlockSpec((4, DK), lambda b, h: (0, h)),
        pl.BlockSpec((2, DK), lambda b, h: (0, h)),
        pl.BlockSpec((1, DV), lambda b, h: (0, h)),
    ]
    args = [x, mod_all, n1g, w_in, w_in, w_in, w_in, w_in, lb, log_1m_lb, norm_g]
    og_spec = pl.BlockSpec((SEQ_BLOCK, DV), lambda b, h: (b + grp.first, h))
    og_shape = jax.ShapeDtypeStruct((t, D_MODEL), BF16)
    aliases = {}
    if has_s0:
        in_specs.append(pl.BlockSpec((None, None, 2, None, DK, DV), lambda b, h: (b, j, 0, h, 0, 0)))
        args.append(s0)
        out_specs, out_shape = og_spec, og_shape
    else:
        st_shape = (grp.n * n_seq, n_layers, 2, HEADS, DK, DV)
        out_specs = (og_spec,
                     pl.BlockSpec((n_seq, None, 2, None, DK, DV), lambda b, h: (b, j, 0, h, 0, 0)))
        out_shape = (og_shape, jax.ShapeDtypeStruct(st_shape, F32))
        if states is not None:
            in_specs.append(pl.BlockSpec(memory_space=pl.ANY))
            args.append(states)
            aliases[len(args) - 1] = 1
    if mix is not None:
        in_specs.append(pl.BlockSpec(memory_space=pl.ANY))
        args.append(mix)
        aliases[len(args) - 1] = 0
    col = pltpu.VMEM((SEQ_BLOCK, DK), F32)
    colb = pltpu.VMEM((SEQ_BLOCK, DK), BF16)
    upd = pltpu.VMEM((n_chunks, DV, DK), F32)
    dec = pltpu.VMEM((n_chunks, 1, DK), F32)
    start = pltpu.VMEM((n_chunks, DV, DK), BF16)
    res = pl.pallas_call(
        functools.partial(_hgrn_body, n_in=len(args), n_seq=n_seq, cps=cps, has_s0=has_s0),
        grid=(grp.n, HEADS),
        in_specs=in_specs,
        out_specs=out_specs,
        out_shape=out_shape,
        scratch_shapes=[pltpu.VMEM((SEQ_BLOCK, D_MODEL), BF16)] + [col] * 9 + [colb, colb, upd, upd, dec, dec,
                                                                                  start, start, w_cat, decays],
        input_output_aliases=aliases,
        compiler_params=_cparams("arbitrary", "arbitrary"),
        name="hgrn_lat" if has_s0 else "hgrn_ctx",
    )(*args)
    return (res, None) if has_s0 else res


def _fourier_body(*refs, n_in, seq_len):
    x_ref, mod_ref, n1g_ref, cl_ref, sl_ref, cc_ref = refs[:6]
    z_ref, h_scr = refs[n_in:]
    rt = pl.program_id(1)
    n_tiles = SEQ_BLOCK // ROW_TILE

    @pl.when(rt == 0)
    def _():
        m = mod_ref[0]
        g = n1g_ref[...]

        def tile(t, carry):
            rows = pl.ds(pl.multiple_of(t * ROW_TILE, ROW_TILE), ROW_TILE)
            h_scr[rows, :] = _norm_mod(x_ref[rows, :], g, m[0:1], m[1:2]).astype(BF16)
            return carry
        lax.fori_loop(0, n_tiles, tile, 0)

    if seq_len == ROW_TILE:
        h = h_scr[pl.ds(pl.multiple_of(rt * ROW_TILE, ROW_TILE), ROW_TILE), :]
    else:
        h = h_scr[...]
    zc = jnp.dot(cl_ref[...], h, preferred_element_type=F32).astype(BF16)
    zs = jnp.dot(sl_ref[...], h, preferred_element_type=F32).astype(BF16)
    scale = 1.0 / math.sqrt(seq_len * FOURIER_CG)
    cg = FOURIER_CG
    for g in range(FOURIER_GROUPS):
        cat = jnp.concatenate([zc[:, g * cg:(g + 1) * cg], zs[:, g * cg:(g + 1) * cg]], axis=1)
        out = jnp.dot(cat, cc_ref[...], preferred_element_type=F32) * scale
        z_ref[:, g * cg:(g + 1) * cg] = out.astype(BF16)


def _dft_tables(n):
    j = jnp.arange(n, dtype=jnp.int32)

    def direct(k):
        ang = ((j[:, None] * k[None, :]) % n).astype(F32) * (2.0 * math.pi / n)
        return jnp.cos(ang), jnp.sin(ang)
    if n <= DFT_SPLIT:
        return direct(j)
    ca, sa = direct(jnp.arange(n // DFT_SPLIT, dtype=jnp.int32) * DFT_SPLIT)
    cb, sb = direct(jnp.arange(DFT_SPLIT, dtype=jnp.int32))
    cos = ca[:, :, None] * cb[:, None, :] - sa[:, :, None] * sb[:, None, :]
    sin = sa[:, :, None] * cb[:, None, :] + ca[:, :, None] * sb[:, None, :]
    return cos.reshape(n, n), sin.reshape(n, n)


def _fourier_call(x, mix, mod_all, layer, n1g, grp):
    t = x.shape[0]
    seq_len = grp.seq_len
    n_rt = SEQ_BLOCK // ROW_TILE
    cl, sl = _dft_tables(seq_len)
    cc, sc = _dft_tables(FOURIER_CG)
    cc2 = jnp.concatenate([cc, -sc], axis=0).astype(BF16)
    if seq_len == ROW_TILE:
        pos_spec = pl.BlockSpec((ROW_TILE, seq_len), lambda b, r: (0, 0))
    else:
        pos_spec = pl.BlockSpec((ROW_TILE, seq_len), lambda b, r: (r, 0))
    in_specs = [
        pl.BlockSpec((SEQ_BLOCK, D_MODEL), lambda b, r: (b + grp.first, 0)),
        pl.BlockSpec((None, 1, 6, D_MODEL), lambda b, r: (layer, grp.cond_row(b), 0, 0)),
        pl.BlockSpec((1, D_MODEL), lambda b, r: (0, 0)),
        pos_spec, pos_spec,
        pl.BlockSpec((2 * FOURIER_CG, FOURIER_CG), lambda b, r: (0, 0)),
    ]
    args = [x, mod_all, n1g, cl.astype(BF16), sl.astype(BF16), cc2]
    aliases = {}
    if mix is not None:
        in_specs.append(pl.BlockSpec(memory_space=pl.ANY))
        args.append(mix)
        aliases[len(args) - 1] = 0
    return pl.pallas_call(
        functools.partial(_fourier_body, n_in=len(args), seq_len=seq_len),
        grid=(grp.n, n_rt),
        in_specs=in_specs,
        out_specs=pl.BlockSpec((ROW_TILE, D_MODEL), lambda b, r: ((b + grp.first) * n_rt + r, 0)),
        out_shape=jax.ShapeDtypeStruct((t, D_MODEL), BF16),
        scratch_shapes=[pltpu.VMEM((SEQ_BLOCK, D_MODEL), BF16)],
        input_output_aliases=aliases,
        compiler_params=_cparams("arbitrary", "arbitrary"),
        name="fourier_lat" if grp.latent else "fourier_ctx",
    )(*args)


def _route(lg):
    grp = [lg[g:g + 1] for g in range(MOE_GROUPS)]
    gmax = functools.reduce(jnp.maximum, grp)
    gi = jnp.where(grp[0] == gmax, 0, jnp.where(grp[1] == gmax, 1, jnp.where(grp[2] == gmax, 2, 3)))
    pgv = 1.0 / functools.reduce(lambda a, b: a + b, [jnp.exp(g - gmax) for g in grp])
    sel = []
    for j in range(MOE_EPG):
        rows = [lg[MOE_GROUPS + g * MOE_EPG + j:MOE_GROUPS + g * MOE_EPG + j + 1] for g in range(MOE_GROUPS)]
        sel.append(jnp.where(gi == 0, rows[0], jnp.where(gi == 1, rows[1], jnp.where(gi == 2, rows[2], rows[3]))))
    m1 = functools.reduce(jnp.maximum, sel)
    i1 = jnp.where(sel[0] == m1, 0, jnp.where(sel[1] == m1, 1, jnp.where(sel[2] == m1, 2, 3)))
    rest = [jnp.where(i1 == j, -jnp.inf, sel[j]) for j in range(MOE_EPG)]
    m2 = functools.reduce(jnp.maximum, rest)
    i2 = jnp.where(rest[0] == m2, 0, jnp.where(rest[1] == m2, 1, jnp.where(rest[2] == m2, 2, 3)))
    e2 = jnp.exp(m2 - m1)
    w1 = pgv / (1.0 + e2)
    w2 = pgv * e2 / (1.0 + e2)
    r = lg.shape[1]
    ex1 = gi * MOE_EPG + i1
    ex2 = gi * MOE_EPG + i2
    eid = lax.broadcasted_iota(jnp.int32, (MOE_EXPERTS, r), 0)
    member = (eid == ex1) | (eid == ex2)
    t0 = lax.broadcasted_iota(jnp.int32, (r, r), 0)
    t1 = lax.broadcasted_iota(jnp.int32, (r, r), 1)
    rank = jnp.dot(member.astype(BF16), (t0 < t1).astype(BF16), preferred_element_type=F32)
    count = jnp.sum(member.astype(F32), axis=1, keepdims=True)
    padded = jnp.floor((count + (ROW_UNIT - 1)) * (1.0 / ROW_UNIT)) * ROW_UNIT
    e0 = lax.broadcasted_iota(jnp.int32, (MOE_EXPERTS, MOE_EXPERTS), 0)
    e1 = lax.broadcasted_iota(jnp.int32, (MOE_EXPERTS, MOE_EXPERTS), 1)
    start = jnp.dot((e1 < e0).astype(F32), jnp.broadcast_to(padded, (MOE_EXPERTS, LANES)),
                    precision=HIGHEST, preferred_element_type=F32)[:, 0:1]
    row = start + rank
    row1 = jnp.sum(jnp.where(eid == ex1, row, 0.0), axis=0, keepdims=True)
    row2 = jnp.sum(jnp.where(eid == ex2, row, 0.0), axis=0, keepdims=True)
    rid = lax.broadcasted_iota(jnp.int32, (LANES, r), 0)
    fields = (row1, row2, w1, w2, ex1.astype(F32), ex2.astype(F32))
    table = jnp.zeros((LANES, r), F32)
    for i, f in enumerate(fields):
        table = jnp.where(rid == i, f, table)
    return table, count


def _post_body(x_ref, mix_ref, mod_ref, w_ref, n2g_ref, wr_ref, br_ref, xo_ref, xb_ref, ct_ref, cnt_ref):
    m = mod_ref[0]
    out = jnp.dot(mix_ref[...], w_ref[...], preferred_element_type=F32)
    xn = x_ref[...] + m[2:3] * out
    xo_ref[...] = xn
    h2 = _norm_mod(xn, n2g_ref[...], m[3:4], m[4:5])
    wr_hi, wr_lo = _split2(wr_ref[...])
    h2_hi, h2_lo = _split2(h2)
    lg = _qk(wr_hi, h2_hi) + (_qk(wr_hi, h2_lo) + _qk(wr_lo, h2_hi)) + br_ref[...]
    h2 = h2_hi
    rid = lax.broadcasted_iota(jnp.int32, (BLOCK_ROWS, TOKEN_BLOCK), 0)
    for u in range(POST_BLOCKS):
        cols = slice(u * TOKEN_BLOCK, (u + 1) * TOKEN_BLOCK)
        table, count = _route(lg[:, cols])
        onehot = ((rid == table[0:1].astype(jnp.int32)) | (rid == table[1:2].astype(jnp.int32))).astype(BF16)
        xb_ref[u * BLOCK_ROWS:(u + 1) * BLOCK_ROWS, :] = jnp.dot(
            onehot, h2[cols, :], preferred_element_type=F32).astype(BF16)
        ct_ref[cols, :] = table.T
        cnt_ref[u] = jnp.broadcast_to(count, (MOE_EXPERTS, LANES))


def _post_call(x, mix, mod_all, layer, w_out, j, n2g, wr_t, br, mod_row):
    t = x.shape[0]
    nb = t // TOKEN_BLOCK
    rows = POST_BLOCKS * TOKEN_BLOCK
    tile = lambda i: (i, 0)
    full = lambda i: (0, 0)
    return pl.pallas_call(
        _post_body,
        grid=(nb // POST_BLOCKS,),
        in_specs=[
            pl.BlockSpec((rows, D_MODEL), tile),
            pl.BlockSpec((rows, D_MODEL), tile),
            pl.BlockSpec((None, 1, 6, D_MODEL), lambda i: (layer, mod_row(i * POST_BLOCKS), 0, 0)),
            pl.BlockSpec((None, D_MODEL, D_MODEL), lambda i: (j, 0, 0)),
            pl.BlockSpec((1, D_MODEL), full),
            pl.BlockSpec((ROUTER_ROWS, D_MODEL), full),
            pl.BlockSpec((ROUTER_ROWS, 1), full),
        ],
        out_specs=(
            pl.BlockSpec((rows, D_MODEL), tile),
            pl.BlockSpec((POST_BLOCKS * BLOCK_ROWS, D_MODEL), tile),
            pl.BlockSpec((rows, LANES), tile),
            pl.BlockSpec((POST_BLOCKS, MOE_EXPERTS, LANES), lambda i: (i, 0, 0)),
        ),
        out_shape=(
            jax.ShapeDtypeStruct((t, D_MODEL), F32),
            jax.ShapeDtypeStruct((nb * BLOCK_ROWS, D_MODEL), BF16),
            jax.ShapeDtypeStruct((t, LANES), F32),
            jax.ShapeDtypeStruct((nb, MOE_EXPERTS, LANES), F32),
        ),
        compiler_params=_cparams("arbitrary"),
        name="post",
    )(x, mix, mod_all, w_out, n2g, wr_t, br)


def _moe_schedule(counts, n_tiles_max):
    units = (counts + ROW_UNIT - 1) // ROW_UNIT
    local = jnp.cumsum(units, axis=1) - units
    total = jnp.sum(units, axis=0)
    per_tile = FFN_TILE // ROW_UNIT
    tiles = (total + per_tile - 1) // per_tile
    region = tiles * per_tile
    region_start = jnp.cumsum(region) - region
    glob = region_start[None, :] + jnp.cumsum(units, axis=0) - units
    tile_end = jnp.cumsum(tiles)
    tile_start = tile_end - tiles
    n_tiles = tile_end[-1]
    s = jnp.arange(n_tiles_max, dtype=jnp.int32)
    live = s < n_tiles
    tile_expert = jnp.sum((tile_end[None, :] <= jnp.minimum(s, n_tiles - 1)[:, None]).astype(jnp.int32), axis=1)
    first = jnp.any((s[:, None] == tile_start[None, :]) & (tiles[None, :] > 0), axis=1) & live
    g = jnp.arange(n_tiles_max * per_tile, dtype=jnp.int32)
    block_units = BLOCK_ROWS // ROW_UNIT
    blocks = jnp.arange(counts.shape[0], dtype=jnp.int32)[:, None]
    seg_start = glob.T.reshape(-1)
    seg_offset = (blocks * block_units + local - glob).T.reshape(-1)
    step = seg_offset - jnp.concatenate([jnp.zeros((1,), jnp.int32), seg_offset[:-1]])
    source = g + jnp.sum(jnp.where(seg_start[None, :] <= g[:, None], step[None, :], 0), axis=1)
    real = jnp.any((g[:, None] >= region_start[None, :]) & (g[:, None] < (region_start + total)[None, :]), axis=1)
    source = jnp.where(real, source, block_units - 1)
    i32 = lambda a: a.astype(jnp.int32).reshape(-1)
    return {
        "local": i32(local), "units": i32(units), "glob": i32(glob), "source": i32(source),
        "tile_expert": tile_expert, "tile_first": i32(first), "n_tiles": i32(n_tiles),
    }


def _segment_copies(local_ref, units_ref, glob_ref, blk, make_copy):
    def per_expert(e, n):
        idx = blk * MOE_EXPERTS + e
        loc, cnt, glo = local_ref[idx], units_ref[idx], glob_ref[idx]

        def unit(u, carry):
            make_copy(pl.multiple_of((loc + u) * ROW_UNIT, ROW_UNIT),
                      pl.multiple_of((glo + u) * ROW_UNIT, ROW_UNIT)).start()
            return carry
        lax.fori_loop(0, cnt, unit, 0)
        return n + cnt
    return lax.fori_loop(0, MOE_EXPERTS, per_expert, 0)


def _wait_copies(n, make_copy):
    def one(u, carry):
        make_copy(0, 0).wait()
        return carry
    lax.fori_loop(0, n, one, 0)


def _ffn_body(expert_ref, first_ref, n_ref, source_ref, xb_ref, wg_ref, wu_ref, wd_ref, y_ref,
              x_scr, sem, wg_scr, wu_scr, wd_scr):
    s = pl.program_id(0)
    slot = s % 2
    per_tile = FFN_TILE // ROW_UNIT

    def gather(tile, buf, start):
        for u in range(per_tile):
            src = source_ref[tile * per_tile + u] if start else 0
            copy = pltpu.make_async_copy(xb_ref.at[pl.ds(pl.multiple_of(src * ROW_UNIT, ROW_UNIT), ROW_UNIT)],
                                         x_scr.at[buf, pl.ds(u * ROW_UNIT, ROW_UNIT)], sem.at[buf])
            if start:
                copy.start()
            else:
                copy.wait()

    @pl.when(s == 0)
    def _():
        gather(0, 0, start=True)

    @pl.when(s + 1 < n_ref[0])
    def _():
        gather(s + 1, 1 - slot, start=True)

    @pl.when(s < n_ref[0])
    def _():
        @pl.when(first_ref[s] == 1)
        def _():
            wg_scr[...] = wg_ref[...].astype(BF16)
            wu_scr[...] = wu_ref[...].astype(BF16)
            wd_scr[...] = wd_ref[...].astype(BF16)
        gather(s, slot, start=False)
        x = x_scr[slot]
        a = jnp.dot(x, wg_scr[...], preferred_element_type=F32)
        u = jnp.dot(x, wu_scr[...], preferred_element_type=F32)
        hid = (_silu(a) * u).astype(BF16)
        y_ref[...] = jnp.dot(hid, wd_scr[...], preferred_element_type=F32).astype(BF16)

    @pl.when(s >= n_ref[0])
    def _():
        y_ref[...] = jnp.zeros_like(y_ref)


def _ffn_call(xb, sched, w_gate, w_up, w_down, layer, n_tiles_max):
    def weight(s, expert, first, n, source):
        return (layer, expert[s], 0, 0)
    return pl.pallas_call(
        _ffn_body,
        grid_spec=pltpu.PrefetchScalarGridSpec(
            num_scalar_prefetch=4,
            grid=(n_tiles_max,),
            in_specs=[
                pl.BlockSpec(memory_space=pl.ANY),
                pl.BlockSpec((None, None, D_MODEL, MOE_D_FF), weight),
                pl.BlockSpec((None, None, D_MODEL, MOE_D_FF), weight),
                pl.BlockSpec((None, None, MOE_D_FF, D_MODEL), weight),
            ],
            out_specs=pl.BlockSpec((FFN_TILE, D_MODEL), lambda s, *_: (s, 0)),
            scratch_shapes=[pltpu.VMEM((2, FFN_TILE, D_MODEL), BF16), pltpu.SemaphoreType.DMA((2,)),
                            pltpu.VMEM((D_MODEL, MOE_D_FF), BF16), pltpu.VMEM((D_MODEL, MOE_D_FF), BF16),
                            pltpu.VMEM((MOE_D_FF, D_MODEL), BF16)],
        ),
        out_shape=jax.ShapeDtypeStruct((n_tiles_max * FFN_TILE, D_MODEL), BF16),
        compiler_params=_cparams("arbitrary"),
        name="moe_ffn",
    )(sched["tile_expert"], sched["tile_first"], sched["n_tiles"], sched["source"], xb, w_gate, w_up, w_down)


def _combine_body(*refs, split):
    final = split is not None
    local_ref, units_ref, glob_ref, x_ref, ct_ref, mod_ref = refs[:6]
    fg_ref = refs[6] if final else None
    ys_ref = refs[7 if final else 6]
    out_refs = refs[(8 if final else 7):-2]
    rows_scr, sem = refs[-2:]
    blk = pl.program_id(0)

    slot = blk % 2

    def copier(s):
        def copy(dst_row, src_row):
            return pltpu.make_async_copy(ys_ref.at[pl.ds(src_row, ROW_UNIT)],
                                         rows_scr.at[s, pl.ds(dst_row, ROW_UNIT)], sem.at[s])
        return copy

    @pl.when(blk == 0)
    def _():
        rows_scr[...] = jnp.zeros_like(rows_scr)
        _segment_copies(local_ref, units_ref, glob_ref, blk, copier(0))

    @pl.when(blk + 1 < pl.num_programs(0))
    def _():
        _segment_copies(local_ref, units_ref, glob_ref, blk + 1, copier(1 - slot))

    n_copies = lax.fori_loop(0, MOE_EXPERTS, lambda e, n: n + units_ref[blk * MOE_EXPERTS + e], 0)
    _wait_copies(n_copies, copier(slot))

    table = ct_ref[...]
    row1 = table[:, 0:1].astype(jnp.int32)
    row2 = table[:, 1:2].astype(jnp.int32)
    rid = lax.broadcasted_iota(jnp.int32, (TOKEN_BLOCK, BLOCK_ROWS), 1)
    weights = (jnp.where(rid == row1, table[:, 2:3], 0.0) + jnp.where(rid == row2, table[:, 3:4], 0.0))
    y = jnp.dot(weights.astype(BF16), rows_scr[slot], preferred_element_type=F32)
    out = x_ref[...] + mod_ref[0][5:6] * y
    if not final:
        out_refs[0][...] = out
    else:
        ms = jnp.mean(out * out, axis=-1, keepdims=True)
        out = out * lax.rsqrt(ms + NORM_EPS) * fg_ref[...]

        @pl.when(blk < split)
        def _():
            out_refs[0][...] = out

        @pl.when(blk >= split)
        def _():
            out_refs[1][...] = out


def _combine_call(x, route_c, ys, sched, mod_all, layer, final_g, split, mod_row):
    t = x.shape[0]
    final = final_g is not None
    tok = (TOKEN_BLOCK, D_MODEL)
    if final:
        out_specs = (pl.BlockSpec(tok, lambda i, *_: (jnp.minimum(i, split - 1), 0)),
                     pl.BlockSpec(tok, lambda i, *_: (jnp.maximum(i - split, 0), 0)))
        out_shape = (jax.ShapeDtypeStruct((split * TOKEN_BLOCK, D_MODEL), F32),
                     jax.ShapeDtypeStruct((t - split * TOKEN_BLOCK, D_MODEL), F32))
    else:
        out_specs = pl.BlockSpec(tok, lambda i, *_: (i, 0))
        out_shape = jax.ShapeDtypeStruct((t, D_MODEL), F32)
    in_specs = [
        pl.BlockSpec((TOKEN_BLOCK, D_MODEL), lambda i, *_: (i, 0)),
        pl.BlockSpec((TOKEN_BLOCK, LANES), lambda i, *_: (i, 0)),
        pl.BlockSpec((None, 1, 6, D_MODEL), lambda i, *_: (layer, mod_row(i), 0, 0)),
    ]
    args = [x, route_c, mod_all]
    if final:
        in_specs.append(pl.BlockSpec((1, D_MODEL), lambda i, *_: (0, 0)))
        args.append(final_g)
    in_specs.append(pl.BlockSpec(memory_space=pl.ANY))
    args.append(ys)
    return pl.pallas_call(
        functools.partial(_combine_body, split=split if final else None),
        grid_spec=pltpu.PrefetchScalarGridSpec(
            num_scalar_prefetch=3,
            grid=(t // TOKEN_BLOCK,),
            in_specs=in_specs,
            out_specs=out_specs,
            scratch_shapes=[pltpu.VMEM((2, BLOCK_ROWS, D_MODEL), BF16), pltpu.SemaphoreType.DMA((2,))],
        ),
        out_shape=out_shape,
        compiler_params=_cparams("arbitrary"),
        name="moe_combine",
    )(sched["local"], sched["units"], sched["glob"], *args)


def _moe(x, xb, route, counts, mod_all, layer, w_gate, w_up, w_down, final_g, split, mod_row):
    t = x.shape[0]
    nb = t // TOKEN_BLOCK
    max_rows = MOE_TOP_K * t + nb * MOE_EXPERTS * (ROW_UNIT - 1) + MOE_EXPERTS * (FFN_TILE - 1)
    n_tiles_max = -(-max_rows // FFN_TILE)
    sched = _moe_schedule(counts[:, :, 0].astype(jnp.int32), n_tiles_max)
    ys = _ffn_call(xb, sched, w_gate, w_up, w_down, layer, n_tiles_max)
    return _combine_call(x, route, ys, sched, mod_all, layer, final_g, split, mod_row)


def _grid_pos_embed(n_tokens):
    t = jnp.arange(n_tokens)
    r = (t // GRID_W).astype(F32)
    col = (t % GRID_W).astype(F32)
    quarter = D_MODEL // 4
    omega = 1.0 / (POS_BASE ** (jnp.arange(quarter, dtype=F32) / quarter))
    ar = r[:, None] * omega[None, :]
    ac = col[:, None] * omega[None, :]
    return jnp.concatenate([jnp.sin(ar), jnp.cos(ar), jnp.sin(ac), jnp.cos(ac)], axis=-1)


def _run_trunk(x, ctx, lat, init_states, mod_all, p):
    n_ctx_blocks = ctx.n * (SEQ_BLOCK // TOKEN_BLOCK)
    lat_per_seq = lat.seq_len // TOKEN_BLOCK

    def cond_row(i):
        return jnp.where(i < n_ctx_blocks, CTX_ROW, (i - n_ctx_blocks) // lat_per_seq)

    states = None
    for i in range(DEPTH):
        j = i // 2
        n1g = p["norm1_g"][i:i + 1]
        if i % 2 == 0:
            hgrn = functools.partial(_hgrn_call, x, mod_all=mod_all, layer=i, n1g=n1g, w_in=p["hgrn_w_in"],
                                     lb=p["lb"][j], log_1m_lb=p["log_1m_lb"][j],
                                     norm_g=p["hgrn_norm_g"][j:j + 1], j=j)
            mix, states = hgrn(mix=None, grp=ctx, s0=None, states=states)
            mix, _ = hgrn(mix=mix, grp=lat, s0=init_states, states=None)
            w_out = p["hgrn_w_out"]
        else:
            mix = _fourier_call(x, None, mod_all, i, n1g, ctx)
            mix = _fourier_call(x, mix, mod_all, i, n1g, lat)
            w_out = p["fourier_w_out"]
        x, xb, route, counts = _post_call(x, mix, mod_all, i, w_out, j, p["norm2_g"][i:i + 1],
                                          p["router_w_t"][i], p["router_b"][i], cond_row)
        final_g = p["final_norm_g"] if i == DEPTH - 1 else None
        x = _moe(x, xb, route, counts, mod_all, i, p["moe_w_gate"], p["moe_w_up"], p["moe_w_down"],
                 final_g, n_ctx_blocks, cond_row)
    return x, states


def kernel(x_prompt, x_sample, state_hgrn, c, c_ctx, w_mod, b_mod, norm1_g, norm2_g, hgrn_w_in,
           hgrn_lb_logits, hgrn_norm_g, hgrn_w_out, fourier_w_out, router_group_w, router_group_b,
           router_expert_w, router_expert_b, moe_w_gate, moe_w_up, moe_w_down, final_norm_g):
    batch, seq, _ = x_prompt.shape
    dec_batch, dec_seq, _ = x_sample.shape
    assert dec_batch <= CTX_ROW and seq == ROW_TILE and dec_seq == SEQ_BLOCK

    cond = jnp.zeros((COND_ROWS, D_MODEL), F32).at[:dec_batch].set(c).at[CTX_ROW].set(c_ctx)
    mod_all = _mod_call(cond, w_mod, b_mod).reshape(DEPTH, COND_ROWS, 6, D_MODEL)

    probs = jax.nn.softmax(hgrn_lb_logits.astype(F32), axis=0)
    cs = jnp.cumsum(probs, axis=0)
    lb = cs - cs[0:1]
    router_w = jnp.concatenate([router_group_w, router_expert_w], axis=-1)
    router_b = jnp.concatenate([router_group_b, router_expert_b], axis=-1)
    pad = ROUTER_ROWS - router_w.shape[-1]
    p = {
        "norm1_g": norm1_g, "norm2_g": norm2_g, "hgrn_norm_g": hgrn_norm_g,
        "hgrn_w_in": hgrn_w_in, "hgrn_w_out": hgrn_w_out.astype(BF16),
        "fourier_w_out": fourier_w_out.astype(BF16),
        "lb": jnp.concatenate([lb, 1.0 - lb], axis=1), "log_1m_lb": jnp.log1p(-lb),
        "router_w_t": jnp.pad(jnp.swapaxes(router_w, 1, 2), ((0, 0), (0, pad), (0, 0))),
        "router_b": jnp.pad(router_b, ((0, 0), (0, pad)))[..., None],
        "moe_w_gate": moe_w_gate, "moe_w_up": moe_w_up, "moe_w_down": moe_w_down,
        "final_norm_g": final_norm_g.reshape(1, D_MODEL),
    }

    t_ctx = batch * seq
    t_lat = dec_batch * dec_seq
    ctx = _Group(0, t_ctx // SEQ_BLOCK, seq, latent=False)
    lat = _Group(t_ctx // SEQ_BLOCK, t_lat // SEQ_BLOCK, dec_seq, latent=True)
    x = _embed_call(x_prompt.reshape(t_ctx, D_MODEL), x_sample.reshape(t_lat, D_MODEL), _grid_pos_embed(dec_seq))
    (y_ctx, y_lat), new_state = _run_trunk(x, ctx, lat, state_hgrn, mod_all, p)
    return (y_ctx.reshape(batch, seq, D_MODEL), y_lat.reshape(dec_batch, dec_seq, D_MODEL), new_state)
```

```python
import functools
import math

import jax
import jax.numpy as jnp
from jax import lax
from jax.experimental import pallas as pl
from jax.experimental.pallas import tpu as pltpu

F32 = jnp.float32
BF16 = jnp.bfloat16
HIGHEST = lax.Precision.HIGHEST

D_MODEL = 1024
DEPTH = 4
GRID_W = 64
HEADS = 8
DK = 128
DV = 128
FOURIER_GROUPS = 4
FOURIER_CG = D_MODEL // FOURIER_GROUPS
MOE_GROUPS = 4
MOE_EPG = 4
MOE_EXPERTS = 16
MOE_D_FF = 512
NORM_EPS = 1e-6
POS_BASE = 10000.0
DFT_SPLIT = 64
MAX_SUB_DECAY = 100.0
LOG2_E = 1.0 / math.log(2.0)

COND_ROWS = 8
CTX_ROW = 4
ROUTER_ROWS = 32
LANES = 128
SCAN_CHUNK = 64
SCAN_SUB = 16
GROUP = 4
GROUP_ROWS = GROUP * SCAN_CHUNK
LOCAL_GROUPS = 4
PROJ_TILES = 8
SEQ_BLOCK = 2048
ROW_TILE = 256
TOKEN_BLOCK = 256
POST_BLOCKS = 4
MOE_TOP_K = 2
ROW_UNIT = 16
BLOCK_ROWS = 768
FFN_TILE = 512
VMEM_LIMIT = 56 * 1024 * 1024


def _cparams(*sem):
    return pltpu.CompilerParams(dimension_semantics=sem, vmem_limit_bytes=VMEM_LIMIT)


def _silu(x):
    return x * jax.nn.sigmoid(x)


def _norm_mod(x, g, shift, scale):
    ms = jnp.mean(x * x, axis=-1, keepdims=True)
    return (x * lax.rsqrt(ms + NORM_EPS) * g) * (1.0 + scale) + shift


def _mod_body(c_ref, w_ref, b_ref, o_ref):
    s = _silu(c_ref[...])
    o_ref[0] = jnp.dot(s, w_ref[0], precision=HIGHEST, preferred_element_type=F32) + b_ref[0]


def _mod_call(cond, w_mod, b_mod):
    n_col = 6 * D_MODEL // D_MODEL
    return pl.pallas_call(
        _mod_body,
        grid=(DEPTH, n_col),
        in_specs=[
            pl.BlockSpec((COND_ROWS, D_MODEL), lambda i, n: (0, 0)),
            pl.BlockSpec((1, D_MODEL, D_MODEL), lambda i, n: (i, 0, n)),
            pl.BlockSpec((1, 1, D_MODEL), lambda i, n: (i, 0, n)),
        ],
        out_specs=pl.BlockSpec((1, COND_ROWS, D_MODEL), lambda i, n: (i, 0, n)),
        out_shape=jax.ShapeDtypeStruct((DEPTH, COND_ROWS, 6 * D_MODEL), F32),
        compiler_params=_cparams("arbitrary", "arbitrary"),
        name="mod",
    )(cond, w_mod, b_mod.reshape(DEPTH, 1, 6 * D_MODEL))


def _embed_body(xc_ref, xl_ref, p_ref, o_ref, *, n_ctx):
    i = pl.program_id(0)

    @pl.when(i < n_ctx)
    def _():
        o_ref[...] = xc_ref[...]

    @pl.when(i >= n_ctx)
    def _():
        o_ref[...] = xl_ref[...] + p_ref[...]


def _embed_call(x_ctx, x_lat, pos):
    n_ctx = x_ctx.shape[0] // TOKEN_BLOCK
    n_lat = x_lat.shape[0] // TOKEN_BLOCK
    per_seq = pos.shape[0] // TOKEN_BLOCK
    return pl.pallas_call(
        functools.partial(_embed_body, n_ctx=n_ctx),
        grid=(n_ctx + n_lat,),
        in_specs=[
            pl.BlockSpec((TOKEN_BLOCK, D_MODEL), lambda i: (jnp.minimum(i, n_ctx - 1), 0)),
            pl.BlockSpec((TOKEN_BLOCK, D_MODEL), lambda i: (jnp.maximum(i - n_ctx, 0), 0)),
            pl.BlockSpec((TOKEN_BLOCK, D_MODEL), lambda i: (jnp.maximum(i - n_ctx, 0) % per_seq, 0)),
        ],
        out_specs=pl.BlockSpec((TOKEN_BLOCK, D_MODEL), lambda i: (i, 0)),
        out_shape=jax.ShapeDtypeStruct((x_ctx.shape[0] + x_lat.shape[0], D_MODEL), F32),
        compiler_params=_cparams("arbitrary"),
        name="embed",
    )(x_ctx, x_lat, pos)


def _log_f_and_key(z, lb, one_m_lb, log_1m_lb):
    t = jnp.exp(-jnp.abs(z))
    big = 1.0 / (1.0 + t)
    small = t * big
    pos = z >= 0.0
    f = lb + one_m_lb * jnp.where(pos, big, small)
    log_f = jnp.where(f > 0.0, jnp.log(f), log_1m_lb + z)
    return log_f, one_m_lb * jnp.where(pos, small, big)


def _group_masks():
    n = GROUP_ROWS
    t = lax.broadcasted_iota(jnp.int32, (n, n), 0)
    s = lax.broadcasted_iota(jnp.int32, (n, n), 1)
    same_chunk = (t // SCAN_CHUNK) == (s // SCAN_CHUNK)
    same_sub = (t // SCAN_SUB) == (s // SCAN_SUB)
    prefix = (same_chunk & (t >= s)).astype(BF16)
    return prefix, same_chunk, same_sub & (t >= s), same_sub & (t <= s)


def _qk(a, b):
    return lax.dot_general(a, b, (((1,), (1,)), ((), ())), preferred_element_type=F32)


def _split2(x):
    hi = x.astype(BF16)
    return [hi, (x - hi.astype(F32)).astype(BF16)]


def _sum2(s, i):
    o = 2 * i * DK
    return s[:, o:o + DK] + s[:, o + DK:o + 2 * DK]


def _same_sub_block_pairs(b, q, k, v, reverse):
    n = GROUP_ROWS
    pos = lax.broadcasted_iota(jnp.int32, (n, 1), 0) % SCAN_SUB
    vf = v.astype(F32)
    out = jnp.zeros((n, DV), F32)
    for d in range(SCAN_SUB):
        shift = (n - d) % n if reverse else d
        ks, bs, vs = (pltpu.roll(a, shift, 0) if shift else a for a in (k, b, vf))
        inside = (pos + d < SCAN_SUB) if reverse else (pos >= d)
        w = jnp.sum(q * ks * jnp.exp2(jnp.minimum(b - bs, 0.0)), axis=1, keepdims=True)
        out = out + jnp.where(inside, w, 0.0) * vs
    return out


def _group_dir(b, q, k, v, vt, m_chunk, m_diag, unsafe, reverse):
    c, sb = SCAN_CHUNK, SCAN_SUB
    nb = c // sb
    zero = jnp.zeros((sb, DK), BF16)
    qd, kd, qs, kl, decs = [], [], [], [], []
    q_lev = [[] for _ in range(nb - 1)]
    k_lev = [[] for _ in range(nb - 1)]
    for g in range(GROUP):
        starts = [g * c + i * sb for i in range(nb)]
        pos = [nb - 1 - i for i in range(nb)] if reverse else list(range(nb))
        end_at = {}
        for i, r in enumerate(starts):
            e = r if reverse else r + sb - 1
            end_at[pos[i]] = b[e:e + 1]
        edge = end_at[nb - 1]
        decs.append(jnp.exp2(edge))
        for i, r in enumerate(starts):
            m = r + sb // 2 if reverse else r + sb // 2 - 1
            bi, qi, ki, mid = b[r:r + sb], q[r:r + sb], k[r:r + sb], b[m:m + 1]
            qd.append((qi * jnp.exp2(bi - mid)).astype(BF16))
            kd.append((ki * jnp.exp2(mid - bi)).astype(BF16))
            for j in range(nb - 1):
                q_lev[j].append((qi * jnp.exp2(bi - end_at[j])).astype(BF16) if pos[i] > j else zero)
                k_lev[j].append((ki * jnp.exp2(end_at[j] - bi)).astype(BF16) if pos[i] == j else zero)
            qs.append((qi * jnp.exp2(bi)).astype(BF16))
            kl.append((ki * jnp.exp2(edge - bi)).astype(BF16))

    def rows(blocks):
        return jnp.concatenate(blocks, axis=0)
    s_lev = _qk(jnp.concatenate([rows(x) for x in q_lev], axis=1),
                jnp.concatenate([rows(x) for x in k_lev], axis=1))

    s_diag = _qk(rows(qd), rows(kd))
    scores = jnp.where(m_diag & jnp.logical_not(unsafe), s_diag, 0.0) + jnp.where(m_chunk, s_lev, 0.0)
    o = jnp.dot(scores.astype(BF16), v, preferred_element_type=F32)
    kl_wide = jnp.concatenate(
        [rows([blk if n // nb == g else zero for n, blk in enumerate(kl)]) for g in range(GROUP)], axis=1)
    upd_t = jnp.dot(vt, kl_wide, preferred_element_type=F32)
    return o, rows(qs), upd_t, decs


def _hgrn_body(*refs, n_in, n_seq, cps, has_s0):
    x_ref, mod_ref, n1g_ref = refs[:3]
    w_refs = refs[3:8]
    lb_ref, l1m_ref, ng_ref = refs[8:11]
    s0_ref = refs[11] if has_s0 else None
    og_ref = refs[n_in]
    st_ref = None if has_s0 else refs[n_in + 1]
    (h_scr, q_scr, v_scr, g_scr, lff_scr, kf_scr, lfb_scr, kb_scr, of_scr, ob_scr, qsf_scr, qsb_scr,
     uf_scr, ub_scr, decf_scr, decb_scr, sf_scr, sb_scr, w_ref, b_scr) = refs[n_in + (1 if has_s0 else 2):]
    c = SCAN_CHUNK
    n_chunks = SEQ_BLOCK // c
    n_tiles = SEQ_BLOCK // ROW_TILE
    for s, ref in enumerate(w_refs):
        w_ref[:, s * DK:(s + 1) * DK] = ref[...].astype(BF16)

    @pl.when(pl.program_id(1) == 0)
    def _():
        m = mod_ref[0]
        g = n1g_ref[...]

        def tile(t, carry):
            rows = pl.ds(pl.multiple_of(t * ROW_TILE, ROW_TILE), ROW_TILE)
            h_scr[rows, :] = _norm_mod(x_ref[rows, :], g, m[0:1], m[1:2]).astype(BF16)
            return carry
        lax.fori_loop(0, n_tiles, tile, 0)

    lb = lb_ref[...]
    l1m = l1m_ref[...]

    def proj_tiles(g, carry):
        tiles = []
        for u in range(PROJ_TILES):
            rows = pl.ds(pl.multiple_of((PROJ_TILES * g + u) * ROW_TILE, ROW_TILE), ROW_TILE)
            tiles.append((rows, h_scr[rows, :]))
        outs = []
        for rows, h in tiles:
            p = jnp.dot(h, w_ref[...], preferred_element_type=F32)
            outs.append((rows, p, _log_f_and_key(p[:, DK:2 * DK], lb[0:1], lb[2:3], l1m[0:1]),
                         _log_f_and_key(p[:, 2 * DK:3 * DK], lb[1:2], lb[3:4], l1m[1:2])))
        for rows, p, (lf_f, k_f), (lf_b, k_b) in outs:
            q_scr[rows, :] = p[:, 0:DK]
            v_scr[rows, :] = p[:, 3 * DK:3 * DK + DV]
            g_scr[rows, :] = p[:, 3 * DK + DV:3 * DK + 2 * DV]
            lff_scr[rows, :] = lf_f
            kf_scr[rows, :] = k_f
            lfb_scr[rows, :] = lf_b
            kb_scr[rows, :] = k_b
        return carry
    lax.fori_loop(0, n_tiles // PROJ_TILES, proj_tiles, 0)

    prefix, m_chunk, m_diag_f, m_diag_b = _group_masks()

    def group_rows(grp):
        return pl.ds(pl.multiple_of(grp * GROUP_ROWS, GROUP_ROWS), GROUP_ROWS)

    def local_load(grp):
        rows = group_rows(grp)
        return (q_scr[rows, :], v_scr[rows, :], lff_scr[rows, :], kf_scr[rows, :], lfb_scr[rows, :],
                kb_scr[rows, :])

    def log2_decays(lf_f, lf_b):
        sums = jnp.dot(prefix, jnp.concatenate(_split2(lf_f) + _split2(lf_b), axis=1),
                       preferred_element_type=F32)
        pre_b = _sum2(sums, 1)
        total_b = jnp.concatenate(
            [jnp.broadcast_to(pre_b[g * c + c - 1:g * c + c], (c, DK)) for g in range(GROUP)], axis=0)
        return _sum2(sums, 0) * LOG2_E, ((total_b - pre_b) + lf_b) * LOG2_E

    def sub_block_span(b):
        return functools.reduce(jnp.maximum, [jnp.abs(b[r:r + 1] - b[r + SCAN_SUB - 1:r + SCAN_SUB])
                                              for r in range(0, GROUP_ROWS, SCAN_SUB)])

    def local_compute(b_f, b_b, unsafe, q, v32, lf_f, k_f, lf_b, k_b):
        v = v32.astype(BF16)
        vt = v32.T.astype(BF16)
        return (_group_dir(b_f, q, k_f, v, vt, m_chunk, m_diag_f, unsafe, reverse=False),
                _group_dir(b_b, q, k_b, v, vt, m_chunk, m_diag_b, unsafe, reverse=True))

    def local_store(grp, fwd, bwd):
        rows = group_rows(grp)
        for (o, qs, upd_t, dec), o_scr, qs_scr, u_scr, dec_scr in (
                (fwd, of_scr, qsf_scr, uf_scr, decf_scr), (bwd, ob_scr, qsb_scr, ub_scr, decb_scr)):
            o_scr[rows, :] = o
            qs_scr[rows, :] = qs
            for g in range(GROUP):
                u_scr[grp * GROUP + g] = upd_t[:, g * DK:(g + 1) * DK]
                dec_scr[grp * GROUP + g] = dec[g]

    def local(it, carry):
        groups = [it * LOCAL_GROUPS + u for u in range(LOCAL_GROUPS)]
        loaded = [local_load(grp) for grp in groups]
        decays = [log2_decays(vals[2], vals[4]) for vals in loaded]
        span = functools.reduce(jnp.maximum, [sub_block_span(b) for pair in decays for b in pair])
        unsafe = jnp.max(span) > MAX_SUB_DECAY
        for u, (b_f, b_b) in enumerate(decays):
            b_scr[u, 0] = b_f
            b_scr[u, 1] = b_b
        results = [local_compute(b_f, b_b, unsafe, *vals) for (b_f, b_b), vals in zip(decays, loaded)]
        for grp, (fwd, bwd) in zip(groups, results):
            local_store(grp, fwd, bwd)

        @pl.when(unsafe)
        def _():
            for u, grp in enumerate(groups):
                rows = group_rows(grp)
                q, v = q_scr[rows, :], v_scr[rows, :].astype(BF16)
                of_scr[rows, :] += _same_sub_block_pairs(b_scr[u, 0], q, kf_scr[rows, :], v, reverse=False)
                ob_scr[rows, :] += _same_sub_block_pairs(b_scr[u, 1], q, kb_scr[rows, :], v, reverse=True)
        return carry
    lax.fori_loop(0, n_chunks // GROUP // LOCAL_GROUPS, local, 0)

    def advance(ci, cj, sf, sb):
        sf_scr[ci] = sf.astype(BF16)
        sb_scr[cj] = sb.astype(BF16)
        return sf * decf_scr[ci] + uf_scr[ci], sb * decb_scr[cj] + ub_scr[cj]

    if has_s0:
        def step(i, carry):
            return advance(i, n_chunks - 1 - i, *carry)
        lax.fori_loop(0, n_chunks, step, (s0_ref[0].T, s0_ref[1].T), unroll=4)
    else:
        def seq(s, carry):
            sf = jnp.zeros((DV, DK), F32)
            sb = jnp.zeros((DV, DK), F32)
            for i in range(cps):
                sf, sb = advance(s * cps + i, s * cps + cps - 1 - i, sf, sb)
            st_ref[s, 0] = sf.T
            st_ref[s, 1] = sb.T
            return carry
        lax.fori_loop(0, n_seq, seq, 0)

    ng = ng_ref[...]
    zero_chunk = jnp.zeros((c, DK), BF16)

    def widen(qs):
        return jnp.concatenate(
            [jnp.concatenate([qs[h * c:(h + 1) * c] if h == g else zero_chunk for h in range(GROUP)], axis=0)
             for g in range(GROUP)], axis=1)

    def out_groups(it, carry):
        loaded = []
        for u in range(LOCAL_GROUPS):
            grp = it * LOCAL_GROUPS + u
            rows = group_rows(grp)
            qs = jnp.concatenate([widen(qsf_scr[rows, :]), widen(qsb_scr[rows, :])], axis=1)
            st = jnp.concatenate([sf_scr[grp * GROUP + g] for g in range(GROUP)]
                                 + [sb_scr[grp * GROUP + g] for g in range(GROUP)], axis=1)
            loaded.append((rows, qs, st, of_scr[rows, :] + ob_scr[rows, :], g_scr[rows, :]))
        for rows, qs, st, o_local, gate in loaded:
            o = o_local + _qk(qs, st)
            o = o * lax.rsqrt(jnp.mean(o * o, axis=-1, keepdims=True) + NORM_EPS) * ng
            og_ref[rows, :] = (o * _silu(gate)).astype(BF16)
        return carry
    lax.fori_loop(0, n_chunks // GROUP // LOCAL_GROUPS, out_groups, 0)


class _Group:
    def __init__(self, first, n, seq_len, latent):
        self.first, self.n, self.seq_len, self.latent = first, n, seq_len, latent

    def cond_row(self, block):
        return block if self.latent else CTX_ROW


def _hgrn_call(x, mix, mod_all, layer, n1g, w_in, lb, log_1m_lb, norm_g, j, grp, s0, states):
    t = x.shape[0]
    n_seq = SEQ_BLOCK // grp.seq_len
    cps = grp.seq_len // SCAN_CHUNK
    n_chunks = SEQ_BLOCK // SCAN_CHUNK
    has_s0 = s0 is not None
    n_layers = w_in.shape[0]
    w_cat = pltpu.VMEM((D_MODEL, 3 * DK + 2 * DV), BF16)
    decays = pltpu.VMEM((LOCAL_GROUPS, 2, GROUP_ROWS, DK), F32)

    def section(s):
        return pl.BlockSpec((None, D_MODEL, DK), lambda b, h: (j, 0, s * HEADS + h))

    in_specs = [
        pl.BlockSpec((SEQ_BLOCK, D_MODEL), lambda b, h: (b + grp.first, 0)),
        pl.BlockSpec((None, 1, 6, D_MODEL), lambda b, h: (layer, grp.cond_row(b), 0, 0)),
        pl.BlockSpec((1, D_MODEL), lambda b, h: (0, 0)),
        section(0), section(1), section(2), section(3), section(4),
        pl.BlockSpec((4, DK), lambda b, h: (0, h)),
        pl.BlockSpec((2, DK), lambda b, h: (0, h)),
        pl.BlockSpec((1, DV), lambda b, h: (0, h)),
    ]
    args = [x, mod_all, n1g, w_in, w_in, w_in, w_in, w_in, lb, log_1m_lb, norm_g]
    og_spec = pl.BlockSpec((SEQ_BLOCK, DV), lambda b, h: (b + grp.first, h))
    og_shape = jax.ShapeDtypeStruct((t, D_MODEL), BF16)
    aliases = {}
    if has_s0:
        in_specs.append(pl.BlockSpec((None, None, 2, None, DK, DV), lambda b, h: (b, j, 0, h, 0, 0)))
        args.append(s0)
        out_specs, out_shape = og_spec, og_shape
    else:
        st_shape = (grp.n * n_seq, n_layers, 2, HEADS, DK, DV)
        out_specs = (og_spec,
                     pl.BlockSpec((n_seq, None, 2, None, DK, DV), lambda b, h: (b, j, 0, h, 0, 0)))
        out_shape = (og_shape, jax.ShapeDtypeStruct(st_shape, F32))
        if states is not None:
            in_specs.append(pl.BlockSpec(memory_space=pl.ANY))
            args.append(states)
            aliases[len(args) - 1] = 1
    if mix is not None:
        in_specs.append(pl.BlockSpec(memory_space=pl.ANY))
        args.append(mix)
        aliases[len(args) - 1] = 0
    col = pltpu.VMEM((SEQ_BLOCK, DK), F32)
    colb = pltpu.VMEM((SEQ_BLOCK, DK), BF16)
    upd = pltpu.VMEM((n_chunks, DV, DK), F32)
    dec = pltpu.VMEM((n_chunks, 1, DK), F32)
    start = pltpu.VMEM((n_chunks, DV, DK), BF16)
    res = pl.pallas_call(
        functools.partial(_hgrn_body, n_in=len(args), n_seq=n_seq, cps=cps, has_s0=has_s0),
        grid=(grp.n, HEADS),
        in_specs=in_specs,
        out_specs=out_specs,
        out_shape=out_shape,
        scratch_shapes=[pltpu.VMEM((SEQ_BLOCK, D_MODEL), BF16)] + [col] * 9 + [colb, colb, upd, upd, dec, dec,
                                                                                  start, start, w_cat, decays],
        input_output_aliases=aliases,
        compiler_params=_cparams("arbitrary", "arbitrary"),
        name="hgrn_lat" if has_s0 else "hgrn_ctx",
    )(*args)
    return (res, None) if has_s0 else res


def _fourier_body(*refs, n_in, seq_len):
    x_ref, mod_ref, n1g_ref, cl_ref, sl_ref, cc_ref = refs[:6]
    z_ref, h_scr = refs[n_in:]
    rt = pl.program_id(1)
    n_tiles = SEQ_BLOCK // ROW_TILE

    @pl.when(rt == 0)
    def _():
        m = mod_ref[0]
        g = n1g_ref[...]

        def tile(t, carry):
            rows = pl.ds(pl.multiple_of(t * ROW_TILE, ROW_TILE), ROW_TILE)
            h_scr[rows, :] = _norm_mod(x_ref[rows, :], g, m[0:1], m[1:2]).astype(BF16)
            return carry
        lax.fori_loop(0, n_tiles, tile, 0)

    if seq_len == ROW_TILE:
        h = h_scr[pl.ds(pl.multiple_of(rt * ROW_TILE, ROW_TILE), ROW_TILE), :]
    else:
        h = h_scr[...]
    zc = jnp.dot(cl_ref[...], h, preferred_element_type=F32).astype(BF16)
    zs = jnp.dot(sl_ref[...], h, preferred_element_type=F32).astype(BF16)
    scale = 1.0 / math.sqrt(seq_len * FOURIER_CG)
    cg = FOURIER_CG
    for g in range(FOURIER_GROUPS):
        cat = jnp.concatenate([zc[:, g * cg:(g + 1) * cg], zs[:, g * cg:(g + 1) * cg]], axis=1)
        out = jnp.dot(cat, cc_ref[...], preferred_element_type=F32) * scale
        z_ref[:, g * cg:(g + 1) * cg] = out.astype(BF16)


def _dft_tables(n):
    j = jnp.arange(n, dtype=jnp.int32)

    def direct(k):
        ang = ((j[:, None] * k[None, :]) % n).astype(F32) * (2.0 * math.pi / n)
        return jnp.cos(ang), jnp.sin(ang)
    if n <= DFT_SPLIT:
        return direct(j)
    ca, sa = direct(jnp.arange(n // DFT_SPLIT, dtype=jnp.int32) * DFT_SPLIT)
    cb, sb = direct(jnp.arange(DFT_SPLIT, dtype=jnp.int32))
    cos = ca[:, :, None] * cb[:, None, :] - sa[:, :, None] * sb[:, None, :]
    sin = sa[:, :, None] * cb[:, None, :] + ca[:, :, None] * sb[:, None, :]
    return cos.reshape(n, n), sin.reshape(n, n)


def _fourier_call(x, mix, mod_all, layer, n1g, grp):
    t = x.shape[0]
    seq_len = grp.seq_len
    n_rt = SEQ_BLOCK // ROW_TILE
    cl, sl = _dft_tables(seq_len)
    cc, sc = _dft_tables(FOURIER_CG)
    cc2 = jnp.concatenate([cc, -sc], axis=0).astype(BF16)
    if seq_len == ROW_TILE:
        pos_spec = pl.BlockSpec((ROW_TILE, seq_len), lambda b, r: (0, 0))
    else:
        pos_spec = pl.BlockSpec((ROW_TILE, seq_len), lambda b, r: (r, 0))
    in_specs = [
        pl.BlockSpec((SEQ_BLOCK, D_MODEL), lambda b, r: (b + grp.first, 0)),
        pl.BlockSpec((None, 1, 6, D_MODEL), lambda b, r: (layer, grp.cond_row(b), 0, 0)),
        pl.BlockSpec((1, D_MODEL), lambda b, r: (0, 0)),
        pos_spec, pos_spec,
        pl.BlockSpec((2 * FOURIER_CG, FOURIER_CG), lambda b, r: (0, 0)),
    ]
    args = [x, mod_all, n1g, cl.astype(BF16), sl.astype(BF16), cc2]
    aliases = {}
    if mix is not None:
        in_specs.append(pl.BlockSpec(memory_space=pl.ANY))
        args.append(mix)
        aliases[len(args) - 1] = 0
    return pl.pallas_call(
        functools.partial(_fourier_body, n_in=len(args), seq_len=seq_len),
        grid=(grp.n, n_rt),
        in_specs=in_specs,
        out_specs=pl.BlockSpec((ROW_TILE, D_MODEL), lambda b, r: ((b + grp.first) * n_rt + r, 0)),
        out_shape=jax.ShapeDtypeStruct((t, D_MODEL), BF16),
        scratch_shapes=[pltpu.VMEM((SEQ_BLOCK, D_MODEL), BF16)],
        input_output_aliases=aliases,
        compiler_params=_cparams("arbitrary", "arbitrary"),
        name="fourier_lat" if grp.latent else "fourier_ctx",
    )(*args)


def _route(lg):
    grp = [lg[g:g + 1] for g in range(MOE_GROUPS)]
    gmax = functools.reduce(jnp.maximum, grp)
    gi = jnp.where(grp[0] == gmax, 0, jnp.where(grp[1] == gmax, 1, jnp.where(grp[2] == gmax, 2, 3)))
    pgv = 1.0 / functools.reduce(lambda a, b: a + b, [jnp.exp(g - gmax) for g in grp])
    sel = []
    for j in range(MOE_EPG):
        rows = [lg[MOE_GROUPS + g * MOE_EPG + j:MOE_GROUPS + g * MOE_EPG + j + 1] for g in range(MOE_GROUPS)]
        sel.append(jnp.where(gi == 0, rows[0], jnp.where(gi == 1, rows[1], jnp.where(gi == 2, rows[2], rows[3]))))
    m1 = functools.reduce(jnp.maximum, sel)
    i1 = jnp.where(sel[0] == m1, 0, jnp.where(sel[1] == m1, 1, jnp.where(sel[2] == m1, 2, 3)))
    rest = [jnp.where(i1 == j, -jnp.inf, sel[j]) for j in range(MOE_EPG)]
    m2 = functools.reduce(jnp.maximum, rest)
    i2 = jnp.where(rest[0] == m2, 0, jnp.where(rest[1] == m2, 1, jnp.where(rest[2] == m2, 2, 3)))
    e2 = jnp.exp(m2 - m1)
    w1 = pgv / (1.0 + e2)
    w2 = pgv * e2 / (1.0 + e2)
    r = lg.shape[1]
    ex1 = gi * MOE_EPG + i1
    ex2 = gi * MOE_EPG + i2
    eid = lax.broadcasted_iota(jnp.int32, (MOE_EXPERTS, r), 0)
    member = (eid == ex1) | (eid == ex2)
    t0 = lax.broadcasted_iota(jnp.int32, (r, r), 0)
    t1 = lax.broadcasted_iota(jnp.int32, (r, r), 1)
    rank = jnp.dot(member.astype(BF16), (t0 < t1).astype(BF16), preferred_element_type=F32)
    count = jnp.sum(member.astype(F32), axis=1, keepdims=True)
    padded = jnp.floor((count + (ROW_UNIT - 1)) * (1.0 / ROW_UNIT)) * ROW_UNIT
    e0 = lax.broadcasted_iota(jnp.int32, (MOE_EXPERTS, MOE_EXPERTS), 0)
    e1 = lax.broadcasted_iota(jnp.int32, (MOE_EXPERTS, MOE_EXPERTS), 1)
    start = jnp.dot((e1 < e0).astype(F32), jnp.broadcast_to(padded, (MOE_EXPERTS, LANES)),
                    precision=HIGHEST, preferred_element_type=F32)[:, 0:1]
    row = start + rank
    row1 = jnp.sum(jnp.where(eid == ex1, row, 0.0), axis=0, keepdims=True)
    row2 = jnp.sum(jnp.where(eid == ex2, row, 0.0), axis=0, keepdims=True)
    rid = lax.broadcasted_iota(jnp.int32, (LANES, r), 0)
    fields = (row1, row2, w1, w2, ex1.astype(F32), ex2.astype(F32))
    table = jnp.zeros((LANES, r), F32)
    for i, f in enumerate(fields):
        table = jnp.where(rid == i, f, table)
    return table, count


def _post_body(x_ref, mix_ref, mod_ref, w_ref, n2g_ref, wr_ref, br_ref, xo_ref, xb_ref, ct_ref, cnt_ref):
    m = mod_ref[0]
    out = jnp.dot(mix_ref[...], w_ref[...], preferred_element_type=F32)
    xn = x_ref[...] + m[2:3] * out
    xo_ref[...] = xn
    h2 = _norm_mod(xn, n2g_ref[...], m[3:4], m[4:5])
    wr_hi, wr_lo = _split2(wr_ref[...])
    h2_hi, h2_lo = _split2(h2)
    lg = _qk(wr_hi, h2_hi) + (_qk(wr_hi, h2_lo) + _qk(wr_lo, h2_hi)) + br_ref[...]
    h2 = h2_hi
    rid = lax.broadcasted_iota(jnp.int32, (BLOCK_ROWS, TOKEN_BLOCK), 0)
    for u in range(POST_BLOCKS):
        cols = slice(u * TOKEN_BLOCK, (u + 1) * TOKEN_BLOCK)
        table, count = _route(lg[:, cols])
        onehot = ((rid == table[0:1].astype(jnp.int32)) | (rid == table[1:2].astype(jnp.int32))).astype(BF16)
        xb_ref[u * BLOCK_ROWS:(u + 1) * BLOCK_ROWS, :] = jnp.dot(
            onehot, h2[cols, :], preferred_element_type=F32).astype(BF16)
        ct_ref[cols, :] = table.T
        cnt_ref[u] = jnp.broadcast_to(count, (MOE_EXPERTS, LANES))


def _post_call(x, mix, mod_all, layer, w_out, j, n2g, wr_t, br, mod_row):
    t = x.shape[0]
    nb = t // TOKEN_BLOCK
    rows = POST_BLOCKS * TOKEN_BLOCK
    tile = lambda i: (i, 0)
    full = lambda i: (0, 0)
    return pl.pallas_call(
        _post_body,
        grid=(nb // POST_BLOCKS,),
        in_specs=[
            pl.BlockSpec((rows, D_MODEL), tile),
            pl.BlockSpec((rows, D_MODEL), tile),
            pl.BlockSpec((None, 1, 6, D_MODEL), lambda i: (layer, mod_row(i * POST_BLOCKS), 0, 0)),
            pl.BlockSpec((None, D_MODEL, D_MODEL), lambda i: (j, 0, 0)),
            pl.BlockSpec((1, D_MODEL), full),
            pl.BlockSpec((ROUTER_ROWS, D_MODEL), full),
            pl.BlockSpec((ROUTER_ROWS, 1), full),
        ],
        out_specs=(
            pl.BlockSpec((rows, D_MODEL), tile),
            pl.BlockSpec((POST_BLOCKS * BLOCK_ROWS, D_MODEL), tile),
            pl.BlockSpec((rows, LANES), tile),
            pl.BlockSpec((POST_BLOCKS, MOE_EXPERTS, LANES), lambda i: (i, 0, 0)),
        ),
        out_shape=(
            jax.ShapeDtypeStruct((t, D_MODEL), F32),
            jax.ShapeDtypeStruct((nb * BLOCK_ROWS, D_MODEL), BF16),
            jax.ShapeDtypeStruct((t, LANES), F32),
            jax.ShapeDtypeStruct((nb, MOE_EXPERTS, LANES), F32),
        ),
        compiler_params=_cparams("arbitrary"),
        name="post",
    )(x, mix, mod_all, w_out, n2g, wr_t, br)


def _moe_schedule(counts, n_tiles_max):
    units = (counts + ROW_UNIT - 1) // ROW_UNIT
    local = jnp.cumsum(units, axis=1) - units
    total = jnp.sum(units, axis=0)
    per_tile = FFN_TILE // ROW_UNIT
    tiles = (total + per_tile - 1) // per_tile
    region = tiles * per_tile
    region_start = jnp.cumsum(region) - region
    glob = region_start[None, :] + jnp.cumsum(units, axis=0) - units
    tile_end = jnp.cumsum(tiles)
    tile_start = tile_end - tiles
    n_tiles = tile_end[-1]
    s = jnp.arange(n_tiles_max, dtype=jnp.int32)
    live = s < n_tiles
    tile_expert = jnp.sum((tile_end[None, :] <= jnp.minimum(s, n_tiles - 1)[:, None]).astype(jnp.int32), axis=1)
    first = jnp.any((s[:, None] == tile_start[None, :]) & (tiles[None, :] > 0), axis=1) & live
    g = jnp.arange(n_tiles_max * per_tile, dtype=jnp.int32)
    block_units = BLOCK_ROWS // ROW_UNIT
    blocks = jnp.arange(counts.shape[0], dtype=jnp.int32)[:, None]
    seg_start = glob.T.reshape(-1)
    seg_offset = (blocks * block_units + local - glob).T.reshape(-1)
    step = seg_offset - jnp.concatenate([jnp.zeros((1,), jnp.int32), seg_offset[:-1]])
    source = g + jnp.sum(jnp.where(seg_start[None, :] <= g[:, None], step[None, :], 0), axis=1)
    real = jnp.any((g[:, None] >= region_start[None, :]) & (g[:, None] < (region_start + total)[None, :]), axis=1)
    source = jnp.where(real, source, block_units - 1)
    i32 = lambda a: a.astype(jnp.int32).reshape(-1)
    return {
        "local": i32(local), "units": i32(units), "glob": i32(glob), "source": i32(source),
        "tile_expert": tile_expert, "tile_first": i32(first), "n_tiles": i32(n_tiles),
    }


def _segment_copies(local_ref, units_ref, glob_ref, blk, make_copy):
    def per_expert(e, n):
        idx = blk * MOE_EXPERTS + e
        loc, cnt, glo = local_ref[idx], units_ref[idx], glob_ref[idx]

        def unit(u, carry):
            make_copy(pl.multiple_of((loc + u) * ROW_UNIT, ROW_UNIT),
                      pl.multiple_of((glo + u) * ROW_UNIT, ROW_UNIT)).start()
            return carry
        lax.fori_loop(0, cnt, unit, 0)
        return n + cnt
    return lax.fori_loop(0, MOE_EXPERTS, per_expert, 0)


def _wait_copies(n, make_copy):
    def one(u, carry):
        make_copy(0, 0).wait()
        return carry
    lax.fori_loop(0, n, one, 0)


def _ffn_body(expert_ref, first_ref, n_ref, source_ref, xb_ref, wg_ref, wu_ref, wd_ref, y_ref,
              x_scr, sem, wg_scr, wu_scr, wd_scr):
    s = pl.program_id(0)
    slot = s % 2
    per_tile = FFN_TILE // ROW_UNIT

    def gather(tile, buf, start):
        for u in range(per_tile):
            src = source_ref[tile * per_tile + u] if start else 0
            copy = pltpu.make_async_copy(xb_ref.at[pl.ds(pl.multiple_of(src * ROW_UNIT, ROW_UNIT), ROW_UNIT)],
                                         x_scr.at[buf, pl.ds(u * ROW_UNIT, ROW_UNIT)], sem.at[buf])
            if start:
                copy.start()
            else:
                copy.wait()

    @pl.when(s == 0)
    def _():
        gather(0, 0, start=True)

    @pl.when(s + 1 < n_ref[0])
    def _():
        gather(s + 1, 1 - slot, start=True)

    @pl.when(s < n_ref[0])
    def _():
        @pl.when(first_ref[s] == 1)
        def _():
            wg_scr[...] = wg_ref[...].astype(BF16)
            wu_scr[...] = wu_ref[...].astype(BF16)
            wd_scr[...] = wd_ref[...].astype(BF16)
        gather(s, slot, start=False)
        x = x_scr[slot]
        a = jnp.dot(x, wg_scr[...], preferred_element_type=F32)
        u = jnp.dot(x, wu_scr[...], preferred_element_type=F32)
        hid = (_silu(a) * u).astype(BF16)
        y_ref[...] = jnp.dot(hid, wd_scr[...], preferred_element_type=F32).astype(BF16)

    @pl.when(s >= n_ref[0])
    def _():
        y_ref[...] = jnp.zeros_like(y_ref)


def _ffn_call(xb, sched, w_gate, w_up, w_down, layer, n_tiles_max):
    def weight(s, expert, first, n, source):
        return (layer, expert[s], 0, 0)
    return pl.pallas_call(
        _ffn_body,
        grid_spec=pltpu.PrefetchScalarGridSpec(
            num_scalar_prefetch=4,
            grid=(n_tiles_max,),
            in_specs=[
                pl.BlockSpec(memory_space=pl.ANY),
                pl.BlockSpec((None, None, D_MODEL, MOE_D_FF), weight),
                pl.BlockSpec((None, None, D_MODEL, MOE_D_FF), weight),
                pl.BlockSpec((None, None, MOE_D_FF, D_MODEL), weight),
            ],
            out_specs=pl.BlockSpec((FFN_TILE, D_MODEL), lambda s, *_: (s, 0)),
            scratch_shapes=[pltpu.VMEM((2, FFN_TILE, D_MODEL), BF16), pltpu.SemaphoreType.DMA((2,)),
                            pltpu.VMEM((D_MODEL, MOE_D_FF), BF16), pltpu.VMEM((D_MODEL, MOE_D_FF), BF16),
                            pltpu.VMEM((MOE_D_FF, D_MODEL), BF16)],
        ),
        out_shape=jax.ShapeDtypeStruct((n_tiles_max * FFN_TILE, D_MODEL), BF16),
        compiler_params=_cparams("arbitrary"),
        name="moe_ffn",
    )(sched["tile_expert"], sched["tile_first"], sched["n_tiles"], sched["source"], xb, w_gate, w_up, w_down)


def _combine_body(*refs, split):
    final = split is not None
    local_ref, units_ref, glob_ref, x_ref, ct_ref, mod_ref = refs[:6]
    fg_ref = refs[6] if final else None
    ys_ref = refs[7 if final else 6]
    out_refs = refs[(8 if final else 7):-2]
    rows_scr, sem = refs[-2:]
    blk = pl.program_id(0)

    slot = blk % 2

    def copier(s):
        def copy(dst_row, src_row):
            return pltpu.make_async_copy(ys_ref.at[pl.ds(src_row, ROW_UNIT)],
                                         rows_scr.at[s, pl.ds(dst_row, ROW_UNIT)], sem.at[s])
        return copy

    @pl.when(blk == 0)
    def _():
        rows_scr[...] = jnp.zeros_like(rows_scr)
        _segment_copies(local_ref, units_ref, glob_ref, blk, copier(0))

    @pl.when(blk + 1 < pl.num_programs(0))
    def _():
        _segment_copies(local_ref, units_ref, glob_ref, blk + 1, copier(1 - slot))

    n_copies = lax.fori_loop(0, MOE_EXPERTS, lambda e, n: n + units_ref[blk * MOE_EXPERTS + e], 0)
    _wait_copies(n_copies, copier(slot))

    table = ct_ref[...]
    row1 = table[:, 0:1].astype(jnp.int32)
    row2 = table[:, 1:2].astype(jnp.int32)
    rid = lax.broadcasted_iota(jnp.int32, (TOKEN_BLOCK, BLOCK_ROWS), 1)
    weights = (jnp.where(rid == row1, table[:, 2:3], 0.0) + jnp.where(rid == row2, table[:, 3:4], 0.0))
    y = jnp.dot(weights.astype(BF16), rows_scr[slot], preferred_element_type=F32)
    out = x_ref[...] + mod_ref[0][5:6] * y
    if not final:
        out_refs[0][...] = out
    else:
        ms = jnp.mean(out * out, axis=-1, keepdims=True)
        out = out * lax.rsqrt(ms + NORM_EPS) * fg_ref[...]

        @pl.when(blk < split)
        def _():
            out_refs[0][...] = out

        @pl.when(blk >= split)
        def _():
            out_refs[1][...] = out


def _combine_call(x, route_c, ys, sched, mod_all, layer, final_g, split, mod_row):
    t = x.shape[0]
    final = final_g is not None
    tok = (TOKEN_BLOCK, D_MODEL)
    if final:
        out_specs = (pl.BlockSpec(tok, lambda i, *_: (jnp.minimum(i, split - 1), 0)),
                     pl.BlockSpec(tok, lambda i, *_: (jnp.maximum(i - split, 0), 0)))
        out_shape = (jax.ShapeDtypeStruct((split * TOKEN_BLOCK, D_MODEL), F32),
                     jax.ShapeDtypeStruct((t - split * TOKEN_BLOCK, D_MODEL), F32))
    else:
        out_specs = pl.BlockSpec(tok, lambda i, *_: (i, 0))
        out_shape = jax.ShapeDtypeStruct((t, D_MODEL), F32)
    in_specs = [
        pl.BlockSpec((TOKEN_BLOCK, D_MODEL), lambda i, *_: (i, 0)),
        pl.BlockSpec((TOKEN_BLOCK, LANES), lambda i, *_: (i, 0)),
        pl.BlockSpec((None, 1, 6, D_MODEL), lambda i, *_: (layer, mod_row(i), 0, 0)),
    ]
    args = [x, route_c, mod_all]
    if final:
        in_specs.append(pl.BlockSpec((1, D_MODEL), lambda i, *_: (0, 0)))
        args.append(final_g)
    in_specs.append(pl.BlockSpec(memory_space=pl.ANY))
    args.append(ys)
    return pl.pallas_call(
        functools.partial(_combine_body, split=split if final else None),
        grid_spec=pltpu.PrefetchScalarGridSpec(
            num_scalar_prefetch=3,
            grid=(t // TOKEN_BLOCK,),
            in_specs=in_specs,
            out_specs=out_specs,
            scratch_shapes=[pltpu.VMEM((2, BLOCK_ROWS, D_MODEL), BF16), pltpu.SemaphoreType.DMA((2,))],
        ),
        out_shape=out_shape,
        compiler_params=_cparams("arbitrary"),
        name="moe_combine",
    )(sched["local"], sched["units"], sched["glob"], *args)


def _moe(x, xb, route, counts, mod_all, layer, w_gate, w_up, w_down, final_g, split, mod_row):
    t = x.shape[0]
    nb = t // TOKEN_BLOCK
    max_rows = MOE_TOP_K * t + nb * MOE_EXPERTS * (ROW_UNIT - 1) + MOE_EXPERTS * (FFN_TILE - 1)
    n_tiles_max = -(-max_rows // FFN_TILE)
    sched = _moe_schedule(counts[:, :, 0].astype(jnp.int32), n_tiles_max)
    ys = _ffn_call(xb, sched, w_gate, w_up, w_down, layer, n_tiles_max)
    return _combine_call(x, route, ys, sched, mod_all, layer, final_g, split, mod_row)


def _grid_pos_embed(n_tokens):
    t = jnp.arange(n_tokens)
    r = (t // GRID_W).astype(F32)
    col = (t % GRID_W).astype(F32)
    quarter = D_MODEL // 4
    omega = 1.0 / (POS_BASE ** (jnp.arange(quarter, dtype=F32) / quarter))
    ar = r[:, None] * omega[None, :]
    ac = col[:, None] * omega[None, :]
    return jnp.concatenate([jnp.sin(ar), jnp.cos(ar), jnp.sin(ac), jnp.cos(ac)], axis=-1)


def _run_trunk(x, ctx, lat, init_states, mod_all, p):
    n_ctx_blocks = ctx.n * (SEQ_BLOCK // TOKEN_BLOCK)
    lat_per_seq = lat.seq_len // TOKEN_BLOCK

    def cond_row(i):
        return jnp.where(i < n_ctx_blocks, CTX_ROW, (i - n_ctx_blocks) // lat_per_seq)

    states = None
    for i in range(DEPTH):
        j = i // 2
        n1g = p["norm1_g"][i:i + 1]
        if i % 2 == 0:
            hgrn = functools.partial(_hgrn_call, x, mod_all=mod_all, layer=i, n1g=n1g, w_in=p["hgrn_w_in"],
                                     lb=p["lb"][j], log_1m_lb=p["log_1m_lb"][j],
                                     norm_g=p["hgrn_norm_g"][j:j + 1], j=j)
            mix, states = hgrn(mix=None, grp=ctx, s0=None, states=states)
            mix, _ = hgrn(mix=mix, grp=lat, s0=init_states, states=None)
            w_out = p["hgrn_w_out"]
        else:
            mix = _fourier_call(x, None, mod_all, i, n1g, ctx)
            mix = _fourier_call(x, mix, mod_all, i, n1g, lat)
            w_out = p["fourier_w_out"]
        x, xb, route, counts = _post_call(x, mix, mod_all, i, w_out, j, p["norm2_g"][i:i + 1],
                                          p["router_w_t"][i], p["router_b"][i], cond_row)
        final_g = p["final_norm_g"] if i == DEPTH - 1 else None
        x = _moe(x, xb, route, counts, mod_all, i, p["moe_w_gate"], p["moe_w_up"], p["moe_w_down"],
                 final_g, n_ctx_blocks, cond_row)
    return x, states


def kernel(x_prompt, x_sample, state_hgrn, c, c_ctx, w_mod, b_mod, norm1_g, norm2_g, hgrn_w_in,
           hgrn_lb_logits, hgrn_norm_g, hgrn_w_out, fourier_w_out, router_group_w, router_group_b,
           router_expert_w, router_expert_b, moe_w_gate, moe_w_up, moe_w_down, final_norm_g):
    batch, seq, _ = x_prompt.shape
    dec_batch, dec_seq, _ = x_sample.shape
    assert dec_batch <= CTX_ROW and seq == ROW_TILE and dec_seq == SEQ_BLOCK

    cond = jnp.zeros((COND_ROWS, D_MODEL), F32).at[:dec_batch].set(c).at[CTX_ROW].set(c_ctx)
    mod_all = _mod_call(cond, w_mod, b_mod).reshape(DEPTH, COND_ROWS, 6, D_MODEL)

    probs = jax.nn.softmax(hgrn_lb_logits.astype(F32), axis=0)
    cs = jnp.cumsum(probs, axis=0)
    lb = cs - cs[0:1]
    router_w = jnp.concatenate([router_group_w, router_expert_w], axis=-1)
    router_b = jnp.concatenate([router_group_b, router_expert_b], axis=-1)
    pad = ROUTER_ROWS - router_w.shape[-1]
    p = {
        "norm1_g": norm1_g, "norm2_g": norm2_g, "hgrn_norm_g": hgrn_norm_g,
        "hgrn_w_in": hgrn_w_in, "hgrn_w_out": hgrn_w_out.astype(BF16),
        "fourier_w_out": fourier_w_out.astype(BF16),
        "lb": jnp.concatenate([lb, 1.0 - lb], axis=1), "log_1m_lb": jnp.log1p(-lb),
        "router_w_t": jnp.pad(jnp.swapaxes(router_w, 1, 2), ((0, 0), (0, pad), (0, 0))),
        "router_b": jnp.pad(router_b, ((0, 0), (0, pad)))[..., None],
        "moe_w_gate": moe_w_gate, "moe_w_up": moe_w_up, "moe_w_down": moe_w_down,
        "final_norm_g": final_norm_g.reshape(1, D_MODEL),
    }

    t_ctx = batch * seq
    t_lat = dec_batch * dec_seq
    ctx = _Group(0, t_ctx // SEQ_BLOCK, seq, latent=False)
    lat = _Group(t_ctx // SEQ_BLOCK, t_lat // SEQ_BLOCK, dec_seq, latent=True)
    x = _embed_call(x_prompt.reshape(t_ctx, D_MODEL), x_sample.reshape(t_lat, D_MODEL), _grid_pos_embed(dec_seq))
    (y_ctx, y_lat), new_state = _run_trunk(x, ctx, lat, state_hgrn, mod_all, p)
    return (y_ctx.reshape(batch, seq, D_MODEL), y_lat.reshape(dec_batch, dec_seq, D_MODEL), new_state)
```

```python
import functools
import math

import jax
import jax.numpy as jnp
from jax import lax
from jax.experimental import pallas as pl
from jax.experimental.pallas import tpu as pltpu

F32 = jnp.float32
BF16 = jnp.bfloat16
HIGHEST = lax.Precision.HIGHEST

D_MODEL = 1024
DEPTH = 4
GRID_W = 64
HEADS = 8
DK = 128
DV = 128
FOURIER_GROUPS = 4
FOURIER_CG = D_MODEL // FOURIER_GROUPS
MOE_GROUPS = 4
MOE_EPG = 4
MOE_EXPERTS = 16
MOE_D_FF = 512
NORM_EPS = 1e-6
POS_BASE = 10000.0
DFT_SPLIT = 64
MAX_SUB_DECAY = 100.0
LOG2_E = 1.0 / math.log(2.0)

COND_ROWS = 8
CTX_ROW = 4
ROUTER_ROWS = 32
LANES = 128
SCAN_CHUNK = 64
SCAN_SUB = 16
GROUP = 4
GROUP_ROWS = GROUP * SCAN_CHUNK
LOCAL_GROUPS = 4
PROJ_TILES = 8
SEQ_BLOCK = 2048
ROW_TILE = 256
TOKEN_BLOCK = 256
POST_BLOCKS = 4
COMBINE_BLOCKS = 2
MOE_TOP_K = 2
ROW_UNIT = 16
BLOCK_ROWS = 768
FFN_TILE = 512
VMEM_LIMIT = 56 * 1024 * 1024


def _cparams(*sem):
    return pltpu.CompilerParams(dimension_semantics=sem, vmem_limit_bytes=VMEM_LIMIT)


def _silu(x):
    return x * jax.nn.sigmoid(x)


def _norm_mod(x, g, shift, scale):
    ms = jnp.mean(x * x, axis=-1, keepdims=True)
    return (x * lax.rsqrt(ms + NORM_EPS) * g) * (1.0 + scale) + shift


def _mod_body(c_ref, w_ref, b_ref, o_ref):
    s = _silu(c_ref[...])
    o_ref[0] = jnp.dot(s, w_ref[0], precision=HIGHEST, preferred_element_type=F32) + b_ref[0]


def _mod_call(cond, w_mod, b_mod):
    n_col = 6 * D_MODEL // D_MODEL
    return pl.pallas_call(
        _mod_body,
        grid=(DEPTH, n_col),
        in_specs=[
            pl.BlockSpec((COND_ROWS, D_MODEL), lambda i, n: (0, 0)),
            pl.BlockSpec((1, D_MODEL, D_MODEL), lambda i, n: (i, 0, n)),
            pl.BlockSpec((1, 1, D_MODEL), lambda i, n: (i, 0, n)),
        ],
        out_specs=pl.BlockSpec((1, COND_ROWS, D_MODEL), lambda i, n: (i, 0, n)),
        out_shape=jax.ShapeDtypeStruct((DEPTH, COND_ROWS, 6 * D_MODEL), F32),
        compiler_params=_cparams("arbitrary", "arbitrary"),
        name="mod",
    )(cond, w_mod, b_mod.reshape(DEPTH, 1, 6 * D_MODEL))


def _embed_body(xc_ref, xl_ref, p_ref, o_ref, *, n_ctx):
    i = pl.program_id(0)

    @pl.when(i < n_ctx)
    def _():
        o_ref[...] = xc_ref[...]

    @pl.when(i >= n_ctx)
    def _():
        o_ref[...] = xl_ref[...] + p_ref[...]


def _embed_call(x_ctx, x_lat, pos):
    n_ctx = x_ctx.shape[0] // TOKEN_BLOCK
    n_lat = x_lat.shape[0] // TOKEN_BLOCK
    per_seq = pos.shape[0] // TOKEN_BLOCK
    return pl.pallas_call(
        functools.partial(_embed_body, n_ctx=n_ctx),
        grid=(n_ctx + n_lat,),
        in_specs=[
            pl.BlockSpec((TOKEN_BLOCK, D_MODEL), lambda i: (jnp.minimum(i, n_ctx - 1), 0)),
            pl.BlockSpec((TOKEN_BLOCK, D_MODEL), lambda i: (jnp.maximum(i - n_ctx, 0), 0)),
            pl.BlockSpec((TOKEN_BLOCK, D_MODEL), lambda i: (jnp.maximum(i - n_ctx, 0) % per_seq, 0)),
        ],
        out_specs=pl.BlockSpec((TOKEN_BLOCK, D_MODEL), lambda i: (i, 0)),
        out_shape=jax.ShapeDtypeStruct((x_ctx.shape[0] + x_lat.shape[0], D_MODEL), F32),
        compiler_params=_cparams("arbitrary"),
        name="embed",
    )(x_ctx, x_lat, pos)


def _log_f_and_key(z, lb, one_m_lb, log_1m_lb):
    t = jnp.exp(-jnp.abs(z))
    big = 1.0 / (1.0 + t)
    small = t * big
    pos = z >= 0.0
    f = lb + one_m_lb * jnp.where(pos, big, small)
    log_f = jnp.where(f > 0.0, jnp.log(f), log_1m_lb + z)
    return log_f, one_m_lb * jnp.where(pos, small, big)


def _group_masks():
    n = GROUP_ROWS
    t = lax.broadcasted_iota(jnp.int32, (n, n), 0)
    s = lax.broadcasted_iota(jnp.int32, (n, n), 1)
    same_chunk = (t // SCAN_CHUNK) == (s // SCAN_CHUNK)
    same_sub = (t // SCAN_SUB) == (s // SCAN_SUB)
    prefix = (same_chunk & (t >= s)).astype(BF16)
    return prefix, same_chunk, same_sub & (t >= s), same_sub & (t <= s)


def _qk(a, b):
    return lax.dot_general(a, b, (((1,), (1,)), ((), ())), preferred_element_type=F32)


def _split2(x):
    hi = x.astype(BF16)
    return [hi, (x - hi.astype(F32)).astype(BF16)]


def _sum2(s, i):
    o = 2 * i * DK
    return s[:, o:o + DK] + s[:, o + DK:o + 2 * DK]


def _same_sub_block_pairs(b, q, k, v, reverse):
    n = GROUP_ROWS
    pos = lax.broadcasted_iota(jnp.int32, (n, 1), 0) % SCAN_SUB
    vf = v.astype(F32)
    out = jnp.zeros((n, DV), F32)
    for d in range(SCAN_SUB):
        shift = (n - d) % n if reverse else d
        ks, bs, vs = (pltpu.roll(a, shift, 0) if shift else a for a in (k, b, vf))
        inside = (pos + d < SCAN_SUB) if reverse else (pos >= d)
        w = jnp.sum(q * ks * jnp.exp2(jnp.minimum(b - bs, 0.0)), axis=1, keepdims=True)
        out = out + jnp.where(inside, w, 0.0) * vs
    return out


def _group_dir(b, q, k, v, vt, m_chunk, m_diag, unsafe, reverse):
    c, sb = SCAN_CHUNK, SCAN_SUB
    nb = c // sb
    zero = jnp.zeros((sb, DK), BF16)
    qd, kd, qs, kl, decs = [], [], [], [], []
    q_lev = [[] for _ in range(nb - 1)]
    k_lev = [[] for _ in range(nb - 1)]
    for g in range(GROUP):
        starts = [g * c + i * sb for i in range(nb)]
        pos = [nb - 1 - i for i in range(nb)] if reverse else list(range(nb))
        end_at = {}
        for i, r in enumerate(starts):
            e = r if reverse else r + sb - 1
            end_at[pos[i]] = b[e:e + 1]
        edge = end_at[nb - 1]
        decs.append(jnp.exp2(edge))
        for i, r in enumerate(starts):
            m = r + sb // 2 if reverse else r + sb // 2 - 1
            bi, qi, ki, mid = b[r:r + sb], q[r:r + sb], k[r:r + sb], b[m:m + 1]
            qd.append((qi * jnp.exp2(bi - mid)).astype(BF16))
            kd.append((ki * jnp.exp2(mid - bi)).astype(BF16))
            for j in range(nb - 1):
                q_lev[j].append((qi * jnp.exp2(bi - end_at[j])).astype(BF16) if pos[i] > j else zero)
                k_lev[j].append((ki * jnp.exp2(end_at[j] - bi)).astype(BF16) if pos[i] == j else zero)
            qs.append((qi * jnp.exp2(bi)).astype(BF16))
            kl.append((ki * jnp.exp2(edge - bi)).astype(BF16))

    def rows(blocks):
        return jnp.concatenate(blocks, axis=0)
    s_lev = _qk(jnp.concatenate([rows(x) for x in q_lev], axis=1),
                jnp.concatenate([rows(x) for x in k_lev], axis=1))

    s_diag = _qk(rows(qd), rows(kd))
    scores = jnp.where(m_diag & jnp.logical_not(unsafe), s_diag, 0.0) + jnp.where(m_chunk, s_lev, 0.0)
    o = jnp.dot(scores.astype(BF16), v, preferred_element_type=F32)
    kl_wide = jnp.concatenate(
        [rows([blk if n // nb == g else zero for n, blk in enumerate(kl)]) for g in range(GROUP)], axis=1)
    upd_t = jnp.dot(vt, kl_wide, preferred_element_type=F32)
    return o, rows(qs), upd_t, decs


def _hgrn_body(*refs, n_in, n_seq, cps, has_s0):
    x_ref, mod_ref, n1g_ref = refs[:3]
    w_refs = refs[3:8]
    lb_ref, l1m_ref, ng_ref = refs[8:11]
    s0_ref = refs[11] if has_s0 else None
    og_ref = refs[n_in]
    st_ref = None if has_s0 else refs[n_in + 1]
    (h_scr, q_scr, v_scr, g_scr, lff_scr, kf_scr, lfb_scr, kb_scr, of_scr, ob_scr, qsf_scr, qsb_scr,
     uf_scr, ub_scr, decf_scr, decb_scr, sf_scr, sb_scr, w_ref, b_scr) = refs[n_in + (1 if has_s0 else 2):]
    c = SCAN_CHUNK
    n_chunks = SEQ_BLOCK // c
    n_tiles = SEQ_BLOCK // ROW_TILE
    for s, ref in enumerate(w_refs):
        w_ref[:, s * DK:(s + 1) * DK] = ref[...].astype(BF16)

    @pl.when(pl.program_id(1) == 0)
    def _():
        m = mod_ref[0]
        g = n1g_ref[...]

        def tile(t, carry):
            rows = pl.ds(pl.multiple_of(t * ROW_TILE, ROW_TILE), ROW_TILE)
            h_scr[rows, :] = _norm_mod(x_ref[rows, :], g, m[0:1], m[1:2]).astype(BF16)
            return carry
        lax.fori_loop(0, n_tiles, tile, 0)

    lb = lb_ref[...]
    l1m = l1m_ref[...]

    def proj_tiles(g, carry):
        tiles = []
        for u in range(PROJ_TILES):
            rows = pl.ds(pl.multiple_of((PROJ_TILES * g + u) * ROW_TILE, ROW_TILE), ROW_TILE)
            tiles.append((rows, h_scr[rows, :]))
        outs = []
        for rows, h in tiles:
            p = jnp.dot(h, w_ref[...], preferred_element_type=F32)
            outs.append((rows, p, _log_f_and_key(p[:, DK:2 * DK], lb[0:1], lb[2:3], l1m[0:1]),
                         _log_f_and_key(p[:, 2 * DK:3 * DK], lb[1:2], lb[3:4], l1m[1:2])))
        for rows, p, (lf_f, k_f), (lf_b, k_b) in outs:
            q_scr[rows, :] = p[:, 0:DK]
            v_scr[rows, :] = p[:, 3 * DK:3 * DK + DV]
            g_scr[rows, :] = p[:, 3 * DK + DV:3 * DK + 2 * DV]
            lff_scr[rows, :] = lf_f
            kf_scr[rows, :] = k_f
            lfb_scr[rows, :] = lf_b
            kb_scr[rows, :] = k_b
        return carry
    lax.fori_loop(0, n_tiles // PROJ_TILES, proj_tiles, 0)

    prefix, m_chunk, m_diag_f, m_diag_b = _group_masks()

    def group_rows(grp):
        return pl.ds(pl.multiple_of(grp * GROUP_ROWS, GROUP_ROWS), GROUP_ROWS)

    def local_load(grp):
        rows = group_rows(grp)
        return (q_scr[rows, :], v_scr[rows, :], lff_scr[rows, :], kf_scr[rows, :], lfb_scr[rows, :],
                kb_scr[rows, :])

    def log2_decays(lf_f, lf_b):
        sums = jnp.dot(prefix, jnp.concatenate(_split2(lf_f) + _split2(lf_b), axis=1),
                       preferred_element_type=F32)
        pre_b = _sum2(sums, 1)
        total_b = jnp.concatenate(
            [jnp.broadcast_to(pre_b[g * c + c - 1:g * c + c], (c, DK)) for g in range(GROUP)], axis=0)
        return _sum2(sums, 0) * LOG2_E, ((total_b - pre_b) + lf_b) * LOG2_E

    def sub_block_span(b):
        return functools.reduce(jnp.maximum, [jnp.abs(b[r:r + 1] - b[r + SCAN_SUB - 1:r + SCAN_SUB])
                                              for r in range(0, GROUP_ROWS, SCAN_SUB)])

    def local_compute(b_f, b_b, unsafe, q, v32, lf_f, k_f, lf_b, k_b):
        v = v32.astype(BF16)
        vt = v32.T.astype(BF16)
        return (_group_dir(b_f, q, k_f, v, vt, m_chunk, m_diag_f, unsafe, reverse=False),
                _group_dir(b_b, q, k_b, v, vt, m_chunk, m_diag_b, unsafe, reverse=True))

    def local_store(grp, fwd, bwd):
        rows = group_rows(grp)
        for (o, qs, upd_t, dec), o_scr, qs_scr, u_scr, dec_scr in (
                (fwd, of_scr, qsf_scr, uf_scr, decf_scr), (bwd, ob_scr, qsb_scr, ub_scr, decb_scr)):
            o_scr[rows, :] = o
            qs_scr[rows, :] = qs
            for g in range(GROUP):
                u_scr[grp * GROUP + g] = upd_t[:, g * DK:(g + 1) * DK]
                dec_scr[grp * GROUP + g] = dec[g]

    def local(it, carry):
        groups = [it * LOCAL_GROUPS + u for u in range(LOCAL_GROUPS)]
        loaded = [local_load(grp) for grp in groups]
        decays = [log2_decays(vals[2], vals[4]) for vals in loaded]
        span = functools.reduce(jnp.maximum, [sub_block_span(b) for pair in decays for b in pair])
        unsafe = jnp.max(span) > MAX_SUB_DECAY
        for u, (b_f, b_b) in enumerate(decays):
            b_scr[u, 0] = b_f
            b_scr[u, 1] = b_b
        results = [local_compute(b_f, b_b, unsafe, *vals) for (b_f, b_b), vals in zip(decays, loaded)]
        for grp, (fwd, bwd) in zip(groups, results):
            local_store(grp, fwd, bwd)

        @pl.when(unsafe)
        def _():
            for u, grp in enumerate(groups):
                rows = group_rows(grp)
                q, v = q_scr[rows, :], v_scr[rows, :].astype(BF16)
                of_scr[rows, :] += _same_sub_block_pairs(b_scr[u, 0], q, kf_scr[rows, :], v, reverse=False)
                ob_scr[rows, :] += _same_sub_block_pairs(b_scr[u, 1], q, kb_scr[rows, :], v, reverse=True)
        return carry
    lax.fori_loop(0, n_chunks // GROUP // LOCAL_GROUPS, local, 0)

    def advance(ci, cj, sf, sb):
        sf_scr[ci] = sf.astype(BF16)
        sb_scr[cj] = sb.astype(BF16)
        return sf * decf_scr[ci] + uf_scr[ci], sb * decb_scr[cj] + ub_scr[cj]

    if has_s0:
        def step(i, carry):
            return advance(i, n_chunks - 1 - i, *carry)
        lax.fori_loop(0, n_chunks, step, (s0_ref[0].T, s0_ref[1].T), unroll=4)
    else:
        def seq(s, carry):
            sf = jnp.zeros((DV, DK), F32)
            sb = jnp.zeros((DV, DK), F32)
            for i in range(cps):
                sf, sb = advance(s * cps + i, s * cps + cps - 1 - i, sf, sb)
            st_ref[s, 0] = sf.T
            st_ref[s, 1] = sb.T
            return carry
        lax.fori_loop(0, n_seq, seq, 0)

    ng = ng_ref[...]
    zero_chunk = jnp.zeros((c, DK), BF16)

    def widen(qs):
        return jnp.concatenate(
            [jnp.concatenate([qs[h * c:(h + 1) * c] if h == g else zero_chunk for h in range(GROUP)], axis=0)
             for g in range(GROUP)], axis=1)

    def out_groups(it, carry):
        loaded = []
        for u in range(LOCAL_GROUPS):
            grp = it * LOCAL_GROUPS + u
            rows = group_rows(grp)
            qs = jnp.concatenate([widen(qsf_scr[rows, :]), widen(qsb_scr[rows, :])], axis=1)
            st = jnp.concatenate([sf_scr[grp * GROUP + g] for g in range(GROUP)]
                                 + [sb_scr[grp * GROUP + g] for g in range(GROUP)], axis=1)
            loaded.append((rows, qs, st, of_scr[rows, :] + ob_scr[rows, :], g_scr[rows, :]))
        for rows, qs, st, o_local, gate in loaded:
            o = o_local + _qk(qs, st)
            o = o * lax.rsqrt(jnp.mean(o * o, axis=-1, keepdims=True) + NORM_EPS) * ng
            og_ref[rows, :] = (o * _silu(gate)).astype(BF16)
        return carry
    lax.fori_loop(0, n_chunks // GROUP // LOCAL_GROUPS, out_groups, 0)


class _Group:
    def __init__(self, first, n, seq_len, latent):
        self.first, self.n, self.seq_len, self.latent = first, n, seq_len, latent

    def cond_row(self, block):
        return block if self.latent else CTX_ROW


def _hgrn_call(x, mix, mod_all, layer, n1g, w_in, lb, log_1m_lb, norm_g, j, grp, s0, states):
    t = x.shape[0]
    n_seq = SEQ_BLOCK // grp.seq_len
    cps = grp.seq_len // SCAN_CHUNK
    n_chunks = SEQ_BLOCK // SCAN_CHUNK
    has_s0 = s0 is not None
    n_layers = w_in.shape[0]
    w_cat = pltpu.VMEM((D_MODEL, 3 * DK + 2 * DV), BF16)
    decays = pltpu.VMEM((LOCAL_GROUPS, 2, GROUP_ROWS, DK), F32)

    def section(s):
        return pl.BlockSpec((None, D_MODEL, DK), lambda b, h: (j, 0, s * HEADS + h))

    in_specs = [
        pl.BlockSpec((SEQ_BLOCK, D_MODEL), lambda b, h: (b + grp.first, 0)),
        pl.BlockSpec((None, 1, 6, D_MODEL), lambda b, h: (layer, grp.cond_row(b), 0, 0)),
        pl.BlockSpec((1, D_MODEL), lambda b, h: (0, 0)),
        section(0), section(1), section(2), section(3), section(4),
        pl.BlockSpec((4, DK), lambda b, h: (0, h)),
        pl.BlockSpec((2, DK), lambda b, h: (0, h)),
        pl.BlockSpec((1, DV), lambda b, h: (0, h)),
    ]
    args = [x, mod_all, n1g, w_in, w_in, w_in, w_in, w_in, lb, log_1m_lb, norm_g]
    og_spec = pl.BlockSpec((SEQ_BLOCK, DV), lambda b, h: (b + grp.first, h))
    og_shape = jax.ShapeDtypeStruct((t, D_MODEL), BF16)
    aliases = {}
    if has_s0:
        in_specs.append(pl.BlockSpec((None, None, 2, None, DK, DV), lambda b, h: (b, j, 0, h, 0, 0)))
        args.append(s0)
        out_specs, out_shape = og_spec, og_shape
    else:
        st_shape = (grp.n * n_seq, n_layers, 2, HEADS, DK, DV)
        out_specs = (og_spec,
                     pl.BlockSpec((n_seq, None, 2, None, DK, DV), lambda b, h: (b, j, 0, h, 0, 0)))
        out_shape = (og_shape, jax.ShapeDtypeStruct(st_shape, F32))
        if states is not None:
            in_specs.append(pl.BlockSpec(memory_space=pl.ANY))
            args.append(states)
            aliases[len(args) - 1] = 1
    if mix is not None:
        in_specs.append(pl.BlockSpec(memory_space=pl.ANY))
        args.append(mix)
        aliases[len(args) - 1] = 0
    col = pltpu.VMEM((SEQ_BLOCK, DK), F32)
    colb = pltpu.VMEM((SEQ_BLOCK, DK), BF16)
    upd = pltpu.VMEM((n_chunks, DV, DK), F32)
    dec = pltpu.VMEM((n_chunks, 1, DK), F32)
    start = pltpu.VMEM((n_chunks, DV, DK), BF16)
    res = pl.pallas_call(
        functools.partial(_hgrn_body, n_in=len(args), n_seq=n_seq, cps=cps, has_s0=has_s0),
        grid=(grp.n, HEADS),
        in_specs=in_specs,
        out_specs=out_specs,
        out_shape=out_shape,
        scratch_shapes=[pltpu.VMEM((SEQ_BLOCK, D_MODEL), BF16)] + [col] * 9 + [colb, colb, upd, upd, dec, dec,
                                                                                  start, start, w_cat, decays],
        input_output_aliases=aliases,
        compiler_params=_cparams("arbitrary", "arbitrary"),
        name="hgrn_lat" if has_s0 else "hgrn_ctx",
    )(*args)
    return (res, None) if has_s0 else res


def _fourier_body(*refs, n_in, seq_len):
    x_ref, mod_ref, n1g_ref, cl_ref, sl_ref, cc_ref = refs[:6]
    z_ref, h_scr = refs[n_in:]
    rt = pl.program_id(1)
    n_tiles = SEQ_BLOCK // ROW_TILE

    @pl.when(rt == 0)
    def _():
        m = mod_ref[0]
        g = n1g_ref[...]

        def tile(t, carry):
            rows = pl.ds(pl.multiple_of(t * ROW_TILE, ROW_TILE), ROW_TILE)
            h_scr[rows, :] = _norm_mod(x_ref[rows, :], g, m[0:1], m[1:2]).astype(BF16)
            return carry
        lax.fori_loop(0, n_tiles, tile, 0)

    if seq_len == ROW_TILE:
        h = h_scr[pl.ds(pl.multiple_of(rt * ROW_TILE, ROW_TILE), ROW_TILE), :]
    else:
        h = h_scr[...]
    zc = jnp.dot(cl_ref[...], h, preferred_element_type=F32).astype(BF16)
    zs = jnp.dot(sl_ref[...], h, preferred_element_type=F32).astype(BF16)
    scale = 1.0 / math.sqrt(seq_len * FOURIER_CG)
    cg = FOURIER_CG
    for g in range(FOURIER_GROUPS):
        cat = jnp.concatenate([zc[:, g * cg:(g + 1) * cg], zs[:, g * cg:(g + 1) * cg]], axis=1)
        out = jnp.dot(cat, cc_ref[...], preferred_element_type=F32) * scale
        z_ref[:, g * cg:(g + 1) * cg] = out.astype(BF16)


def _dft_tables(n):
    j = jnp.arange(n, dtype=jnp.int32)

    def direct(k):
        ang = ((j[:, None] * k[None, :]) % n).astype(F32) * (2.0 * math.pi / n)
        return jnp.cos(ang), jnp.sin(ang)
    if n <= DFT_SPLIT:
        return direct(j)
    ca, sa = direct(jnp.arange(n // DFT_SPLIT, dtype=jnp.int32) * DFT_SPLIT)
    cb, sb = direct(jnp.arange(DFT_SPLIT, dtype=jnp.int32))
    cos = ca[:, :, None] * cb[:, None, :] - sa[:, :, None] * sb[:, None, :]
    sin = sa[:, :, None] * cb[:, None, :] + ca[:, :, None] * sb[:, None, :]
    return cos.reshape(n, n), sin.reshape(n, n)


def _fourier_call(x, mix, mod_all, layer, n1g, grp):
    t = x.shape[0]
    seq_len = grp.seq_len
    n_rt = SEQ_BLOCK // ROW_TILE
    cl, sl = _dft_tables(seq_len)
    cc, sc = _dft_tables(FOURIER_CG)
    cc2 = jnp.concatenate([cc, -sc], axis=0).astype(BF16)
    if seq_len == ROW_TILE:
        pos_spec = pl.BlockSpec((ROW_TILE, seq_len), lambda b, r: (0, 0))
    else:
        pos_spec = pl.BlockSpec((ROW_TILE, seq_len), lambda b, r: (r, 0))
    in_specs = [
        pl.BlockSpec((SEQ_BLOCK, D_MODEL), lambda b, r: (b + grp.first, 0)),
        pl.BlockSpec((None, 1, 6, D_MODEL), lambda b, r: (layer, grp.cond_row(b), 0, 0)),
        pl.BlockSpec((1, D_MODEL), lambda b, r: (0, 0)),
        pos_spec, pos_spec,
        pl.BlockSpec((2 * FOURIER_CG, FOURIER_CG), lambda b, r: (0, 0)),
    ]
    args = [x, mod_all, n1g, cl.astype(BF16), sl.astype(BF16), cc2]
    aliases = {}
    if mix is not None:
        in_specs.append(pl.BlockSpec(memory_space=pl.ANY))
        args.append(mix)
        aliases[len(args) - 1] = 0
    return pl.pallas_call(
        functools.partial(_fourier_body, n_in=len(args), seq_len=seq_len),
        grid=(grp.n, n_rt),
        in_specs=in_specs,
        out_specs=pl.BlockSpec((ROW_TILE, D_MODEL), lambda b, r: ((b + grp.first) * n_rt + r, 0)),
        out_shape=jax.ShapeDtypeStruct((t, D_MODEL), BF16),
        scratch_shapes=[pltpu.VMEM((SEQ_BLOCK, D_MODEL), BF16)],
        input_output_aliases=aliases,
        compiler_params=_cparams("arbitrary", "arbitrary"),
        name="fourier_lat" if grp.latent else "fourier_ctx",
    )(*args)


def _route(lg):
    grp = [lg[g:g + 1] for g in range(MOE_GROUPS)]
    gmax = functools.reduce(jnp.maximum, grp)
    gi = jnp.where(grp[0] == gmax, 0, jnp.where(grp[1] == gmax, 1, jnp.where(grp[2] == gmax, 2, 3)))
    pgv = 1.0 / functools.reduce(lambda a, b: a + b, [jnp.exp(g - gmax) for g in grp])
    sel = []
    for j in range(MOE_EPG):
        rows = [lg[MOE_GROUPS + g * MOE_EPG + j:MOE_GROUPS + g * MOE_EPG + j + 1] for g in range(MOE_GROUPS)]
        sel.append(jnp.where(gi == 0, rows[0], jnp.where(gi == 1, rows[1], jnp.where(gi == 2, rows[2], rows[3]))))
    m1 = functools.reduce(jnp.maximum, sel)
    i1 = jnp.where(sel[0] == m1, 0, jnp.where(sel[1] == m1, 1, jnp.where(sel[2] == m1, 2, 3)))
    rest = [jnp.where(i1 == j, -jnp.inf, sel[j]) for j in range(MOE_EPG)]
    m2 = functools.reduce(jnp.maximum, rest)
    i2 = jnp.where(rest[0] == m2, 0, jnp.where(rest[1] == m2, 1, jnp.where(rest[2] == m2, 2, 3)))
    e2 = jnp.exp(m2 - m1)
    w1 = pgv / (1.0 + e2)
    w2 = pgv * e2 / (1.0 + e2)
    r = lg.shape[1]
    ex1 = gi * MOE_EPG + i1
    ex2 = gi * MOE_EPG + i2
    eid = lax.broadcasted_iota(jnp.int32, (MOE_EXPERTS, r), 0)
    member = (eid == ex1) | (eid == ex2)
    t0 = lax.broadcasted_iota(jnp.int32, (r, r), 0)
    t1 = lax.broadcasted_iota(jnp.int32, (r, r), 1)
    rank = jnp.dot(member.astype(BF16), (t0 < t1).astype(BF16), preferred_element_type=F32)
    count = jnp.sum(member.astype(F32), axis=1, keepdims=True)
    padded = jnp.floor((count + (ROW_UNIT - 1)) * (1.0 / ROW_UNIT)) * ROW_UNIT
    e0 = lax.broadcasted_iota(jnp.int32, (MOE_EXPERTS, MOE_EXPERTS), 0)
    e1 = lax.broadcasted_iota(jnp.int32, (MOE_EXPERTS, MOE_EXPERTS), 1)
    start = jnp.dot((e1 < e0).astype(F32), jnp.broadcast_to(padded, (MOE_EXPERTS, LANES)),
                    precision=HIGHEST, preferred_element_type=F32)[:, 0:1]
    row = start + rank
    row1 = jnp.sum(jnp.where(eid == ex1, row, 0.0), axis=0, keepdims=True)
    row2 = jnp.sum(jnp.where(eid == ex2, row, 0.0), axis=0, keepdims=True)
    rid = lax.broadcasted_iota(jnp.int32, (LANES, r), 0)
    fields = (row1, row2, w1, w2, ex1.astype(F32), ex2.astype(F32))
    table = jnp.zeros((LANES, r), F32)
    for i, f in enumerate(fields):
        table = jnp.where(rid == i, f, table)
    return table, count


def _post_body(x_ref, mix_ref, mod_ref, w_ref, n2g_ref, wr_ref, br_ref, xo_ref, xb_ref, ct_ref, cnt_ref):
    m = mod_ref[0]
    out = jnp.dot(mix_ref[...], w_ref[...], preferred_element_type=F32)
    xn = x_ref[...] + m[2:3] * out
    xo_ref[...] = xn
    h2 = _norm_mod(xn, n2g_ref[...], m[3:4], m[4:5])
    wr_hi, wr_lo = _split2(wr_ref[...])
    h2_hi, h2_lo = _split2(h2)
    lg = _qk(wr_hi, h2_hi) + (_qk(wr_hi, h2_lo) + _qk(wr_lo, h2_hi)) + br_ref[...]
    h2 = h2_hi
    rid = lax.broadcasted_iota(jnp.int32, (BLOCK_ROWS, TOKEN_BLOCK), 0)
    for u in range(POST_BLOCKS):
        cols = slice(u * TOKEN_BLOCK, (u + 1) * TOKEN_BLOCK)
        table, count = _route(lg[:, cols])
        onehot = ((rid == table[0:1].astype(jnp.int32)) | (rid == table[1:2].astype(jnp.int32))).astype(BF16)
        xb_ref[u * BLOCK_ROWS:(u + 1) * BLOCK_ROWS, :] = jnp.dot(
            onehot, h2[cols, :], preferred_element_type=F32).astype(BF16)
        ct_ref[cols, :] = table.T
        cnt_ref[u] = jnp.broadcast_to(count, (MOE_EXPERTS, LANES))


def _post_call(x, mix, mod_all, layer, w_out, j, n2g, wr_t, br, mod_row):
    t = x.shape[0]
    nb = t // TOKEN_BLOCK
    rows = POST_BLOCKS * TOKEN_BLOCK
    tile = lambda i: (i, 0)
    full = lambda i: (0, 0)
    return pl.pallas_call(
        _post_body,
        grid=(nb // POST_BLOCKS,),
        in_specs=[
            pl.BlockSpec((rows, D_MODEL), tile),
            pl.BlockSpec((rows, D_MODEL), tile),
            pl.BlockSpec((None, 1, 6, D_MODEL), lambda i: (layer, mod_row(i * POST_BLOCKS), 0, 0)),
            pl.BlockSpec((None, D_MODEL, D_MODEL), lambda i: (j, 0, 0)),
            pl.BlockSpec((1, D_MODEL), full),
            pl.BlockSpec((ROUTER_ROWS, D_MODEL), full),
            pl.BlockSpec((ROUTER_ROWS, 1), full),
        ],
        out_specs=(
            pl.BlockSpec((rows, D_MODEL), tile),
            pl.BlockSpec((POST_BLOCKS * BLOCK_ROWS, D_MODEL), tile),
            pl.BlockSpec((rows, LANES), tile),
            pl.BlockSpec((POST_BLOCKS, MOE_EXPERTS, LANES), lambda i: (i, 0, 0)),
        ),
        out_shape=(
            jax.ShapeDtypeStruct((t, D_MODEL), F32),
            jax.ShapeDtypeStruct((nb * BLOCK_ROWS, D_MODEL), BF16),
            jax.ShapeDtypeStruct((t, LANES), F32),
            jax.ShapeDtypeStruct((nb, MOE_EXPERTS, LANES), F32),
        ),
        compiler_params=_cparams("arbitrary"),
        name="post",
    )(x, mix, mod_all, w_out, n2g, wr_t, br)


def _moe_schedule(counts, n_tiles_max):
    units = (counts + ROW_UNIT - 1) // ROW_UNIT
    local = jnp.cumsum(units, axis=1) - units
    total = jnp.sum(units, axis=0)
    per_tile = FFN_TILE // ROW_UNIT
    tiles = (total + per_tile - 1) // per_tile
    region = tiles * per_tile
    region_start = jnp.cumsum(region) - region
    glob = region_start[None, :] + jnp.cumsum(units, axis=0) - units
    tile_end = jnp.cumsum(tiles)
    tile_start = tile_end - tiles
    n_tiles = tile_end[-1]
    s = jnp.arange(n_tiles_max, dtype=jnp.int32)
    live = s < n_tiles
    tile_expert = jnp.sum((tile_end[None, :] <= jnp.minimum(s, n_tiles - 1)[:, None]).astype(jnp.int32), axis=1)
    first = jnp.any((s[:, None] == tile_start[None, :]) & (tiles[None, :] > 0), axis=1) & live
    g = jnp.arange(n_tiles_max * per_tile, dtype=jnp.int32)
    block_units = BLOCK_ROWS // ROW_UNIT
    blocks = jnp.arange(counts.shape[0], dtype=jnp.int32)[:, None]
    seg_start = glob.T.reshape(-1)
    seg_offset = (blocks * block_units + local - glob).T.reshape(-1)
    step = seg_offset - jnp.concatenate([jnp.zeros((1,), jnp.int32), seg_offset[:-1]])
    source = g + jnp.sum(jnp.where(seg_start[None, :] <= g[:, None], step[None, :], 0), axis=1)
    real = jnp.any((g[:, None] >= region_start[None, :]) & (g[:, None] < (region_start + total)[None, :]), axis=1)
    source = jnp.where(real, source, block_units - 1)
    i32 = lambda a: a.astype(jnp.int32).reshape(-1)
    return {
        "local": i32(local), "units": i32(units), "glob": i32(glob), "source": i32(source),
        "tile_expert": tile_expert, "tile_first": i32(first), "n_tiles": i32(n_tiles),
    }


def _segment_copies(local_ref, units_ref, glob_ref, blk, make_copy):
    def per_expert(e, n):
        idx = blk * MOE_EXPERTS + e
        loc, cnt, glo = local_ref[idx], units_ref[idx], glob_ref[idx]

        def unit(u, carry):
            make_copy(pl.multiple_of((loc + u) * ROW_UNIT, ROW_UNIT),
                      pl.multiple_of((glo + u) * ROW_UNIT, ROW_UNIT)).start()
            return carry
        lax.fori_loop(0, cnt, unit, 0)
        return n + cnt
    return lax.fori_loop(0, MOE_EXPERTS, per_expert, 0)


def _wait_copies(n, make_copy):
    def one(u, carry):
        make_copy(0, 0).wait()
        return carry
    lax.fori_loop(0, n, one, 0)


def _ffn_body(expert_ref, first_ref, n_ref, source_ref, xb_ref, wg_ref, wu_ref, wd_ref, y_ref,
              x_scr, sem, wg_scr, wu_scr, wd_scr):
    s = pl.program_id(0)
    slot = s % 2
    per_tile = FFN_TILE // ROW_UNIT

    def gather(tile, buf, start):
        for u in range(per_tile):
            src = source_ref[tile * per_tile + u] if start else 0
            copy = pltpu.make_async_copy(xb_ref.at[pl.ds(pl.multiple_of(src * ROW_UNIT, ROW_UNIT), ROW_UNIT)],
                                         x_scr.at[buf, pl.ds(u * ROW_UNIT, ROW_UNIT)], sem.at[buf])
            if start:
                copy.start()
            else:
                copy.wait()

    @pl.when(s == 0)
    def _():
        gather(0, 0, start=True)

    @pl.when(s + 1 < n_ref[0])
    def _():
        gather(s + 1, 1 - slot, start=True)

    @pl.when(s < n_ref[0])
    def _():
        @pl.when(first_ref[s] == 1)
        def _():
            wg_scr[...] = wg_ref[...].astype(BF16)
            wu_scr[...] = wu_ref[...].astype(BF16)
            wd_scr[...] = wd_ref[...].astype(BF16)
        gather(s, slot, start=False)
        x = x_scr[slot]
        a = jnp.dot(x, wg_scr[...], preferred_element_type=F32)
        u = jnp.dot(x, wu_scr[...], preferred_element_type=F32)
        hid = (_silu(a) * u).astype(BF16)
        y_ref[...] = jnp.dot(hid, wd_scr[...], preferred_element_type=F32).astype(BF16)

    @pl.when(s >= n_ref[0])
    def _():
        y_ref[...] = jnp.zeros_like(y_ref)


def _ffn_call(xb, sched, w_gate, w_up, w_down, layer, n_tiles_max):
    def weight(s, expert, first, n, source):
        return (layer, expert[s], 0, 0)
    return pl.pallas_call(
        _ffn_body,
        grid_spec=pltpu.PrefetchScalarGridSpec(
            num_scalar_prefetch=4,
            grid=(n_tiles_max,),
            in_specs=[
                pl.BlockSpec(memory_space=pl.ANY),
                pl.BlockSpec((None, None, D_MODEL, MOE_D_FF), weight),
                pl.BlockSpec((None, None, D_MODEL, MOE_D_FF), weight),
                pl.BlockSpec((None, None, MOE_D_FF, D_MODEL), weight),
            ],
            out_specs=pl.BlockSpec((FFN_TILE, D_MODEL), lambda s, *_: (s, 0)),
            scratch_shapes=[pltpu.VMEM((2, FFN_TILE, D_MODEL), BF16), pltpu.SemaphoreType.DMA((2,)),
                            pltpu.VMEM((D_MODEL, MOE_D_FF), BF16), pltpu.VMEM((D_MODEL, MOE_D_FF), BF16),
                            pltpu.VMEM((MOE_D_FF, D_MODEL), BF16)],
        ),
        out_shape=jax.ShapeDtypeStruct((n_tiles_max * FFN_TILE, D_MODEL), BF16),
        compiler_params=_cparams("arbitrary"),
        name="moe_ffn",
    )(sched["tile_expert"], sched["tile_first"], sched["n_tiles"], sched["source"], xb, w_gate, w_up, w_down)


def _combine_body(*refs, split):
    final = split is not None
    local_ref, units_ref, glob_ref, x_ref, ct_ref, mod_ref = refs[:6]
    fg_ref = refs[6] if final else None
    ys_ref = refs[7 if final else 6]
    out_refs = refs[(8 if final else 7):-2]
    rows_scr, sem = refs[-2:]
    step = pl.program_id(0)
    slot = step % 2

    def copier(s, j):
        def copy(dst_row, src_row):
            return pltpu.make_async_copy(ys_ref.at[pl.ds(src_row, ROW_UNIT)],
                                         rows_scr.at[s, j, pl.ds(dst_row, ROW_UNIT)], sem.at[s])
        return copy

    def request(st, s):
        for j in range(COMBINE_BLOCKS):
            _segment_copies(local_ref, units_ref, glob_ref, st * COMBINE_BLOCKS + j, copier(s, j))

    @pl.when(step == 0)
    def _():
        rows_scr[...] = jnp.zeros_like(rows_scr)
        request(0, 0)

    @pl.when(step + 1 < pl.num_programs(0))
    def _():
        request(step + 1, 1 - slot)

    first_seg = step * COMBINE_BLOCKS * MOE_EXPERTS
    n_copies = lax.fori_loop(0, COMBINE_BLOCKS * MOE_EXPERTS, lambda e, n: n + units_ref[first_seg + e], 0)
    _wait_copies(n_copies, copier(slot, 0))

    rid = lax.broadcasted_iota(jnp.int32, (TOKEN_BLOCK, BLOCK_ROWS), 1)
    ys = []
    for j in range(COMBINE_BLOCKS):
        table = ct_ref[j * TOKEN_BLOCK:(j + 1) * TOKEN_BLOCK, :]
        row1 = table[:, 0:1].astype(jnp.int32)
        row2 = table[:, 1:2].astype(jnp.int32)
        weights = (jnp.where(rid == row1, table[:, 2:3], 0.0) + jnp.where(rid == row2, table[:, 3:4], 0.0))
        ys.append(jnp.dot(weights.astype(BF16), rows_scr[slot, j], preferred_element_type=F32))
    out = x_ref[...] + mod_ref[0][5:6] * jnp.concatenate(ys, axis=0)
    if not final:
        out_refs[0][...] = out
    else:
        ms = jnp.mean(out * out, axis=-1, keepdims=True)
        out = out * lax.rsqrt(ms + NORM_EPS) * fg_ref[...]

        @pl.when(step < split // COMBINE_BLOCKS)
        def _():
            out_refs[0][...] = out

        @pl.when(step >= split // COMBINE_BLOCKS)
        def _():
            out_refs[1][...] = out


def _combine_call(x, route_c, ys, sched, mod_all, layer, final_g, split, mod_row):
    t = x.shape[0]
    final = final_g is not None
    rows = COMBINE_BLOCKS * TOKEN_BLOCK
    tok = (rows, D_MODEL)
    if final:
        split_steps = split // COMBINE_BLOCKS
        out_specs = (pl.BlockSpec(tok, lambda i, *_: (jnp.minimum(i, split_steps - 1), 0)),
                     pl.BlockSpec(tok, lambda i, *_: (jnp.maximum(i - split_steps, 0), 0)))
        out_shape = (jax.ShapeDtypeStruct((split * TOKEN_BLOCK, D_MODEL), F32),
                     jax.ShapeDtypeStruct((t - split * TOKEN_BLOCK, D_MODEL), F32))
    else:
        out_specs = pl.BlockSpec(tok, lambda i, *_: (i, 0))
        out_shape = jax.ShapeDtypeStruct((t, D_MODEL), F32)
    in_specs = [
        pl.BlockSpec((rows, D_MODEL), lambda i, *_: (i, 0)),
        pl.BlockSpec((rows, LANES), lambda i, *_: (i, 0)),
        pl.BlockSpec((None, 1, 6, D_MODEL), lambda i, *_: (layer, mod_row(i * COMBINE_BLOCKS), 0, 0)),
    ]
    args = [x, route_c, mod_all]
    if final:
        in_specs.append(pl.BlockSpec((1, D_MODEL), lambda i, *_: (0, 0)))
        args.append(final_g)
    in_specs.append(pl.BlockSpec(memory_space=pl.ANY))
    args.append(ys)
    return pl.pallas_call(
        functools.partial(_combine_body, split=split if final else None),
        grid_spec=pltpu.PrefetchScalarGridSpec(
            num_scalar_prefetch=3,
            grid=(t // rows,),
            in_specs=in_specs,
            out_specs=out_specs,
            scratch_shapes=[pltpu.VMEM((2, COMBINE_BLOCKS, BLOCK_ROWS, D_MODEL), BF16),
                            pltpu.SemaphoreType.DMA((2,))],
        ),
        out_shape=out_shape,
        compiler_params=_cparams("arbitrary"),
        name="moe_combine",
    )(sched["local"], sched["units"], sched["glob"], *args)


def _moe(x, xb, route, counts, mod_all, layer, w_gate, w_up, w_down, final_g, split, mod_row):
    t = x.shape[0]
    nb = t // TOKEN_BLOCK
    max_rows = MOE_TOP_K * t + nb * MOE_EXPERTS * (ROW_UNIT - 1) + MOE_EXPERTS * (FFN_TILE - 1)
    n_tiles_max = -(-max_rows // FFN_TILE)
    sched = _moe_schedule(counts[:, :, 0].astype(jnp.int32), n_tiles_max)
    ys = _ffn_call(xb, sched, w_gate, w_up, w_down, layer, n_tiles_max)
    return _combine_call(x, route, ys, sched, mod_all, layer, final_g, split, mod_row)


def _grid_pos_embed(n_tokens):
    t = jnp.arange(n_tokens)
    r = (t // GRID_W).astype(F32)
    col = (t % GRID_W).astype(F32)
    quarter = D_MODEL // 4
    omega = 1.0 / (POS_BASE ** (jnp.arange(quarter, dtype=F32) / quarter))
    ar = r[:, None] * omega[None, :]
    ac = col[:, None] * omega[None, :]
    return jnp.concatenate([jnp.sin(ar), jnp.cos(ar), jnp.sin(ac), jnp.cos(ac)], axis=-1)


def _run_trunk(x, ctx, lat, init_states, mod_all, p):
    n_ctx_blocks = ctx.n * (SEQ_BLOCK // TOKEN_BLOCK)
    lat_per_seq = lat.seq_len // TOKEN_BLOCK

    def cond_row(i):
        return jnp.where(i < n_ctx_blocks, CTX_ROW, (i - n_ctx_blocks) // lat_per_seq)

    states = None
    for i in range(DEPTH):
        j = i // 2
        n1g = p["norm1_g"][i:i + 1]
        if i % 2 == 0:
            hgrn = functools.partial(_hgrn_call, x, mod_all=mod_all, layer=i, n1g=n1g, w_in=p["hgrn_w_in"],
                                     lb=p["lb"][j], log_1m_lb=p["log_1m_lb"][j],
                                     norm_g=p["hgrn_norm_g"][j:j + 1], j=j)
            mix, states = hgrn(mix=None, grp=ctx, s0=None, states=states)
            mix, _ = hgrn(mix=mix, grp=lat, s0=init_states, states=None)
            w_out = p["hgrn_w_out"]
        else:
            mix = _fourier_call(x, None, mod_all, i, n1g, ctx)
            mix = _fourier_call(x, mix, mod_all, i, n1g, lat)
            w_out = p["fourier_w_out"]
        x, xb, route, counts = _post_call(x, mix, mod_all, i, w_out, j, p["norm2_g"][i:i + 1],
                                          p["router_w_t"][i], p["router_b"][i], cond_row)
        final_g = p["final_norm_g"] if i == DEPTH - 1 else None
        x = _moe(x, xb, route, counts, mod_all, i, p["moe_w_gate"], p["moe_w_up"], p["moe_w_down"],
                 final_g, n_ctx_blocks, cond_row)
    return x, states


def kernel(x_prompt, x_sample, state_hgrn, c, c_ctx, w_mod, b_mod, norm1_g, norm2_g, hgrn_w_in,
           hgrn_lb_logits, hgrn_norm_g, hgrn_w_out, fourier_w_out, router_group_w, router_group_b,
           router_expert_w, router_expert_b, moe_w_gate, moe_w_up, moe_w_down, final_norm_g):
    batch, seq, _ = x_prompt.shape
    dec_batch, dec_seq, _ = x_sample.shape
    assert dec_batch <= CTX_ROW and seq == ROW_TILE and dec_seq == SEQ_BLOCK

    cond = jnp.zeros((COND_ROWS, D_MODEL), F32).at[:dec_batch].set(c).at[CTX_ROW].set(c_ctx)
    mod_all = _mod_call(cond, w_mod, b_mod).reshape(DEPTH, COND_ROWS, 6, D_MODEL)

    probs = jax.nn.softmax(hgrn_lb_logits.astype(F32), axis=0)
    cs = jnp.cumsum(probs, axis=0)
    lb = cs - cs[0:1]
    router_w = jnp.concatenate([router_group_w, router_expert_w], axis=-1)
    router_b = jnp.concatenate([router_group_b, router_expert_b], axis=-1)
    pad = ROUTER_ROWS - router_w.shape[-1]
    p = {
        "norm1_g": norm1_g, "norm2_g": norm2_g, "hgrn_norm_g": hgrn_norm_g,
        "hgrn_w_in": hgrn_w_in, "hgrn_w_out": hgrn_w_out.astype(BF16),
        "fourier_w_out": fourier_w_out.astype(BF16),
        "lb": jnp.concatenate([lb, 1.0 - lb], axis=1), "log_1m_lb": jnp.log1p(-lb),
        "router_w_t": jnp.pad(jnp.swapaxes(router_w, 1, 2), ((0, 0), (0, pad), (0, 0))),
        "router_b": jnp.pad(router_b, ((0, 0), (0, pad)))[..., None],
        "moe_w_gate": moe_w_gate, "moe_w_up": moe_w_up, "moe_w_down": moe_w_down,
        "final_norm_g": final_norm_g.reshape(1, D_MODEL),
    }

    t_ctx = batch * seq
    t_lat = dec_batch * dec_seq
    ctx = _Group(0, t_ctx // SEQ_BLOCK, seq, latent=False)
    lat = _Group(t_ctx // SEQ_BLOCK, t_lat // SEQ_BLOCK, dec_seq, latent=True)
    x = _embed_call(x_prompt.reshape(t_ctx, D_MODEL), x_sample.reshape(t_lat, D_MODEL), _grid_pos_embed(dec_seq))
    (y_ctx, y_lat), new_state = _run_trunk(x, ctx, lat, state_hgrn, mod_all, p)
    return (y_ctx.reshape(batch, seq, D_MODEL), y_lat.reshape(dec_batch, dec_seq, D_MODEL), new_state)
```

```python
import functools
import math

import jax
import jax.numpy as jnp
from jax import lax
from jax.experimental import pallas as pl
from jax.experimental.pallas import tpu as pltpu

F32 = jnp.float32
BF16 = jnp.bfloat16
HIGHEST = lax.Precision.HIGHEST

D_MODEL = 1024
DEPTH = 4
GRID_W = 64
HEADS = 8
DK = 128
DV = 128
FOURIER_GROUPS = 4
FOURIER_CG = D_MODEL // FOURIER_GROUPS
MOE_GROUPS = 4
MOE_EPG = 4
MOE_EXPERTS = 16
MOE_D_FF = 512
NORM_EPS = 1e-6
POS_BASE = 10000.0
DFT_SPLIT = 64
MAX_SUB_DECAY = 100.0
LOG2_E = 1.0 / math.log(2.0)

COND_ROWS = 8
CTX_ROW = 4
ROUTER_ROWS = 32
LANES = 128
SCAN_CHUNK = 64
SCAN_SUB = 16
GROUP = 4
GROUP_ROWS = GROUP * SCAN_CHUNK
LOCAL_GROUPS = 4
PROJ_TILES = 8
SEQ_BLOCK = 2048
ROW_TILE = 256
TOKEN_BLOCK = 256
POST_BLOCKS = 4
COMBINE_BLOCKS = 2
MOE_TOP_K = 2
ROW_UNIT = 16
BLOCK_ROWS = 768
FFN_TILE = 512
VMEM_LIMIT = 56 * 1024 * 1024


def _cparams(*sem):
    return pltpu.CompilerParams(dimension_semantics=sem, vmem_limit_bytes=VMEM_LIMIT)


def _silu(x):
    return x * jax.nn.sigmoid(x)


def _norm_mod(x, g, shift, scale):
    ms = jnp.mean(x * x, axis=-1, keepdims=True)
    return (x * lax.rsqrt(ms + NORM_EPS) * g) * (1.0 + scale) + shift


def _mod_body(c_ref, w_ref, b_ref, o_ref):
    s = _silu(c_ref[...])
    o_ref[0] = jnp.dot(s, w_ref[0], precision=HIGHEST, preferred_element_type=F32) + b_ref[0]


def _mod_call(cond, w_mod, b_mod):
    n_col = 6 * D_MODEL // D_MODEL
    return pl.pallas_call(
        _mod_body,
        grid=(DEPTH, n_col),
        in_specs=[
            pl.BlockSpec((COND_ROWS, D_MODEL), lambda i, n: (0, 0)),
            pl.BlockSpec((1, D_MODEL, D_MODEL), lambda i, n: (i, 0, n)),
            pl.BlockSpec((1, 1, D_MODEL), lambda i, n: (i, 0, n)),
        ],
        out_specs=pl.BlockSpec((1, COND_ROWS, D_MODEL), lambda i, n: (i, 0, n)),
        out_shape=jax.ShapeDtypeStruct((DEPTH, COND_ROWS, 6 * D_MODEL), F32),
        compiler_params=_cparams("arbitrary", "arbitrary"),
        name="mod",
    )(cond, w_mod, b_mod.reshape(DEPTH, 1, 6 * D_MODEL))


def _embed_body(x_ref, p_ref, o_ref):
    o_ref[...] = x_ref[...] + p_ref[...]


def _embed_call(x_lat, pos):
    rows = 2 * TOKEN_BLOCK
    per_seq = pos.shape[0] // rows
    return pl.pallas_call(
        _embed_body,
        grid=(x_lat.shape[0] // rows,),
        in_specs=[
            pl.BlockSpec((rows, D_MODEL), lambda i: (i, 0)),
            pl.BlockSpec((rows, D_MODEL), lambda i: (i % per_seq, 0)),
        ],
        out_specs=pl.BlockSpec((rows, D_MODEL), lambda i: (i, 0)),
        out_shape=jax.ShapeDtypeStruct(x_lat.shape, F32),
        compiler_params=_cparams("arbitrary"),
        name="embed",
    )(x_lat, pos)


def _log_f_and_key(z, lb, one_m_lb, log_1m_lb):
    t = jnp.exp(-jnp.abs(z))
    big = 1.0 / (1.0 + t)
    small = t * big
    pos = z >= 0.0
    f = lb + one_m_lb * jnp.where(pos, big, small)
    log_f = jnp.where(f > 0.0, jnp.log(f), log_1m_lb + z)
    return log_f, one_m_lb * jnp.where(pos, small, big)


def _group_masks():
    n = GROUP_ROWS
    t = lax.broadcasted_iota(jnp.int32, (n, n), 0)
    s = lax.broadcasted_iota(jnp.int32, (n, n), 1)
    same_chunk = (t // SCAN_CHUNK) == (s // SCAN_CHUNK)
    same_sub = (t // SCAN_SUB) == (s // SCAN_SUB)
    prefix = (same_chunk & (t >= s)).astype(BF16)
    return prefix, same_chunk, same_sub & (t >= s), same_sub & (t <= s)


def _qk(a, b):
    return lax.dot_general(a, b, (((1,), (1,)), ((), ())), preferred_element_type=F32)


def _split2(x):
    hi = x.astype(BF16)
    return [hi, (x - hi.astype(F32)).astype(BF16)]


def _sum2(s, i):
    o = 2 * i * DK
    return s[:, o:o + DK] + s[:, o + DK:o + 2 * DK]


def _same_sub_block_pairs(b, q, k, v, reverse):
    n = GROUP_ROWS
    pos = lax.broadcasted_iota(jnp.int32, (n, 1), 0) % SCAN_SUB
    vf = v.astype(F32)
    out = jnp.zeros((n, DV), F32)
    for d in range(SCAN_SUB):
        shift = (n - d) % n if reverse else d
        ks, bs, vs = (pltpu.roll(a, shift, 0) if shift else a for a in (k, b, vf))
        inside = (pos + d < SCAN_SUB) if reverse else (pos >= d)
        w = jnp.sum(q * ks * jnp.exp2(jnp.minimum(b - bs, 0.0)), axis=1, keepdims=True)
        out = out + jnp.where(inside, w, 0.0) * vs
    return out


def _group_dir(b, q, k, v, vt, m_chunk, m_diag, unsafe, reverse):
    c, sb = SCAN_CHUNK, SCAN_SUB
    nb = c // sb
    zero = jnp.zeros((sb, DK), BF16)
    qd, kd, qs, kl, decs = [], [], [], [], []
    q_lev = [[] for _ in range(nb - 1)]
    k_lev = [[] for _ in range(nb - 1)]
    for g in range(GROUP):
        starts = [g * c + i * sb for i in range(nb)]
        pos = [nb - 1 - i for i in range(nb)] if reverse else list(range(nb))
        end_at = {}
        for i, r in enumerate(starts):
            e = r if reverse else r + sb - 1
            end_at[pos[i]] = b[e:e + 1]
        edge = end_at[nb - 1]
        decs.append(jnp.exp2(edge))
        for i, r in enumerate(starts):
            m = r + sb // 2 if reverse else r + sb // 2 - 1
            bi, qi, ki, mid = b[r:r + sb], q[r:r + sb], k[r:r + sb], b[m:m + 1]
            qd.append((qi * jnp.exp2(bi - mid)).astype(BF16))
            kd.append((ki * jnp.exp2(mid - bi)).astype(BF16))
            for j in range(nb - 1):
                q_lev[j].append((qi * jnp.exp2(bi - end_at[j])).astype(BF16) if pos[i] > j else zero)
                k_lev[j].append((ki * jnp.exp2(end_at[j] - bi)).astype(BF16) if pos[i] == j else zero)
            qs.append((qi * jnp.exp2(bi)).astype(BF16))
            kl.append((ki * jnp.exp2(edge - bi)).astype(BF16))

    def rows(blocks):
        return jnp.concatenate(blocks, axis=0)
    s_lev = _qk(jnp.concatenate([rows(x) for x in q_lev], axis=1),
                jnp.concatenate([rows(x) for x in k_lev], axis=1))

    s_diag = _qk(rows(qd), rows(kd))
    scores = jnp.where(m_diag & jnp.logical_not(unsafe), s_diag, 0.0) + jnp.where(m_chunk, s_lev, 0.0)
    o = jnp.dot(scores.astype(BF16), v, preferred_element_type=F32)
    kl_wide = jnp.concatenate(
        [rows([blk if n // nb == g else zero for n, blk in enumerate(kl)]) for g in range(GROUP)], axis=1)
    upd_t = jnp.dot(vt, kl_wide, preferred_element_type=F32)
    return o, rows(qs), upd_t, decs


def _hgrn_body(*refs, n_in, n_seq, cps, has_s0):
    x_ref, mod_ref, n1g_ref = refs[:3]
    w_refs = refs[3:8]
    lb_ref, l1m_ref, ng_ref = refs[8:11]
    s0_ref = refs[11] if has_s0 else None
    og_ref = refs[n_in]
    st_ref = None if has_s0 else refs[n_in + 1]
    (h_scr, q_scr, v_scr, g_scr, lff_scr, kf_scr, lfb_scr, kb_scr, of_scr, ob_scr, qsf_scr, qsb_scr,
     uf_scr, ub_scr, decf_scr, decb_scr, sf_scr, sb_scr, w_ref, b_scr) = refs[n_in + (1 if has_s0 else 2):]
    c = SCAN_CHUNK
    n_chunks = SEQ_BLOCK // c
    n_tiles = SEQ_BLOCK // ROW_TILE
    for s, ref in enumerate(w_refs):
        w_ref[:, s * DK:(s + 1) * DK] = ref[...].astype(BF16)

    @pl.when(pl.program_id(1) == 0)
    def _():
        m = mod_ref[0]
        g = n1g_ref[...]

        def tile(t, carry):
            rows = pl.ds(pl.multiple_of(t * ROW_TILE, ROW_TILE), ROW_TILE)
            h_scr[rows, :] = _norm_mod(x_ref[rows, :], g, m[0:1], m[1:2]).astype(BF16)
            return carry
        lax.fori_loop(0, n_tiles, tile, 0)

    lb = lb_ref[...]
    l1m = l1m_ref[...]

    def proj_tiles(g, carry):
        tiles = []
        for u in range(PROJ_TILES):
            rows = pl.ds(pl.multiple_of((PROJ_TILES * g + u) * ROW_TILE, ROW_TILE), ROW_TILE)
            tiles.append((rows, h_scr[rows, :]))
        outs = []
        for rows, h in tiles:
            p = jnp.dot(h, w_ref[...], preferred_element_type=F32)
            outs.append((rows, p, _log_f_and_key(p[:, DK:2 * DK], lb[0:1], lb[2:3], l1m[0:1]),
                         _log_f_and_key(p[:, 2 * DK:3 * DK], lb[1:2], lb[3:4], l1m[1:2])))
        for rows, p, (lf_f, k_f), (lf_b, k_b) in outs:
            q_scr[rows, :] = p[:, 0:DK]
            v_scr[rows, :] = p[:, 3 * DK:3 * DK + DV]
            g_scr[rows, :] = p[:, 3 * DK + DV:3 * DK + 2 * DV]
            lff_scr[rows, :] = lf_f
            kf_scr[rows, :] = k_f
            lfb_scr[rows, :] = lf_b
            kb_scr[rows, :] = k_b
        return carry
    lax.fori_loop(0, n_tiles // PROJ_TILES, proj_tiles, 0)

    prefix, m_chunk, m_diag_f, m_diag_b = _group_masks()

    def group_rows(grp):
        return pl.ds(pl.multiple_of(grp * GROUP_ROWS, GROUP_ROWS), GROUP_ROWS)

    def local_load(grp):
        rows = group_rows(grp)
        return (q_scr[rows, :], v_scr[rows, :], lff_scr[rows, :], kf_scr[rows, :], lfb_scr[rows, :],
                kb_scr[rows, :])

    def log2_decays(lf_f, lf_b):
        sums = jnp.dot(prefix, jnp.concatenate(_split2(lf_f) + _split2(lf_b), axis=1),
                       preferred_element_type=F32)
        pre_b = _sum2(sums, 1)
        total_b = jnp.concatenate(
            [jnp.broadcast_to(pre_b[g * c + c - 1:g * c + c], (c, DK)) for g in range(GROUP)], axis=0)
        return _sum2(sums, 0) * LOG2_E, ((total_b - pre_b) + lf_b) * LOG2_E

    def sub_block_span(b):
        return functools.reduce(jnp.maximum, [jnp.abs(b[r:r + 1] - b[r + SCAN_SUB - 1:r + SCAN_SUB])
                                              for r in range(0, GROUP_ROWS, SCAN_SUB)])

    def local_compute(b_f, b_b, unsafe, q, v32, lf_f, k_f, lf_b, k_b):
        v = v32.astype(BF16)
        vt = v32.T.astype(BF16)
        return (_group_dir(b_f, q, k_f, v, vt, m_chunk, m_diag_f, unsafe, reverse=False),
                _group_dir(b_b, q, k_b, v, vt, m_chunk, m_diag_b, unsafe, reverse=True))

    def local_store(grp, fwd, bwd):
        rows = group_rows(grp)
        for (o, qs, upd_t, dec), o_scr, qs_scr, u_scr, dec_scr in (
                (fwd, of_scr, qsf_scr, uf_scr, decf_scr), (bwd, ob_scr, qsb_scr, ub_scr, decb_scr)):
            o_scr[rows, :] = o
            qs_scr[rows, :] = qs
            for g in range(GROUP):
                u_scr[grp * GROUP + g] = upd_t[:, g * DK:(g + 1) * DK]
                dec_scr[grp * GROUP + g] = dec[g]

    def local(it, carry):
        groups = [it * LOCAL_GROUPS + u for u in range(LOCAL_GROUPS)]
        loaded = [local_load(grp) for grp in groups]
        decays = [log2_decays(vals[2], vals[4]) for vals in loaded]
        span = functools.reduce(jnp.maximum, [sub_block_span(b) for pair in decays for b in pair])
        unsafe = jnp.max(span) > MAX_SUB_DECAY
        for u, (b_f, b_b) in enumerate(decays):
            b_scr[u, 0] = b_f
            b_scr[u, 1] = b_b
        results = [local_compute(b_f, b_b, unsafe, *vals) for (b_f, b_b), vals in zip(decays, loaded)]
        for grp, (fwd, bwd) in zip(groups, results):
            local_store(grp, fwd, bwd)

        @pl.when(unsafe)
        def _():
            for u, grp in enumerate(groups):
                rows = group_rows(grp)
                q, v = q_scr[rows, :], v_scr[rows, :].astype(BF16)
                of_scr[rows, :] += _same_sub_block_pairs(b_scr[u, 0], q, kf_scr[rows, :], v, reverse=False)
                ob_scr[rows, :] += _same_sub_block_pairs(b_scr[u, 1], q, kb_scr[rows, :], v, reverse=True)
        return carry
    lax.fori_loop(0, n_chunks // GROUP // LOCAL_GROUPS, local, 0)

    def advance(ci, cj, sf, sb):
        sf_scr[ci] = sf.astype(BF16)
        sb_scr[cj] = sb.astype(BF16)
        return sf * decf_scr[ci] + uf_scr[ci], sb * decb_scr[cj] + ub_scr[cj]

    if has_s0:
        def step(i, carry):
            return advance(i, n_chunks - 1 - i, *carry)
        lax.fori_loop(0, n_chunks, step, (s0_ref[0].T, s0_ref[1].T), unroll=4)
    else:
        def seq(s, carry):
            sf = jnp.zeros((DV, DK), F32)
            sb = jnp.zeros((DV, DK), F32)
            for i in range(cps):
                sf, sb = advance(s * cps + i, s * cps + cps - 1 - i, sf, sb)
            st_ref[s, 0] = sf.T
            st_ref[s, 1] = sb.T
            return carry
        lax.fori_loop(0, n_seq, seq, 0)

    ng = ng_ref[...]
    zero_chunk = jnp.zeros((c, DK), BF16)

    def widen(qs):
        return jnp.concatenate(
            [jnp.concatenate([qs[h * c:(h + 1) * c] if h == g else zero_chunk for h in range(GROUP)], axis=0)
             for g in range(GROUP)], axis=1)

    def out_groups(it, carry):
        loaded = []
        for u in range(LOCAL_GROUPS):
            grp = it * LOCAL_GROUPS + u
            rows = group_rows(grp)
            qs = jnp.concatenate([widen(qsf_scr[rows, :]), widen(qsb_scr[rows, :])], axis=1)
            st = jnp.concatenate([sf_scr[grp * GROUP + g] for g in range(GROUP)]
                                 + [sb_scr[grp * GROUP + g] for g in range(GROUP)], axis=1)
            loaded.append((rows, qs, st, of_scr[rows, :] + ob_scr[rows, :], g_scr[rows, :]))
        for rows, qs, st, o_local, gate in loaded:
            o = o_local + _qk(qs, st)
            o = o * lax.rsqrt(jnp.mean(o * o, axis=-1, keepdims=True) + NORM_EPS) * ng
            og_ref[rows, :] = (o * _silu(gate)).astype(BF16)
        return carry
    lax.fori_loop(0, n_chunks // GROUP // LOCAL_GROUPS, out_groups, 0)


class _Group:
    def __init__(self, first, n, seq_len, latent):
        self.first, self.n, self.seq_len, self.latent = first, n, seq_len, latent

    def cond_row(self, block):
        return block if self.latent else CTX_ROW


def _hgrn_call(x, mix, mod_all, layer, n1g, w_in, lb, log_1m_lb, norm_g, j, grp, s0, states, t, x_first=None):
    x_first = grp.first if x_first is None else x_first
    n_seq = SEQ_BLOCK // grp.seq_len
    cps = grp.seq_len // SCAN_CHUNK
    n_chunks = SEQ_BLOCK // SCAN_CHUNK
    has_s0 = s0 is not None
    n_layers = w_in.shape[0]
    w_cat = pltpu.VMEM((D_MODEL, 3 * DK + 2 * DV), BF16)
    decays = pltpu.VMEM((LOCAL_GROUPS, 2, GROUP_ROWS, DK), F32)

    def section(s):
        return pl.BlockSpec((None, D_MODEL, DK), lambda b, h: (j, 0, s * HEADS + h))

    in_specs = [
        pl.BlockSpec((SEQ_BLOCK, D_MODEL), lambda b, h: (b + x_first, 0)),
        pl.BlockSpec((None, 1, 6, D_MODEL), lambda b, h: (layer, grp.cond_row(b), 0, 0)),
        pl.BlockSpec((1, D_MODEL), lambda b, h: (0, 0)),
        section(0), section(1), section(2), section(3), section(4),
        pl.BlockSpec((4, DK), lambda b, h: (0, h)),
        pl.BlockSpec((2, DK), lambda b, h: (0, h)),
        pl.BlockSpec((1, DV), lambda b, h: (0, h)),
    ]
    args = [x, mod_all, n1g, w_in, w_in, w_in, w_in, w_in, lb, log_1m_lb, norm_g]
    og_spec = pl.BlockSpec((SEQ_BLOCK, DV), lambda b, h: (b + grp.first, h))
    og_shape = jax.ShapeDtypeStruct((t, D_MODEL), BF16)
    aliases = {}
    if has_s0:
        in_specs.append(pl.BlockSpec((None, None, 2, None, DK, DV), lambda b, h: (b, j, 0, h, 0, 0)))
        args.append(s0)
        out_specs, out_shape = og_spec, og_shape
    else:
        st_shape = (grp.n * n_seq, n_layers, 2, HEADS, DK, DV)
        out_specs = (og_spec,
                     pl.BlockSpec((n_seq, None, 2, None, DK, DV), lambda b, h: (b, j, 0, h, 0, 0)))
        out_shape = (og_shape, jax.ShapeDtypeStruct(st_shape, F32))
        if states is not None:
            in_specs.append(pl.BlockSpec(memory_space=pl.ANY))
            args.append(states)
            aliases[len(args) - 1] = 1
    if mix is not None:
        in_specs.append(pl.BlockSpec(memory_space=pl.ANY))
        args.append(mix)
        aliases[len(args) - 1] = 0
    col = pltpu.VMEM((SEQ_BLOCK, DK), F32)
    colb = pltpu.VMEM((SEQ_BLOCK, DK), BF16)
    upd = pltpu.VMEM((n_chunks, DV, DK), F32)
    dec = pltpu.VMEM((n_chunks, 1, DK), F32)
    start = pltpu.VMEM((n_chunks, DV, DK), BF16)
    res = pl.pallas_call(
        functools.partial(_hgrn_body, n_in=len(args), n_seq=n_seq, cps=cps, has_s0=has_s0),
        grid=(grp.n, HEADS),
        in_specs=in_specs,
        out_specs=out_specs,
        out_shape=out_shape,
        scratch_shapes=[pltpu.VMEM((SEQ_BLOCK, D_MODEL), BF16)] + [col] * 9 + [colb, colb, upd, upd, dec, dec,
                                                                                  start, start, w_cat, decays],
        input_output_aliases=aliases,
        compiler_params=_cparams("arbitrary", "arbitrary"),
        name="hgrn_lat" if has_s0 else "hgrn_ctx",
    )(*args)
    return (res, None) if has_s0 else res


def _fourier_body(*refs, n_in, seq_len):
    x_ref, mod_ref, n1g_ref, cl_ref, sl_ref, cc_ref = refs[:6]
    z_ref, h_scr = refs[n_in:]
    rt = pl.program_id(1)
    n_tiles = SEQ_BLOCK // ROW_TILE

    @pl.when(rt == 0)
    def _():
        m = mod_ref[0]
        g = n1g_ref[...]

        def tile(t, carry):
            rows = pl.ds(pl.multiple_of(t * ROW_TILE, ROW_TILE), ROW_TILE)
            h_scr[rows, :] = _norm_mod(x_ref[rows, :], g, m[0:1], m[1:2]).astype(BF16)
            return carry
        lax.fori_loop(0, n_tiles, tile, 0)

    if seq_len == ROW_TILE:
        h = h_scr[pl.ds(pl.multiple_of(rt * ROW_TILE, ROW_TILE), ROW_TILE), :]
    else:
        h = h_scr[...]
    zc = jnp.dot(cl_ref[...], h, preferred_element_type=F32).astype(BF16)
    zs = jnp.dot(sl_ref[...], h, preferred_element_type=F32).astype(BF16)
    scale = 1.0 / math.sqrt(seq_len * FOURIER_CG)
    cg = FOURIER_CG
    for g in range(FOURIER_GROUPS):
        cat = jnp.concatenate([zc[:, g * cg:(g + 1) * cg], zs[:, g * cg:(g + 1) * cg]], axis=1)
        out = jnp.dot(cat, cc_ref[...], preferred_element_type=F32) * scale
        z_ref[:, g * cg:(g + 1) * cg] = out.astype(BF16)


def _dft_tables(n):
    j = jnp.arange(n, dtype=jnp.int32)

    def direct(k):
        ang = ((j[:, None] * k[None, :]) % n).astype(F32) * (2.0 * math.pi / n)
        return jnp.cos(ang), jnp.sin(ang)
    if n <= DFT_SPLIT:
        return direct(j)
    ca, sa = direct(jnp.arange(n // DFT_SPLIT, dtype=jnp.int32) * DFT_SPLIT)
    cb, sb = direct(jnp.arange(DFT_SPLIT, dtype=jnp.int32))
    cos = ca[:, :, None] * cb[:, None, :] - sa[:, :, None] * sb[:, None, :]
    sin = sa[:, :, None] * cb[:, None, :] + ca[:, :, None] * sb[:, None, :]
    return cos.reshape(n, n), sin.reshape(n, n)


def _fourier_call(x, mix, mod_all, layer, n1g, grp):
    t = x.shape[0]
    seq_len = grp.seq_len
    n_rt = SEQ_BLOCK // ROW_TILE
    cl, sl = _dft_tables(seq_len)
    cc, sc = _dft_tables(FOURIER_CG)
    cc2 = jnp.concatenate([cc, -sc], axis=0).astype(BF16)
    if seq_len == ROW_TILE:
        pos_spec = pl.BlockSpec((ROW_TILE, seq_len), lambda b, r: (0, 0))
    else:
        pos_spec = pl.BlockSpec((ROW_TILE, seq_len), lambda b, r: (r, 0))
    in_specs = [
        pl.BlockSpec((SEQ_BLOCK, D_MODEL), lambda b, r: (b + grp.first, 0)),
        pl.BlockSpec((None, 1, 6, D_MODEL), lambda b, r: (layer, grp.cond_row(b), 0, 0)),
        pl.BlockSpec((1, D_MODEL), lambda b, r: (0, 0)),
        pos_spec, pos_spec,
        pl.BlockSpec((2 * FOURIER_CG, FOURIER_CG), lambda b, r: (0, 0)),
    ]
    args = [x, mod_all, n1g, cl.astype(BF16), sl.astype(BF16), cc2]
    aliases = {}
    if mix is not None:
        in_specs.append(pl.BlockSpec(memory_space=pl.ANY))
        args.append(mix)
        aliases[len(args) - 1] = 0
    return pl.pallas_call(
        functools.partial(_fourier_body, n_in=len(args), seq_len=seq_len),
        grid=(grp.n, n_rt),
        in_specs=in_specs,
        out_specs=pl.BlockSpec((ROW_TILE, D_MODEL), lambda b, r: ((b + grp.first) * n_rt + r, 0)),
        out_shape=jax.ShapeDtypeStruct((t, D_MODEL), BF16),
        scratch_shapes=[pltpu.VMEM((SEQ_BLOCK, D_MODEL), BF16)],
        input_output_aliases=aliases,
        compiler_params=_cparams("arbitrary", "arbitrary"),
        name="fourier_lat" if grp.latent else "fourier_ctx",
    )(*args)


def _route(lg):
    grp = [lg[g:g + 1] for g in range(MOE_GROUPS)]
    gmax = functools.reduce(jnp.maximum, grp)
    gi = jnp.where(grp[0] == gmax, 0, jnp.where(grp[1] == gmax, 1, jnp.where(grp[2] == gmax, 2, 3)))
    pgv = 1.0 / functools.reduce(lambda a, b: a + b, [jnp.exp(g - gmax) for g in grp])
    sel = []
    for j in range(MOE_EPG):
        rows = [lg[MOE_GROUPS + g * MOE_EPG + j:MOE_GROUPS + g * MOE_EPG + j + 1] for g in range(MOE_GROUPS)]
        sel.append(jnp.where(gi == 0, rows[0], jnp.where(gi == 1, rows[1], jnp.where(gi == 2, rows[2], rows[3]))))
    m1 = functools.reduce(jnp.maximum, sel)
    i1 = jnp.where(sel[0] == m1, 0, jnp.where(sel[1] == m1, 1, jnp.where(sel[2] == m1, 2, 3)))
    rest = [jnp.where(i1 == j, -jnp.inf, sel[j]) for j in range(MOE_EPG)]
    m2 = functools.reduce(jnp.maximum, rest)
    i2 = jnp.where(rest[0] == m2, 0, jnp.where(rest[1] == m2, 1, jnp.where(rest[2] == m2, 2, 3)))
    e2 = jnp.exp(m2 - m1)
    w1 = pgv / (1.0 + e2)
    w2 = pgv * e2 / (1.0 + e2)
    r = lg.shape[1]
    ex1 = gi * MOE_EPG + i1
    ex2 = gi * MOE_EPG + i2
    eid = lax.broadcasted_iota(jnp.int32, (MOE_EXPERTS, r), 0)
    member = (eid == ex1) | (eid == ex2)
    t0 = lax.broadcasted_iota(jnp.int32, (r, r), 0)
    t1 = lax.broadcasted_iota(jnp.int32, (r, r), 1)
    rank = jnp.dot(member.astype(BF16), (t0 < t1).astype(BF16), preferred_element_type=F32)
    count = jnp.sum(member.astype(F32), axis=1, keepdims=True)
    padded = jnp.floor((count + (ROW_UNIT - 1)) * (1.0 / ROW_UNIT)) * ROW_UNIT
    e0 = lax.broadcasted_iota(jnp.int32, (MOE_EXPERTS, MOE_EXPERTS), 0)
    e1 = lax.broadcasted_iota(jnp.int32, (MOE_EXPERTS, MOE_EXPERTS), 1)
    start = jnp.dot((e1 < e0).astype(F32), jnp.broadcast_to(padded, (MOE_EXPERTS, LANES)),
                    precision=HIGHEST, preferred_element_type=F32)[:, 0:1]
    row = start + rank
    row1 = jnp.sum(jnp.where(eid == ex1, row, 0.0), axis=0, keepdims=True)
    row2 = jnp.sum(jnp.where(eid == ex2, row, 0.0), axis=0, keepdims=True)
    rid = lax.broadcasted_iota(jnp.int32, (LANES, r), 0)
    fields = (row1, row2, w1, w2, ex1.astype(F32), ex2.astype(F32))
    table = jnp.zeros((LANES, r), F32)
    for i, f in enumerate(fields):
        table = jnp.where(rid == i, f, table)
    return table, count


def _post_body(*refs, ctx_steps):
    if ctx_steps is None:
        x = refs[0][...]
    else:
        x = jnp.where(pl.program_id(0) < ctx_steps, refs[0][...], refs[1][...])
        refs = refs[1:]
    _, mix_ref, mod_ref, w_ref, n2g_ref, wr_ref, br_ref, xo_ref, xb_ref, ct_ref, cnt_ref = refs
    m = mod_ref[0]
    out = jnp.dot(mix_ref[...], w_ref[...], preferred_element_type=F32)
    xn = x + m[2:3] * out
    xo_ref[...] = xn
    h2 = _norm_mod(xn, n2g_ref[...], m[3:4], m[4:5])
    wr_hi, wr_lo = _split2(wr_ref[...])
    h2_hi, h2_lo = _split2(h2)
    lg = _qk(wr_hi, h2_hi) + (_qk(wr_hi, h2_lo) + _qk(wr_lo, h2_hi)) + br_ref[...]
    h2 = h2_hi
    rid = lax.broadcasted_iota(jnp.int32, (BLOCK_ROWS, TOKEN_BLOCK), 0)
    for u in range(POST_BLOCKS):
        cols = slice(u * TOKEN_BLOCK, (u + 1) * TOKEN_BLOCK)
        table, count = _route(lg[:, cols])
        onehot = ((rid == table[0:1].astype(jnp.int32)) | (rid == table[1:2].astype(jnp.int32))).astype(BF16)
        xb_ref[u * BLOCK_ROWS:(u + 1) * BLOCK_ROWS, :] = jnp.dot(
            onehot, h2[cols, :], preferred_element_type=F32).astype(BF16)
        ct_ref[cols, :] = table.T
        cnt_ref[u] = jnp.broadcast_to(count, (MOE_EXPERTS, LANES))


def _post_call(x, mix, mod_all, layer, w_out, j, n2g, wr_t, br, mod_row):
    t = mix.shape[0]
    nb = t // TOKEN_BLOCK
    rows = POST_BLOCKS * TOKEN_BLOCK
    tile = lambda i: (i, 0)
    full = lambda i: (0, 0)
    if isinstance(x, tuple):
        ctx_steps = x[0].shape[0] // rows
        x_args = list(x)
        x_specs = [pl.BlockSpec((rows, D_MODEL), lambda i: (jnp.minimum(i, ctx_steps - 1), 0)),
                   pl.BlockSpec((rows, D_MODEL), lambda i: (jnp.maximum(i - ctx_steps, 0), 0))]
    else:
        ctx_steps, x_args, x_specs = None, [x], [pl.BlockSpec((rows, D_MODEL), tile)]
    return pl.pallas_call(
        functools.partial(_post_body, ctx_steps=ctx_steps),
        grid=(nb // POST_BLOCKS,),
        in_specs=x_specs + [
            pl.BlockSpec((rows, D_MODEL), tile),
            pl.BlockSpec((None, 1, 6, D_MODEL), lambda i: (layer, mod_row(i * POST_BLOCKS), 0, 0)),
            pl.BlockSpec((None, D_MODEL, D_MODEL), lambda i: (j, 0, 0)),
            pl.BlockSpec((1, D_MODEL), full),
            pl.BlockSpec((ROUTER_ROWS, D_MODEL), full),
            pl.BlockSpec((ROUTER_ROWS, 1), full),
        ],
        out_specs=(
            pl.BlockSpec((rows, D_MODEL), tile),
            pl.BlockSpec((POST_BLOCKS * BLOCK_ROWS, D_MODEL), tile),
            pl.BlockSpec((rows, LANES), tile),
            pl.BlockSpec((POST_BLOCKS, MOE_EXPERTS, LANES), lambda i: (i, 0, 0)),
        ),
        out_shape=(
            jax.ShapeDtypeStruct((t, D_MODEL), F32),
            jax.ShapeDtypeStruct((nb * BLOCK_ROWS, D_MODEL), BF16),
            jax.ShapeDtypeStruct((t, LANES), F32),
            jax.ShapeDtypeStruct((nb, MOE_EXPERTS, LANES), F32),
        ),
        compiler_params=_cparams("arbitrary"),
        name="post",
    )(*x_args, mix, mod_all, w_out, n2g, wr_t, br)


def _moe_schedule(counts, n_tiles_max):
    units = (counts + ROW_UNIT - 1) // ROW_UNIT
    local = jnp.cumsum(units, axis=1) - units
    total = jnp.sum(units, axis=0)
    per_tile = FFN_TILE // ROW_UNIT
    tiles = (total + per_tile - 1) // per_tile
    region = tiles * per_tile
    region_start = jnp.cumsum(region) - region
    glob = region_start[None, :] + jnp.cumsum(units, axis=0) - units
    tile_end = jnp.cumsum(tiles)
    tile_start = tile_end - tiles
    n_tiles = tile_end[-1]
    s = jnp.arange(n_tiles_max, dtype=jnp.int32)
    live = s < n_tiles
    tile_expert = jnp.sum((tile_end[None, :] <= jnp.minimum(s, n_tiles - 1)[:, None]).astype(jnp.int32), axis=1)
    first = jnp.any((s[:, None] == tile_start[None, :]) & (tiles[None, :] > 0), axis=1) & live
    g = jnp.arange(n_tiles_max * per_tile, dtype=jnp.int32)
    block_units = BLOCK_ROWS // ROW_UNIT
    blocks = jnp.arange(counts.shape[0], dtype=jnp.int32)[:, None]
    seg_start = glob.T.reshape(-1)
    seg_offset = (blocks * block_units + local - glob).T.reshape(-1)
    step = seg_offset - jnp.concatenate([jnp.zeros((1,), jnp.int32), seg_offset[:-1]])
    source = g + jnp.sum(jnp.where(seg_start[None, :] <= g[:, None], step[None, :], 0), axis=1)
    real = jnp.any((g[:, None] >= region_start[None, :]) & (g[:, None] < (region_start + total)[None, :]), axis=1)
    source = jnp.where(real, source, block_units - 1)
    i32 = lambda a: a.astype(jnp.int32).reshape(-1)
    return {
        "local": i32(local), "units": i32(units), "glob": i32(glob), "source": i32(source),
        "tile_expert": tile_expert, "tile_first": i32(first), "n_tiles": i32(n_tiles),
    }


def _segment_copies(local_ref, units_ref, glob_ref, blk, make_copy):
    def per_expert(e, n):
        idx = blk * MOE_EXPERTS + e
        loc, cnt, glo = local_ref[idx], units_ref[idx], glob_ref[idx]

        def unit(u, carry):
            make_copy(pl.multiple_of((loc + u) * ROW_UNIT, ROW_UNIT),
                      pl.multiple_of((glo + u) * ROW_UNIT, ROW_UNIT)).start()
            return carry
        lax.fori_loop(0, cnt, unit, 0)
        return n + cnt
    return lax.fori_loop(0, MOE_EXPERTS, per_expert, 0)


def _wait_copies(n, make_copy):
    def one(u, carry):
        make_copy(0, 0).wait()
        return carry
    lax.fori_loop(0, n, one, 0)


def _ffn_body(expert_ref, first_ref, n_ref, source_ref, xb_ref, wg_ref, wu_ref, wd_ref, y_ref,
              x_scr, sem, wg_scr, wu_scr, wd_scr):
    s = pl.program_id(0)
    slot = s % 2
    per_tile = FFN_TILE // ROW_UNIT

    def gather(tile, buf, start):
        for u in range(per_tile):
            src = source_ref[tile * per_tile + u] if start else 0
            copy = pltpu.make_async_copy(xb_ref.at[pl.ds(pl.multiple_of(src * ROW_UNIT, ROW_UNIT), ROW_UNIT)],
                                         x_scr.at[buf, pl.ds(u * ROW_UNIT, ROW_UNIT)], sem.at[buf])
            if start:
                copy.start()
            else:
                copy.wait()

    @pl.when(s == 0)
    def _():
        gather(0, 0, start=True)

    @pl.when(s + 1 < n_ref[0])
    def _():
        gather(s + 1, 1 - slot, start=True)

    @pl.when(s < n_ref[0])
    def _():
        @pl.when(first_ref[s] == 1)
        def _():
            wg_scr[...] = wg_ref[...].astype(BF16)
            wu_scr[...] = wu_ref[...].astype(BF16)
            wd_scr[...] = wd_ref[...].astype(BF16)
        gather(s, slot, start=False)
        x = x_scr[slot]
        a = jnp.dot(x, wg_scr[...], preferred_element_type=F32)
        u = jnp.dot(x, wu_scr[...], preferred_element_type=F32)
        hid = (_silu(a) * u).astype(BF16)
        y_ref[...] = jnp.dot(hid, wd_scr[...], preferred_element_type=F32).astype(BF16)

    @pl.when(s >= n_ref[0])
    def _():
        y_ref[...] = jnp.zeros_like(y_ref)


def _ffn_call(xb, sched, w_gate, w_up, w_down, layer, n_tiles_max):
    def weight(s, expert, first, n, source):
        return (layer, expert[s], 0, 0)
    return pl.pallas_call(
        _ffn_body,
        grid_spec=pltpu.PrefetchScalarGridSpec(
            num_scalar_prefetch=4,
            grid=(n_tiles_max,),
            in_specs=[
                pl.BlockSpec(memory_space=pl.ANY),
                pl.BlockSpec((None, None, D_MODEL, MOE_D_FF), weight),
                pl.BlockSpec((None, None, D_MODEL, MOE_D_FF), weight),
                pl.BlockSpec((None, None, MOE_D_FF, D_MODEL), weight),
            ],
            out_specs=pl.BlockSpec((FFN_TILE, D_MODEL), lambda s, *_: (s, 0)),
            scratch_shapes=[pltpu.VMEM((2, FFN_TILE, D_MODEL), BF16), pltpu.SemaphoreType.DMA((2,)),
                            pltpu.VMEM((D_MODEL, MOE_D_FF), BF16), pltpu.VMEM((D_MODEL, MOE_D_FF), BF16),
                            pltpu.VMEM((MOE_D_FF, D_MODEL), BF16)],
        ),
        out_shape=jax.ShapeDtypeStruct((n_tiles_max * FFN_TILE, D_MODEL), BF16),
        compiler_params=_cparams("arbitrary"),
        name="moe_ffn",
    )(sched["tile_expert"], sched["tile_first"], sched["n_tiles"], sched["source"], xb, w_gate, w_up, w_down)


def _combine_body(*refs, split):
    final = split is not None
    local_ref, units_ref, glob_ref, x_ref, ct_ref, mod_ref = refs[:6]
    fg_ref = refs[6] if final else None
    ys_ref = refs[7 if final else 6]
    out_refs = refs[(8 if final else 7):-2]
    rows_scr, sem = refs[-2:]
    step = pl.program_id(0)
    slot = step % 2

    def copier(s, j):
        def copy(dst_row, src_row):
            return pltpu.make_async_copy(ys_ref.at[pl.ds(src_row, ROW_UNIT)],
                                         rows_scr.at[s, j, pl.ds(dst_row, ROW_UNIT)], sem.at[s])
        return copy

    def request(st, s):
        for j in range(COMBINE_BLOCKS):
            _segment_copies(local_ref, units_ref, glob_ref, st * COMBINE_BLOCKS + j, copier(s, j))

    @pl.when(step == 0)
    def _():
        rows_scr[...] = jnp.zeros_like(rows_scr)
        request(0, 0)

    @pl.when(step + 1 < pl.num_programs(0))
    def _():
        request(step + 1, 1 - slot)

    first_seg = step * COMBINE_BLOCKS * MOE_EXPERTS
    n_copies = lax.fori_loop(0, COMBINE_BLOCKS * MOE_EXPERTS, lambda e, n: n + units_ref[first_seg + e], 0)
    _wait_copies(n_copies, copier(slot, 0))

    rid = lax.broadcasted_iota(jnp.int32, (TOKEN_BLOCK, BLOCK_ROWS), 1)
    ys = []
    for j in range(COMBINE_BLOCKS):
        table = ct_ref[j * TOKEN_BLOCK:(j + 1) * TOKEN_BLOCK, :]
        row1 = table[:, 0:1].astype(jnp.int32)
        row2 = table[:, 1:2].astype(jnp.int32)
        weights = (jnp.where(rid == row1, table[:, 2:3], 0.0) + jnp.where(rid == row2, table[:, 3:4], 0.0))
        ys.append(jnp.dot(weights.astype(BF16), rows_scr[slot, j], preferred_element_type=F32))
    out = x_ref[...] + mod_ref[0][5:6] * jnp.concatenate(ys, axis=0)
    if not final:
        out_refs[0][...] = out
    else:
        ms = jnp.mean(out * out, axis=-1, keepdims=True)
        out = out * lax.rsqrt(ms + NORM_EPS) * fg_ref[...]

        @pl.when(step < split // COMBINE_BLOCKS)
        def _():
            out_refs[0][...] = out

        @pl.when(step >= split // COMBINE_BLOCKS)
        def _():
            out_refs[1][...] = out


def _combine_call(x, route_c, ys, sched, mod_all, layer, final_g, split, mod_row):
    t = x.shape[0]
    final = final_g is not None
    rows = COMBINE_BLOCKS * TOKEN_BLOCK
    tok = (rows, D_MODEL)
    if final:
        split_steps = split // COMBINE_BLOCKS
        out_specs = (pl.BlockSpec(tok, lambda i, *_: (jnp.minimum(i, split_steps - 1), 0)),
                     pl.BlockSpec(tok, lambda i, *_: (jnp.maximum(i - split_steps, 0), 0)))
        out_shape = (jax.ShapeDtypeStruct((split * TOKEN_BLOCK, D_MODEL), F32),
                     jax.ShapeDtypeStruct((t - split * TOKEN_BLOCK, D_MODEL), F32))
    else:
        out_specs = pl.BlockSpec(tok, lambda i, *_: (i, 0))
        out_shape = jax.ShapeDtypeStruct((t, D_MODEL), F32)
    in_specs = [
        pl.BlockSpec((rows, D_MODEL), lambda i, *_: (i, 0)),
        pl.BlockSpec((rows, LANES), lambda i, *_: (i, 0)),
        pl.BlockSpec((None, 1, 6, D_MODEL), lambda i, *_: (layer, mod_row(i * COMBINE_BLOCKS), 0, 0)),
    ]
    args = [x, route_c, mod_all]
    if final:
        in_specs.append(pl.BlockSpec((1, D_MODEL), lambda i, *_: (0, 0)))
        args.append(final_g)
    in_specs.append(pl.BlockSpec(memory_space=pl.ANY))
    args.append(ys)
    return pl.pallas_call(
        functools.partial(_combine_body, split=split if final else None),
        grid_spec=pltpu.PrefetchScalarGridSpec(
            num_scalar_prefetch=3,
            grid=(t // rows,),
            in_specs=in_specs,
            out_specs=out_specs,
            scratch_shapes=[pltpu.VMEM((2, COMBINE_BLOCKS, BLOCK_ROWS, D_MODEL), BF16),
                            pltpu.SemaphoreType.DMA((2,))],
        ),
        out_shape=out_shape,
        compiler_params=_cparams("arbitrary"),
        name="moe_combine",
    )(sched["local"], sched["units"], sched["glob"], *args)


def _moe(x, xb, route, counts, mod_all, layer, w_gate, w_up, w_down, final_g, split, mod_row):
    t = x.shape[0]
    nb = t // TOKEN_BLOCK
    max_rows = MOE_TOP_K * t + nb * MOE_EXPERTS * (ROW_UNIT - 1) + MOE_EXPERTS * (FFN_TILE - 1)
    n_tiles_max = -(-max_rows // FFN_TILE)
    sched = _moe_schedule(counts[:, :, 0].astype(jnp.int32), n_tiles_max)
    ys = _ffn_call(xb, sched, w_gate, w_up, w_down, layer, n_tiles_max)
    return _combine_call(x, route, ys, sched, mod_all, layer, final_g, split, mod_row)


def _grid_pos_embed(n_tokens):
    t = jnp.arange(n_tokens)
    r = (t // GRID_W).astype(F32)
    col = (t % GRID_W).astype(F32)
    quarter = D_MODEL // 4
    omega = 1.0 / (POS_BASE ** (jnp.arange(quarter, dtype=F32) / quarter))
    ar = r[:, None] * omega[None, :]
    ac = col[:, None] * omega[None, :]
    return jnp.concatenate([jnp.sin(ar), jnp.cos(ar), jnp.sin(ac), jnp.cos(ac)], axis=-1)


def _run_trunk(x_ctx, x_lat, ctx, lat, init_states, mod_all, p):
    n_ctx_blocks = ctx.n * (SEQ_BLOCK // TOKEN_BLOCK)
    lat_per_seq = lat.seq_len // TOKEN_BLOCK
    t = x_ctx.shape[0] + x_lat.shape[0]
    x = (x_ctx, x_lat)

    def cond_row(i):
        return jnp.where(i < n_ctx_blocks, CTX_ROW, (i - n_ctx_blocks) // lat_per_seq)

    states = None
    for i in range(DEPTH):
        j = i // 2
        n1g = p["norm1_g"][i:i + 1]
        if i % 2 == 0:
            hgrn = functools.partial(_hgrn_call, mod_all=mod_all, layer=i, n1g=n1g, w_in=p["hgrn_w_in"],
                                     lb=p["lb"][j], log_1m_lb=p["log_1m_lb"][j],
                                     norm_g=p["hgrn_norm_g"][j:j + 1], j=j, t=t)
            if isinstance(x, tuple):
                mix, states = hgrn(x[0], mix=None, grp=ctx, s0=None, states=states, x_first=0)
                mix, _ = hgrn(x[1], mix=mix, grp=lat, s0=init_states, states=None, x_first=0)
            else:
                mix, states = hgrn(x, mix=None, grp=ctx, s0=None, states=states)
                mix, _ = hgrn(x, mix=mix, grp=lat, s0=init_states, states=None)
            w_out = p["hgrn_w_out"]
        else:
            mix = _fourier_call(x, None, mod_all, i, n1g, ctx)
            mix = _fourier_call(x, mix, mod_all, i, n1g, lat)
            w_out = p["fourier_w_out"]
        x, xb, route, counts = _post_call(x, mix, mod_all, i, w_out, j, p["norm2_g"][i:i + 1],
                                          p["router_w_t"][i], p["router_b"][i], cond_row)
        final_g = p["final_norm_g"] if i == DEPTH - 1 else None
        x = _moe(x, xb, route, counts, mod_all, i, p["moe_w_gate"], p["moe_w_up"], p["moe_w_down"],
                 final_g, n_ctx_blocks, cond_row)
    return x, states


def kernel(x_prompt, x_sample, state_hgrn, c, c_ctx, w_mod, b_mod, norm1_g, norm2_g, hgrn_w_in,
           hgrn_lb_logits, hgrn_norm_g, hgrn_w_out, fourier_w_out, router_group_w, router_group_b,
           router_expert_w, router_expert_b, moe_w_gate, moe_w_up, moe_w_down, final_norm_g):
    batch, seq, _ = x_prompt.shape
    dec_batch, dec_seq, _ = x_sample.shape
    assert dec_batch <= CTX_ROW and seq == ROW_TILE and dec_seq == SEQ_BLOCK

    cond = jnp.zeros((COND_ROWS, D_MODEL), F32).at[:dec_batch].set(c).at[CTX_ROW].set(c_ctx)
    mod_all = _mod_call(cond, w_mod, b_mod).reshape(DEPTH, COND_ROWS, 6, D_MODEL)

    probs = jax.nn.softmax(hgrn_lb_logits.astype(F32), axis=0)
    cs = jnp.cumsum(probs, axis=0)
    lb = cs - cs[0:1]
    router_w = jnp.concatenate([router_group_w, router_expert_w], axis=-1)
    router_b = jnp.concatenate([router_group_b, router_expert_b], axis=-1)
    pad = ROUTER_ROWS - router_w.shape[-1]
    p = {
        "norm1_g": norm1_g, "norm2_g": norm2_g, "hgrn_norm_g": hgrn_norm_g,
        "hgrn_w_in": hgrn_w_in, "hgrn_w_out": hgrn_w_out.astype(BF16),
        "fourier_w_out": fourier_w_out.astype(BF16),
        "lb": jnp.concatenate([lb, 1.0 - lb], axis=1), "log_1m_lb": jnp.log1p(-lb),
        "router_w_t": jnp.pad(jnp.swapaxes(router_w, 1, 2), ((0, 0), (0, pad), (0, 0))),
        "router_b": jnp.pad(router_b, ((0, 0), (0, pad)))[..., None],
        "moe_w_gate": moe_w_gate, "moe_w_up": moe_w_up, "moe_w_down": moe_w_down,
        "final_norm_g": final_norm_g.reshape(1, D_MODEL),
    }

    t_ctx = batch * seq
    t_lat = dec_batch * dec_seq
    ctx = _Group(0, t_ctx // SEQ_BLOCK, seq, latent=False)
    lat = _Group(t_ctx // SEQ_BLOCK, t_lat // SEQ_BLOCK, dec_seq, latent=True)
    x_lat = _embed_call(x_sample.reshape(t_lat, D_MODEL), _grid_pos_embed(dec_seq))
    (y_ctx, y_lat), new_state = _run_trunk(x_prompt.reshape(t_ctx, D_MODEL), x_lat, ctx, lat, state_hgrn,
                                           mod_all, p)
    return (y_ctx.reshape(batch, seq, D_MODEL), y_lat.reshape(dec_batch, dec_seq, D_MODEL), new_state)
```

```python
import functools
import math

import jax
import jax.numpy as jnp
from jax import lax
from jax.experimental import pallas as pl
from jax.experimental.pallas import tpu as pltpu

F32 = jnp.float32
BF16 = jnp.bfloat16
HIGHEST = lax.Precision.HIGHEST

D_MODEL = 1024
DEPTH = 4
GRID_W = 64
HEADS = 8
DK = 128
DV = 128
FOURIER_GROUPS = 4
FOURIER_CG = D_MODEL // FOURIER_GROUPS
MOE_GROUPS = 4
MOE_EPG = 4
MOE_EXPERTS = 16
MOE_D_FF = 512
NORM_EPS = 1e-6
POS_BASE = 10000.0
DFT_SPLIT = 64
MAX_SUB_DECAY = 100.0
LOG2_E = 1.0 / math.log(2.0)

COND_ROWS = 8
CTX_ROW = 4
ROUTER_ROWS = 32
LANES = 128
SCAN_CHUNK = 64
SCAN_SUB = 16
GROUP = 4
GROUP_ROWS = GROUP * SCAN_CHUNK
LOCAL_GROUPS = 8
PROJ_TILES = 8
SEQ_BLOCK = 2048
ROW_TILE = 256
TOKEN_BLOCK = 256
POST_BLOCKS = 4
COMBINE_BLOCKS = 2
MOE_TOP_K = 2
ROW_UNIT = 16
BLOCK_ROWS = 768
FFN_TILE = 512
VMEM_LIMIT = 56 * 1024 * 1024


def _cparams(*sem):
    return pltpu.CompilerParams(dimension_semantics=sem, vmem_limit_bytes=VMEM_LIMIT)


def _silu(x):
    return x * jax.nn.sigmoid(x)


def _norm_mod(x, g, shift, scale):
    ms = jnp.mean(x * x, axis=-1, keepdims=True)
    return (x * lax.rsqrt(ms + NORM_EPS) * g) * (1.0 + scale) + shift


def _mod_body(c_ref, w_ref, b_ref, o_ref):
    s = _silu(c_ref[...])
    o_ref[0] = jnp.dot(s, w_ref[0], precision=HIGHEST, preferred_element_type=F32) + b_ref[0]


def _mod_call(cond, w_mod, b_mod):
    n_col = 6 * D_MODEL // D_MODEL
    return pl.pallas_call(
        _mod_body,
        grid=(DEPTH, n_col),
        in_specs=[
            pl.BlockSpec((COND_ROWS, D_MODEL), lambda i, n: (0, 0)),
            pl.BlockSpec((1, D_MODEL, D_MODEL), lambda i, n: (i, 0, n)),
            pl.BlockSpec((1, 1, D_MODEL), lambda i, n: (i, 0, n)),
        ],
        out_specs=pl.BlockSpec((1, COND_ROWS, D_MODEL), lambda i, n: (i, 0, n)),
        out_shape=jax.ShapeDtypeStruct((DEPTH, COND_ROWS, 6 * D_MODEL), F32),
        compiler_params=_cparams("arbitrary", "arbitrary"),
        name="mod",
    )(cond, w_mod, b_mod.reshape(DEPTH, 1, 6 * D_MODEL))


def _embed_body(x_ref, p_ref, o_ref):
    o_ref[...] = x_ref[...] + p_ref[...]


def _embed_call(x_lat, pos):
    rows = 2 * TOKEN_BLOCK
    per_seq = pos.shape[0] // rows
    return pl.pallas_call(
        _embed_body,
        grid=(x_lat.shape[0] // rows,),
        in_specs=[
            pl.BlockSpec((rows, D_MODEL), lambda i: (i, 0)),
            pl.BlockSpec((rows, D_MODEL), lambda i: (i % per_seq, 0)),
        ],
        out_specs=pl.BlockSpec((rows, D_MODEL), lambda i: (i, 0)),
        out_shape=jax.ShapeDtypeStruct(x_lat.shape, F32),
        compiler_params=_cparams("arbitrary"),
        name="embed",
    )(x_lat, pos)


def _log_f_and_key(z, lb, one_m_lb, log_1m_lb):
    t = jnp.exp(-jnp.abs(z))
    big = 1.0 / (1.0 + t)
    small = t * big
    pos = z >= 0.0
    f = lb + one_m_lb * jnp.where(pos, big, small)
    log_f = jnp.where(f > 0.0, jnp.log(f), log_1m_lb + z)
    return log_f, one_m_lb * jnp.where(pos, small, big)


def _group_masks():
    n = GROUP_ROWS
    t = lax.broadcasted_iota(jnp.int32, (n, n), 0)
    s = lax.broadcasted_iota(jnp.int32, (n, n), 1)
    same_chunk = (t // SCAN_CHUNK) == (s // SCAN_CHUNK)
    same_sub = (t // SCAN_SUB) == (s // SCAN_SUB)
    prefix = (same_chunk & (t >= s)).astype(BF16)
    return prefix, same_chunk, same_sub & (t >= s), same_sub & (t <= s)


def _qk(a, b):
    return lax.dot_general(a, b, (((1,), (1,)), ((), ())), preferred_element_type=F32)


def _split2(x):
    hi = x.astype(BF16)
    return [hi, (x - hi.astype(F32)).astype(BF16)]


def _sum2(s, i):
    o = 2 * i * DK
    return s[:, o:o + DK] + s[:, o + DK:o + 2 * DK]


def _same_sub_block_pairs(b, q, k, v, reverse):
    n = GROUP_ROWS
    pos = lax.broadcasted_iota(jnp.int32, (n, 1), 0) % SCAN_SUB
    vf = v.astype(F32)
    out = jnp.zeros((n, DV), F32)
    for d in range(SCAN_SUB):
        shift = (n - d) % n if reverse else d
        ks, bs, vs = (pltpu.roll(a, shift, 0) if shift else a for a in (k, b, vf))
        inside = (pos + d < SCAN_SUB) if reverse else (pos >= d)
        w = jnp.sum(q * ks * jnp.exp2(jnp.minimum(b - bs, 0.0)), axis=1, keepdims=True)
        out = out + jnp.where(inside, w, 0.0) * vs
    return out


def _group_dir(b, q, k, v, vt, m_chunk, m_diag, unsafe, reverse):
    c, sb = SCAN_CHUNK, SCAN_SUB
    nb = c // sb
    zero = jnp.zeros((sb, DK), BF16)
    qd, kd, qs, kl, decs = [], [], [], [], []
    q_lev = [[] for _ in range(nb - 1)]
    k_lev = [[] for _ in range(nb - 1)]
    for g in range(GROUP):
        starts = [g * c + i * sb for i in range(nb)]
        pos = [nb - 1 - i for i in range(nb)] if reverse else list(range(nb))
        end_at = {}
        for i, r in enumerate(starts):
            e = r if reverse else r + sb - 1
            end_at[pos[i]] = b[e:e + 1]
        edge = end_at[nb - 1]
        decs.append(jnp.exp2(edge))
        for i, r in enumerate(starts):
            m = r + sb // 2 if reverse else r + sb // 2 - 1
            bi, qi, ki, mid = b[r:r + sb], q[r:r + sb], k[r:r + sb], b[m:m + 1]
            qd.append((qi * jnp.exp2(bi - mid)).astype(BF16))
            kd.append((ki * jnp.exp2(mid - bi)).astype(BF16))
            for j in range(nb - 1):
                q_lev[j].append((qi * jnp.exp2(bi - end_at[j])).astype(BF16) if pos[i] > j else zero)
                k_lev[j].append((ki * jnp.exp2(end_at[j] - bi)).astype(BF16) if pos[i] == j else zero)
            qs.append((qi * jnp.exp2(bi)).astype(BF16))
            kl.append((ki * jnp.exp2(edge - bi)).astype(BF16))

    def rows(blocks):
        return jnp.concatenate(blocks, axis=0)
    s_lev = _qk(jnp.concatenate([rows(x) for x in q_lev], axis=1),
                jnp.concatenate([rows(x) for x in k_lev], axis=1))

    s_diag = _qk(rows(qd), rows(kd))
    scores = jnp.where(m_diag & jnp.logical_not(unsafe), s_diag, 0.0) + jnp.where(m_chunk, s_lev, 0.0)
    o = jnp.dot(scores.astype(BF16), v, preferred_element_type=F32)
    kl_wide = jnp.concatenate(
        [rows([blk if n // nb == g else zero for n, blk in enumerate(kl)]) for g in range(GROUP)], axis=1)
    upd_t = jnp.dot(vt, kl_wide, preferred_element_type=F32)
    return o, rows(qs), upd_t, decs


def _hgrn_body(*refs, n_in, n_seq, cps, has_s0):
    x_ref, mod_ref, n1g_ref = refs[:3]
    w_refs = refs[3:8]
    lb_ref, l1m_ref, ng_ref = refs[8:11]
    s0_ref = refs[11] if has_s0 else None
    og_ref = refs[n_in]
    st_ref = None if has_s0 else refs[n_in + 1]
    (h_scr, q_scr, v_scr, g_scr, lff_scr, kf_scr, lfb_scr, kb_scr, of_scr, ob_scr, qsf_scr, qsb_scr,
     uf_scr, ub_scr, decf_scr, decb_scr, sf_scr, sb_scr, w_ref, b_scr) = refs[n_in + (1 if has_s0 else 2):]
    c = SCAN_CHUNK
    n_chunks = SEQ_BLOCK // c
    n_tiles = SEQ_BLOCK // ROW_TILE
    for s, ref in enumerate(w_refs):
        w_ref[:, s * DK:(s + 1) * DK] = ref[...].astype(BF16)

    @pl.when(pl.program_id(1) == 0)
    def _():
        m = mod_ref[0]
        g = n1g_ref[...]

        def tile(t, carry):
            rows = pl.ds(pl.multiple_of(t * ROW_TILE, ROW_TILE), ROW_TILE)
            h_scr[rows, :] = _norm_mod(x_ref[rows, :], g, m[0:1], m[1:2]).astype(BF16)
            return carry
        lax.fori_loop(0, n_tiles, tile, 0)

    lb = lb_ref[...]
    l1m = l1m_ref[...]

    def proj_tiles(g, carry):
        tiles = []
        for u in range(PROJ_TILES):
            rows = pl.ds(pl.multiple_of((PROJ_TILES * g + u) * ROW_TILE, ROW_TILE), ROW_TILE)
            tiles.append((rows, h_scr[rows, :]))
        outs = []
        for rows, h in tiles:
            p = jnp.dot(h, w_ref[...], preferred_element_type=F32)
            outs.append((rows, p, _log_f_and_key(p[:, DK:2 * DK], lb[0:1], lb[2:3], l1m[0:1]),
                         _log_f_and_key(p[:, 2 * DK:3 * DK], lb[1:2], lb[3:4], l1m[1:2])))
        for rows, p, (lf_f, k_f), (lf_b, k_b) in outs:
            q_scr[rows, :] = p[:, 0:DK]
            v_scr[rows, :] = p[:, 3 * DK:3 * DK + DV]
            g_scr[rows, :] = p[:, 3 * DK + DV:3 * DK + 2 * DV]
            lff_scr[rows, :] = lf_f
            kf_scr[rows, :] = k_f
            lfb_scr[rows, :] = lf_b
            kb_scr[rows, :] = k_b
        return carry
    lax.fori_loop(0, n_tiles // PROJ_TILES, proj_tiles, 0)

    prefix, m_chunk, m_diag_f, m_diag_b = _group_masks()

    def group_rows(grp):
        return pl.ds(pl.multiple_of(grp * GROUP_ROWS, GROUP_ROWS), GROUP_ROWS)

    def local_load(grp):
        rows = group_rows(grp)
        return (q_scr[rows, :], v_scr[rows, :], lff_scr[rows, :], kf_scr[rows, :], lfb_scr[rows, :],
                kb_scr[rows, :])

    def log2_decays(lf_f, lf_b):
        sums = jnp.dot(prefix, jnp.concatenate(_split2(lf_f) + _split2(lf_b), axis=1),
                       preferred_element_type=F32)
        pre_b = _sum2(sums, 1)
        total_b = jnp.concatenate(
            [jnp.broadcast_to(pre_b[g * c + c - 1:g * c + c], (c, DK)) for g in range(GROUP)], axis=0)
        return _sum2(sums, 0) * LOG2_E, ((total_b - pre_b) + lf_b) * LOG2_E

    def sub_block_span(b):
        return functools.reduce(jnp.maximum, [jnp.abs(b[r:r + 1] - b[r + SCAN_SUB - 1:r + SCAN_SUB])
                                              for r in range(0, GROUP_ROWS, SCAN_SUB)])

    def local_compute(b_f, b_b, unsafe, q, v32, lf_f, k_f, lf_b, k_b):
        v = v32.astype(BF16)
        vt = v32.T.astype(BF16)
        return (_group_dir(b_f, q, k_f, v, vt, m_chunk, m_diag_f, unsafe, reverse=False),
                _group_dir(b_b, q, k_b, v, vt, m_chunk, m_diag_b, unsafe, reverse=True))

    def local_store(grp, fwd, bwd):
        rows = group_rows(grp)
        for (o, qs, upd_t, dec), o_scr, qs_scr, u_scr, dec_scr in (
                (fwd, of_scr, qsf_scr, uf_scr, decf_scr), (bwd, ob_scr, qsb_scr, ub_scr, decb_scr)):
            o_scr[rows, :] = o
            qs_scr[rows, :] = qs
            for g in range(GROUP):
                u_scr[grp * GROUP + g] = upd_t[:, g * DK:(g + 1) * DK]
                dec_scr[grp * GROUP + g] = dec[g]

    def local(it, carry):
        groups = [it * LOCAL_GROUPS + u for u in range(LOCAL_GROUPS)]
        loaded = [local_load(grp) for grp in groups]
        decays = [log2_decays(vals[2], vals[4]) for vals in loaded]
        span = functools.reduce(jnp.maximum, [sub_block_span(b) for pair in decays for b in pair])
        unsafe = jnp.max(span) > MAX_SUB_DECAY
        for u, (b_f, b_b) in enumerate(decays):
            b_scr[u, 0] = b_f
            b_scr[u, 1] = b_b
        results = [local_compute(b_f, b_b, unsafe, *vals) for (b_f, b_b), vals in zip(decays, loaded)]
        for grp, (fwd, bwd) in zip(groups, results):
            local_store(grp, fwd, bwd)

        @pl.when(unsafe)
        def _():
            for u, grp in enumerate(groups):
                rows = group_rows(grp)
                q, v = q_scr[rows, :], v_scr[rows, :].astype(BF16)
                of_scr[rows, :] += _same_sub_block_pairs(b_scr[u, 0], q, kf_scr[rows, :], v, reverse=False)
                ob_scr[rows, :] += _same_sub_block_pairs(b_scr[u, 1], q, kb_scr[rows, :], v, reverse=True)
        return carry
    lax.fori_loop(0, n_chunks // GROUP // LOCAL_GROUPS, local, 0)

    def advance(ci, cj, sf, sb):
        sf_scr[ci] = sf.astype(BF16)
        sb_scr[cj] = sb.astype(BF16)
        return sf * decf_scr[ci] + uf_scr[ci], sb * decb_scr[cj] + ub_scr[cj]

    if has_s0:
        def step(i, carry):
            return advance(i, n_chunks - 1 - i, *carry)
        lax.fori_loop(0, n_chunks, step, (s0_ref[0].T, s0_ref[1].T), unroll=4)
    else:
        def seq(s, carry):
            sf = jnp.zeros((DV, DK), F32)
            sb = jnp.zeros((DV, DK), F32)
            for i in range(cps):
                sf, sb = advance(s * cps + i, s * cps + cps - 1 - i, sf, sb)
            st_ref[s, 0] = sf.T
            st_ref[s, 1] = sb.T
            return carry
        lax.fori_loop(0, n_seq, seq, 0)

    ng = ng_ref[...]
    zero_chunk = jnp.zeros((c, DK), BF16)

    def widen(qs):
        return jnp.concatenate(
            [jnp.concatenate([qs[h * c:(h + 1) * c] if h == g else zero_chunk for h in range(GROUP)], axis=0)
             for g in range(GROUP)], axis=1)

    def out_groups(it, carry):
        loaded = []
        for u in range(LOCAL_GROUPS):
            grp = it * LOCAL_GROUPS + u
            rows = group_rows(grp)
            qs = jnp.concatenate([widen(qsf_scr[rows, :]), widen(qsb_scr[rows, :])], axis=1)
            st = jnp.concatenate([sf_scr[grp * GROUP + g] for g in range(GROUP)]
                                 + [sb_scr[grp * GROUP + g] for g in range(GROUP)], axis=1)
            loaded.append((rows, qs, st, of_scr[rows, :] + ob_scr[rows, :], g_scr[rows, :]))
        for rows, qs, st, o_local, gate in loaded:
            o = o_local + _qk(qs, st)
            o = o * lax.rsqrt(jnp.mean(o * o, axis=-1, keepdims=True) + NORM_EPS) * ng
            og_ref[rows, :] = (o * _silu(gate)).astype(BF16)
        return carry
    lax.fori_loop(0, n_chunks // GROUP // LOCAL_GROUPS, out_groups, 0)


class _Group:
    def __init__(self, first, n, seq_len, latent):
        self.first, self.n, self.seq_len, self.latent = first, n, seq_len, latent

    def cond_row(self, block):
        return block if self.latent else CTX_ROW


def _hgrn_call(x, mix, mod_all, layer, n1g, w_in, lb, log_1m_lb, norm_g, j, grp, s0, states, t, x_first=None):
    x_first = grp.first if x_first is None else x_first
    n_seq = SEQ_BLOCK // grp.seq_len
    cps = grp.seq_len // SCAN_CHUNK
    n_chunks = SEQ_BLOCK // SCAN_CHUNK
    has_s0 = s0 is not None
    n_layers = w_in.shape[0]
    w_cat = pltpu.VMEM((D_MODEL, 3 * DK + 2 * DV), BF16)
    decays = pltpu.VMEM((LOCAL_GROUPS, 2, GROUP_ROWS, DK), F32)

    def section(s):
        return pl.BlockSpec((None, D_MODEL, DK), lambda b, h: (j, 0, s * HEADS + h))

    in_specs = [
        pl.BlockSpec((SEQ_BLOCK, D_MODEL), lambda b, h: (b + x_first, 0)),
        pl.BlockSpec((None, 1, 6, D_MODEL), lambda b, h: (layer, grp.cond_row(b), 0, 0)),
        pl.BlockSpec((1, D_MODEL), lambda b, h: (0, 0)),
        section(0), section(1), section(2), section(3), section(4),
        pl.BlockSpec((4, DK), lambda b, h: (0, h)),
        pl.BlockSpec((2, DK), lambda b, h: (0, h)),
        pl.BlockSpec((1, DV), lambda b, h: (0, h)),
    ]
    args = [x, mod_all, n1g, w_in, w_in, w_in, w_in, w_in, lb, log_1m_lb, norm_g]
    og_spec = pl.BlockSpec((SEQ_BLOCK, DV), lambda b, h: (b + grp.first, h))
    og_shape = jax.ShapeDtypeStruct((t, D_MODEL), BF16)
    aliases = {}
    if has_s0:
        in_specs.append(pl.BlockSpec((None, None, 2, None, DK, DV), lambda b, h: (b, j, 0, h, 0, 0)))
        args.append(s0)
        out_specs, out_shape = og_spec, og_shape
    else:
        st_shape = (grp.n * n_seq, n_layers, 2, HEADS, DK, DV)
        out_specs = (og_spec,
                     pl.BlockSpec((n_seq, None, 2, None, DK, DV), lambda b, h: (b, j, 0, h, 0, 0)))
        out_shape = (og_shape, jax.ShapeDtypeStruct(st_shape, F32))
        if states is not None:
            in_specs.append(pl.BlockSpec(memory_space=pl.ANY))
            args.append(states)
            aliases[len(args) - 1] = 1
    if mix is not None:
        in_specs.append(pl.BlockSpec(memory_space=pl.ANY))
        args.append(mix)
        aliases[len(args) - 1] = 0
    col = pltpu.VMEM((SEQ_BLOCK, DK), F32)
    colb = pltpu.VMEM((SEQ_BLOCK, DK), BF16)
    upd = pltpu.VMEM((n_chunks, DV, DK), F32)
    dec = pltpu.VMEM((n_chunks, 1, DK), F32)
    start = pltpu.VMEM((n_chunks, DV, DK), BF16)
    res = pl.pallas_call(
        functools.partial(_hgrn_body, n_in=len(args), n_seq=n_seq, cps=cps, has_s0=has_s0),
        grid=(grp.n, HEADS),
        in_specs=in_specs,
        out_specs=out_specs,
        out_shape=out_shape,
        scratch_shapes=[pltpu.VMEM((SEQ_BLOCK, D_MODEL), BF16)] + [col] * 9 + [colb, colb, upd, upd, dec, dec,
                                                                                  start, start, w_cat, decays],
        input_output_aliases=aliases,
        compiler_params=_cparams("arbitrary", "arbitrary"),
        name="hgrn_lat" if has_s0 else "hgrn_ctx",
    )(*args)
    return (res, None) if has_s0 else res


def _fourier_body(*refs, n_in, seq_len):
    x_ref, mod_ref, n1g_ref, cl_ref, sl_ref, cc_ref = refs[:6]
    z_ref, h_scr = refs[n_in:]
    rt = pl.program_id(1)
    n_tiles = SEQ_BLOCK // ROW_TILE

    @pl.when(rt == 0)
    def _():
        m = mod_ref[0]
        g = n1g_ref[...]

        def tile(t, carry):
            rows = pl.ds(pl.multiple_of(t * ROW_TILE, ROW_TILE), ROW_TILE)
            h_scr[rows, :] = _norm_mod(x_ref[rows, :], g, m[0:1], m[1:2]).astype(BF16)
            return carry
        lax.fori_loop(0, n_tiles, tile, 0)

    if seq_len == ROW_TILE:
        h = h_scr[pl.ds(pl.multiple_of(rt * ROW_TILE, ROW_TILE), ROW_TILE), :]
    else:
        h = h_scr[...]
    zc = jnp.dot(cl_ref[...], h, preferred_element_type=F32).astype(BF16)
    zs = jnp.dot(sl_ref[...], h, preferred_element_type=F32).astype(BF16)
    scale = 1.0 / math.sqrt(seq_len * FOURIER_CG)
    cg = FOURIER_CG
    for g in range(FOURIER_GROUPS):
        cat = jnp.concatenate([zc[:, g * cg:(g + 1) * cg], zs[:, g * cg:(g + 1) * cg]], axis=1)
        out = jnp.dot(cat, cc_ref[...], preferred_element_type=F32) * scale
        z_ref[:, g * cg:(g + 1) * cg] = out.astype(BF16)


def _dft_tables(n):
    j = jnp.arange(n, dtype=jnp.int32)

    def direct(k):
        ang = ((k[:, None] * j[None, :]) % n).astype(F32) * (2.0 * math.pi / n)
        return jnp.cos(ang), jnp.sin(ang)
    if n <= DFT_SPLIT:
        return direct(j)
    ca, sa = direct(jnp.arange(n // DFT_SPLIT, dtype=jnp.int32) * DFT_SPLIT)
    cb, sb = direct(jnp.arange(DFT_SPLIT, dtype=jnp.int32))
    cos = ca[:, None, :] * cb[None, :, :] - sa[:, None, :] * sb[None, :, :]
    sin = sa[:, None, :] * cb[None, :, :] + ca[:, None, :] * sb[None, :, :]
    return cos.reshape(n, n), sin.reshape(n, n)


def _fourier_call(x, mix, mod_all, layer, n1g, grp):
    t = x.shape[0]
    seq_len = grp.seq_len
    n_rt = SEQ_BLOCK // ROW_TILE
    cl, sl = _dft_tables(seq_len)
    cc, sc = _dft_tables(FOURIER_CG)
    cc2 = jnp.concatenate([cc, -sc], axis=0).astype(BF16)
    if seq_len == ROW_TILE:
        pos_spec = pl.BlockSpec((ROW_TILE, seq_len), lambda b, r: (0, 0))
    else:
        pos_spec = pl.BlockSpec((ROW_TILE, seq_len), lambda b, r: (r, 0))
    in_specs = [
        pl.BlockSpec((SEQ_BLOCK, D_MODEL), lambda b, r: (b + grp.first, 0)),
        pl.BlockSpec((None, 1, 6, D_MODEL), lambda b, r: (layer, grp.cond_row(b), 0, 0)),
        pl.BlockSpec((1, D_MODEL), lambda b, r: (0, 0)),
        pos_spec, pos_spec,
        pl.BlockSpec((2 * FOURIER_CG, FOURIER_CG), lambda b, r: (0, 0)),
    ]
    args = [x, mod_all, n1g, cl.astype(BF16), sl.astype(BF16), cc2]
    aliases = {}
    if mix is not None:
        in_specs.append(pl.BlockSpec(memory_space=pl.ANY))
        args.append(mix)
        aliases[len(args) - 1] = 0
    return pl.pallas_call(
        functools.partial(_fourier_body, n_in=len(args), seq_len=seq_len),
        grid=(grp.n, n_rt),
        in_specs=in_specs,
        out_specs=pl.BlockSpec((ROW_TILE, D_MODEL), lambda b, r: ((b + grp.first) * n_rt + r, 0)),
        out_shape=jax.ShapeDtypeStruct((t, D_MODEL), BF16),
        scratch_shapes=[pltpu.VMEM((SEQ_BLOCK, D_MODEL), BF16)],
        input_output_aliases=aliases,
        compiler_params=_cparams("arbitrary", "arbitrary"),
        name="fourier_lat" if grp.latent else "fourier_ctx",
    )(*args)


def _route(lg):
    grp = [lg[g:g + 1] for g in range(MOE_GROUPS)]
    gmax = functools.reduce(jnp.maximum, grp)
    gi = jnp.where(grp[0] == gmax, 0, jnp.where(grp[1] == gmax, 1, jnp.where(grp[2] == gmax, 2, 3)))
    pgv = 1.0 / functools.reduce(lambda a, b: a + b, [jnp.exp(g - gmax) for g in grp])
    sel = []
    for j in range(MOE_EPG):
        rows = [lg[MOE_GROUPS + g * MOE_EPG + j:MOE_GROUPS + g * MOE_EPG + j + 1] for g in range(MOE_GROUPS)]
        sel.append(jnp.where(gi == 0, rows[0], jnp.where(gi == 1, rows[1], jnp.where(gi == 2, rows[2], rows[3]))))
    m1 = functools.reduce(jnp.maximum, sel)
    i1 = jnp.where(sel[0] == m1, 0, jnp.where(sel[1] == m1, 1, jnp.where(sel[2] == m1, 2, 3)))
    rest = [jnp.where(i1 == j, -jnp.inf, sel[j]) for j in range(MOE_EPG)]
    m2 = functools.reduce(jnp.maximum, rest)
    i2 = jnp.where(rest[0] == m2, 0, jnp.where(rest[1] == m2, 1, jnp.where(rest[2] == m2, 2, 3)))
    e2 = jnp.exp(m2 - m1)
    w1 = pgv / (1.0 + e2)
    w2 = pgv * e2 / (1.0 + e2)
    r = lg.shape[1]
    ex1 = gi * MOE_EPG + i1
    ex2 = gi * MOE_EPG + i2
    eid = lax.broadcasted_iota(jnp.int32, (MOE_EXPERTS, r), 0)
    member = (eid == ex1) | (eid == ex2)
    t0 = lax.broadcasted_iota(jnp.int32, (r, r), 0)
    t1 = lax.broadcasted_iota(jnp.int32, (r, r), 1)
    rank = jnp.dot(member.astype(BF16), (t0 < t1).astype(BF16), preferred_element_type=F32)
    count = jnp.sum(member.astype(F32), axis=1, keepdims=True)
    padded = jnp.floor((count + (ROW_UNIT - 1)) * (1.0 / ROW_UNIT)) * ROW_UNIT
    e0 = lax.broadcasted_iota(jnp.int32, (MOE_EXPERTS, MOE_EXPERTS), 0)
    e1 = lax.broadcasted_iota(jnp.int32, (MOE_EXPERTS, MOE_EXPERTS), 1)
    start = jnp.dot((e1 < e0).astype(F32), jnp.broadcast_to(padded, (MOE_EXPERTS, LANES)),
                    precision=HIGHEST, preferred_element_type=F32)[:, 0:1]
    row = start + rank
    row1 = jnp.sum(jnp.where(eid == ex1, row, 0.0), axis=0, keepdims=True)
    row2 = jnp.sum(jnp.where(eid == ex2, row, 0.0), axis=0, keepdims=True)
    rid = lax.broadcasted_iota(jnp.int32, (LANES, r), 0)
    fields = (row1, row2, w1, w2, ex1.astype(F32), ex2.astype(F32))
    table = jnp.zeros((LANES, r), F32)
    for i, f in enumerate(fields):
        table = jnp.where(rid == i, f, table)
    return table, count


def _post_body(*refs, ctx_steps):
    if ctx_steps is None:
        x = refs[0][...]
    else:
        x = jnp.where(pl.program_id(0) < ctx_steps, refs[0][...], refs[1][...])
        refs = refs[1:]
    _, mix_ref, mod_ref, w_ref, n2g_ref, wr_ref, br_ref, xo_ref, xb_ref, ct_ref, cnt_ref = refs
    m = mod_ref[0]
    out = jnp.dot(mix_ref[...], w_ref[...], preferred_element_type=F32)
    xn = x + m[2:3] * out
    xo_ref[...] = xn
    h2 = _norm_mod(xn, n2g_ref[...], m[3:4], m[4:5])
    wr_hi, wr_lo = _split2(wr_ref[...])
    h2_hi, h2_lo = _split2(h2)
    lg = _qk(wr_hi, h2_hi) + (_qk(wr_hi, h2_lo) + _qk(wr_lo, h2_hi)) + br_ref[...]
    h2 = h2_hi
    rid = lax.broadcasted_iota(jnp.int32, (BLOCK_ROWS, TOKEN_BLOCK), 0)
    for u in range(POST_BLOCKS):
        cols = slice(u * TOKEN_BLOCK, (u + 1) * TOKEN_BLOCK)
        table, count = _route(lg[:, cols])
        onehot = ((rid == table[0:1].astype(jnp.int32)) | (rid == table[1:2].astype(jnp.int32))).astype(BF16)
        xb_ref[u * BLOCK_ROWS:(u + 1) * BLOCK_ROWS, :] = jnp.dot(
            onehot, h2[cols, :], preferred_element_type=F32).astype(BF16)
        ct_ref[cols, :] = table.T
        cnt_ref[u] = jnp.broadcast_to(count, (MOE_EXPERTS, LANES))


def _post_call(x, mix, mod_all, layer, w_out, j, n2g, wr_t, br, mod_row):
    t = mix.shape[0]
    nb = t // TOKEN_BLOCK
    rows = POST_BLOCKS * TOKEN_BLOCK
    tile = lambda i: (i, 0)
    full = lambda i: (0, 0)
    if isinstance(x, tuple):
        ctx_steps = x[0].shape[0] // rows
        x_args = list(x)
        x_specs = [pl.BlockSpec((rows, D_MODEL), lambda i: (jnp.minimum(i, ctx_steps - 1), 0)),
                   pl.BlockSpec((rows, D_MODEL), lambda i: (jnp.maximum(i - ctx_steps, 0), 0))]
    else:
        ctx_steps, x_args, x_specs = None, [x], [pl.BlockSpec((rows, D_MODEL), tile)]
    return pl.pallas_call(
        functools.partial(_post_body, ctx_steps=ctx_steps),
        grid=(nb // POST_BLOCKS,),
        in_specs=x_specs + [
            pl.BlockSpec((rows, D_MODEL), tile),
            pl.BlockSpec((None, 1, 6, D_MODEL), lambda i: (layer, mod_row(i * POST_BLOCKS), 0, 0)),
            pl.BlockSpec((None, D_MODEL, D_MODEL), lambda i: (j, 0, 0)),
            pl.BlockSpec((1, D_MODEL), full),
            pl.BlockSpec((ROUTER_ROWS, D_MODEL), full),
            pl.BlockSpec((ROUTER_ROWS, 1), full),
        ],
        out_specs=(
            pl.BlockSpec((rows, D_MODEL), tile),
            pl.BlockSpec((POST_BLOCKS * BLOCK_ROWS, D_MODEL), tile),
            pl.BlockSpec((rows, LANES), tile),
            pl.BlockSpec((POST_BLOCKS, MOE_EXPERTS, LANES), lambda i: (i, 0, 0)),
        ),
        out_shape=(
            jax.ShapeDtypeStruct((t, D_MODEL), F32),
            jax.ShapeDtypeStruct((nb * BLOCK_ROWS, D_MODEL), BF16),
            jax.ShapeDtypeStruct((t, LANES), F32),
            jax.ShapeDtypeStruct((nb, MOE_EXPERTS, LANES), F32),
        ),
        compiler_params=_cparams("arbitrary"),
        name="post",
    )(*x_args, mix, mod_all, w_out, n2g, wr_t, br)


def _moe_schedule(counts, n_tiles_max):
    units = (counts + ROW_UNIT - 1) // ROW_UNIT
    local = jnp.cumsum(units, axis=1) - units
    total = jnp.sum(units, axis=0)
    per_tile = FFN_TILE // ROW_UNIT
    tiles = (total + per_tile - 1) // per_tile
    region = tiles * per_tile
    region_start = jnp.cumsum(region) - region
    glob = region_start[None, :] + jnp.cumsum(units, axis=0) - units
    tile_end = jnp.cumsum(tiles)
    tile_start = tile_end - tiles
    n_tiles = tile_end[-1]
    s = jnp.arange(n_tiles_max, dtype=jnp.int32)
    live = s < n_tiles
    tile_expert = jnp.sum((tile_end[None, :] <= jnp.minimum(s, n_tiles - 1)[:, None]).astype(jnp.int32), axis=1)
    first = jnp.any((s[:, None] == tile_start[None, :]) & (tiles[None, :] > 0), axis=1) & live
    g = jnp.arange(n_tiles_max * per_tile, dtype=jnp.int32)
    block_units = BLOCK_ROWS // ROW_UNIT
    blocks = jnp.arange(counts.shape[0], dtype=jnp.int32)[:, None]
    seg_start = glob.T.reshape(-1)
    seg_offset = (blocks * block_units + local - glob).T.reshape(-1)
    step = seg_offset - jnp.concatenate([jnp.zeros((1,), jnp.int32), seg_offset[:-1]])
    source = g + jnp.sum(jnp.where(seg_start[None, :] <= g[:, None], step[None, :], 0), axis=1)
    real = jnp.any((g[:, None] >= region_start[None, :]) & (g[:, None] < (region_start + total)[None, :]), axis=1)
    source = jnp.where(real, source, block_units - 1)
    i32 = lambda a: a.astype(jnp.int32).reshape(-1)
    return {
        "local": i32(local), "units": i32(units), "glob": i32(glob), "source": i32(source),
        "tile_expert": tile_expert, "tile_first": i32(first), "n_tiles": i32(n_tiles),
    }


def _segment_copies(local_ref, units_ref, glob_ref, blk, make_copy):
    def per_expert(e, n):
        idx = blk * MOE_EXPERTS + e
        loc, cnt, glo = local_ref[idx], units_ref[idx], glob_ref[idx]

        def unit(u, carry):
            make_copy(pl.multiple_of((loc + u) * ROW_UNIT, ROW_UNIT),
                      pl.multiple_of((glo + u) * ROW_UNIT, ROW_UNIT)).start()
            return carry
        lax.fori_loop(0, cnt, unit, 0)
        return n + cnt
    return lax.fori_loop(0, MOE_EXPERTS, per_expert, 0)


def _wait_copies(n, make_copy):
    def one(u, carry):
        make_copy(0, 0).wait()
        return carry
    lax.fori_loop(0, n, one, 0)


def _ffn_body(expert_ref, first_ref, n_ref, source_ref, xb_ref, wg_ref, wu_ref, wd_ref, y_ref,
              x_scr, sem, wg_scr, wu_scr, wd_scr):
    s = pl.program_id(0)
    slot = s % 2
    per_tile = FFN_TILE // ROW_UNIT

    def gather(tile, buf, start):
        for u in range(per_tile):
            src = source_ref[tile * per_tile + u] if start else 0
            copy = pltpu.make_async_copy(xb_ref.at[pl.ds(pl.multiple_of(src * ROW_UNIT, ROW_UNIT), ROW_UNIT)],
                                         x_scr.at[buf, pl.ds(u * ROW_UNIT, ROW_UNIT)], sem.at[buf])
            if start:
                copy.start()
            else:
                copy.wait()

    @pl.when(s == 0)
    def _():
        gather(0, 0, start=True)

    @pl.when(s + 1 < n_ref[0])
    def _():
        gather(s + 1, 1 - slot, start=True)

    @pl.when(s < n_ref[0])
    def _():
        @pl.when(first_ref[s] == 1)
        def _():
            wg_scr[...] = wg_ref[...].astype(BF16)
            wu_scr[...] = wu_ref[...].astype(BF16)
            wd_scr[...] = wd_ref[...].astype(BF16)
        gather(s, slot, start=False)
        x = x_scr[slot]
        a = jnp.dot(x, wg_scr[...], preferred_element_type=F32)
        u = jnp.dot(x, wu_scr[...], preferred_element_type=F32)
        hid = (_silu(a) * u).astype(BF16)
        y_ref[...] = jnp.dot(hid, wd_scr[...], preferred_element_type=F32).astype(BF16)

    @pl.when(s >= n_ref[0])
    def _():
        y_ref[...] = jnp.zeros_like(y_ref)


def _ffn_call(xb, sched, w_gate, w_up, w_down, layer, n_tiles_max):
    def weight(s, expert, first, n, source):
        return (layer, expert[s], 0, 0)
    return pl.pallas_call(
        _ffn_body,
        grid_spec=pltpu.PrefetchScalarGridSpec(
            num_scalar_prefetch=4,
            grid=(n_tiles_max,),
            in_specs=[
                pl.BlockSpec(memory_space=pl.ANY),
                pl.BlockSpec((None, None, D_MODEL, MOE_D_FF), weight),
                pl.BlockSpec((None, None, D_MODEL, MOE_D_FF), weight),
                pl.BlockSpec((None, None, MOE_D_FF, D_MODEL), weight),
            ],
            out_specs=pl.BlockSpec((FFN_TILE, D_MODEL), lambda s, *_: (s, 0)),
            scratch_shapes=[pltpu.VMEM((2, FFN_TILE, D_MODEL), BF16), pltpu.SemaphoreType.DMA((2,)),
                            pltpu.VMEM((D_MODEL, MOE_D_FF), BF16), pltpu.VMEM((D_MODEL, MOE_D_FF), BF16),
                            pltpu.VMEM((MOE_D_FF, D_MODEL), BF16)],
        ),
        out_shape=jax.ShapeDtypeStruct((n_tiles_max * FFN_TILE, D_MODEL), BF16),
        compiler_params=_cparams("arbitrary"),
        name="moe_ffn",
    )(sched["tile_expert"], sched["tile_first"], sched["n_tiles"], sched["source"], xb, w_gate, w_up, w_down)


def _combine_body(*refs, split):
    final = split is not None
    local_ref, units_ref, glob_ref, x_ref, ct_ref, mod_ref = refs[:6]
    fg_ref = refs[6] if final else None
    ys_ref = refs[7 if final else 6]
    out_refs = refs[(8 if final else 7):-2]
    rows_scr, sem = refs[-2:]
    step = pl.program_id(0)
    slot = step % 2

    def copier(s, j):
        def copy(dst_row, src_row):
            return pltpu.make_async_copy(ys_ref.at[pl.ds(src_row, ROW_UNIT)],
                                         rows_scr.at[s, j, pl.ds(dst_row, ROW_UNIT)], sem.at[s])
        return copy

    def request(st, s):
        for j in range(COMBINE_BLOCKS):
            _segment_copies(local_ref, units_ref, glob_ref, st * COMBINE_BLOCKS + j, copier(s, j))

    @pl.when(step == 0)
    def _():
        rows_scr[...] = jnp.zeros_like(rows_scr)
        request(0, 0)

    @pl.when(step + 1 < pl.num_programs(0))
    def _():
        request(step + 1, 1 - slot)

    first_seg = step * COMBINE_BLOCKS * MOE_EXPERTS
    n_copies = lax.fori_loop(0, COMBINE_BLOCKS * MOE_EXPERTS, lambda e, n: n + units_ref[first_seg + e], 0)
    _wait_copies(n_copies, copier(slot, 0))

    rid = lax.broadcasted_iota(jnp.int32, (TOKEN_BLOCK, BLOCK_ROWS), 1)
    ys = []
    for j in range(COMBINE_BLOCKS):
        table = ct_ref[j * TOKEN_BLOCK:(j + 1) * TOKEN_BLOCK, :]
        row1 = table[:, 0:1].astype(jnp.int32)
        row2 = table[:, 1:2].astype(jnp.int32)
        weights = (jnp.where(rid == row1, table[:, 2:3], 0.0) + jnp.where(rid == row2, table[:, 3:4], 0.0))
        ys.append(jnp.dot(weights.astype(BF16), rows_scr[slot, j], preferred_element_type=F32))
    out = x_ref[...] + mod_ref[0][5:6] * jnp.concatenate(ys, axis=0)
    if not final:
        out_refs[0][...] = out
    else:
        ms = jnp.mean(out * out, axis=-1, keepdims=True)
        out = out * lax.rsqrt(ms + NORM_EPS) * fg_ref[...]

        @pl.when(step < split // COMBINE_BLOCKS)
        def _():
            out_refs[0][...] = out

        @pl.when(step >= split // COMBINE_BLOCKS)
        def _():
            out_refs[1][...] = out


def _combine_call(x, route_c, ys, sched, mod_all, layer, final_g, split, mod_row):
    t = x.shape[0]
    final = final_g is not None
    rows = COMBINE_BLOCKS * TOKEN_BLOCK
    tok = (rows, D_MODEL)
    if final:
        split_steps = split // COMBINE_BLOCKS
        out_specs = (pl.BlockSpec(tok, lambda i, *_: (jnp.minimum(i, split_steps - 1), 0)),
                     pl.BlockSpec(tok, lambda i, *_: (jnp.maximum(i - split_steps, 0), 0)))
        out_shape = (jax.ShapeDtypeStruct((split * TOKEN_BLOCK, D_MODEL), F32),
                     jax.ShapeDtypeStruct((t - split * TOKEN_BLOCK, D_MODEL), F32))
    else:
        out_specs = pl.BlockSpec(tok, lambda i, *_: (i, 0))
        out_shape = jax.ShapeDtypeStruct((t, D_MODEL), F32)
    in_specs = [
        pl.BlockSpec((rows, D_MODEL), lambda i, *_: (i, 0)),
        pl.BlockSpec((rows, LANES), lambda i, *_: (i, 0)),
        pl.BlockSpec((None, 1, 6, D_MODEL), lambda i, *_: (layer, mod_row(i * COMBINE_BLOCKS), 0, 0)),
    ]
    args = [x, route_c, mod_all]
    if final:
        in_specs.append(pl.BlockSpec((1, D_MODEL), lambda i, *_: (0, 0)))
        args.append(final_g)
    in_specs.append(pl.BlockSpec(memory_space=pl.ANY))
    args.append(ys)
    return pl.pallas_call(
        functools.partial(_combine_body, split=split if final else None),
        grid_spec=pltpu.PrefetchScalarGridSpec(
            num_scalar_prefetch=3,
            grid=(t // rows,),
            in_specs=in_specs,
            out_specs=out_specs,
            scratch_shapes=[pltpu.VMEM((2, COMBINE_BLOCKS, BLOCK_ROWS, D_MODEL), BF16),
                            pltpu.SemaphoreType.DMA((2,))],
        ),
        out_shape=out_shape,
        compiler_params=_cparams("arbitrary"),
        name="moe_combine",
    )(sched["local"], sched["units"], sched["glob"], *args)


def _moe(x, xb, route, counts, mod_all, layer, w_gate, w_up, w_down, final_g, split, mod_row):
    t = x.shape[0]
    nb = t // TOKEN_BLOCK
    max_rows = MOE_TOP_K * t + nb * MOE_EXPERTS * (ROW_UNIT - 1) + MOE_EXPERTS * (FFN_TILE - 1)
    n_tiles_max = -(-max_rows // FFN_TILE)
    sched = _moe_schedule(counts[:, :, 0].astype(jnp.int32), n_tiles_max)
    ys = _ffn_call(xb, sched, w_gate, w_up, w_down, layer, n_tiles_max)
    return _combine_call(x, route, ys, sched, mod_all, layer, final_g, split, mod_row)


def _grid_pos_embed(n_tokens):
    t = jnp.arange(n_tokens)
    r = (t // GRID_W).astype(F32)
    col = (t % GRID_W).astype(F32)
    quarter = D_MODEL // 4
    omega = 1.0 / (POS_BASE ** (jnp.arange(quarter, dtype=F32) / quarter))
    ar = r[:, None] * omega[None, :]
    ac = col[:, None] * omega[None, :]
    return jnp.concatenate([jnp.sin(ar), jnp.cos(ar), jnp.sin(ac), jnp.cos(ac)], axis=-1)


def _run_trunk(x_ctx, x_lat, ctx, lat, init_states, mod_all, p):
    n_ctx_blocks = ctx.n * (SEQ_BLOCK // TOKEN_BLOCK)
    lat_per_seq = lat.seq_len // TOKEN_BLOCK
    t = x_ctx.shape[0] + x_lat.shape[0]
    x = (x_ctx, x_lat)

    def cond_row(i):
        return jnp.where(i < n_ctx_blocks, CTX_ROW, (i - n_ctx_blocks) // lat_per_seq)

    states = None
    for i in range(DEPTH):
        j = i // 2
        n1g = p["norm1_g"][i:i + 1]
        if i % 2 == 0:
            hgrn = functools.partial(_hgrn_call, mod_all=mod_all, layer=i, n1g=n1g, w_in=p["hgrn_w_in"],
                                     lb=p["lb"][j], log_1m_lb=p["log_1m_lb"][j],
                                     norm_g=p["hgrn_norm_g"][j:j + 1], j=j, t=t)
            if isinstance(x, tuple):
                mix, states = hgrn(x[0], mix=None, grp=ctx, s0=None, states=states, x_first=0)
                mix, _ = hgrn(x[1], mix=mix, grp=lat, s0=init_states, states=None, x_first=0)
            else:
                mix, states = hgrn(x, mix=None, grp=ctx, s0=None, states=states)
                mix, _ = hgrn(x, mix=mix, grp=lat, s0=init_states, states=None)
            w_out = p["hgrn_w_out"]
        else:
            mix = _fourier_call(x, None, mod_all, i, n1g, ctx)
            mix = _fourier_call(x, mix, mod_all, i, n1g, lat)
            w_out = p["fourier_w_out"]
        x, xb, route, counts = _post_call(x, mix, mod_all, i, w_out, j, p["norm2_g"][i:i + 1],
                                          p["router_w_t"][i], p["router_b"][i], cond_row)
        final_g = p["final_norm_g"] if i == DEPTH - 1 else None
        x = _moe(x, xb, route, counts, mod_all, i, p["moe_w_gate"], p["moe_w_up"], p["moe_w_down"],
                 final_g, n_ctx_blocks, cond_row)
    return x, states


def kernel(x_prompt, x_sample, state_hgrn, c, c_ctx, w_mod, b_mod, norm1_g, norm2_g, hgrn_w_in,
           hgrn_lb_logits, hgrn_norm_g, hgrn_w_out, fourier_w_out, router_group_w, router_group_b,
           router_expert_w, router_expert_b, moe_w_gate, moe_w_up, moe_w_down, final_norm_g):
    batch, seq, _ = x_prompt.shape
    dec_batch, dec_seq, _ = x_sample.shape
    assert dec_batch <= CTX_ROW and seq == ROW_TILE and dec_seq == SEQ_BLOCK

    cond = jnp.zeros((COND_ROWS, D_MODEL), F32).at[:dec_batch].set(c).at[CTX_ROW].set(c_ctx)
    mod_all = _mod_call(cond, w_mod, b_mod).reshape(DEPTH, COND_ROWS, 6, D_MODEL)

    probs = jax.nn.softmax(hgrn_lb_logits.astype(F32), axis=0)
    cs = jnp.cumsum(probs, axis=0)
    lb = cs - cs[0:1]
    router_w = jnp.concatenate([router_group_w, router_expert_w], axis=-1)
    router_b = jnp.concatenate([router_group_b, router_expert_b], axis=-1)
    pad = ROUTER_ROWS - router_w.shape[-1]
    p = {
        "norm1_g": norm1_g, "norm2_g": norm2_g, "hgrn_norm_g": hgrn_norm_g,
        "hgrn_w_in": hgrn_w_in, "hgrn_w_out": hgrn_w_out.astype(BF16),
        "fourier_w_out": fourier_w_out.astype(BF16),
        "lb": jnp.concatenate([lb, 1.0 - lb], axis=1), "log_1m_lb": jnp.log1p(-lb),
        "router_w_t": jnp.pad(jnp.swapaxes(router_w, 1, 2), ((0, 0), (0, pad), (0, 0))),
        "router_b": jnp.pad(router_b, ((0, 0), (0, pad)))[..., None],
        "moe_w_gate": moe_w_gate, "moe_w_up": moe_w_up, "moe_w_down": moe_w_down,
        "final_norm_g": final_norm_g.reshape(1, D_MODEL),
    }

    t_ctx = batch * seq
    t_lat = dec_batch * dec_seq
    ctx = _Group(0, t_ctx // SEQ_BLOCK, seq, latent=False)
    lat = _Group(t_ctx // SEQ_BLOCK, t_lat // SEQ_BLOCK, dec_seq, latent=True)
    x_lat = _embed_call(x_sample.reshape(t_lat, D_MODEL), _grid_pos_embed(dec_seq))
    (y_ctx, y_lat), new_state = _run_trunk(x_prompt.reshape(t_ctx, D_MODEL), x_lat, ctx, lat, state_hgrn,
                                           mod_all, p)
    return (y_ctx.reshape(batch, seq, D_MODEL), y_lat.reshape(dec_batch, dec_seq, D_MODEL), new_state)
```

```python
import functools
import math

import jax
import jax.numpy as jnp
from jax import lax
from jax.experimental import pallas as pl
from jax.experimental.pallas import tpu as pltpu

F32 = jnp.float32
BF16 = jnp.bfloat16
HIGHEST = lax.Precision.HIGHEST

D_MODEL = 1024
DEPTH = 4
GRID_W = 64
HEADS = 8
DK = 128
DV = 128
FOURIER_GROUPS = 4
FOURIER_CG = D_MODEL // FOURIER_GROUPS
MOE_GROUPS = 4
MOE_EPG = 4
MOE_EXPERTS = 16
MOE_D_FF = 512
NORM_EPS = 1e-6
POS_BASE = 10000.0
DFT_SPLIT = 64
MAX_SUB_DECAY = 100.0
LOG2_E = 1.0 / math.log(2.0)

COND_ROWS = 8
CTX_ROW = 4
ROUTER_ROWS = 32
LANES = 128
SCAN_CHUNK = 64
SCAN_SUB = 16
GROUP = 4
GROUP_ROWS = GROUP * SCAN_CHUNK
LOCAL_GROUPS = 8
PROJ_TILES = 8
SEQ_BLOCK = 2048
ROW_TILE = 256
TOKEN_BLOCK = 256
POST_BLOCKS = 4
COMBINE_BLOCKS = 4
MOE_TOP_K = 2
ROW_UNIT = 16
BLOCK_ROWS = 768
FFN_TILE = 512
VMEM_LIMIT = 56 * 1024 * 1024


def _cparams(*sem):
    return pltpu.CompilerParams(dimension_semantics=sem, vmem_limit_bytes=VMEM_LIMIT)


def _silu(x):
    return x * jax.nn.sigmoid(x)


def _norm_mod(x, g, shift, scale):
    ms = jnp.mean(x * x, axis=-1, keepdims=True)
    return (x * lax.rsqrt(ms + NORM_EPS) * g) * (1.0 + scale) + shift


def _mod_body(c_ref, w_ref, b_ref, o_ref):
    s = _silu(c_ref[...])
    o_ref[0] = jnp.dot(s, w_ref[0], precision=HIGHEST, preferred_element_type=F32) + b_ref[0]


def _mod_call(cond, w_mod, b_mod):
    n_col = 6 * D_MODEL // D_MODEL
    return pl.pallas_call(
        _mod_body,
        grid=(DEPTH, n_col),
        in_specs=[
            pl.BlockSpec((COND_ROWS, D_MODEL), lambda i, n: (0, 0)),
            pl.BlockSpec((1, D_MODEL, D_MODEL), lambda i, n: (i, 0, n)),
            pl.BlockSpec((1, 1, D_MODEL), lambda i, n: (i, 0, n)),
        ],
        out_specs=pl.BlockSpec((1, COND_ROWS, D_MODEL), lambda i, n: (i, 0, n)),
        out_shape=jax.ShapeDtypeStruct((DEPTH, COND_ROWS, 6 * D_MODEL), F32),
        compiler_params=_cparams("arbitrary", "arbitrary"),
        name="mod",
    )(cond, w_mod, b_mod.reshape(DEPTH, 1, 6 * D_MODEL))


def _embed_body(x_ref, p_ref, o_ref):
    o_ref[...] = x_ref[...] + p_ref[...]


def _embed_call(x_lat, pos):
    rows = 2 * TOKEN_BLOCK
    per_seq = pos.shape[0] // rows
    return pl.pallas_call(
        _embed_body,
        grid=(x_lat.shape[0] // rows,),
        in_specs=[
            pl.BlockSpec((rows, D_MODEL), lambda i: (i, 0)),
            pl.BlockSpec((rows, D_MODEL), lambda i: (i % per_seq, 0)),
        ],
        out_specs=pl.BlockSpec((rows, D_MODEL), lambda i: (i, 0)),
        out_shape=jax.ShapeDtypeStruct(x_lat.shape, F32),
        compiler_params=_cparams("arbitrary"),
        name="embed",
    )(x_lat, pos)


def _log_f_and_key(z, lb, one_m_lb, log_1m_lb):
    t = jnp.exp(-jnp.abs(z))
    big = 1.0 / (1.0 + t)
    small = t * big
    pos = z >= 0.0
    f = lb + one_m_lb * jnp.where(pos, big, small)
    log_f = jnp.where(f > 0.0, jnp.log(f), log_1m_lb + z)
    return log_f, one_m_lb * jnp.where(pos, small, big)


def _group_masks():
    n = GROUP_ROWS
    t = lax.broadcasted_iota(jnp.int32, (n, n), 0)
    s = lax.broadcasted_iota(jnp.int32, (n, n), 1)
    same_chunk = (t // SCAN_CHUNK) == (s // SCAN_CHUNK)
    same_sub = (t // SCAN_SUB) == (s // SCAN_SUB)
    prefix = (same_chunk & (t >= s)).astype(BF16)
    return prefix, same_chunk, same_sub & (t >= s), same_sub & (t <= s)


def _qk(a, b):
    return lax.dot_general(a, b, (((1,), (1,)), ((), ())), preferred_element_type=F32)


def _split2(x):
    hi = x.astype(BF16)
    return [hi, (x - hi.astype(F32)).astype(BF16)]


def _sum2(s, i):
    o = 2 * i * DK
    return s[:, o:o + DK] + s[:, o + DK:o + 2 * DK]


def _same_sub_block_pairs(b, q, k, v, reverse):
    n = GROUP_ROWS
    pos = lax.broadcasted_iota(jnp.int32, (n, 1), 0) % SCAN_SUB
    vf = v.astype(F32)
    out = jnp.zeros((n, DV), F32)
    for d in range(SCAN_SUB):
        shift = (n - d) % n if reverse else d
        ks, bs, vs = (pltpu.roll(a, shift, 0) if shift else a for a in (k, b, vf))
        inside = (pos + d < SCAN_SUB) if reverse else (pos >= d)
        w = jnp.sum(q * ks * jnp.exp2(jnp.minimum(b - bs, 0.0)), axis=1, keepdims=True)
        out = out + jnp.where(inside, w, 0.0) * vs
    return out


def _group_dir(b, q, k, v, vt, m_chunk, m_diag, unsafe, reverse):
    c, sb = SCAN_CHUNK, SCAN_SUB
    nb = c // sb
    zero = jnp.zeros((sb, DK), BF16)
    qd, kd, qs, kl, decs = [], [], [], [], []
    q_lev = [[] for _ in range(nb - 1)]
    k_lev = [[] for _ in range(nb - 1)]
    for g in range(GROUP):
        starts = [g * c + i * sb for i in range(nb)]
        pos = [nb - 1 - i for i in range(nb)] if reverse else list(range(nb))
        end_at = {}
        for i, r in enumerate(starts):
            e = r if reverse else r + sb - 1
            end_at[pos[i]] = b[e:e + 1]
        edge = end_at[nb - 1]
        decs.append(jnp.exp2(edge))
        for i, r in enumerate(starts):
            m = r + sb // 2 if reverse else r + sb // 2 - 1
            bi, qi, ki, mid = b[r:r + sb], q[r:r + sb], k[r:r + sb], b[m:m + 1]
            qd.append((qi * jnp.exp2(bi - mid)).astype(BF16))
            kd.append((ki * jnp.exp2(mid - bi)).astype(BF16))
            for j in range(nb - 1):
                q_lev[j].append((qi * jnp.exp2(bi - end_at[j])).astype(BF16) if pos[i] > j else zero)
                k_lev[j].append((ki * jnp.exp2(end_at[j] - bi)).astype(BF16) if pos[i] == j else zero)
            qs.append((qi * jnp.exp2(bi)).astype(BF16))
            kl.append((ki * jnp.exp2(edge - bi)).astype(BF16))

    def rows(blocks):
        return jnp.concatenate(blocks, axis=0)
    s_lev = _qk(jnp.concatenate([rows(x) for x in q_lev], axis=1),
                jnp.concatenate([rows(x) for x in k_lev], axis=1))

    s_diag = _qk(rows(qd), rows(kd))
    scores = jnp.where(m_diag & jnp.logical_not(unsafe), s_diag, 0.0) + jnp.where(m_chunk, s_lev, 0.0)
    o = jnp.dot(scores.astype(BF16), v, preferred_element_type=F32)
    kl_wide = jnp.concatenate(
        [rows([blk if n // nb == g else zero for n, blk in enumerate(kl)]) for g in range(GROUP)], axis=1)
    upd_t = jnp.dot(vt, kl_wide, preferred_element_type=F32)
    return o, rows(qs), upd_t, decs


def _hgrn_body(*refs, n_in, n_seq, cps, has_s0):
    x_ref, mod_ref, n1g_ref = refs[:3]
    w_refs = refs[3:8]
    lb_ref, l1m_ref, ng_ref = refs[8:11]
    s0_ref = refs[11] if has_s0 else None
    og_ref = refs[n_in]
    st_ref = None if has_s0 else refs[n_in + 1]
    (h_scr, q_scr, v_scr, g_scr, lff_scr, kf_scr, lfb_scr, kb_scr, of_scr, ob_scr, qsf_scr, qsb_scr,
     uf_scr, ub_scr, decf_scr, decb_scr, sf_scr, sb_scr, w_ref, b_scr) = refs[n_in + (1 if has_s0 else 2):]
    c = SCAN_CHUNK
    n_chunks = SEQ_BLOCK // c
    n_tiles = SEQ_BLOCK // ROW_TILE
    for s, ref in enumerate(w_refs):
        w_ref[:, s * DK:(s + 1) * DK] = ref[...].astype(BF16)

    @pl.when(pl.program_id(1) == 0)
    def _():
        m = mod_ref[0]
        g = n1g_ref[...]

        def tile(t, carry):
            rows = pl.ds(pl.multiple_of(t * ROW_TILE, ROW_TILE), ROW_TILE)
            h_scr[rows, :] = _norm_mod(x_ref[rows, :], g, m[0:1], m[1:2]).astype(BF16)
            return carry
        lax.fori_loop(0, n_tiles, tile, 0)

    lb = lb_ref[...]
    l1m = l1m_ref[...]

    def proj_tiles(g, carry):
        tiles = []
        for u in range(PROJ_TILES):
            rows = pl.ds(pl.multiple_of((PROJ_TILES * g + u) * ROW_TILE, ROW_TILE), ROW_TILE)
            tiles.append((rows, h_scr[rows, :]))
        outs = []
        for rows, h in tiles:
            p = jnp.dot(h, w_ref[...], preferred_element_type=F32)
            outs.append((rows, p, _log_f_and_key(p[:, DK:2 * DK], lb[0:1], lb[2:3], l1m[0:1]),
                         _log_f_and_key(p[:, 2 * DK:3 * DK], lb[1:2], lb[3:4], l1m[1:2])))
        for rows, p, (lf_f, k_f), (lf_b, k_b) in outs:
            q_scr[rows, :] = p[:, 0:DK]
            v_scr[rows, :] = p[:, 3 * DK:3 * DK + DV]
            g_scr[rows, :] = p[:, 3 * DK + DV:3 * DK + 2 * DV]
            lff_scr[rows, :] = lf_f
            kf_scr[rows, :] = k_f
            lfb_scr[rows, :] = lf_b
            kb_scr[rows, :] = k_b
        return carry
    lax.fori_loop(0, n_tiles // PROJ_TILES, proj_tiles, 0)

    prefix, m_chunk, m_diag_f, m_diag_b = _group_masks()

    def group_rows(grp):
        return pl.ds(pl.multiple_of(grp * GROUP_ROWS, GROUP_ROWS), GROUP_ROWS)

    def local_load(grp):
        rows = group_rows(grp)
        return (q_scr[rows, :], v_scr[rows, :], lff_scr[rows, :], kf_scr[rows, :], lfb_scr[rows, :],
                kb_scr[rows, :])

    def log2_decays(lf_f, lf_b):
        sums = jnp.dot(prefix, jnp.concatenate(_split2(lf_f) + _split2(lf_b), axis=1),
                       preferred_element_type=F32)
        pre_b = _sum2(sums, 1)
        total_b = jnp.concatenate(
            [jnp.broadcast_to(pre_b[g * c + c - 1:g * c + c], (c, DK)) for g in range(GROUP)], axis=0)
        return _sum2(sums, 0) * LOG2_E, ((total_b - pre_b) + lf_b) * LOG2_E

    def sub_block_span(b):
        return functools.reduce(jnp.maximum, [jnp.abs(b[r:r + 1] - b[r + SCAN_SUB - 1:r + SCAN_SUB])
                                              for r in range(0, GROUP_ROWS, SCAN_SUB)])

    def local_compute(b_f, b_b, unsafe, q, v32, lf_f, k_f, lf_b, k_b):
        v = v32.astype(BF16)
        vt = v32.T.astype(BF16)
        return (_group_dir(b_f, q, k_f, v, vt, m_chunk, m_diag_f, unsafe, reverse=False),
                _group_dir(b_b, q, k_b, v, vt, m_chunk, m_diag_b, unsafe, reverse=True))

    def local_store(grp, fwd, bwd):
        rows = group_rows(grp)
        for (o, qs, upd_t, dec), o_scr, qs_scr, u_scr, dec_scr in (
                (fwd, of_scr, qsf_scr, uf_scr, decf_scr), (bwd, ob_scr, qsb_scr, ub_scr, decb_scr)):
            o_scr[rows, :] = o
            qs_scr[rows, :] = qs
            for g in range(GROUP):
                u_scr[grp * GROUP + g] = upd_t[:, g * DK:(g + 1) * DK]
                dec_scr[grp * GROUP + g] = dec[g]

    def local(it, carry):
        groups = [it * LOCAL_GROUPS + u for u in range(LOCAL_GROUPS)]
        loaded = [local_load(grp) for grp in groups]
        decays = [log2_decays(vals[2], vals[4]) for vals in loaded]
        span = functools.reduce(jnp.maximum, [sub_block_span(b) for pair in decays for b in pair])
        unsafe = jnp.max(span) > MAX_SUB_DECAY
        for u, (b_f, b_b) in enumerate(decays):
            b_scr[u, 0] = b_f
            b_scr[u, 1] = b_b
        results = [local_compute(b_f, b_b, unsafe, *vals) for (b_f, b_b), vals in zip(decays, loaded)]
        for grp, (fwd, bwd) in zip(groups, results):
            local_store(grp, fwd, bwd)

        @pl.when(unsafe)
        def _():
            for u, grp in enumerate(groups):
                rows = group_rows(grp)
                q, v = q_scr[rows, :], v_scr[rows, :].astype(BF16)
                of_scr[rows, :] += _same_sub_block_pairs(b_scr[u, 0], q, kf_scr[rows, :], v, reverse=False)
                ob_scr[rows, :] += _same_sub_block_pairs(b_scr[u, 1], q, kb_scr[rows, :], v, reverse=True)
        return carry
    lax.fori_loop(0, n_chunks // GROUP // LOCAL_GROUPS, local, 0)

    def advance(ci, cj, sf, sb):
        sf_scr[ci] = sf.astype(BF16)
        sb_scr[cj] = sb.astype(BF16)
        return sf * decf_scr[ci] + uf_scr[ci], sb * decb_scr[cj] + ub_scr[cj]

    if has_s0:
        def step(i, carry):
            return advance(i, n_chunks - 1 - i, *carry)
        lax.fori_loop(0, n_chunks, step, (s0_ref[0].T, s0_ref[1].T), unroll=4)
    else:
        def seq(s, carry):
            sf = jnp.zeros((DV, DK), F32)
            sb = jnp.zeros((DV, DK), F32)
            for i in range(cps):
                sf, sb = advance(s * cps + i, s * cps + cps - 1 - i, sf, sb)
            st_ref[s, 0] = sf.T
            st_ref[s, 1] = sb.T
            return carry
        lax.fori_loop(0, n_seq, seq, 0)

    ng = ng_ref[...]
    zero_chunk = jnp.zeros((c, DK), BF16)

    def widen(qs):
        return jnp.concatenate(
            [jnp.concatenate([qs[h * c:(h + 1) * c] if h == g else zero_chunk for h in range(GROUP)], axis=0)
             for g in range(GROUP)], axis=1)

    def out_groups(it, carry):
        loaded = []
        for u in range(LOCAL_GROUPS):
            grp = it * LOCAL_GROUPS + u
            rows = group_rows(grp)
            qs = jnp.concatenate([widen(qsf_scr[rows, :]), widen(qsb_scr[rows, :])], axis=1)
            st = jnp.concatenate([sf_scr[grp * GROUP + g] for g in range(GROUP)]
                                 + [sb_scr[grp * GROUP + g] for g in range(GROUP)], axis=1)
            loaded.append((rows, qs, st, of_scr[rows, :] + ob_scr[rows, :], g_scr[rows, :]))
        for rows, qs, st, o_local, gate in loaded:
            o = o_local + _qk(qs, st)
            o = o * lax.rsqrt(jnp.mean(o * o, axis=-1, keepdims=True) + NORM_EPS) * ng
            og_ref[rows, :] = (o * _silu(gate)).astype(BF16)
        return carry
    lax.fori_loop(0, n_chunks // GROUP // LOCAL_GROUPS, out_groups, 0)


class _Group:
    def __init__(self, first, n, seq_len, latent):
        self.first, self.n, self.seq_len, self.latent = first, n, seq_len, latent

    def cond_row(self, block):
        return block if self.latent else CTX_ROW


def _hgrn_call(x, mix, mod_all, layer, n1g, w_in, lb, log_1m_lb, norm_g, j, grp, s0, states, t, x_first=None):
    x_first = grp.first if x_first is None else x_first
    n_seq = SEQ_BLOCK // grp.seq_len
    cps = grp.seq_len // SCAN_CHUNK
    n_chunks = SEQ_BLOCK // SCAN_CHUNK
    has_s0 = s0 is not None
    n_layers = w_in.shape[0]
    w_cat = pltpu.VMEM((D_MODEL, 3 * DK + 2 * DV), BF16)
    decays = pltpu.VMEM((LOCAL_GROUPS, 2, GROUP_ROWS, DK), F32)

    def section(s):
        return pl.BlockSpec((None, D_MODEL, DK), lambda b, h: (j, 0, s * HEADS + h))

    in_specs = [
        pl.BlockSpec((SEQ_BLOCK, D_MODEL), lambda b, h: (b + x_first, 0)),
        pl.BlockSpec((None, 1, 6, D_MODEL), lambda b, h: (layer, grp.cond_row(b), 0, 0)),
        pl.BlockSpec((1, D_MODEL), lambda b, h: (0, 0)),
        section(0), section(1), section(2), section(3), section(4),
        pl.BlockSpec((4, DK), lambda b, h: (0, h)),
        pl.BlockSpec((2, DK), lambda b, h: (0, h)),
        pl.BlockSpec((1, DV), lambda b, h: (0, h)),
    ]
    args = [x, mod_all, n1g, w_in, w_in, w_in, w_in, w_in, lb, log_1m_lb, norm_g]
    og_spec = pl.BlockSpec((SEQ_BLOCK, DV), lambda b, h: (b + grp.first, h))
    og_shape = jax.ShapeDtypeStruct((t, D_MODEL), BF16)
    aliases = {}
    if has_s0:
        in_specs.append(pl.BlockSpec((None, None, 2, None, DK, DV), lambda b, h: (b, j, 0, h, 0, 0)))
        args.append(s0)
        out_specs, out_shape = og_spec, og_shape
    else:
        st_shape = (grp.n * n_seq, n_layers, 2, HEADS, DK, DV)
        out_specs = (og_spec,
                     pl.BlockSpec((n_seq, None, 2, None, DK, DV), lambda b, h: (b, j, 0, h, 0, 0)))
        out_shape = (og_shape, jax.ShapeDtypeStruct(st_shape, F32))
        if states is not None:
            in_specs.append(pl.BlockSpec(memory_space=pl.ANY))
            args.append(states)
            aliases[len(args) - 1] = 1
    if mix is not None:
        in_specs.append(pl.BlockSpec(memory_space=pl.ANY))
        args.append(mix)
        aliases[len(args) - 1] = 0
    col = pltpu.VMEM((SEQ_BLOCK, DK), F32)
    colb = pltpu.VMEM((SEQ_BLOCK, DK), BF16)
    upd = pltpu.VMEM((n_chunks, DV, DK), F32)
    dec = pltpu.VMEM((n_chunks, 1, DK), F32)
    start = pltpu.VMEM((n_chunks, DV, DK), BF16)
    res = pl.pallas_call(
        functools.partial(_hgrn_body, n_in=len(args), n_seq=n_seq, cps=cps, has_s0=has_s0),
        grid=(grp.n, HEADS),
        in_specs=in_specs,
        out_specs=out_specs,
        out_shape=out_shape,
        scratch_shapes=[pltpu.VMEM((SEQ_BLOCK, D_MODEL), BF16)] + [col] * 9 + [colb, colb, upd, upd, dec, dec,
                                                                                  start, start, w_cat, decays],
        input_output_aliases=aliases,
        compiler_params=_cparams("arbitrary", "arbitrary"),
        name="hgrn_lat" if has_s0 else "hgrn_ctx",
    )(*args)
    return (res, None) if has_s0 else res


def _fourier_body(*refs, n_in, seq_len):
    x_ref, mod_ref, n1g_ref, cl_ref, sl_ref, cc_ref = refs[:6]
    z_ref, h_scr = refs[n_in:]
    rt = pl.program_id(1)
    n_tiles = SEQ_BLOCK // ROW_TILE

    @pl.when(rt == 0)
    def _():
        m = mod_ref[0]
        g = n1g_ref[...]

        def tile(t, carry):
            rows = pl.ds(pl.multiple_of(t * ROW_TILE, ROW_TILE), ROW_TILE)
            h_scr[rows, :] = _norm_mod(x_ref[rows, :], g, m[0:1], m[1:2]).astype(BF16)
            return carry
        lax.fori_loop(0, n_tiles, tile, 0)

    if seq_len == ROW_TILE:
        h = h_scr[pl.ds(pl.multiple_of(rt * ROW_TILE, ROW_TILE), ROW_TILE), :]
    else:
        h = h_scr[...]
    zc = jnp.dot(cl_ref[...], h, preferred_element_type=F32).astype(BF16)
    zs = jnp.dot(sl_ref[...], h, preferred_element_type=F32).astype(BF16)
    scale = 1.0 / math.sqrt(seq_len * FOURIER_CG)
    cg = FOURIER_CG
    for g in range(FOURIER_GROUPS):
        cat = jnp.concatenate([zc[:, g * cg:(g + 1) * cg], zs[:, g * cg:(g + 1) * cg]], axis=1)
        out = jnp.dot(cat, cc_ref[...], preferred_element_type=F32) * scale
        z_ref[:, g * cg:(g + 1) * cg] = out.astype(BF16)


def _dft_tables(n):
    j = jnp.arange(n, dtype=jnp.int32)

    def direct(k):
        ang = ((k[:, None] * j[None, :]) % n).astype(F32) * (2.0 * math.pi / n)
        return jnp.cos(ang), jnp.sin(ang)
    if n <= DFT_SPLIT:
        return direct(j)
    ca, sa = direct(jnp.arange(n // DFT_SPLIT, dtype=jnp.int32) * DFT_SPLIT)
    cb, sb = direct(jnp.arange(DFT_SPLIT, dtype=jnp.int32))
    cos = ca[:, None, :] * cb[None, :, :] - sa[:, None, :] * sb[None, :, :]
    sin = sa[:, None, :] * cb[None, :, :] + ca[:, None, :] * sb[None, :, :]
    return cos.reshape(n, n), sin.reshape(n, n)


def _fourier_call(x, mix, mod_all, layer, n1g, grp):
    t = x.shape[0]
    seq_len = grp.seq_len
    n_rt = SEQ_BLOCK // ROW_TILE
    cl, sl = _dft_tables(seq_len)
    cc, sc = _dft_tables(FOURIER_CG)
    cc2 = jnp.concatenate([cc, -sc], axis=0).astype(BF16)
    if seq_len == ROW_TILE:
        pos_spec = pl.BlockSpec((ROW_TILE, seq_len), lambda b, r: (0, 0))
    else:
        pos_spec = pl.BlockSpec((ROW_TILE, seq_len), lambda b, r: (r, 0))
    in_specs = [
        pl.BlockSpec((SEQ_BLOCK, D_MODEL), lambda b, r: (b + grp.first, 0)),
        pl.BlockSpec((None, 1, 6, D_MODEL), lambda b, r: (layer, grp.cond_row(b), 0, 0)),
        pl.BlockSpec((1, D_MODEL), lambda b, r: (0, 0)),
        pos_spec, pos_spec,
        pl.BlockSpec((2 * FOURIER_CG, FOURIER_CG), lambda b, r: (0, 0)),
    ]
    args = [x, mod_all, n1g, cl.astype(BF16), sl.astype(BF16), cc2]
    aliases = {}
    if mix is not None:
        in_specs.append(pl.BlockSpec(memory_space=pl.ANY))
        args.append(mix)
        aliases[len(args) - 1] = 0
    return pl.pallas_call(
        functools.partial(_fourier_body, n_in=len(args), seq_len=seq_len),
        grid=(grp.n, n_rt),
        in_specs=in_specs,
        out_specs=pl.BlockSpec((ROW_TILE, D_MODEL), lambda b, r: ((b + grp.first) * n_rt + r, 0)),
        out_shape=jax.ShapeDtypeStruct((t, D_MODEL), BF16),
        scratch_shapes=[pltpu.VMEM((SEQ_BLOCK, D_MODEL), BF16)],
        input_output_aliases=aliases,
        compiler_params=_cparams("arbitrary", "arbitrary"),
        name="fourier_lat" if grp.latent else "fourier_ctx",
    )(*args)


def _route(lg):
    grp = [lg[g:g + 1] for g in range(MOE_GROUPS)]
    gmax = functools.reduce(jnp.maximum, grp)
    gi = jnp.where(grp[0] == gmax, 0, jnp.where(grp[1] == gmax, 1, jnp.where(grp[2] == gmax, 2, 3)))
    pgv = 1.0 / functools.reduce(lambda a, b: a + b, [jnp.exp(g - gmax) for g in grp])
    sel = []
    for j in range(MOE_EPG):
        rows = [lg[MOE_GROUPS + g * MOE_EPG + j:MOE_GROUPS + g * MOE_EPG + j + 1] for g in range(MOE_GROUPS)]
        sel.append(jnp.where(gi == 0, rows[0], jnp.where(gi == 1, rows[1], jnp.where(gi == 2, rows[2], rows[3]))))
    m1 = functools.reduce(jnp.maximum, sel)
    i1 = jnp.where(sel[0] == m1, 0, jnp.where(sel[1] == m1, 1, jnp.where(sel[2] == m1, 2, 3)))
    rest = [jnp.where(i1 == j, -jnp.inf, sel[j]) for j in range(MOE_EPG)]
    m2 = functools.reduce(jnp.maximum, rest)
    i2 = jnp.where(rest[0] == m2, 0, jnp.where(rest[1] == m2, 1, jnp.where(rest[2] == m2, 2, 3)))
    e2 = jnp.exp(m2 - m1)
    w1 = pgv / (1.0 + e2)
    w2 = pgv * e2 / (1.0 + e2)
    r = lg.shape[1]
    ex1 = gi * MOE_EPG + i1
    ex2 = gi * MOE_EPG + i2
    eid = lax.broadcasted_iota(jnp.int32, (MOE_EXPERTS, r), 0)
    member = (eid == ex1) | (eid == ex2)
    t0 = lax.broadcasted_iota(jnp.int32, (r, r), 0)
    t1 = lax.broadcasted_iota(jnp.int32, (r, r), 1)
    rank = jnp.dot(member.astype(BF16), (t0 < t1).astype(BF16), preferred_element_type=F32)
    count = jnp.sum(member.astype(F32), axis=1, keepdims=True)
    padded = jnp.floor((count + (ROW_UNIT - 1)) * (1.0 / ROW_UNIT)) * ROW_UNIT
    e0 = lax.broadcasted_iota(jnp.int32, (MOE_EXPERTS, MOE_EXPERTS), 0)
    e1 = lax.broadcasted_iota(jnp.int32, (MOE_EXPERTS, MOE_EXPERTS), 1)
    start = jnp.dot((e1 < e0).astype(F32), jnp.broadcast_to(padded, (MOE_EXPERTS, LANES)),
                    precision=HIGHEST, preferred_element_type=F32)[:, 0:1]
    row = start + rank
    row1 = jnp.sum(jnp.where(eid == ex1, row, 0.0), axis=0, keepdims=True)
    row2 = jnp.sum(jnp.where(eid == ex2, row, 0.0), axis=0, keepdims=True)
    rid = lax.broadcasted_iota(jnp.int32, (LANES, r), 0)
    fields = (row1, row2, w1, w2, ex1.astype(F32), ex2.astype(F32))
    table = jnp.zeros((LANES, r), F32)
    for i, f in enumerate(fields):
        table = jnp.where(rid == i, f, table)
    return table, count


def _post_body(*refs, ctx_steps):
    if ctx_steps is None:
        x = refs[0][...]
    else:
        x = jnp.where(pl.program_id(0) < ctx_steps, refs[0][...], refs[1][...])
        refs = refs[1:]
    _, mix_ref, mod_ref, w_ref, n2g_ref, wr_ref, br_ref, xo_ref, xb_ref, ct_ref, cnt_ref = refs
    m = mod_ref[0]
    out = jnp.dot(mix_ref[...], w_ref[...], preferred_element_type=F32)
    xn = x + m[2:3] * out
    xo_ref[...] = xn
    h2 = _norm_mod(xn, n2g_ref[...], m[3:4], m[4:5])
    wr_hi, wr_lo = _split2(wr_ref[...])
    h2_hi, h2_lo = _split2(h2)
    lg = _qk(wr_hi, h2_hi) + (_qk(wr_hi, h2_lo) + _qk(wr_lo, h2_hi)) + br_ref[...]
    h2 = h2_hi
    rid = lax.broadcasted_iota(jnp.int32, (BLOCK_ROWS, TOKEN_BLOCK), 0)
    for u in range(POST_BLOCKS):
        cols = slice(u * TOKEN_BLOCK, (u + 1) * TOKEN_BLOCK)
        table, count = _route(lg[:, cols])
        onehot = ((rid == table[0:1].astype(jnp.int32)) | (rid == table[1:2].astype(jnp.int32))).astype(BF16)
        xb_ref[u * BLOCK_ROWS:(u + 1) * BLOCK_ROWS, :] = jnp.dot(
            onehot, h2[cols, :], preferred_element_type=F32).astype(BF16)
        ct_ref[cols, :] = table.T
        cnt_ref[u] = jnp.broadcast_to(count, (MOE_EXPERTS, LANES))


def _post_call(x, mix, mod_all, layer, w_out, j, n2g, wr_t, br, mod_row):
    t = mix.shape[0]
    nb = t // TOKEN_BLOCK
    rows = POST_BLOCKS * TOKEN_BLOCK
    tile = lambda i: (i, 0)
    full = lambda i: (0, 0)
    if isinstance(x, tuple):
        ctx_steps = x[0].shape[0] // rows
        x_args = list(x)
        x_specs = [pl.BlockSpec((rows, D_MODEL), lambda i: (jnp.minimum(i, ctx_steps - 1), 0)),
                   pl.BlockSpec((rows, D_MODEL), lambda i: (jnp.maximum(i - ctx_steps, 0), 0))]
    else:
        ctx_steps, x_args, x_specs = None, [x], [pl.BlockSpec((rows, D_MODEL), tile)]
    return pl.pallas_call(
        functools.partial(_post_body, ctx_steps=ctx_steps),
        grid=(nb // POST_BLOCKS,),
        in_specs=x_specs + [
            pl.BlockSpec((rows, D_MODEL), tile),
            pl.BlockSpec((None, 1, 6, D_MODEL), lambda i: (layer, mod_row(i * POST_BLOCKS), 0, 0)),
            pl.BlockSpec((None, D_MODEL, D_MODEL), lambda i: (j, 0, 0)),
            pl.BlockSpec((1, D_MODEL), full),
            pl.BlockSpec((ROUTER_ROWS, D_MODEL), full),
            pl.BlockSpec((ROUTER_ROWS, 1), full),
        ],
        out_specs=(
            pl.BlockSpec((rows, D_MODEL), tile),
            pl.BlockSpec((POST_BLOCKS * BLOCK_ROWS, D_MODEL), tile),
            pl.BlockSpec((rows, LANES), tile),
            pl.BlockSpec((POST_BLOCKS, MOE_EXPERTS, LANES), lambda i: (i, 0, 0)),
        ),
        out_shape=(
            jax.ShapeDtypeStruct((t, D_MODEL), F32),
            jax.ShapeDtypeStruct((nb * BLOCK_ROWS, D_MODEL), BF16),
            jax.ShapeDtypeStruct((t, LANES), F32),
            jax.ShapeDtypeStruct((nb, MOE_EXPERTS, LANES), F32),
        ),
        compiler_params=_cparams("arbitrary"),
        name="post",
    )(*x_args, mix, mod_all, w_out, n2g, wr_t, br)


def _moe_schedule(counts, n_tiles_max):
    units = (counts + ROW_UNIT - 1) // ROW_UNIT
    local = jnp.cumsum(units, axis=1) - units
    total = jnp.sum(units, axis=0)
    per_tile = FFN_TILE // ROW_UNIT
    tiles = (total + per_tile - 1) // per_tile
    region = tiles * per_tile
    region_start = jnp.cumsum(region) - region
    glob = region_start[None, :] + jnp.cumsum(units, axis=0) - units
    tile_end = jnp.cumsum(tiles)
    tile_start = tile_end - tiles
    n_tiles = tile_end[-1]
    s = jnp.arange(n_tiles_max, dtype=jnp.int32)
    live = s < n_tiles
    tile_expert = jnp.sum((tile_end[None, :] <= jnp.minimum(s, n_tiles - 1)[:, None]).astype(jnp.int32), axis=1)
    first = jnp.any((s[:, None] == tile_start[None, :]) & (tiles[None, :] > 0), axis=1) & live
    g = jnp.arange(n_tiles_max * per_tile, dtype=jnp.int32)
    block_units = BLOCK_ROWS // ROW_UNIT
    blocks = jnp.arange(counts.shape[0], dtype=jnp.int32)[:, None]
    seg_start = glob.T.reshape(-1)
    seg_offset = (blocks * block_units + local - glob).T.reshape(-1)
    step = seg_offset - jnp.concatenate([jnp.zeros((1,), jnp.int32), seg_offset[:-1]])
    source = g + jnp.sum(jnp.where(seg_start[None, :] <= g[:, None], step[None, :], 0), axis=1)
    real = jnp.any((g[:, None] >= region_start[None, :]) & (g[:, None] < (region_start + total)[None, :]), axis=1)
    source = jnp.where(real, source, block_units - 1)
    i32 = lambda a: a.astype(jnp.int32).reshape(-1)
    return {
        "local": i32(local), "units": i32(units), "glob": i32(glob), "source": i32(source),
        "tile_expert": tile_expert, "tile_first": i32(first), "n_tiles": i32(n_tiles),
    }


def _segment_copies(local_ref, units_ref, glob_ref, blk, make_copy):
    def per_expert(e, n):
        idx = blk * MOE_EXPERTS + e
        loc, cnt, glo = local_ref[idx], units_ref[idx], glob_ref[idx]

        def unit(u, carry):
            make_copy(pl.multiple_of((loc + u) * ROW_UNIT, ROW_UNIT),
                      pl.multiple_of((glo + u) * ROW_UNIT, ROW_UNIT)).start()
            return carry
        lax.fori_loop(0, cnt, unit, 0)
        return n + cnt
    return lax.fori_loop(0, MOE_EXPERTS, per_expert, 0)


def _wait_copies(n, make_copy):
    def one(u, carry):
        make_copy(0, 0).wait()
        return carry
    lax.fori_loop(0, n, one, 0)


def _ffn_body(expert_ref, first_ref, n_ref, source_ref, xb_ref, wg_ref, wu_ref, wd_ref, y_ref,
              x_scr, sem, wg_scr, wu_scr, wd_scr):
    s = pl.program_id(0)
    slot = s % 2
    per_tile = FFN_TILE // ROW_UNIT

    def gather(tile, buf, start):
        for u in range(per_tile):
            src = source_ref[tile * per_tile + u] if start else 0
            copy = pltpu.make_async_copy(xb_ref.at[pl.ds(pl.multiple_of(src * ROW_UNIT, ROW_UNIT), ROW_UNIT)],
                                         x_scr.at[buf, pl.ds(u * ROW_UNIT, ROW_UNIT)], sem.at[buf])
            if start:
                copy.start()
            else:
                copy.wait()

    @pl.when(s == 0)
    def _():
        gather(0, 0, start=True)

    @pl.when(s + 1 < n_ref[0])
    def _():
        gather(s + 1, 1 - slot, start=True)

    @pl.when(s < n_ref[0])
    def _():
        @pl.when(first_ref[s] == 1)
        def _():
            wg_scr[...] = wg_ref[...].astype(BF16)
            wu_scr[...] = wu_ref[...].astype(BF16)
            wd_scr[...] = wd_ref[...].astype(BF16)
        gather(s, slot, start=False)
        x = x_scr[slot]
        a = jnp.dot(x, wg_scr[...], preferred_element_type=F32)
        u = jnp.dot(x, wu_scr[...], preferred_element_type=F32)
        hid = (_silu(a) * u).astype(BF16)
        y_ref[...] = jnp.dot(hid, wd_scr[...], preferred_element_type=F32).astype(BF16)

    @pl.when(s >= n_ref[0])
    def _():
        y_ref[...] = jnp.zeros_like(y_ref)


def _ffn_call(xb, sched, w_gate, w_up, w_down, layer, n_tiles_max):
    def weight(s, expert, first, n, source):
        return (layer, expert[s], 0, 0)
    return pl.pallas_call(
        _ffn_body,
        grid_spec=pltpu.PrefetchScalarGridSpec(
            num_scalar_prefetch=4,
            grid=(n_tiles_max,),
            in_specs=[
                pl.BlockSpec(memory_space=pl.ANY),
                pl.BlockSpec((None, None, D_MODEL, MOE_D_FF), weight),
                pl.BlockSpec((None, None, D_MODEL, MOE_D_FF), weight),
                pl.BlockSpec((None, None, MOE_D_FF, D_MODEL), weight),
            ],
            out_specs=pl.BlockSpec((FFN_TILE, D_MODEL), lambda s, *_: (s, 0)),
            scratch_shapes=[pltpu.VMEM((2, FFN_TILE, D_MODEL), BF16), pltpu.SemaphoreType.DMA((2,)),
                            pltpu.VMEM((D_MODEL, MOE_D_FF), BF16), pltpu.VMEM((D_MODEL, MOE_D_FF), BF16),
                            pltpu.VMEM((MOE_D_FF, D_MODEL), BF16)],
        ),
        out_shape=jax.ShapeDtypeStruct((n_tiles_max * FFN_TILE, D_MODEL), BF16),
        compiler_params=_cparams("arbitrary"),
        name="moe_ffn",
    )(sched["tile_expert"], sched["tile_first"], sched["n_tiles"], sched["source"], xb, w_gate, w_up, w_down)


def _combine_body(*refs, split):
    final = split is not None
    local_ref, units_ref, glob_ref, x_ref, ct_ref, mod_ref = refs[:6]
    fg_ref = refs[6] if final else None
    ys_ref = refs[7 if final else 6]
    out_refs = refs[(8 if final else 7):-2]
    rows_scr, sem = refs[-2:]
    step = pl.program_id(0)
    slot = step % 2

    def copier(s, j):
        def copy(dst_row, src_row):
            return pltpu.make_async_copy(ys_ref.at[pl.ds(src_row, ROW_UNIT)],
                                         rows_scr.at[s, j, pl.ds(dst_row, ROW_UNIT)], sem.at[s])
        return copy

    def request(st, s):
        for j in range(COMBINE_BLOCKS):
            _segment_copies(local_ref, units_ref, glob_ref, st * COMBINE_BLOCKS + j, copier(s, j))

    @pl.when(step == 0)
    def _():
        rows_scr[...] = jnp.zeros_like(rows_scr)
        request(0, 0)

    @pl.when(step + 1 < pl.num_programs(0))
    def _():
        request(step + 1, 1 - slot)

    first_seg = step * COMBINE_BLOCKS * MOE_EXPERTS
    n_copies = lax.fori_loop(0, COMBINE_BLOCKS * MOE_EXPERTS, lambda e, n: n + units_ref[first_seg + e], 0)
    _wait_copies(n_copies, copier(slot, 0))

    rid = lax.broadcasted_iota(jnp.int32, (TOKEN_BLOCK, BLOCK_ROWS), 1)
    ys = []
    for j in range(COMBINE_BLOCKS):
        table = ct_ref[j * TOKEN_BLOCK:(j + 1) * TOKEN_BLOCK, :]
        row1 = table[:, 0:1].astype(jnp.int32)
        row2 = table[:, 1:2].astype(jnp.int32)
        weights = (jnp.where(rid == row1, table[:, 2:3], 0.0) + jnp.where(rid == row2, table[:, 3:4], 0.0))
        ys.append(jnp.dot(weights.astype(BF16), rows_scr[slot, j], preferred_element_type=F32))
    out = x_ref[...] + mod_ref[0][5:6] * jnp.concatenate(ys, axis=0)
    if not final:
        out_refs[0][...] = out
    else:
        ms = jnp.mean(out * out, axis=-1, keepdims=True)
        out = out * lax.rsqrt(ms + NORM_EPS) * fg_ref[...]

        @pl.when(step < split // COMBINE_BLOCKS)
        def _():
            out_refs[0][...] = out

        @pl.when(step >= split // COMBINE_BLOCKS)
        def _():
            out_refs[1][...] = out


def _combine_call(x, route_c, ys, sched, mod_all, layer, final_g, split, mod_row):
    t = x.shape[0]
    final = final_g is not None
    rows = COMBINE_BLOCKS * TOKEN_BLOCK
    tok = (rows, D_MODEL)
    if final:
        split_steps = split // COMBINE_BLOCKS
        out_specs = (pl.BlockSpec(tok, lambda i, *_: (jnp.minimum(i, split_steps - 1), 0)),
                     pl.BlockSpec(tok, lambda i, *_: (jnp.maximum(i - split_steps, 0), 0)))
        out_shape = (jax.ShapeDtypeStruct((split * TOKEN_BLOCK, D_MODEL), F32),
                     jax.ShapeDtypeStruct((t - split * TOKEN_BLOCK, D_MODEL), F32))
    else:
        out_specs = pl.BlockSpec(tok, lambda i, *_: (i, 0))
        out_shape = jax.ShapeDtypeStruct((t, D_MODEL), F32)
    in_specs = [
        pl.BlockSpec((rows, D_MODEL), lambda i, *_: (i, 0)),
        pl.BlockSpec((rows, LANES), lambda i, *_: (i, 0)),
        pl.BlockSpec((None, 1, 6, D_MODEL), lambda i, *_: (layer, mod_row(i * COMBINE_BLOCKS), 0, 0)),
    ]
    args = [x, route_c, mod_all]
    if final:
        in_specs.append(pl.BlockSpec((1, D_MODEL), lambda i, *_: (0, 0)))
        args.append(final_g)
    in_specs.append(pl.BlockSpec(memory_space=pl.ANY))
    args.append(ys)
    return pl.pallas_call(
        functools.partial(_combine_body, split=split if final else None),
        grid_spec=pltpu.PrefetchScalarGridSpec(
            num_scalar_prefetch=3,
            grid=(t // rows,),
            in_specs=in_specs,
            out_specs=out_specs,
            scratch_shapes=[pltpu.VMEM((2, COMBINE_BLOCKS, BLOCK_ROWS, D_MODEL), BF16),
                            pltpu.SemaphoreType.DMA((2,))],
        ),
        out_shape=out_shape,
        compiler_params=_cparams("arbitrary"),
        name="moe_combine",
    )(sched["local"], sched["units"], sched["glob"], *args)


def _moe(x, xb, route, counts, mod_all, layer, w_gate, w_up, w_down, final_g, split, mod_row):
    t = x.shape[0]
    nb = t // TOKEN_BLOCK
    max_rows = MOE_TOP_K * t + nb * MOE_EXPERTS * (ROW_UNIT - 1) + MOE_EXPERTS * (FFN_TILE - 1)
    n_tiles_max = -(-max_rows // FFN_TILE)
    sched = _moe_schedule(counts[:, :, 0].astype(jnp.int32), n_tiles_max)
    ys = _ffn_call(xb, sched, w_gate, w_up, w_down, layer, n_tiles_max)
    return _combine_call(x, route, ys, sched, mod_all, layer, final_g, split, mod_row)


def _grid_pos_embed(n_tokens):
    t = jnp.arange(n_tokens)
    r = (t // GRID_W).astype(F32)
    col = (t % GRID_W).astype(F32)
    quarter = D_MODEL // 4
    omega = 1.0 / (POS_BASE ** (jnp.arange(quarter, dtype=F32) / quarter))
    ar = r[:, None] * omega[None, :]
    ac = col[:, None] * omega[None, :]
    return jnp.concatenate([jnp.sin(ar), jnp.cos(ar), jnp.sin(ac), jnp.cos(ac)], axis=-1)


def _run_trunk(x_ctx, x_lat, ctx, lat, init_states, mod_all, p):
    n_ctx_blocks = ctx.n * (SEQ_BLOCK // TOKEN_BLOCK)
    lat_per_seq = lat.seq_len // TOKEN_BLOCK
    t = x_ctx.shape[0] + x_lat.shape[0]
    x = (x_ctx, x_lat)

    def cond_row(i):
        return jnp.where(i < n_ctx_blocks, CTX_ROW, (i - n_ctx_blocks) // lat_per_seq)

    states = None
    for i in range(DEPTH):
        j = i // 2
        n1g = p["norm1_g"][i:i + 1]
        if i % 2 == 0:
            hgrn = functools.partial(_hgrn_call, mod_all=mod_all, layer=i, n1g=n1g, w_in=p["hgrn_w_in"],
                                     lb=p["lb"][j], log_1m_lb=p["log_1m_lb"][j],
                                     norm_g=p["hgrn_norm_g"][j:j + 1], j=j, t=t)
            if isinstance(x, tuple):
                mix, states = hgrn(x[0], mix=None, grp=ctx, s0=None, states=states, x_first=0)
                mix, _ = hgrn(x[1], mix=mix, grp=lat, s0=init_states, states=None, x_first=0)
            else:
                mix, states = hgrn(x, mix=None, grp=ctx, s0=None, states=states)
                mix, _ = hgrn(x, mix=mix, grp=lat, s0=init_states, states=None)
            w_out = p["hgrn_w_out"]
        else:
            mix = _fourier_call(x, None, mod_all, i, n1g, ctx)
            mix = _fourier_call(x, mix, mod_all, i, n1g, lat)
            w_out = p["fourier_w_out"]
        x, xb, route, counts = _post_call(x, mix, mod_all, i, w_out, j, p["norm2_g"][i:i + 1],
                                          p["router_w_t"][i], p["router_b"][i], cond_row)
        final_g = p["final_norm_g"] if i == DEPTH - 1 else None
        x = _moe(x, xb, route, counts, mod_all, i, p["moe_w_gate"], p["moe_w_up"], p["moe_w_down"],
                 final_g, n_ctx_blocks, cond_row)
    return x, states


def kernel(x_prompt, x_sample, state_hgrn, c, c_ctx, w_mod, b_mod, norm1_g, norm2_g, hgrn_w_in,
           hgrn_lb_logits, hgrn_norm_g, hgrn_w_out, fourier_w_out, router_group_w, router_group_b,
           router_expert_w, router_expert_b, moe_w_gate, moe_w_up, moe_w_down, final_norm_g):
    batch, seq, _ = x_prompt.shape
    dec_batch, dec_seq, _ = x_sample.shape
    assert dec_batch <= CTX_ROW and seq == ROW_TILE and dec_seq == SEQ_BLOCK

    cond = jnp.zeros((COND_ROWS, D_MODEL), F32).at[:dec_batch].set(c).at[CTX_ROW].set(c_ctx)
    mod_all = _mod_call(cond, w_mod, b_mod).reshape(DEPTH, COND_ROWS, 6, D_MODEL)

    probs = jax.nn.softmax(hgrn_lb_logits.astype(F32), axis=0)
    cs = jnp.cumsum(probs, axis=0)
    lb = cs - cs[0:1]
    router_w = jnp.concatenate([router_group_w, router_expert_w], axis=-1)
    router_b = jnp.concatenate([router_group_b, router_expert_b], axis=-1)
    pad = ROUTER_ROWS - router_w.shape[-1]
    p = {
        "norm1_g": norm1_g, "norm2_g": norm2_g, "hgrn_norm_g": hgrn_norm_g,
        "hgrn_w_in": hgrn_w_in, "hgrn_w_out": hgrn_w_out.astype(BF16),
        "fourier_w_out": fourier_w_out.astype(BF16),
        "lb": jnp.concatenate([lb, 1.0 - lb], axis=1), "log_1m_lb": jnp.log1p(-lb),
        "router_w_t": jnp.pad(jnp.swapaxes(router_w, 1, 2), ((0, 0), (0, pad), (0, 0))),
        "router_b": jnp.pad(router_b, ((0, 0), (0, pad)))[..., None],
        "moe_w_gate": moe_w_gate, "moe_w_up": moe_w_up, "moe_w_down": moe_w_down,
        "final_norm_g": final_norm_g.reshape(1, D_MODEL),
    }

    t_ctx = batch * seq
    t_lat = dec_batch * dec_seq
    ctx = _Group(0, t_ctx // SEQ_BLOCK, seq, latent=False)
    lat = _Group(t_ctx // SEQ_BLOCK, t_lat // SEQ_BLOCK, dec_seq, latent=True)
    x_lat = _embed_call(x_sample.reshape(t_lat, D_MODEL), _grid_pos_embed(dec_seq))
    (y_ctx, y_lat), new_state = _run_trunk(x_prompt.reshape(t_ctx, D_MODEL), x_lat, ctx, lat, state_hgrn,
                                           mod_all, p)
    return (y_ctx.reshape(batch, seq, D_MODEL), y_lat.reshape(dec_batch, dec_seq, D_MODEL), new_state)
```

```python
import functools
import math

import jax
import jax.numpy as jnp
from jax import lax
from jax.experimental import pallas as pl
from jax.experimental.pallas import tpu as pltpu

F32 = jnp.float32
BF16 = jnp.bfloat16
HIGHEST = lax.Precision.HIGHEST

D_MODEL = 1024
DEPTH = 4
GRID_W = 64
HEADS = 8
DK = 128
DV = 128
FOURIER_GROUPS = 4
FOURIER_CG = D_MODEL // FOURIER_GROUPS
MOE_GROUPS = 4
MOE_EPG = 4
MOE_EXPERTS = 16
MOE_D_FF = 512
NORM_EPS = 1e-6
POS_BASE = 10000.0
DFT_SPLIT = 64
MAX_SUB_DECAY = 100.0
LOG2_E = 1.0 / math.log(2.0)

COND_ROWS = 8
CTX_ROW = 4
ROUTER_ROWS = 32
LANES = 128
SCAN_CHUNK = 64
SCAN_SUB = 16
GROUP = 4
GROUP_ROWS = GROUP * SCAN_CHUNK
LOCAL_GROUPS = 8
PROJ_TILES = 8
SEQ_BLOCK = 2048
ROW_TILE = 256
TOKEN_BLOCK = 256
POST_BLOCKS = 4
COMBINE_BLOCKS = 4
MOE_TOP_K = 2
ROW_UNIT = 16
BLOCK_ROWS = 768
FFN_TILE = 512
VMEM_LIMIT = 56 * 1024 * 1024


def _cparams(*sem):
    return pltpu.CompilerParams(dimension_semantics=sem, vmem_limit_bytes=VMEM_LIMIT)


def _silu(x):
    return x * jax.nn.sigmoid(x)


def _norm_mod(x, g, shift, scale):
    ms = jnp.mean(x * x, axis=-1, keepdims=True)
    return (x * lax.rsqrt(ms + NORM_EPS) * g) * (1.0 + scale) + shift


def _mod_body(c_ref, w_ref, b_ref, o_ref):
    s_hi, s_lo = _split2(_silu(c_ref[...]))
    w_hi, w_lo = _split2(w_ref[0])
    dot = functools.partial(jnp.dot, preferred_element_type=F32)
    o_ref[0] = dot(s_hi, w_hi) + (dot(s_hi, w_lo) + dot(s_lo, w_hi)) + b_ref[0]


def _mod_call(cond, w_mod, b_mod):
    n_col = 6 * D_MODEL // D_MODEL
    return pl.pallas_call(
        _mod_body,
        grid=(DEPTH, n_col),
        in_specs=[
            pl.BlockSpec((COND_ROWS, D_MODEL), lambda i, n: (0, 0)),
            pl.BlockSpec((1, D_MODEL, D_MODEL), lambda i, n: (i, 0, n)),
            pl.BlockSpec((1, 1, D_MODEL), lambda i, n: (i, 0, n)),
        ],
        out_specs=pl.BlockSpec((1, COND_ROWS, D_MODEL), lambda i, n: (i, 0, n)),
        out_shape=jax.ShapeDtypeStruct((DEPTH, COND_ROWS, 6 * D_MODEL), F32),
        compiler_params=_cparams("arbitrary", "arbitrary"),
        name="mod",
    )(cond, w_mod, b_mod.reshape(DEPTH, 1, 6 * D_MODEL))


def _embed_body(x_ref, p_ref, o_ref):
    o_ref[...] = x_ref[...] + p_ref[...]


def _embed_call(x_lat, pos):
    rows = 2 * TOKEN_BLOCK
    per_seq = pos.shape[0] // rows
    return pl.pallas_call(
        _embed_body,
        grid=(x_lat.shape[0] // rows,),
        in_specs=[
            pl.BlockSpec((rows, D_MODEL), lambda i: (i, 0)),
            pl.BlockSpec((rows, D_MODEL), lambda i: (i % per_seq, 0)),
        ],
        out_specs=pl.BlockSpec((rows, D_MODEL), lambda i: (i, 0)),
        out_shape=jax.ShapeDtypeStruct(x_lat.shape, F32),
        compiler_params=_cparams("arbitrary"),
        name="embed",
    )(x_lat, pos)


def _log_f_and_key(z, lb, one_m_lb, log_1m_lb):
    t = jnp.exp(-jnp.abs(z))
    big = 1.0 / (1.0 + t)
    small = t * big
    pos = z >= 0.0
    f = lb + one_m_lb * jnp.where(pos, big, small)
    log_f = jnp.where(f > 0.0, jnp.log(f), log_1m_lb + z)
    return log_f, one_m_lb * jnp.where(pos, small, big)


def _group_masks():
    n = GROUP_ROWS
    t = lax.broadcasted_iota(jnp.int32, (n, n), 0)
    s = lax.broadcasted_iota(jnp.int32, (n, n), 1)
    same_chunk = (t // SCAN_CHUNK) == (s // SCAN_CHUNK)
    same_sub = (t // SCAN_SUB) == (s // SCAN_SUB)
    prefix = (same_chunk & (t >= s)).astype(BF16)
    return prefix, same_chunk, same_sub & (t >= s), same_sub & (t <= s)


def _qk(a, b):
    return lax.dot_general(a, b, (((1,), (1,)), ((), ())), preferred_element_type=F32)


def _split2(x):
    hi = x.astype(BF16)
    return [hi, (x - hi.astype(F32)).astype(BF16)]


def _sum2(s, i):
    o = 2 * i * DK
    return s[:, o:o + DK] + s[:, o + DK:o + 2 * DK]


def _same_sub_block_pairs(b, q, k, v, reverse):
    n = GROUP_ROWS
    pos = lax.broadcasted_iota(jnp.int32, (n, 1), 0) % SCAN_SUB
    vf = v.astype(F32)
    out = jnp.zeros((n, DV), F32)
    for d in range(SCAN_SUB):
        shift = (n - d) % n if reverse else d
        ks, bs, vs = (pltpu.roll(a, shift, 0) if shift else a for a in (k, b, vf))
        inside = (pos + d < SCAN_SUB) if reverse else (pos >= d)
        w = jnp.sum(q * ks * jnp.exp2(jnp.minimum(b - bs, 0.0)), axis=1, keepdims=True)
        out = out + jnp.where(inside, w, 0.0) * vs
    return out


def _group_dir(b, q, k, v, vt, m_chunk, m_diag, unsafe, reverse):
    c, sb = SCAN_CHUNK, SCAN_SUB
    nb = c // sb
    zero = jnp.zeros((sb, DK), BF16)
    qd, kd, qs, kl, decs = [], [], [], [], []
    q_lev = [[] for _ in range(nb - 1)]
    k_lev = [[] for _ in range(nb - 1)]
    for g in range(GROUP):
        starts = [g * c + i * sb for i in range(nb)]
        pos = [nb - 1 - i for i in range(nb)] if reverse else list(range(nb))
        end_at = {}
        for i, r in enumerate(starts):
            e = r if reverse else r + sb - 1
            end_at[pos[i]] = b[e:e + 1]
        edge = end_at[nb - 1]
        decs.append(jnp.exp2(edge))
        for i, r in enumerate(starts):
            m = r + sb // 2 if reverse else r + sb // 2 - 1
            bi, qi, ki, mid = b[r:r + sb], q[r:r + sb], k[r:r + sb], b[m:m + 1]
            qd.append((qi * jnp.exp2(bi - mid)).astype(BF16))
            kd.append((ki * jnp.exp2(mid - bi)).astype(BF16))
            for j in range(nb - 1):
                q_lev[j].append((qi * jnp.exp2(bi - end_at[j])).astype(BF16) if pos[i] > j else zero)
                k_lev[j].append((ki * jnp.exp2(end_at[j] - bi)).astype(BF16) if pos[i] == j else zero)
            qs.append((qi * jnp.exp2(bi)).astype(BF16))
            kl.append((ki * jnp.exp2(edge - bi)).astype(BF16))

    def rows(blocks):
        return jnp.concatenate(blocks, axis=0)
    s_lev = _qk(jnp.concatenate([rows(x) for x in q_lev], axis=1),
                jnp.concatenate([rows(x) for x in k_lev], axis=1))

    s_diag = _qk(rows(qd), rows(kd))
    scores = jnp.where(m_diag & jnp.logical_not(unsafe), s_diag, 0.0) + jnp.where(m_chunk, s_lev, 0.0)
    o = jnp.dot(scores.astype(BF16), v, preferred_element_type=F32)
    kl_wide = jnp.concatenate(
        [rows([blk if n // nb == g else zero for n, blk in enumerate(kl)]) for g in range(GROUP)], axis=1)
    upd_t = jnp.dot(vt, kl_wide, preferred_element_type=F32)
    return o, rows(qs), upd_t, decs


def _hgrn_body(*refs, n_in, n_seq, cps, has_s0):
    x_ref, mod_ref, n1g_ref = refs[:3]
    w_refs = refs[3:8]
    lb_ref, l1m_ref, ng_ref = refs[8:11]
    s0_ref = refs[11] if has_s0 else None
    og_ref = refs[n_in]
    st_ref = None if has_s0 else refs[n_in + 1]
    (h_scr, q_scr, v_scr, g_scr, lff_scr, kf_scr, lfb_scr, kb_scr, of_scr, ob_scr, qsf_scr, qsb_scr,
     uf_scr, ub_scr, decf_scr, decb_scr, sf_scr, sb_scr, w_ref, b_scr) = refs[n_in + (1 if has_s0 else 2):]
    c = SCAN_CHUNK
    n_chunks = SEQ_BLOCK // c
    n_tiles = SEQ_BLOCK // ROW_TILE
    for s, ref in enumerate(w_refs):
        w_ref[:, s * DK:(s + 1) * DK] = ref[...].astype(BF16)

    @pl.when(pl.program_id(1) == 0)
    def _():
        m = mod_ref[0]
        g = n1g_ref[...]

        def tile(t, carry):
            rows = pl.ds(pl.multiple_of(t * ROW_TILE, ROW_TILE), ROW_TILE)
            h_scr[rows, :] = _norm_mod(x_ref[rows, :], g, m[0:1], m[1:2]).astype(BF16)
            return carry
        lax.fori_loop(0, n_tiles, tile, 0)

    lb = lb_ref[...]
    l1m = l1m_ref[...]

    def proj_tiles(g, carry):
        tiles = []
        for u in range(PROJ_TILES):
            rows = pl.ds(pl.multiple_of((PROJ_TILES * g + u) * ROW_TILE, ROW_TILE), ROW_TILE)
            tiles.append((rows, h_scr[rows, :]))
        outs = []
        for rows, h in tiles:
            p = jnp.dot(h, w_ref[...], preferred_element_type=F32)
            outs.append((rows, p, _log_f_and_key(p[:, DK:2 * DK], lb[0:1], lb[2:3], l1m[0:1]),
                         _log_f_and_key(p[:, 2 * DK:3 * DK], lb[1:2], lb[3:4], l1m[1:2])))
        for rows, p, (lf_f, k_f), (lf_b, k_b) in outs:
            q_scr[rows, :] = p[:, 0:DK]
            v_scr[rows, :] = p[:, 3 * DK:3 * DK + DV]
            g_scr[rows, :] = p[:, 3 * DK + DV:3 * DK + 2 * DV]
            lff_scr[rows, :] = lf_f
            kf_scr[rows, :] = k_f
            lfb_scr[rows, :] = lf_b
            kb_scr[rows, :] = k_b
        return carry
    lax.fori_loop(0, n_tiles // PROJ_TILES, proj_tiles, 0)

    prefix, m_chunk, m_diag_f, m_diag_b = _group_masks()

    def group_rows(grp):
        return pl.ds(pl.multiple_of(grp * GROUP_ROWS, GROUP_ROWS), GROUP_ROWS)

    def local_load(grp):
        rows = group_rows(grp)
        return (q_scr[rows, :], v_scr[rows, :], lff_scr[rows, :], kf_scr[rows, :], lfb_scr[rows, :],
                kb_scr[rows, :])

    def log2_decays(lf_f, lf_b):
        sums = jnp.dot(prefix, jnp.concatenate(_split2(lf_f) + _split2(lf_b), axis=1),
                       preferred_element_type=F32)
        pre_b = _sum2(sums, 1)
        total_b = jnp.concatenate(
            [jnp.broadcast_to(pre_b[g * c + c - 1:g * c + c], (c, DK)) for g in range(GROUP)], axis=0)
        return _sum2(sums, 0) * LOG2_E, ((total_b - pre_b) + lf_b) * LOG2_E

    def sub_block_span(b):
        return functools.reduce(jnp.maximum, [jnp.abs(b[r:r + 1] - b[r + SCAN_SUB - 1:r + SCAN_SUB])
                                              for r in range(0, GROUP_ROWS, SCAN_SUB)])

    def local_compute(b_f, b_b, unsafe, q, v32, lf_f, k_f, lf_b, k_b):
        v = v32.astype(BF16)
        vt = v32.T.astype(BF16)
        return (_group_dir(b_f, q, k_f, v, vt, m_chunk, m_diag_f, unsafe, reverse=False),
                _group_dir(b_b, q, k_b, v, vt, m_chunk, m_diag_b, unsafe, reverse=True))

    def local_store(grp, fwd, bwd):
        rows = group_rows(grp)
        for (o, qs, upd_t, dec), o_scr, qs_scr, u_scr, dec_scr in (
                (fwd, of_scr, qsf_scr, uf_scr, decf_scr), (bwd, ob_scr, qsb_scr, ub_scr, decb_scr)):
            o_scr[rows, :] = o
            qs_scr[rows, :] = qs
            for g in range(GROUP):
                u_scr[grp * GROUP + g] = upd_t[:, g * DK:(g + 1) * DK]
                dec_scr[grp * GROUP + g] = dec[g]

    def local(it, carry):
        groups = [it * LOCAL_GROUPS + u for u in range(LOCAL_GROUPS)]
        loaded = [local_load(grp) for grp in groups]
        decays = [log2_decays(vals[2], vals[4]) for vals in loaded]
        span = functools.reduce(jnp.maximum, [sub_block_span(b) for pair in decays for b in pair])
        unsafe = jnp.max(span) > MAX_SUB_DECAY
        for u, (b_f, b_b) in enumerate(decays):
            b_scr[u, 0] = b_f
            b_scr[u, 1] = b_b
        results = [local_compute(b_f, b_b, unsafe, *vals) for (b_f, b_b), vals in zip(decays, loaded)]
        for grp, (fwd, bwd) in zip(groups, results):
            local_store(grp, fwd, bwd)

        @pl.when(unsafe)
        def _():
            for u, grp in enumerate(groups):
                rows = group_rows(grp)
                q, v = q_scr[rows, :], v_scr[rows, :].astype(BF16)
                of_scr[rows, :] += _same_sub_block_pairs(b_scr[u, 0], q, kf_scr[rows, :], v, reverse=False)
                ob_scr[rows, :] += _same_sub_block_pairs(b_scr[u, 1], q, kb_scr[rows, :], v, reverse=True)
        return carry
    lax.fori_loop(0, n_chunks // GROUP // LOCAL_GROUPS, local, 0)

    def advance(ci, cj, sf, sb):
        sf_scr[ci] = sf.astype(BF16)
        sb_scr[cj] = sb.astype(BF16)
        return sf * decf_scr[ci] + uf_scr[ci], sb * decb_scr[cj] + ub_scr[cj]

    if has_s0:
        def step(i, carry):
            return advance(i, n_chunks - 1 - i, *carry)
        lax.fori_loop(0, n_chunks, step, (s0_ref[0].T, s0_ref[1].T), unroll=4)
    else:
        def seq(s, carry):
            sf = jnp.zeros((DV, DK), F32)
            sb = jnp.zeros((DV, DK), F32)
            for i in range(cps):
                sf, sb = advance(s * cps + i, s * cps + cps - 1 - i, sf, sb)
            st_ref[s, 0] = sf.T
            st_ref[s, 1] = sb.T
            return carry
        lax.fori_loop(0, n_seq, seq, 0)

    ng = ng_ref[...]
    zero_chunk = jnp.zeros((c, DK), BF16)

    def widen(qs):
        return jnp.concatenate(
            [jnp.concatenate([qs[h * c:(h + 1) * c] if h == g else zero_chunk for h in range(GROUP)], axis=0)
             for g in range(GROUP)], axis=1)

    def out_groups(it, carry):
        loaded = []
        for u in range(LOCAL_GROUPS):
            grp = it * LOCAL_GROUPS + u
            rows = group_rows(grp)
            qs = jnp.concatenate([widen(qsf_scr[rows, :]), widen(qsb_scr[rows, :])], axis=1)
            st = jnp.concatenate([sf_scr[grp * GROUP + g] for g in range(GROUP)]
                                 + [sb_scr[grp * GROUP + g] for g in range(GROUP)], axis=1)
            loaded.append((rows, qs, st, of_scr[rows, :] + ob_scr[rows, :], g_scr[rows, :]))
        for rows, qs, st, o_local, gate in loaded:
            o = o_local + _qk(qs, st)
            o = o * lax.rsqrt(jnp.mean(o * o, axis=-1, keepdims=True) + NORM_EPS) * ng
            og_ref[rows, :] = (o * _silu(gate)).astype(BF16)
        return carry
    lax.fori_loop(0, n_chunks // GROUP // LOCAL_GROUPS, out_groups, 0)


class _Group:
    def __init__(self, first, n, seq_len, latent):
        self.first, self.n, self.seq_len, self.latent = first, n, seq_len, latent

    def cond_row(self, block):
        return block if self.latent else CTX_ROW


def _hgrn_call(x, mix, mod_all, layer, n1g, w_in, lb, log_1m_lb, norm_g, j, grp, s0, states, t, x_first=None):
    x_first = grp.first if x_first is None else x_first
    n_seq = SEQ_BLOCK // grp.seq_len
    cps = grp.seq_len // SCAN_CHUNK
    n_chunks = SEQ_BLOCK // SCAN_CHUNK
    has_s0 = s0 is not None
    n_layers = w_in.shape[0]
    w_cat = pltpu.VMEM((D_MODEL, 3 * DK + 2 * DV), BF16)
    decays = pltpu.VMEM((LOCAL_GROUPS, 2, GROUP_ROWS, DK), F32)

    def section(s):
        return pl.BlockSpec((None, D_MODEL, DK), lambda b, h: (j, 0, s * HEADS + h))

    in_specs = [
        pl.BlockSpec((SEQ_BLOCK, D_MODEL), lambda b, h: (b + x_first, 0)),
        pl.BlockSpec((None, 1, 6, D_MODEL), lambda b, h: (layer, grp.cond_row(b), 0, 0)),
        pl.BlockSpec((1, D_MODEL), lambda b, h: (0, 0)),
        section(0), section(1), section(2), section(3), section(4),
        pl.BlockSpec((4, DK), lambda b, h: (0, h)),
        pl.BlockSpec((2, DK), lambda b, h: (0, h)),
        pl.BlockSpec((1, DV), lambda b, h: (0, h)),
    ]
    args = [x, mod_all, n1g, w_in, w_in, w_in, w_in, w_in, lb, log_1m_lb, norm_g]
    og_spec = pl.BlockSpec((SEQ_BLOCK, DV), lambda b, h: (b + grp.first, h))
    og_shape = jax.ShapeDtypeStruct((t, D_MODEL), BF16)
    aliases = {}
    if has_s0:
        in_specs.append(pl.BlockSpec((None, None, 2, None, DK, DV), lambda b, h: (b, j, 0, h, 0, 0)))
        args.append(s0)
        out_specs, out_shape = og_spec, og_shape
    else:
        st_shape = (grp.n * n_seq, n_layers, 2, HEADS, DK, DV)
        out_specs = (og_spec,
                     pl.BlockSpec((n_seq, None, 2, None, DK, DV), lambda b, h: (b, j, 0, h, 0, 0)))
        out_shape = (og_shape, jax.ShapeDtypeStruct(st_shape, F32))
        if states is not None:
            in_specs.append(pl.BlockSpec(memory_space=pl.ANY))
            args.append(states)
            aliases[len(args) - 1] = 1
    if mix is not None:
        in_specs.append(pl.BlockSpec(memory_space=pl.ANY))
        args.append(mix)
        aliases[len(args) - 1] = 0
    col = pltpu.VMEM((SEQ_BLOCK, DK), F32)
    colb = pltpu.VMEM((SEQ_BLOCK, DK), BF16)
    upd = pltpu.VMEM((n_chunks, DV, DK), F32)
    dec = pltpu.VMEM((n_chunks, 1, DK), F32)
    start = pltpu.VMEM((n_chunks, DV, DK), BF16)
    res = pl.pallas_call(
        functools.partial(_hgrn_body, n_in=len(args), n_seq=n_seq, cps=cps, has_s0=has_s0),
        grid=(grp.n, HEADS),
        in_specs=in_specs,
        out_specs=out_specs,
        out_shape=out_shape,
        scratch_shapes=[pltpu.VMEM((SEQ_BLOCK, D_MODEL), BF16)] + [col] * 9 + [colb, colb, upd, upd, dec, dec,
                                                                                  start, start, w_cat, decays],
        input_output_aliases=aliases,
        compiler_params=_cparams("arbitrary", "arbitrary"),
        name="hgrn_lat" if has_s0 else "hgrn_ctx",
    )(*args)
    return (res, None) if has_s0 else res


def _fourier_body(*refs, n_in, seq_len):
    x_ref, mod_ref, n1g_ref, cl_ref, sl_ref, cc_ref = refs[:6]
    z_ref, h_scr = refs[n_in:]
    rt = pl.program_id(1)
    n_tiles = SEQ_BLOCK // ROW_TILE

    @pl.when(rt == 0)
    def _():
        m = mod_ref[0]
        g = n1g_ref[...]

        def tile(t, carry):
            rows = pl.ds(pl.multiple_of(t * ROW_TILE, ROW_TILE), ROW_TILE)
            h_scr[rows, :] = _norm_mod(x_ref[rows, :], g, m[0:1], m[1:2]).astype(BF16)
            return carry
        lax.fori_loop(0, n_tiles, tile, 0)

    if seq_len == ROW_TILE:
        h = h_scr[pl.ds(pl.multiple_of(rt * ROW_TILE, ROW_TILE), ROW_TILE), :]
    else:
        h = h_scr[...]
    zc = jnp.dot(cl_ref[...], h, preferred_element_type=F32).astype(BF16)
    zs = jnp.dot(sl_ref[...], h, preferred_element_type=F32).astype(BF16)
    scale = 1.0 / math.sqrt(seq_len * FOURIER_CG)
    cg = FOURIER_CG
    for g in range(FOURIER_GROUPS):
        cat = jnp.concatenate([zc[:, g * cg:(g + 1) * cg], zs[:, g * cg:(g + 1) * cg]], axis=1)
        out = jnp.dot(cat, cc_ref[...], preferred_element_type=F32) * scale
        z_ref[:, g * cg:(g + 1) * cg] = out.astype(BF16)


def _dft_tables(n):
    j = jnp.arange(n, dtype=jnp.int32)

    def direct(k):
        ang = ((k[:, None] * j[None, :]) % n).astype(F32) * (2.0 * math.pi / n)
        return jnp.cos(ang), jnp.sin(ang)
    if n <= DFT_SPLIT:
        return direct(j)
    ca, sa = direct(jnp.arange(n // DFT_SPLIT, dtype=jnp.int32) * DFT_SPLIT)
    cb, sb = direct(jnp.arange(DFT_SPLIT, dtype=jnp.int32))
    cos = ca[:, None, :] * cb[None, :, :] - sa[:, None, :] * sb[None, :, :]
    sin = sa[:, None, :] * cb[None, :, :] + ca[:, None, :] * sb[None, :, :]
    return cos.reshape(n, n), sin.reshape(n, n)


def _fourier_call(x, mix, mod_all, layer, n1g, grp):
    t = x.shape[0]
    seq_len = grp.seq_len
    n_rt = SEQ_BLOCK // ROW_TILE
    cl, sl = _dft_tables(seq_len)
    cc, sc = _dft_tables(FOURIER_CG)
    cc2 = jnp.concatenate([cc, -sc], axis=0).astype(BF16)
    if seq_len == ROW_TILE:
        pos_spec = pl.BlockSpec((ROW_TILE, seq_len), lambda b, r: (0, 0))
    else:
        pos_spec = pl.BlockSpec((ROW_TILE, seq_len), lambda b, r: (r, 0))
    in_specs = [
        pl.BlockSpec((SEQ_BLOCK, D_MODEL), lambda b, r: (b + grp.first, 0)),
        pl.BlockSpec((None, 1, 6, D_MODEL), lambda b, r: (layer, grp.cond_row(b), 0, 0)),
        pl.BlockSpec((1, D_MODEL), lambda b, r: (0, 0)),
        pos_spec, pos_spec,
        pl.BlockSpec((2 * FOURIER_CG, FOURIER_CG), lambda b, r: (0, 0)),
    ]
    args = [x, mod_all, n1g, cl.astype(BF16), sl.astype(BF16), cc2]
    aliases = {}
    if mix is not None:
        in_specs.append(pl.BlockSpec(memory_space=pl.ANY))
        args.append(mix)
        aliases[len(args) - 1] = 0
    return pl.pallas_call(
        functools.partial(_fourier_body, n_in=len(args), seq_len=seq_len),
        grid=(grp.n, n_rt),
        in_specs=in_specs,
        out_specs=pl.BlockSpec((ROW_TILE, D_MODEL), lambda b, r: ((b + grp.first) * n_rt + r, 0)),
        out_shape=jax.ShapeDtypeStruct((t, D_MODEL), BF16),
        scratch_shapes=[pltpu.VMEM((SEQ_BLOCK, D_MODEL), BF16)],
        input_output_aliases=aliases,
        compiler_params=_cparams("arbitrary", "arbitrary"),
        name="fourier_lat" if grp.latent else "fourier_ctx",
    )(*args)


def _route(lg):
    grp = [lg[g:g + 1] for g in range(MOE_GROUPS)]
    gmax = functools.reduce(jnp.maximum, grp)
    gi = jnp.where(grp[0] == gmax, 0, jnp.where(grp[1] == gmax, 1, jnp.where(grp[2] == gmax, 2, 3)))
    pgv = 1.0 / functools.reduce(lambda a, b: a + b, [jnp.exp(g - gmax) for g in grp])
    sel = []
    for j in range(MOE_EPG):
        rows = [lg[MOE_GROUPS + g * MOE_EPG + j:MOE_GROUPS + g * MOE_EPG + j + 1] for g in range(MOE_GROUPS)]
        sel.append(jnp.where(gi == 0, rows[0], jnp.where(gi == 1, rows[1], jnp.where(gi == 2, rows[2], rows[3]))))
    m1 = functools.reduce(jnp.maximum, sel)
    i1 = jnp.where(sel[0] == m1, 0, jnp.where(sel[1] == m1, 1, jnp.where(sel[2] == m1, 2, 3)))
    rest = [jnp.where(i1 == j, -jnp.inf, sel[j]) for j in range(MOE_EPG)]
    m2 = functools.reduce(jnp.maximum, rest)
    i2 = jnp.where(rest[0] == m2, 0, jnp.where(rest[1] == m2, 1, jnp.where(rest[2] == m2, 2, 3)))
    e2 = jnp.exp(m2 - m1)
    w1 = pgv / (1.0 + e2)
    w2 = pgv * e2 / (1.0 + e2)
    r = lg.shape[1]
    ex1 = gi * MOE_EPG + i1
    ex2 = gi * MOE_EPG + i2
    eid = lax.broadcasted_iota(jnp.int32, (MOE_EXPERTS, r), 0)
    member = (eid == ex1) | (eid == ex2)
    t0 = lax.broadcasted_iota(jnp.int32, (r, r), 0)
    t1 = lax.broadcasted_iota(jnp.int32, (r, r), 1)
    rank = jnp.dot(member.astype(BF16), (t0 < t1).astype(BF16), preferred_element_type=F32)
    count = jnp.sum(member.astype(F32), axis=1, keepdims=True)
    padded = jnp.floor((count + (ROW_UNIT - 1)) * (1.0 / ROW_UNIT)) * ROW_UNIT
    e0 = lax.broadcasted_iota(jnp.int32, (MOE_EXPERTS, MOE_EXPERTS), 0)
    e1 = lax.broadcasted_iota(jnp.int32, (MOE_EXPERTS, MOE_EXPERTS), 1)
    start = jnp.dot((e1 < e0).astype(F32), jnp.broadcast_to(padded, (MOE_EXPERTS, LANES)),
                    precision=HIGHEST, preferred_element_type=F32)[:, 0:1]
    row = start + rank
    row1 = jnp.sum(jnp.where(eid == ex1, row, 0.0), axis=0, keepdims=True)
    row2 = jnp.sum(jnp.where(eid == ex2, row, 0.0), axis=0, keepdims=True)
    rid = lax.broadcasted_iota(jnp.int32, (LANES, r), 0)
    fields = (row1, row2, w1, w2, ex1.astype(F32), ex2.astype(F32))
    table = jnp.zeros((LANES, r), F32)
    for i, f in enumerate(fields):
        table = jnp.where(rid == i, f, table)
    return table, count


def _post_body(*refs, ctx_steps):
    if ctx_steps is None:
        x = refs[0][...]
    else:
        x = jnp.where(pl.program_id(0) < ctx_steps, refs[0][...], refs[1][...])
        refs = refs[1:]
    _, mix_ref, mod_ref, w_ref, n2g_ref, wr_ref, br_ref, xo_ref, xb_ref, ct_ref, cnt_ref = refs
    m = mod_ref[0]
    out = jnp.dot(mix_ref[...], w_ref[...], preferred_element_type=F32)
    xn = x + m[2:3] * out
    xo_ref[...] = xn
    h2 = _norm_mod(xn, n2g_ref[...], m[3:4], m[4:5])
    wr_hi, wr_lo = _split2(wr_ref[...])
    h2_hi, h2_lo = _split2(h2)
    lg = _qk(wr_hi, h2_hi) + (_qk(wr_hi, h2_lo) + _qk(wr_lo, h2_hi)) + br_ref[...]
    h2 = h2_hi
    rid = lax.broadcasted_iota(jnp.int32, (BLOCK_ROWS, TOKEN_BLOCK), 0)
    for u in range(POST_BLOCKS):
        cols = slice(u * TOKEN_BLOCK, (u + 1) * TOKEN_BLOCK)
        table, count = _route(lg[:, cols])
        onehot = ((rid == table[0:1].astype(jnp.int32)) | (rid == table[1:2].astype(jnp.int32))).astype(BF16)
        xb_ref[u * BLOCK_ROWS:(u + 1) * BLOCK_ROWS, :] = jnp.dot(
            onehot, h2[cols, :], preferred_element_type=F32).astype(BF16)
        ct_ref[cols, :] = table.T
        cnt_ref[u] = jnp.broadcast_to(count, (MOE_EXPERTS, LANES))


def _post_call(x, mix, mod_all, layer, w_out, j, n2g, wr_t, br, mod_row):
    t = mix.shape[0]
    nb = t // TOKEN_BLOCK
    rows = POST_BLOCKS * TOKEN_BLOCK
    tile = lambda i: (i, 0)
    full = lambda i: (0, 0)
    if isinstance(x, tuple):
        ctx_steps = x[0].shape[0] // rows
        x_args = list(x)
        x_specs = [pl.BlockSpec((rows, D_MODEL), lambda i: (jnp.minimum(i, ctx_steps - 1), 0)),
                   pl.BlockSpec((rows, D_MODEL), lambda i: (jnp.maximum(i - ctx_steps, 0), 0))]
    else:
        ctx_steps, x_args, x_specs = None, [x], [pl.BlockSpec((rows, D_MODEL), tile)]
    return pl.pallas_call(
        functools.partial(_post_body, ctx_steps=ctx_steps),
        grid=(nb // POST_BLOCKS,),
        in_specs=x_specs + [
            pl.BlockSpec((rows, D_MODEL), tile),
            pl.BlockSpec((None, 1, 6, D_MODEL), lambda i: (layer, mod_row(i * POST_BLOCKS), 0, 0)),
            pl.BlockSpec((None, D_MODEL, D_MODEL), lambda i: (j, 0, 0)),
            pl.BlockSpec((1, D_MODEL), full),
            pl.BlockSpec((ROUTER_ROWS, D_MODEL), full),
            pl.BlockSpec((ROUTER_ROWS, 1), full),
        ],
        out_specs=(
            pl.BlockSpec((rows, D_MODEL), tile),
            pl.BlockSpec((POST_BLOCKS * BLOCK_ROWS, D_MODEL), tile),
            pl.BlockSpec((rows, LANES), tile),
            pl.BlockSpec((POST_BLOCKS, MOE_EXPERTS, LANES), lambda i: (i, 0, 0)),
        ),
        out_shape=(
            jax.ShapeDtypeStruct((t, D_MODEL), F32),
            jax.ShapeDtypeStruct((nb * BLOCK_ROWS, D_MODEL), BF16),
            jax.ShapeDtypeStruct((t, LANES), F32),
            jax.ShapeDtypeStruct((nb, MOE_EXPERTS, LANES), F32),
        ),
        compiler_params=_cparams("arbitrary"),
        name="post",
    )(*x_args, mix, mod_all, w_out, n2g, wr_t, br)


def _moe_schedule(counts, n_tiles_max):
    units = (counts + ROW_UNIT - 1) // ROW_UNIT
    local = jnp.cumsum(units, axis=1) - units
    total = jnp.sum(units, axis=0)
    per_tile = FFN_TILE // ROW_UNIT
    tiles = (total + per_tile - 1) // per_tile
    region = tiles * per_tile
    region_start = jnp.cumsum(region) - region
    glob = region_start[None, :] + jnp.cumsum(units, axis=0) - units
    tile_end = jnp.cumsum(tiles)
    tile_start = tile_end - tiles
    n_tiles = tile_end[-1]
    s = jnp.arange(n_tiles_max, dtype=jnp.int32)
    live = s < n_tiles
    tile_expert = jnp.sum((tile_end[None, :] <= jnp.minimum(s, n_tiles - 1)[:, None]).astype(jnp.int32), axis=1)
    first = jnp.any((s[:, None] == tile_start[None, :]) & (tiles[None, :] > 0), axis=1) & live
    g = jnp.arange(n_tiles_max * per_tile, dtype=jnp.int32)
    block_units = BLOCK_ROWS // ROW_UNIT
    blocks = jnp.arange(counts.shape[0], dtype=jnp.int32)[:, None]
    seg_start = glob.T.reshape(-1)
    seg_offset = (blocks * block_units + local - glob).T.reshape(-1)
    step = seg_offset - jnp.concatenate([jnp.zeros((1,), jnp.int32), seg_offset[:-1]])
    source = g + jnp.sum(jnp.where(seg_start[None, :] <= g[:, None], step[None, :], 0), axis=1)
    real = jnp.any((g[:, None] >= region_start[None, :]) & (g[:, None] < (region_start + total)[None, :]), axis=1)
    source = jnp.where(real, source, block_units - 1)
    i32 = lambda a: a.astype(jnp.int32).reshape(-1)
    return {
        "local": i32(local), "units": i32(units), "glob": i32(glob), "source": i32(source),
        "tile_expert": tile_expert, "tile_first": i32(first), "n_tiles": i32(n_tiles),
    }


def _segment_copies(local_ref, units_ref, glob_ref, blk, make_copy):
    def per_expert(e, n):
        idx = blk * MOE_EXPERTS + e
        loc, cnt, glo = local_ref[idx], units_ref[idx], glob_ref[idx]

        def unit(u, carry):
            make_copy(pl.multiple_of((loc + u) * ROW_UNIT, ROW_UNIT),
                      pl.multiple_of((glo + u) * ROW_UNIT, ROW_UNIT)).start()
            return carry
        lax.fori_loop(0, cnt, unit, 0)
        return n + cnt
    return lax.fori_loop(0, MOE_EXPERTS, per_expert, 0)


def _wait_copies(n, make_copy):
    def one(u, carry):
        make_copy(0, 0).wait()
        return carry
    lax.fori_loop(0, n, one, 0)


def _ffn_body(expert_ref, first_ref, n_ref, source_ref, xb_ref, wg_ref, wu_ref, wd_ref, y_ref,
              x_scr, sem, wg_scr, wu_scr, wd_scr):
    s = pl.program_id(0)
    slot = s % 2
    per_tile = FFN_TILE // ROW_UNIT

    def gather(tile, buf, start):
        for u in range(per_tile):
            src = source_ref[tile * per_tile + u] if start else 0
            copy = pltpu.make_async_copy(xb_ref.at[pl.ds(pl.multiple_of(src * ROW_UNIT, ROW_UNIT), ROW_UNIT)],
                                         x_scr.at[buf, pl.ds(u * ROW_UNIT, ROW_UNIT)], sem.at[buf])
            if start:
                copy.start()
            else:
                copy.wait()

    @pl.when(s == 0)
    def _():
        gather(0, 0, start=True)

    @pl.when(s + 1 < n_ref[0])
    def _():
        gather(s + 1, 1 - slot, start=True)

    @pl.when(s < n_ref[0])
    def _():
        @pl.when(first_ref[s] == 1)
        def _():
            wg_scr[...] = wg_ref[...].astype(BF16)
            wu_scr[...] = wu_ref[...].astype(BF16)
            wd_scr[...] = wd_ref[...].astype(BF16)
        gather(s, slot, start=False)
        x = x_scr[slot]
        a = jnp.dot(x, wg_scr[...], preferred_element_type=F32)
        u = jnp.dot(x, wu_scr[...], preferred_element_type=F32)
        hid = (_silu(a) * u).astype(BF16)
        y_ref[...] = jnp.dot(hid, wd_scr[...], preferred_element_type=F32).astype(BF16)

    @pl.when(s >= n_ref[0])
    def _():
        y_ref[...] = jnp.zeros_like(y_ref)


def _ffn_call(xb, sched, w_gate, w_up, w_down, layer, n_tiles_max):
    def weight(s, expert, first, n, source):
        return (layer, expert[s], 0, 0)
    return pl.pallas_call(
        _ffn_body,
        grid_spec=pltpu.PrefetchScalarGridSpec(
            num_scalar_prefetch=4,
            grid=(n_tiles_max,),
            in_specs=[
                pl.BlockSpec(memory_space=pl.ANY),
                pl.BlockSpec((None, None, D_MODEL, MOE_D_FF), weight),
                pl.BlockSpec((None, None, D_MODEL, MOE_D_FF), weight),
                pl.BlockSpec((None, None, MOE_D_FF, D_MODEL), weight),
            ],
            out_specs=pl.BlockSpec((FFN_TILE, D_MODEL), lambda s, *_: (s, 0)),
            scratch_shapes=[pltpu.VMEM((2, FFN_TILE, D_MODEL), BF16), pltpu.SemaphoreType.DMA((2,)),
                            pltpu.VMEM((D_MODEL, MOE_D_FF), BF16), pltpu.VMEM((D_MODEL, MOE_D_FF), BF16),
                            pltpu.VMEM((MOE_D_FF, D_MODEL), BF16)],
        ),
        out_shape=jax.ShapeDtypeStruct((n_tiles_max * FFN_TILE, D_MODEL), BF16),
        compiler_params=_cparams("arbitrary"),
        name="moe_ffn",
    )(sched["tile_expert"], sched["tile_first"], sched["n_tiles"], sched["source"], xb, w_gate, w_up, w_down)


def _combine_body(*refs, split):
    final = split is not None
    local_ref, units_ref, glob_ref, x_ref, ct_ref, mod_ref = refs[:6]
    fg_ref = refs[6] if final else None
    ys_ref = refs[7 if final else 6]
    out_refs = refs[(8 if final else 7):-2]
    rows_scr, sem = refs[-2:]
    step = pl.program_id(0)
    slot = step % 2

    def copier(s, j):
        def copy(dst_row, src_row):
            return pltpu.make_async_copy(ys_ref.at[pl.ds(src_row, ROW_UNIT)],
                                         rows_scr.at[s, j, pl.ds(dst_row, ROW_UNIT)], sem.at[s])
        return copy

    def request(st, s):
        for j in range(COMBINE_BLOCKS):
            _segment_copies(local_ref, units_ref, glob_ref, st * COMBINE_BLOCKS + j, copier(s, j))

    @pl.when(step == 0)
    def _():
        rows_scr[...] = jnp.zeros_like(rows_scr)
        request(0, 0)

    @pl.when(step + 1 < pl.num_programs(0))
    def _():
        request(step + 1, 1 - slot)

    first_seg = step * COMBINE_BLOCKS * MOE_EXPERTS
    n_copies = lax.fori_loop(0, COMBINE_BLOCKS * MOE_EXPERTS, lambda e, n: n + units_ref[first_seg + e], 0)
    _wait_copies(n_copies, copier(slot, 0))

    rid = lax.broadcasted_iota(jnp.int32, (TOKEN_BLOCK, BLOCK_ROWS), 1)
    ys = []
    for j in range(COMBINE_BLOCKS):
        table = ct_ref[j * TOKEN_BLOCK:(j + 1) * TOKEN_BLOCK, :]
        row1 = table[:, 0:1].astype(jnp.int32)
        row2 = table[:, 1:2].astype(jnp.int32)
        weights = (jnp.where(rid == row1, table[:, 2:3], 0.0) + jnp.where(rid == row2, table[:, 3:4], 0.0))
        ys.append(jnp.dot(weights.astype(BF16), rows_scr[slot, j], preferred_element_type=F32))
    out = x_ref[...] + mod_ref[0][5:6] * jnp.concatenate(ys, axis=0)
    if not final:
        out_refs[0][...] = out
    else:
        ms = jnp.mean(out * out, axis=-1, keepdims=True)
        out = out * lax.rsqrt(ms + NORM_EPS) * fg_ref[...]

        @pl.when(step < split // COMBINE_BLOCKS)
        def _():
            out_refs[0][...] = out

        @pl.when(step >= split // COMBINE_BLOCKS)
        def _():
            out_refs[1][...] = out


def _combine_call(x, route_c, ys, sched, mod_all, layer, final_g, split, mod_row):
    t = x.shape[0]
    final = final_g is not None
    rows = COMBINE_BLOCKS * TOKEN_BLOCK
    tok = (rows, D_MODEL)
    if final:
        split_steps = split // COMBINE_BLOCKS
        out_specs = (pl.BlockSpec(tok, lambda i, *_: (jnp.minimum(i, split_steps - 1), 0)),
                     pl.BlockSpec(tok, lambda i, *_: (jnp.maximum(i - split_steps, 0), 0)))
        out_shape = (jax.ShapeDtypeStruct((split * TOKEN_BLOCK, D_MODEL), F32),
                     jax.ShapeDtypeStruct((t - split * TOKEN_BLOCK, D_MODEL), F32))
    else:
        out_specs = pl.BlockSpec(tok, lambda i, *_: (i, 0))
        out_shape = jax.ShapeDtypeStruct((t, D_MODEL), F32)
    in_specs = [
        pl.BlockSpec((rows, D_MODEL), lambda i, *_: (i, 0)),
        pl.BlockSpec((rows, LANES), lambda i, *_: (i, 0)),
        pl.BlockSpec((None, 1, 6, D_MODEL), lambda i, *_: (layer, mod_row(i * COMBINE_BLOCKS), 0, 0)),
    ]
    args = [x, route_c, mod_all]
    if final:
        in_specs.append(pl.BlockSpec((1, D_MODEL), lambda i, *_: (0, 0)))
        args.append(final_g)
    in_specs.append(pl.BlockSpec(memory_space=pl.ANY))
    args.append(ys)
    return pl.pallas_call(
        functools.partial(_combine_body, split=split if final else None),
        grid_spec=pltpu.PrefetchScalarGridSpec(
            num_scalar_prefetch=3,
            grid=(t // rows,),
            in_specs=in_specs,
            out_specs=out_specs,
            scratch_shapes=[pltpu.VMEM((2, COMBINE_BLOCKS, BLOCK_ROWS, D_MODEL), BF16),
                            pltpu.SemaphoreType.DMA((2,))],
        ),
        out_shape=out_shape,
        compiler_params=_cparams("arbitrary"),
        name="moe_combine",
    )(sched["local"], sched["units"], sched["glob"], *args)


def _moe(x, xb, route, counts, mod_all, layer, w_gate, w_up, w_down, final_g, split, mod_row):
    t = x.shape[0]
    nb = t // TOKEN_BLOCK
    max_rows = MOE_TOP_K * t + nb * MOE_EXPERTS * (ROW_UNIT - 1) + MOE_EXPERTS * (FFN_TILE - 1)
    n_tiles_max = -(-max_rows // FFN_TILE)
    sched = _moe_schedule(counts[:, :, 0].astype(jnp.int32), n_tiles_max)
    ys = _ffn_call(xb, sched, w_gate, w_up, w_down, layer, n_tiles_max)
    return _combine_call(x, route, ys, sched, mod_all, layer, final_g, split, mod_row)


def _grid_pos_embed(n_tokens):
    t = jnp.arange(n_tokens)
    r = (t // GRID_W).astype(F32)
    col = (t % GRID_W).astype(F32)
    quarter = D_MODEL // 4
    omega = 1.0 / (POS_BASE ** (jnp.arange(quarter, dtype=F32) / quarter))
    ar = r[:, None] * omega[None, :]
    ac = col[:, None] * omega[None, :]
    return jnp.concatenate([jnp.sin(ar), jnp.cos(ar), jnp.sin(ac), jnp.cos(ac)], axis=-1)


def _run_trunk(x_ctx, x_lat, ctx, lat, init_states, mod_all, p):
    n_ctx_blocks = ctx.n * (SEQ_BLOCK // TOKEN_BLOCK)
    lat_per_seq = lat.seq_len // TOKEN_BLOCK
    t = x_ctx.shape[0] + x_lat.shape[0]
    x = (x_ctx, x_lat)

    def cond_row(i):
        return jnp.where(i < n_ctx_blocks, CTX_ROW, (i - n_ctx_blocks) // lat_per_seq)

    states = None
    for i in range(DEPTH):
        j = i // 2
        n1g = p["norm1_g"][i:i + 1]
        if i % 2 == 0:
            hgrn = functools.partial(_hgrn_call, mod_all=mod_all, layer=i, n1g=n1g, w_in=p["hgrn_w_in"],
                                     lb=p["lb"][j], log_1m_lb=p["log_1m_lb"][j],
                                     norm_g=p["hgrn_norm_g"][j:j + 1], j=j, t=t)
            if isinstance(x, tuple):
                mix, states = hgrn(x[0], mix=None, grp=ctx, s0=None, states=states, x_first=0)
                mix, _ = hgrn(x[1], mix=mix, grp=lat, s0=init_states, states=None, x_first=0)
            else:
                mix, states = hgrn(x, mix=None, grp=ctx, s0=None, states=states)
                mix, _ = hgrn(x, mix=mix, grp=lat, s0=init_states, states=None)
            w_out = p["hgrn_w_out"]
        else:
            mix = _fourier_call(x, None, mod_all, i, n1g, ctx)
            mix = _fourier_call(x, mix, mod_all, i, n1g, lat)
            w_out = p["fourier_w_out"]
        x, xb, route, counts = _post_call(x, mix, mod_all, i, w_out, j, p["norm2_g"][i:i + 1],
                                          p["router_w_t"][i], p["router_b"][i], cond_row)
        final_g = p["final_norm_g"] if i == DEPTH - 1 else None
        x = _moe(x, xb, route, counts, mod_all, i, p["moe_w_gate"], p["moe_w_up"], p["moe_w_down"],
                 final_g, n_ctx_blocks, cond_row)
    return x, states


def kernel(x_prompt, x_sample, state_hgrn, c, c_ctx, w_mod, b_mod, norm1_g, norm2_g, hgrn_w_in,
           hgrn_lb_logits, hgrn_norm_g, hgrn_w_out, fourier_w_out, router_group_w, router_group_b,
           router_expert_w, router_expert_b, moe_w_gate, moe_w_up, moe_w_down, final_norm_g):
    batch, seq, _ = x_prompt.shape
    dec_batch, dec_seq, _ = x_sample.shape
    assert dec_batch <= CTX_ROW and seq == ROW_TILE and dec_seq == SEQ_BLOCK

    cond = jnp.zeros((COND_ROWS, D_MODEL), F32).at[:dec_batch].set(c).at[CTX_ROW].set(c_ctx)
    mod_all = _mod_call(cond, w_mod, b_mod).reshape(DEPTH, COND_ROWS, 6, D_MODEL)

    probs = jax.nn.softmax(hgrn_lb_logits.astype(F32), axis=0)
    cs = jnp.cumsum(probs, axis=0)
    lb = cs - cs[0:1]
    router_w = jnp.concatenate([router_group_w, router_expert_w], axis=-1)
    router_b = jnp.concatenate([router_group_b, router_expert_b], axis=-1)
    pad = ROUTER_ROWS - router_w.shape[-1]
    p = {
        "norm1_g": norm1_g, "norm2_g": norm2_g, "hgrn_norm_g": hgrn_norm_g,
        "hgrn_w_in": hgrn_w_in, "hgrn_w_out": hgrn_w_out.astype(BF16),
        "fourier_w_out": fourier_w_out.astype(BF16),
        "lb": jnp.concatenate([lb, 1.0 - lb], axis=1), "log_1m_lb": jnp.log1p(-lb),
        "router_w_t": jnp.pad(jnp.swapaxes(router_w, 1, 2), ((0, 0), (0, pad), (0, 0))),
        "router_b": jnp.pad(router_b, ((0, 0), (0, pad)))[..., None],
        "moe_w_gate": moe_w_gate, "moe_w_up": moe_w_up, "moe_w_down": moe_w_down,
        "final_norm_g": final_norm_g.reshape(1, D_MODEL),
    }

    t_ctx = batch * seq
    t_lat = dec_batch * dec_seq
    ctx = _Group(0, t_ctx // SEQ_BLOCK, seq, latent=False)
    lat = _Group(t_ctx // SEQ_BLOCK, t_lat // SEQ_BLOCK, dec_seq, latent=True)
    x_lat = _embed_call(x_sample.reshape(t_lat, D_MODEL), _grid_pos_embed(dec_seq))
    (y_ctx, y_lat), new_state = _run_trunk(x_prompt.reshape(t_ctx, D_MODEL), x_lat, ctx, lat, state_hgrn,
                                           mod_all, p)
    return (y_ctx.reshape(batch, seq, D_MODEL), y_lat.reshape(dec_batch, dec_seq, D_MODEL), new_state)
```
